```python
import jax
import jax.numpy as jnp
from jax import lax
import numpy as np

D_MODEL = 1024
BATCH = 2
SEQ = 8192
DEPTH = 2

GRID_W = 64
CTX_LEN = 256
HEAD_DIM = 64
FOURIER_GROUPS = 4
FOURIER_GROUP_DIM = 64
FOURIER_WIDTH = FOURIER_GROUPS * FOURIER_GROUP_DIM
RET_HEADS = 4
RET_DK = 64
RET_DV = 64
RET_QK_WIDTH = RET_HEADS * RET_DK
RET_WIDTH = RET_HEADS * RET_DV
RET_CHUNK = 128
ATT_HEADS = 8
ATT_KV_HEADS = 2
ATT_GROUP = ATT_HEADS // ATT_KV_HEADS
ATT_WIDTH = ATT_HEADS * HEAD_DIM
ATT_KV_WIDTH = ATT_KV_HEADS * HEAD_DIM
WINDOW = 128
ATT_BLOCK = 128
KEY_SPAN = ATT_BLOCK + 2 * WINDOW
D_MIX = FOURIER_WIDTH + RET_WIDTH + ATT_WIDTH
D_FF = 2816
N_MOD = 9
ROPE_BASE = 10000.0
ROPE_AXIS_PAIRS = HEAD_DIM // 4
NORM_EPS = 1e-6
NEG_INF = -1e30
IN_SIZES = (FOURIER_WIDTH, RET_QK_WIDTH, RET_QK_WIDTH, RET_WIDTH, RET_WIDTH, ATT_WIDTH, ATT_KV_WIDTH, ATT_KV_WIDTH)
IN_OFFSETS = tuple(int(o) for o in np.cumsum(IN_SIZES)[:-1])
D_IN = int(sum(IN_SIZES))

kernel_name = 'hybrid_fourier_retention_swa_macaron_dit'


def rms_norm(x, w):
    xf = x.astype(jnp.float32)
    y = xf * lax.rsqrt(jnp.mean(xf * xf, axis=-1, keepdims=True) + NORM_EPS) * w.astype(jnp.float32)
    return y.astype(x.dtype)


def modulate(h, shift, scale):
    return h * (1.0 + scale) + shift


def swiglu(h, w_in, w_out):
    gate, up = jnp.split(h @ w_in, 2, axis=-1)
    return (jax.nn.silu(gate) * up) @ w_out


def to_heads(t, n_heads):
    return t.reshape(t.shape[0], t.shape[1], n_heads, t.shape[2] // n_heads)


def axial_rope_angles(length):
    rows = length // GRID_W
    row = jnp.repeat(jnp.arange(rows, dtype=jnp.float32), GRID_W)
    col = jnp.tile(jnp.arange(GRID_W, dtype=jnp.float32), rows)
    inv_freq = ROPE_BASE ** (-jnp.arange(ROPE_AXIS_PAIRS, dtype=jnp.float32) / ROPE_AXIS_PAIRS)
    ang = jnp.stack([row[:, None] * inv_freq, col[:, None] * inv_freq], axis=1)
    return jnp.cos(ang), jnp.sin(ang)


def apply_axial_rope(x, cos, sin):
    b, l, h, d = x.shape
    xr = x.astype(jnp.float32).reshape(b, l, h, 2, 2, ROPE_AXIS_PAIRS)
    x1, x2 = xr[..., 0, :], xr[..., 1, :]
    cs, sn = cos[:, None], sin[:, None]
    out = jnp.stack([x1 * cs - x2 * sn, x2 * cs + x1 * sn], axis=-2)
    return out.reshape(b, l, h, d).astype(x.dtype)


def fourier_mix(u):
    b, l, _ = u.shape
    g = u.astype(jnp.float32).reshape(b, l, FOURIER_GROUPS, FOURIER_GROUP_DIM)
    y = jnp.fft.fft2(g, axes=(1, 3), norm='ortho').real
    return y.reshape(b, l, FOURIER_WIDTH).astype(u.dtype)


def retention_scan(q, k, v, log_gamma, state0):
    b, l, h, _ = q.shape
    dv = v.shape[-1]
    n = l // RET_CHUNK
    cl = RET_CHUNK

    def chunks(t):
        return t.reshape(b, n, cl, h, t.shape[-1]).transpose(1, 0, 3, 2, 4)

    qc, kc, vc = chunks(q), chunks(k), chunks(v)
    pos = jnp.arange(cl, dtype=jnp.float32)
    diff = pos[:, None] - pos[None, :]
    causal = diff >= 0
    intra_decay = jnp.where(causal[None], jnp.exp(jnp.where(causal, diff, 0.0)[None] * log_gamma[:, None, None]), 0.0)
    q_decay = jnp.exp((pos + 1.0)[None] * log_gamma[:, None])
    k_decay = jnp.exp((cl - 1.0 - pos)[None] * log_gamma[:, None])
    chunk_decay = jnp.exp(cl * log_gamma)
    scores = jnp.einsum('nbhik,nbhjk->nbhij', qc, kc) * intra_decay
    intra = jnp.einsum('nbhij,nbhjv->nbhiv', scores, vc)
    kv = jnp.einsum('nbhjk,hj,nbhjv->nbhkv', kc, k_decay, vc)

    def step(state, kv_chunk):
        return state * chunk_decay[None, :, None, None] + kv_chunk, state

    state_final, state_in = lax.scan(step, state0, kv)
    cross = jnp.einsum('nbhik,nbhkv->nbhiv', qc, state_in) * q_decay[None, None, :, :, None]
    out = (intra + cross).transpose(1, 0, 3, 2, 4).reshape(b, l, h, dv)
    return out, state_final


def bidirectional_retention(q_lat, k_lat, v_lat, q_ctx, k_ctx, v_ctx, log_gamma):
    f32 = jnp.float32
    scale = RET_DK ** -0.5
    q_lat, v_lat, q_ctx, v_ctx = q_lat.astype(f32), v_lat.astype(f32), q_ctx.astype(f32), v_ctx.astype(f32)
    k_lat = k_lat.astype(f32) * scale
    k_ctx = k_ctx.astype(f32) * scale
    state0 = jnp.zeros((q_lat.shape[0], RET_HEADS, RET_DK, RET_DV), f32)
    outs_lat = []
    outs_ctx = []
    for direction in range(2):
        if direction == 0:
            rev = lambda t: t
        else:
            rev = lambda t: jnp.flip(t, axis=1)
        o_c, s_c = retention_scan(rev(q_ctx), rev(k_ctx), rev(v_ctx), log_gamma[direction], state0)
        o_l, _ = retention_scan(rev(q_lat), rev(k_lat), rev(v_lat), log_gamma[direction], s_c)
        outs_ctx.append(rev(o_c))
        outs_lat.append(rev(o_l))
    return outs_lat[0] + outs_lat[1], outs_ctx[0] + outs_ctx[1]


def retention_output(o, gate, gn_w):
    b, l = o.shape[0], o.shape[1]
    mu = jnp.mean(o, axis=-1, keepdims=True)
    var = jnp.mean(jnp.square(o - mu), axis=-1, keepdims=True)
    on = ((o - mu) * lax.rsqrt(var + NORM_EPS)).reshape(b, l, RET_WIDTH) * gn_w.astype(jnp.float32)
    return (jax.nn.silu(gate.astype(jnp.float32)) * on).astype(gate.dtype)


def windowed_attention_latent(q, k, v, k_ctx, v_ctx, sink):
    b, s = q.shape[0], q.shape[1]
    nb = s // ATT_BLOCK
    f32 = jnp.float32
    qb = q.astype(f32).reshape(b, nb, ATT_BLOCK, ATT_KV_HEADS, ATT_GROUP, HEAD_DIM) * (HEAD_DIM ** -0.5)
    pad = ((0, 0), (WINDOW, WINDOW), (0, 0), (0, 0))
    kp = jnp.pad(k.astype(f32), pad)
    vp = jnp.pad(v.astype(f32), pad)
    block_start = jnp.arange(nb) * ATT_BLOCK
    key_idx = block_start[:, None] + jnp.arange(KEY_SPAN)[None, :]
    kw = kp[:, key_idx]
    vw = vp[:, key_idx]
    key_pos = key_idx - WINDOW
    q_pos = block_start[:, None] + jnp.arange(ATT_BLOCK)[None, :]
    rel = key_pos[:, None, :] - q_pos[:, :, None]
    valid = (jnp.abs(rel) <= WINDOW) & (key_pos[:, None, :] >= 0) & (key_pos[:, None, :] < s)
    s_loc = jnp.einsum('bnqhgd,bnkhd->bnhgqk', qb, kw)
    s_loc = jnp.where(valid[None, :, None, None], s_loc, NEG_INF)
    s_ctx = jnp.einsum('bnqhgd,bchd->bnhgqc', qb, k_ctx.astype(f32))
    sink_l = sink.astype(f32).reshape(1, 1, ATT_KV_HEADS, ATT_GROUP, 1, 1)
    m = jnp.maximum(jnp.maximum(s_loc.max(-1, keepdims=True), s_ctx.max(-1, keepdims=True)), sink_l)
    p_loc = jnp.exp(s_loc - m)
    p_ctx = jnp.exp(s_ctx - m)
    denom = p_loc.sum(-1, keepdims=True) + p_ctx.sum(-1, keepdims=True) + jnp.exp(sink_l - m)
    o = jnp.einsum('bnhgqk,bnkhd->bnqhgd', p_loc / denom, vw) + jnp.einsum('bnhgqc,bchd->bnqhgd', p_ctx / denom, v_ctx.astype(f32))
    return o.reshape(b, s, ATT_WIDTH).astype(q.dtype)


def context_attention(q, k, v, sink):
    b, l = q.shape[0], q.shape[1]
    f32 = jnp.float32
    qg = q.astype(f32).reshape(b, l, ATT_KV_HEADS, ATT_GROUP, HEAD_DIM) * (HEAD_DIM ** -0.5)
    sc = jnp.einsum('blhgd,bmhd->bhglm', qg, k.astype(f32))
    sink_l = sink.astype(f32).reshape(1, ATT_KV_HEADS, ATT_GROUP, 1, 1)
    m = jnp.maximum(sc.max(-1, keepdims=True), sink_l)
    p = jnp.exp(sc - m)
    denom = p.sum(-1, keepdims=True) + jnp.exp(sink_l - m)
    o = jnp.einsum('bhglm,bmhd->blhgd', p / denom, v.astype(f32))
    return o.reshape(b, l, ATT_WIDTH).astype(q.dtype)


def token_mix(h_lat, h_ctx, w_in, w_o, log_gamma, gn_w, sink, rope_cos, rope_sin, need_ctx_out):
    f_l, rq_l, rk_l, rv_l, rg_l, aq_l, ak_l, av_l = jnp.split(h_lat @ w_in, IN_OFFSETS, axis=-1)
    f_c, rq_c, rk_c, rv_c, rg_c, aq_c, ak_c, av_c = jnp.split(h_ctx @ w_in, IN_OFFSETS, axis=-1)
    rq_l = apply_axial_rope(to_heads(rq_l, RET_HEADS), rope_cos, rope_sin)
    rk_l = apply_axial_rope(to_heads(rk_l, RET_HEADS), rope_cos, rope_sin)
    ret_l, ret_c = bidirectional_retention(rq_l, rk_l, to_heads(rv_l, RET_HEADS), to_heads(rq_c, RET_HEADS), to_heads(rk_c, RET_HEADS), to_heads(rv_c, RET_HEADS), log_gamma)
    y_ret = retention_output(ret_l, rg_l, gn_w)
    aq_l = apply_axial_rope(to_heads(aq_l, ATT_HEADS), rope_cos, rope_sin)
    ak_l = apply_axial_rope(to_heads(ak_l, ATT_KV_HEADS), rope_cos, rope_sin)
    ak_c = to_heads(ak_c, ATT_KV_HEADS)
    av_c = to_heads(av_c, ATT_KV_HEADS)
    y_att = windowed_attention_latent(aq_l, ak_l, to_heads(av_l, ATT_KV_HEADS), ak_c, av_c, sink)
    y_f = fourier_mix(f_l)
    out_lat = jnp.concatenate([y_f, y_ret, y_att], axis=-1) @ w_o
    if not need_ctx_out:
        return out_lat, None
    y_c = jnp.concatenate([fourier_mix(f_c), retention_output(ret_c, rg_c, gn_w), context_attention(to_heads(aq_c, ATT_HEADS), ak_c, av_c, sink)], axis=-1)
    return out_lat, y_c @ w_o


def setup_inputs(seed: int = 0) -> dict:
    key = jax.random.key(seed)
    ks = jax.random.split(key, 16)
    nrm = jax.random.normal
    f32 = jnp.float32
    x = nrm(ks[0], (BATCH, SEQ, D_MODEL), f32)
    c = nrm(ks[1], (BATCH, D_MODEL), f32)
    ctx = nrm(ks[2], (BATCH, CTX_LEN, D_MODEL), f32)
    c_ctx = nrm(ks[3], (D_MODEL,), f32)
    norm_w = 1.0 + 0.05 * nrm(ks[4], (DEPTH, 3, D_MODEL), f32)
    w_ada = nrm(ks[5], (DEPTH, D_MODEL, N_MOD * D_MODEL), f32) * (0.5 * D_MODEL ** -0.5)
    b_ada = 0.02 * nrm(ks[6], (DEPTH, N_MOD * D_MODEL), f32)
    ffn_w_in = nrm(ks[7], (DEPTH, 2, D_MODEL, 2 * D_FF), f32) * (D_MODEL ** -0.5)
    ffn_w_out = nrm(ks[8], (DEPTH, 2, D_FF, D_MODEL), f32) * (D_FF ** -0.5)
    w_in = nrm(ks[9], (DEPTH, D_MODEL, D_IN), f32) * (D_MODEL ** -0.5)
    w_o = nrm(ks[10], (DEPTH, D_MIX, D_MODEL), f32) * (D_MIX ** -0.5)
    base_logit = jnp.asarray(np.log(2.0 ** (5.0 + np.arange(RET_HEADS)) - 1.0).astype(np.float32))
    ret_decay = base_logit[None, None, :] + 0.1 * nrm(ks[11], (DEPTH, 2, RET_HEADS), f32)
    ret_gn_w = 1.0 + 0.05 * nrm(ks[12], (DEPTH, RET_WIDTH), f32)
    attn_sink = 0.5 * nrm(ks[13], (DEPTH, ATT_HEADS), f32)
    final_norm_w = 1.0 + 0.05 * nrm(ks[14], (D_MODEL,), f32)
    return {'x': x, 'c': c, 'ctx': ctx, 'c_ctx': c_ctx, 'norm_w': norm_w, 'w_ada': w_ada, 'b_ada': b_ada,
            'ffn_w_in': ffn_w_in, 'ffn_w_out': ffn_w_out, 'w_in': w_in, 'w_o': w_o, 'ret_decay': ret_decay,
            'ret_gn_w': ret_gn_w, 'attn_sink': attn_sink, 'final_norm_w': final_norm_w}


def reference(x, c, ctx, c_ctx, norm_w, w_ada, b_ada, ffn_w_in, ffn_w_out, w_in, w_o, ret_decay, ret_gn_w, attn_sink, final_norm_w):
    rope_cos, rope_sin = axial_rope_angles(x.shape[1])
    cond_lat = jax.nn.silu(c)
    cond_ctx = jax.nn.silu(c_ctx)
    h = x
    hc = ctx
    for layer in range(DEPTH):
        need_ctx_out = layer < DEPTH - 1
        mod = (cond_lat @ w_ada[layer] + b_ada[layer]).reshape(x.shape[0], N_MOD, 1, D_MODEL)
        mod_c = (cond_ctx @ w_ada[layer] + b_ada[layer]).reshape(N_MOD, 1, D_MODEL)
        lm = [mod[:, i] for i in range(N_MOD)]
        cm = [mod_c[i] for i in range(N_MOD)]
        h = h + 0.5 * lm[2] * swiglu(modulate(rms_norm(h, norm_w[layer, 0]), lm[0], lm[1]), ffn_w_in[layer, 0], ffn_w_out[layer, 0])
        hc = hc + 0.5 * cm[2] * swiglu(modulate(rms_norm(hc, norm_w[layer, 0]), cm[0], cm[1]), ffn_w_in[layer, 0], ffn_w_out[layer, 0])
        y, yc = token_mix(modulate(rms_norm(h, norm_w[layer, 1]), lm[3], lm[4]),
                          modulate(rms_norm(hc, norm_w[layer, 1]), cm[3], cm[4]),
                          w_in[layer], w_o[layer], jax.nn.log_sigmoid(ret_decay[layer].astype(jnp.float32)),
                          ret_gn_w[layer], attn_sink[layer], rope_cos, rope_sin, need_ctx_out)
        h = h + lm[5] * y
        h = h + 0.5 * lm[8] * swiglu(modulate(rms_norm(h, norm_w[layer, 2]), lm[6], lm[7]), ffn_w_in[layer, 1], ffn_w_out[layer, 1])
        if need_ctx_out:
            hc = hc + cm[5] * yc
            hc = hc + 0.5 * cm[8] * swiglu(modulate(rms_norm(hc, norm_w[layer, 2]), cm[6], cm[7]), ffn_w_in[layer, 1], ffn_w_out[layer, 1])
    return rms_norm(h, final_norm_w)
```

```python
import functools
import math

import numpy as np
import jax
import jax.numpy as jnp
from jax import lax
from jax.experimental import pallas as pl
from jax.experimental.pallas import tpu as pltpu

F32 = jnp.float32
BF16 = jnp.bfloat16

D = 1024
B = 2
S = 8192
DEPTH = 2
GRID_W = 64
CTX = 256
HD = 64
N_MOD = 9
D_FF = 2816
D_IN = 2048
EPS = 1e-6
NEG_INF = -1e30
ROPE_BASE = 10000.0
ROPE_PAIRS = 16
RET_HEADS = 4
ATT_HEADS = 8
ATT_KV = 2
ATT_GROUP = ATT_HEADS // ATT_KV
WINDOW = 128

R_LAT = B * S
R_CTX = B * CTX
R_ALL = R_LAT + R_CTX
TM = 512
NT_LAT = R_LAT // TM
NT_ALL = R_ALL // TM
CH = 128
NCH_LAT = S // CH
NCH_CTX = CTX // CH
FF_CHUNK = 1408
FFT_L1 = 64
FFT_L2 = 128
VMEM_LIMIT = 56 * 1024 * 1024


def _params(sem, vmem=VMEM_LIMIT):
    return pltpu.CompilerParams(dimension_semantics=sem, vmem_limit_bytes=vmem)


def _const_spec(shape, index=None):
    if index is None:
        index = (0,) * len(shape)
    return pl.BlockSpec(shape, lambda *_: index, pipeline_mode=pl.Buffered(1))


def _rope_tables():
    rows = S // GRID_W
    row = np.repeat(np.arange(rows, dtype=np.float64), GRID_W)
    col = np.tile(np.arange(GRID_W, dtype=np.float64), rows)
    inv = ROPE_BASE ** (-np.arange(ROPE_PAIRS, dtype=np.float64) / ROPE_PAIRS)
    ar = row[:, None] * inv
    ac = col[:, None] * inv
    cos64 = np.concatenate([np.cos(ar), np.cos(ar), np.cos(ac), np.cos(ac)], axis=1)
    sin64 = np.concatenate([-np.sin(ar), np.sin(ar), -np.sin(ac), np.sin(ac)], axis=1)
    cos = np.concatenate([cos64, cos64], axis=1)
    sin = np.concatenate([sin64, sin64], axis=1)
    cos = np.concatenate([cos, np.ones((TM, 128))], axis=0)
    sin = np.concatenate([sin, np.zeros((TM, 128))], axis=0)
    return jnp.asarray(cos, F32), jnp.asarray(sin, F32)


def _dft_cs(n):
    k = np.arange(n)
    ang = 2.0 * np.pi * ((k[:, None] * k[None, :]) % n) / n
    return np.cos(ang), np.sin(ang)


def _fourier_tables():
    c64, s64 = _dft_cs(64)
    eye4 = np.eye(4)
    chan = np.concatenate([np.kron(eye4, c64), -np.kron(eye4, s64)], axis=1)
    w1 = np.block([[c64, s64], [-s64, c64]])
    p1 = np.arange(FFT_L1)[:, None]
    l2 = np.arange(FFT_L2)[None, :]
    ang = 2.0 * np.pi * ((p1 * l2) % S) / S
    twr = np.repeat(np.cos(ang)[:, :, None], 128, axis=2)
    twi = np.repeat(np.sin(ang)[:, :, None], 128, axis=2)
    c128, s128 = _dft_cs(128)
    w2 = np.concatenate([c128, s128], axis=1) / math.sqrt(S * 64.0)
    c256, s256 = _dft_cs(CTX)
    wc = np.concatenate([c256, s256], axis=1) / math.sqrt(CTX * 64.0)
    f = lambda a: jnp.asarray(a, F32)
    return (f(chan).astype(BF16), f(w1).astype(BF16), f(twr), f(twi), f(w2).astype(BF16), f(wc).astype(BF16))


def _mod_slice(m, i):
    return m[:, i * D:(i + 1) * D]


def _rms_mod(h, nw, shift, scale):
    ms = jnp.mean(h * h, axis=-1, keepdims=True)
    y = h * lax.rsqrt(ms + EPS) * nw
    return y * (1.0 + scale) + shift


def _swiglu(u, w_in_ref, w_out_ref):
    y = None
    for c in range(D_FF // FF_CHUNK):
        lo = c * FF_CHUNK
        g = jnp.dot(u, w_in_ref[:, lo:lo + FF_CHUNK], preferred_element_type=F32)
        up = jnp.dot(u, w_in_ref[:, D_FF + lo:D_FF + lo + FF_CHUNK], preferred_element_type=F32)
        a = (g * jax.nn.sigmoid(g) * up).astype(BF16)
        t = jnp.dot(a, w_out_ref[lo:lo + FF_CHUNK, :], preferred_element_type=F32)
        y = t if y is None else y + t
    return y


def _ffn_half_step(h, m, nw, mod0, w_in_ref, w_out_ref):
    u = _rms_mod(h, nw, _mod_slice(m, mod0), _mod_slice(m, mod0 + 1)).astype(BF16)
    return h + 0.5 * _mod_slice(m, mod0 + 2) * _swiglu(u, w_in_ref, w_out_ref)


def _log_sigmoid(x):
    return jnp.minimum(x, 0.0) - jnp.log1p(jnp.exp(-jnp.abs(x)))


def _lane_head(width):
    return lax.broadcasted_iota(jnp.int32, (1, width), 1) // HD


def _lane_swap64(x):
    return pltpu.roll(x.astype(F32), 64, axis=1).astype(BF16)


ADA_CB = 1152


def _ada_body(ct_ref, w_ref, b_ref, o_ref):
    ct = ct_ref[...]
    s = ct * jax.nn.sigmoid(ct)
    w = w_ref[0]
    rows = [jnp.sum(w * s[:, r:r + 1], axis=0, keepdims=True) for r in range(3)]
    rows.append(jnp.zeros((5, ADA_CB), F32))
    o_ref[0] = jnp.concatenate(rows, axis=0) + b_ref[0]


def _ada(cond_t, w_ada, b_ada):
    nb = (N_MOD * D) // ADA_CB
    return pl.pallas_call(
        _ada_body,
        grid=(DEPTH, nb),
        in_specs=[
            pl.BlockSpec((D, 8), lambda l, j: (0, 0)),
            pl.BlockSpec((1, D, ADA_CB), lambda l, j: (l, 0, j)),
            pl.BlockSpec((1, 1, ADA_CB), lambda l, j: (l, 0, j)),
        ],
        out_specs=pl.BlockSpec((1, 8, ADA_CB), lambda l, j: (l, 0, j)),
        out_shape=jax.ShapeDtypeStruct((DEPTH, 8, N_MOD * D), F32),
        compiler_params=_params(("arbitrary", "arbitrary")),
        name="ada_mod",
    )(cond_t, w_ada, b_ada.reshape(DEPTH, 1, N_MOD * D))


def _mod_spec(layer):
    return pl.BlockSpec((None, None, 1, N_MOD * D),
                        lambda i: (layer, jnp.minimum(i // (NT_LAT // B), B), 0, 0))


def _norm_spec(layer, k):
    return _const_spec((None, None, 1, D), (layer, k, 0, 0))


def _ffn_w_specs(layer, j):
    return [_const_spec((None, None, D, 2 * D_FF), (layer, j, 0, 0)),
            _const_spec((None, None, D_FF, D), (layer, j, 0, 0))]


def _row_spec(width):
    return pl.BlockSpec((TM, width), lambda i: (i, 0))


def _lat_or_ctx_specs(width):
    return [pl.BlockSpec((TM, width), lambda i: (jnp.minimum(i, NT_LAT - 1), 0)),
            pl.BlockSpec((TM, width), lambda i: (0, 0))]


def _pick(lat_ref, ctx_ref):
    return jnp.where(pl.program_id(0) < NT_LAT, lat_ref[...], ctx_ref[...])


def _ffn_first_body(x_ref, c_ref, m_ref, nw_ref, w_in_ref, w_out_ref, o_ref):
    o_ref[...] = _ffn_half_step(_pick(x_ref, c_ref), m_ref[...], nw_ref[...], 0, w_in_ref, w_out_ref)


def _ffn_first(x2d, c2d, mod_all, norm_w, ffn_in, ffn_out, layer):
    return pl.pallas_call(
        _ffn_first_body,
        grid=(NT_ALL,),
        in_specs=_lat_or_ctx_specs(D) + [_mod_spec(layer), _norm_spec(layer, 0)] + _ffn_w_specs(layer, 0),
        out_specs=_row_spec(D),
        out_shape=jax.ShapeDtypeStruct((R_ALL, D), F32),
        compiler_params=_params(("arbitrary",)),
        name="ffn_first",
    )(x2d, c2d, mod_all, norm_w, ffn_in, ffn_out)


def _ffn_body(h_ref, m_ref, nw_ref, w_in_ref, w_out_ref, o_ref):
    o_ref[...] = _ffn_half_step(h_ref[...], m_ref[...], nw_ref[...], 0, w_in_ref, w_out_ref)


def _ffn(h, mod_all, norm_w, ffn_in, ffn_out, layer):
    return pl.pallas_call(
        _ffn_body,
        grid=(NT_ALL,),
        in_specs=[_row_spec(D), _mod_spec(layer), _norm_spec(layer, 0)] + _ffn_w_specs(layer, 0),
        out_specs=_row_spec(D),
        out_shape=jax.ShapeDtypeStruct((R_ALL, D), F32),
        compiler_params=_params(("arbitrary",)),
        name="ffn_pre",
    )(h, mod_all, norm_w, ffn_in, ffn_out)


def _rope(x, cos, sin, lane_first_half):
    outs = []
    for j in range(x.shape[1] // 128):
        xb = x[:, j * 128:(j + 1) * 128]
        fwd = pltpu.roll(xb, 112, axis=1)
        bwd = pltpu.roll(xb, 16, axis=1)
        partner = jnp.where(lane_first_half, fwd, bwd)
        outs.append(xb * cos + partner * sin)
    return outs[0] if len(outs) == 1 else jnp.concatenate(outs, axis=1)


def _inproj_body(h_ref, m_ref, nw_ref, w_ref, cos_ref, sin_ref, chan_ref,
                 gr_ref, gi_ref, rq_ref, rk_ref, rv_ref, rg_ref, aq_ref, ak_ref, av_ref):
    m = m_ref[...]
    u = _rms_mod(h_ref[...], nw_ref[...], _mod_slice(m, 3), _mod_slice(m, 4)).astype(BF16)
    z = jnp.dot(u, w_ref[...], preferred_element_type=F32)
    cos = cos_ref[...]
    sin = sin_ref[...]
    lane = lax.broadcasted_iota(jnp.int32, (1, 128), 1)
    first_half = (lane // ROPE_PAIRS) % 2 == 0
    g = jnp.dot(z[:, 0:256].astype(BF16), chan_ref[...], preferred_element_type=F32)
    gr_ref[...] = g[:, 0:256].astype(BF16)
    gi_ref[...] = g[:, 256:512].astype(BF16)
    rq_ref[...] = _rope(z[:, 256:512], cos, sin, first_half).astype(BF16)
    rk_ref[...] = (_rope(z[:, 512:768], cos, sin, first_half) * (HD ** -0.5)).astype(BF16)
    rv_ref[...] = z[:, 768:1024].astype(BF16)
    rg_ref[...] = z[:, 1024:1280].astype(BF16)
    aq_ref[...] = (_rope(z[:, 1280:1792], cos, sin, first_half) * (HD ** -0.5)).astype(BF16)
    ak_ref[...] = _rope(z[:, 1792:1920], cos, sin, first_half).astype(BF16)
    av_ref[...] = z[:, 1920:2048].astype(BF16)


def _rope_block(i):
    return jnp.where(i < NT_LAT, i % (S // TM), S // TM)


def _inproj(h, mod_all, norm_w, w_in, cos_t, sin_t, chan, layer):
    widths = (256, 256, 256, 256, 256, 256, 512, 128, 128)
    return pl.pallas_call(
        _inproj_body,
        grid=(NT_ALL,),
        in_specs=[
            _row_spec(D),
            _mod_spec(layer),
            _norm_spec(layer, 1),
            _const_spec((None, D, D_IN), (layer, 0, 0)),
            pl.BlockSpec((TM, 128), lambda i: (_rope_block(i), 0)),
            pl.BlockSpec((TM, 128), lambda i: (_rope_block(i), 0)),
            _const_spec((256, 512)),
        ],
        out_specs=[_row_spec(w) for w in widths],
        out_shape=[jax.ShapeDtypeStruct((R_ALL, w), BF16) for w in widths],
        compiler_params=_params(("arbitrary",)),
        name="in_proj",
    )(h, mod_all, norm_w, w_in, cos_t, sin_t, chan)


FFT_NB = 4096
FFT_P1 = 8


def _fft1_body(w1_ref, xr_ref, xi_ref, z_ref):
    x = jnp.concatenate([xr_ref[...], xi_ref[...]], axis=0)
    z = jnp.dot(w1_ref[...], x, preferred_element_type=F32)
    z_ref[0, 0] = z[0:FFT_L1]
    z_ref[0, 1] = z[FFT_L1:2 * FFT_L1]


def _fft1(w1, gr, gi):
    lanes = FFT_L2 * 256
    grv = gr.reshape(R_ALL * 256 // lanes, lanes)
    giv = gi.reshape(R_ALL * 256 // lanes, lanes)
    return pl.pallas_call(
        _fft1_body,
        grid=(B, lanes // FFT_NB),
        in_specs=[
            _const_spec((128, 128)),
            pl.BlockSpec((FFT_L1, FFT_NB), lambda b, j: (b, j)),
            pl.BlockSpec((FFT_L1, FFT_NB), lambda b, j: (b, j)),
        ],
        out_specs=pl.BlockSpec((1, 2, FFT_L1, FFT_NB), lambda b, j: (b, 0, 0, j)),
        out_shape=jax.ShapeDtypeStruct((B, 2, FFT_L1, lanes), F32),
        compiler_params=_params(("arbitrary", "arbitrary")),
        name="fft_stage1",
    )(w1, grv, giv)


def _fft2_body(w2_ref, twr_ref, twi_ref, z_ref, o_ref):
    w2 = w2_ref[...]
    for p in range(FFT_P1):
        zr = z_ref[0, 0, p]
        zi = z_ref[0, 1, p]
        tr = jnp.concatenate([twr_ref[p]] * 2, axis=1)
        ti = jnp.concatenate([twi_ref[p]] * 2, axis=1)
        yr = tr * zr + ti * zi
        yi = tr * zi - ti * zr
        y = jnp.concatenate([yr, yi], axis=0).astype(BF16)
        o_ref[0, :, p * 256:(p + 1) * 256] = jnp.dot(w2, y, preferred_element_type=F32).astype(BF16)


def _fft2(w2, twr, twi, z):
    zv = z.reshape(B, 2, FFT_L1, FFT_L2, 256)
    out = pl.pallas_call(
        _fft2_body,
        grid=(B, FFT_L1 // FFT_P1),
        in_specs=[
            _const_spec((128, 256)),
            pl.BlockSpec((FFT_P1, FFT_L2, 128), lambda b, j: (j, 0, 0)),
            pl.BlockSpec((FFT_P1, FFT_L2, 128), lambda b, j: (j, 0, 0)),
            pl.BlockSpec((1, 2, FFT_P1, FFT_L2, 256), lambda b, j: (b, 0, j, 0, 0)),
        ],
        out_specs=pl.BlockSpec((1, FFT_L2, FFT_P1 * 256), lambda b, j: (b, 0, j)),
        out_shape=jax.ShapeDtypeStruct((B, FFT_L2, FFT_L1 * 256), BF16),
        compiler_params=_params(("arbitrary", "arbitrary")),
        name="fft_stage2",
    )(w2, twr, twi, zv)
    return out.reshape(R_LAT, 256)


def _fft_ctx_body(wc_ref, gr_ref, gi_ref, o_ref):
    x = jnp.concatenate([gr_ref[...], gi_ref[...]], axis=0)
    o_ref[...] = jnp.dot(wc_ref[...], x, preferred_element_type=F32).astype(BF16)


def _fft_ctx(wc, gr, gi):
    blk0 = R_LAT // CTX
    return pl.pallas_call(
        _fft_ctx_body,
        grid=(B,),
        in_specs=[
            _const_spec((CTX, 2 * CTX)),
            pl.BlockSpec((CTX, 256), lambda b: (blk0 + b, 0)),
            pl.BlockSpec((CTX, 256), lambda b: (blk0 + b, 0)),
        ],
        out_specs=pl.BlockSpec((CTX, 256), lambda b: (b, 0)),
        out_shape=jax.ShapeDtypeStruct((R_CTX, 256), BF16),
        compiler_params=_params(("arbitrary",)),
        name="fft_ctx",
    )(wc, gr, gi)


RS_CH = 8
RS_STEPS = NCH_LAT // RS_CH
N_STATE = (RS_STEPS + 1) * RS_CH
RO_CH = 4
CTX_ROWBLK = R_LAT // CTX


def _head_block_mask():
    r = lax.broadcasted_iota(jnp.int32, (256, 256), 0) // HD
    c = lax.broadcasted_iota(jnp.int32, (256, 256), 1) // HD
    return r == c


def _ret_state_body(dec_ref, kf_ref, vf_ref, kb_ref, vb_ref, kx_ref, vx_ref, sf_ref, sb_ref, accf, accb):
    t = pl.program_id(1)
    lg = _log_sigmoid(dec_ref[...])
    lgf = lg[0:1]
    lgb = lg[1:2]
    pos = lax.broadcasted_iota(jnp.int32, (CH, 1), 0).astype(F32)
    wf = jnp.exp((CH - 1.0 - pos) * lgf)
    wb = jnp.exp(pos * lgb)
    cdf = jnp.exp(float(CH) * lgf)
    cdb = jnp.exp(float(CH) * lgb)
    mask = _head_block_mask()
    dn = (((0,), (0,)), ((), ()))

    def step(acc, out_ref, slot, k, v, w, cd):
        out_ref[0, slot] = acc[...].astype(BF16)
        kd = (k.astype(F32) * w).astype(BF16)
        kv = lax.dot_general(kd, v, dn, preferred_element_type=F32)
        acc[...] = acc[...] * cd + jnp.where(mask, kv, 0.0)

    @pl.when(t == 0)
    def _():
        accf[...] = jnp.zeros_like(accf)
        accb[...] = jnp.zeros_like(accb)
        for j in range(NCH_CTX, RS_CH):
            sf_ref[0, j] = jnp.zeros((256, 256), BF16)
            sb_ref[0, j] = jnp.zeros((256, 256), BF16)
        for j in range(NCH_CTX):
            rows = slice(j * CH, (j + 1) * CH)
            step(accf, sf_ref, j, kx_ref[rows, :], vx_ref[rows, :], wf, cdf)
        for j in reversed(range(NCH_CTX)):
            rows = slice(j * CH, (j + 1) * CH)
            step(accb, sb_ref, j, kx_ref[rows, :], vx_ref[rows, :], wb, cdb)

    @pl.when(t > 0)
    def _():
        for j in range(RS_CH):
            rows = slice(j * CH, (j + 1) * CH)
            step(accf, sf_ref, j, kf_ref[rows, :], vf_ref[rows, :], wf, cdf)
        for j in reversed(range(RS_CH)):
            rows = slice(j * CH, (j + 1) * CH)
            step(accb, sb_ref, j, kb_ref[rows, :], vb_ref[rows, :], wb, cdb)


def _ret_state(dec, rk, rv, layer):
    rows = RS_CH * CH
    fblk = lambda t: jnp.maximum(t - 1, 0)
    bblk = lambda t: RS_STEPS - jnp.maximum(t, 1)
    fmap = lambda b, t: (b * RS_STEPS + fblk(t), 0)
    bmap = lambda b, t: (b * RS_STEPS + bblk(t), 0)
    xmap = lambda b, t: (CTX_ROWBLK + b, 0)
    return pl.pallas_call(
        _ret_state_body,
        grid=(B, RS_STEPS + 1),
        in_specs=[
            pl.BlockSpec((None, 2, 256), lambda b, t: (layer, 0, 0)),
            pl.BlockSpec((rows, 256), fmap),
            pl.BlockSpec((rows, 256), fmap),
            pl.BlockSpec((rows, 256), bmap),
            pl.BlockSpec((rows, 256), bmap),
            pl.BlockSpec((CTX, 256), xmap),
            pl.BlockSpec((CTX, 256), xmap),
        ],
        out_specs=[
            pl.BlockSpec((1, RS_CH, 256, 256), lambda b, t: (b, jnp.where(t == 0, RS_STEPS, t - 1), 0, 0)),
            pl.BlockSpec((1, RS_CH, 256, 256), lambda b, t: (b, jnp.where(t == 0, RS_STEPS, RS_STEPS - t), 0, 0)),
        ],
        out_shape=[jax.ShapeDtypeStruct((B, N_STATE, 256, 256), BF16)] * 2,
        scratch_shapes=[pltpu.VMEM((256, 256), F32), pltpu.VMEM((256, 256), F32)],
        compiler_params=_params(("arbitrary", "arbitrary")),
        name="ret_state",
    )(dec, rk, rv, rk, rv, rk, rv)


def _split_dot(x, w):
    hi = x.astype(BF16)
    lo = (x - hi.astype(F32)).astype(BF16)
    return jnp.dot(hi, w, preferred_element_type=F32) + jnp.dot(lo, w, preferred_element_type=F32)


def _ret_out_body(nck, dec_ref, gn_ref, q_ref, k_ref, v_ref, g_ref, sf_ref, sb_ref, o_ref):
    lg = _log_sigmoid(dec_ref[...])
    lgf = lg[0:1]
    lgb = lg[1:2]
    pos = lax.broadcasted_iota(jnp.int32, (CH, 1), 0).astype(F32)
    wqf = jnp.exp((pos + 1.0) * lgf)
    wqb = jnp.exp((float(CH) - pos) * lgb)
    ri = lax.broadcasted_iota(jnp.int32, (CH, CH), 0)
    ci = lax.broadcasted_iota(jnp.int32, (CH, CH), 1)
    diff = (ri - ci).astype(F32)
    decays = []
    for h in range(RET_HEADS):
        lgf_h = lgf[:, h * HD:h * HD + 1]
        lgb_h = lgb[:, h * HD:h * HD + 1]
        decays.append(jnp.where(ri >= ci, jnp.exp(jnp.maximum(diff, 0.0) * lgf_h), 0.0)
                      + jnp.where(ci >= ri, jnp.exp(jnp.maximum(-diff, 0.0) * lgb_h), 0.0))
    decay = jnp.concatenate(decays, axis=0)
    lane_head = _lane_head(256)
    avg = jnp.where(_head_block_mask(), 1.0 / HD, 0.0).astype(BF16)
    gn = gn_ref[...]
    dn = (((1,), (1,)), ((), ()))
    for c in range(nck):
        rows = slice(c * CH, (c + 1) * CH)
        qb16 = q_ref[rows, :]
        q = qb16.astype(F32)
        o = (jnp.dot((q * wqf).astype(BF16), sf_ref[0, c], preferred_element_type=F32)
             + jnp.dot((q * wqb).astype(BF16), sb_ref[0, c], preferred_element_type=F32))
        qs = jnp.concatenate([jnp.where(lane_head == h, qb16, jnp.zeros_like(qb16))
                              for h in range(RET_HEADS)], axis=0)
        sc = lax.dot_general(qs, k_ref[rows, :], dn, preferred_element_type=F32)
        p = (sc * decay).astype(BF16)
        of = jnp.dot(p, v_ref[rows, :], preferred_element_type=F32)
        for h in range(RET_HEADS):
            o = o + jnp.where(lane_head == h, of[h * CH:(h + 1) * CH], 0.0)
        mu = _split_dot(o, avg)
        cen = o - mu
        var = _split_dot(cen * cen, avg)
        on = cen * lax.rsqrt(var + EPS) * gn
        g = g_ref[rows, :].astype(F32)
        o_ref[rows, :] = (g * jax.nn.sigmoid(g) * on).astype(BF16)


def _ret_out(dec, gn, rq, rk, rv, rg, sf, sb, layer, ctx):
    nck = NCH_CTX if ctx else RO_CH
    rows = nck * CH
    if ctx:
        grid = (B, 1)
        rmap = lambda b, u: (CTX_ROWBLK + b, 0)
        omap = lambda b, u: (b, 0)
        smap = lambda b, u: (b, NCH_LAT // NCH_CTX, 0, 0)
        out_rows = R_CTX
    else:
        grid = (B, NCH_LAT // RO_CH)
        rmap = lambda b, u: (b * (NCH_LAT // RO_CH) + u, 0)
        omap = rmap
        smap = lambda b, u: (b, u, 0, 0)
        out_rows = R_LAT
    return pl.pallas_call(
        functools.partial(_ret_out_body, nck),
        grid=grid,
        in_specs=[
            pl.BlockSpec((None, 2, 256), lambda b, u: (layer, 0, 0)),
            pl.BlockSpec((None, 1, 256), lambda b, u: (layer, 0, 0)),
            pl.BlockSpec((rows, 256), rmap),
            pl.BlockSpec((rows, 256), rmap),
            pl.BlockSpec((rows, 256), rmap),
            pl.BlockSpec((rows, 256), rmap),
            pl.BlockSpec((1, nck, 256, 256), smap),
            pl.BlockSpec((1, nck, 256, 256), smap),
        ],
        out_specs=pl.BlockSpec((rows, 256), omap),
        out_shape=jax.ShapeDtypeStruct((out_rows, 256), BF16),
        compiler_params=_params(("arbitrary", "arbitrary")),
        name="ret_out_ctx" if ctx else "ret_out",
    )(dec, gn, rq, rk, rv, rg, sf, sb)


def _tile_head(x_all, hk):
    own = jnp.where(_lane_head(2 * HD) == hk, x_all, jnp.zeros_like(x_all))
    both = own + _lane_swap64(own)
    return jnp.concatenate([both, both], axis=1)


def _gqa_group(q_grp, k_all, v_all, hk, sinks, mask_fn):
    m = q_grp.shape[0]
    lane_head = _lane_head(256)
    qs = jnp.concatenate([jnp.where(lane_head == g, q_grp, jnp.zeros_like(q_grp))
                          for g in range(ATT_GROUP)], axis=0)
    sink = jnp.concatenate([jnp.full((m, 1), sinks[g], F32) for g in range(ATT_GROUP)], axis=0)
    kt = _tile_head(k_all, hk)
    vt = _tile_head(v_all, hk)
    s = lax.dot_general(qs, kt, (((1,), (1,)), ((), ())), preferred_element_type=F32)
    s = mask_fn(s)
    mx = jnp.maximum(jnp.max(s, axis=1, keepdims=True), sink)
    p = jnp.exp(s - mx)
    den = jnp.sum(p, axis=1, keepdims=True) + jnp.exp(sink - mx)
    of = jnp.dot(p.astype(BF16), vt, preferred_element_type=F32) * (1.0 / den)
    out = None
    for g in range(ATT_GROUP):
        t = jnp.where(lane_head == g, of[g * m:(g + 1) * m], 0.0)
        out = t if out is None else out + t
    return out


def _attn_lat_body(layer, sink_ref, q_ref, kp_ref, kc_ref, kn_ref, vp_ref, vc_ref, vn_ref, kx_ref, vx_ref, o_ref):
    n = pl.program_id(1)
    k_all = jnp.concatenate([kp_ref[...], kc_ref[...], kn_ref[...], kx_ref[...]], axis=0)
    v_all = jnp.concatenate([vp_ref[...], vc_ref[...], vn_ref[...], vx_ref[...]], axis=0)
    r = lax.broadcasted_iota(jnp.int32, (ATT_GROUP * CH, CH), 0) & (CH - 1)
    c = lax.broadcasted_iota(jnp.int32, (ATT_GROUP * CH, CH), 1)
    ok_prev = (c >= r) & (n > 0)
    ok_next = (c <= r) & (n < NCH_LAT - 1)

    def mask_fn(s):
        return jnp.concatenate([jnp.where(ok_prev, s[:, 0:CH], NEG_INF), s[:, CH:2 * CH],
                                jnp.where(ok_next, s[:, 2 * CH:3 * CH], NEG_INF), s[:, 3 * CH:]], axis=1)

    for hk in range(ATT_KV):
        sinks = [sink_ref[layer, hk * ATT_GROUP + g] for g in range(ATT_GROUP)]
        o = _gqa_group(q_ref[:, hk * 256:(hk + 1) * 256], k_all, v_all, hk, sinks, mask_fn)
        o_ref[:, hk * 256:(hk + 1) * 256] = o.astype(BF16)


def _attn_lat(sink, aq, ak, av, layer):
    qmap = lambda b, n: (b * NCH_LAT + n, 0)
    pmap = lambda b, n: (b * NCH_LAT + jnp.maximum(n - 1, 0), 0)
    nmap = lambda b, n: (b * NCH_LAT + jnp.minimum(n + 1, NCH_LAT - 1), 0)
    xmap = lambda b, n: (CTX_ROWBLK + b, 0)
    kv = lambda m: pl.BlockSpec((CH, 2 * HD), m)
    return pl.pallas_call(
        functools.partial(_attn_lat_body, layer),
        grid=(B, NCH_LAT),
        in_specs=[
            pl.BlockSpec(memory_space=pltpu.SMEM),
            pl.BlockSpec((CH, ATT_HEADS * HD), qmap),
            kv(pmap), kv(qmap), kv(nmap),
            kv(pmap), kv(qmap), kv(nmap),
            pl.BlockSpec((CTX, 2 * HD), xmap),
            pl.BlockSpec((CTX, 2 * HD), xmap),
        ],
        out_specs=pl.BlockSpec((CH, ATT_HEADS * HD), qmap),
        out_shape=jax.ShapeDtypeStruct((R_LAT, ATT_HEADS * HD), BF16),
        compiler_params=_params(("arbitrary", "arbitrary")),
        name="swa_attn",
    )(sink, aq, ak, ak, ak, av, av, av, ak, av)


def _attn_ctx_body(layer, sink_ref, q_ref, kx_ref, vx_ref, o_ref):
    for hk in range(ATT_KV):
        sinks = [sink_ref[layer, hk * ATT_GROUP + g] for g in range(ATT_GROUP)]
        o = _gqa_group(q_ref[:, hk * 256:(hk + 1) * 256], kx_ref[...], vx_ref[...], hk, sinks, lambda s: s)
        o_ref[:, hk * 256:(hk + 1) * 256] = o.astype(BF16)


def _attn_ctx(sink, aq, ak, av, layer):
    xmap = lambda b: (CTX_ROWBLK + b, 0)
    return pl.pallas_call(
        functools.partial(_attn_ctx_body, layer),
        grid=(B,),
        in_specs=[
            pl.BlockSpec(memory_space=pltpu.SMEM),
            pl.BlockSpec((CTX, ATT_HEADS * HD), xmap),
            pl.BlockSpec((CTX, 2 * HD), xmap),
            pl.BlockSpec((CTX, 2 * HD), xmap),
        ],
        out_specs=pl.BlockSpec((CTX, ATT_HEADS * HD), lambda b: (b, 0)),
        out_shape=jax.ShapeDtypeStruct((R_CTX, ATT_HEADS * HD), BF16),
        compiler_params=_params(("arbitrary",)),
        name="ctx_attn",
    )(sink, aq, ak, av)


def _mix_ffn_body(has_ctx, final, *refs):
    refs = list(refs)
    h_ref = refs.pop(0)
    mixers = []
    for _ in range(3):
        if has_ctx:
            lat_ref, ctx_ref = refs.pop(0), refs.pop(0)
            mixers.append(_pick(lat_ref, ctx_ref))
        else:
            mixers.append(refs.pop(0)[...])
    m_ref, nw_ref, wo_ref, w_in_ref, w_out_ref = refs[:5]
    rest = refs[5:]
    yf, yr, ya = mixers
    m = m_ref[...]
    y = (jnp.dot(yf, wo_ref[0:256, :], preferred_element_type=F32)
         + jnp.dot(yr, wo_ref[256:512, :], preferred_element_type=F32)
         + jnp.dot(ya, wo_ref[512:1024, :], preferred_element_type=F32))
    h = h_ref[...] + _mod_slice(m, 5) * y
    h = _ffn_half_step(h, m, nw_ref[...], 6, w_in_ref, w_out_ref)
    if final:
        fw_ref, o_ref = rest
        ms = jnp.mean(h * h, axis=-1, keepdims=True)
        h = h * lax.rsqrt(ms + EPS) * fw_ref[...]
    else:
        (o_ref,) = rest
    o_ref[...] = h


def _mix_ffn(h, mixers, mod_all, norm_w, wo, ffn_in, ffn_out, final_w, layer):
    has_ctx = mixers[0][1] is not None
    final = final_w is not None
    nt = NT_ALL if has_ctx else NT_LAT
    in_specs = [_row_spec(D)]
    args = [h]
    for (lat, ctx), width in zip(mixers, (256, 256, 512)):
        if has_ctx:
            in_specs += _lat_or_ctx_specs(width)
            args += [lat, ctx]
        else:
            in_specs += [_row_spec(width)]
            args += [lat]
    in_specs += [_mod_spec(layer), _norm_spec(layer, 2), _const_spec((None, D, D), (layer, 0, 0))]
    in_specs += _ffn_w_specs(layer, 1)
    args += [mod_all, norm_w, wo, ffn_in, ffn_out]
    if final:
        in_specs += [_const_spec((1, D))]
        args += [final_w]
    return pl.pallas_call(
        functools.partial(_mix_ffn_body, has_ctx, final),
        grid=(nt,),
        in_specs=in_specs,
        out_specs=_row_spec(D),
        out_shape=jax.ShapeDtypeStruct((nt * TM, D), F32),
        compiler_params=_params(("arbitrary",)),
        name="mix_ffn_final" if final else "mix_ffn",
    )(*args)


def kernel(x, c, ctx, c_ctx, norm_w, w_ada, b_ada, ffn_w_in, ffn_w_out, w_in, w_o, ret_decay, ret_gn_w,
           attn_sink, final_norm_w):
    assert x.shape == (B, S, D) and ctx.shape == (B, CTX, D)
    cos_t, sin_t = _rope_tables()
    chan, w1, twr, twi, w2, wc = _fourier_tables()

    cond_t = jnp.zeros((D, 8), F32).at[:, 0:B].set(c.T).at[:, B].set(c_ctx)
    mod_all = _ada(cond_t, w_ada, b_ada).reshape(DEPTH, 8, 1, N_MOD * D)

    ffn_in = ffn_w_in.astype(BF16)
    ffn_out = ffn_w_out.astype(BF16)
    w_in_b = w_in.astype(BF16)
    w_o_b = w_o.astype(BF16)
    norm4 = norm_w.reshape(DEPTH, 3, 1, D)
    gn3 = ret_gn_w.reshape(DEPTH, 1, 256)
    dec = jnp.repeat(ret_decay.astype(F32), HD, axis=2)
    sink = attn_sink.astype(F32)

    h = None
    out = None
    for layer in range(DEPTH):
        last = layer == DEPTH - 1
        if layer == 0:
            h = _ffn_first(x.reshape(R_LAT, D), ctx.reshape(R_CTX, D), mod_all, norm4, ffn_in, ffn_out, layer)
        else:
            h = _ffn(h, mod_all, norm4, ffn_in, ffn_out, layer)
        gr, gi, rq, rk, rv, rg, aq, ak, av = _inproj(h, mod_all, norm4, w_in_b, cos_t, sin_t, chan, layer)

        yf = _fft2(w2, twr, twi, _fft1(w1, gr, gi))
        sf, sb = _ret_state(dec, rk, rv, layer)
        yr = _ret_out(dec, gn3, rq, rk, rv, rg, sf, sb, layer, ctx=False)
        ya = _attn_lat(sink, aq, ak, av, layer)
        if last:
            yfc = yrc = yac = None
        else:
            yfc = _fft_ctx(wc, gr, gi)
            yrc = _ret_out(dec, gn3, rq, rk, rv, rg, sf, sb, layer, ctx=True)
            yac = _attn_ctx(sink, aq, ak, av, layer)

        res = _mix_ffn(h, [(yf, yfc), (yr, yrc), (ya, yac)], mod_all, norm4, w_o_b, ffn_in, ffn_out,
                       final_norm_w.reshape(1, D) if last else None, layer)
        if last:
            out = res
        else:
            h = res
    return out.reshape(B, S, D)
```

```python
import functools
import math

import numpy as np
import jax
import jax.numpy as jnp
from jax import lax
from jax.experimental import pallas as pl
from jax.experimental.pallas import tpu as pltpu

F32 = jnp.float32
BF16 = jnp.bfloat16

D = 1024
B = 2
S = 8192
DEPTH = 2
GRID_W = 64
CTX = 256
HD = 64
N_MOD = 9
D_FF = 2816
D_IN = 2048
EPS = 1e-6
NEG_INF = -1e30
ROPE_BASE = 10000.0
ROPE_PAIRS = 16
RET_HEADS = 4
ATT_HEADS = 8
ATT_KV = 2
ATT_GROUP = ATT_HEADS // ATT_KV
ATT_STACK = 2
WINDOW = 128

R_LAT = B * S
R_CTX = B * CTX
R_ALL = R_LAT + R_CTX
TM = 512
NT_LAT = R_LAT // TM
NT_ALL = R_ALL // TM
CH = 128
NCH_LAT = S // CH
NCH_CTX = CTX // CH
FF_CHUNK = 1408
FFT_L1 = 64
FFT_L2 = 128
VMEM_LIMIT = 56 * 1024 * 1024


def _params(sem, vmem=VMEM_LIMIT):
    return pltpu.CompilerParams(dimension_semantics=sem, vmem_limit_bytes=vmem)


def _const_spec(shape, index=None):
    if index is None:
        index = (0,) * len(shape)
    return pl.BlockSpec(shape, lambda *_: index, pipeline_mode=pl.Buffered(1))


def _rope_tables():
    rows = S // GRID_W
    row = np.repeat(np.arange(rows, dtype=np.float64), GRID_W)
    col = np.tile(np.arange(GRID_W, dtype=np.float64), rows)
    inv = ROPE_BASE ** (-np.arange(ROPE_PAIRS, dtype=np.float64) / ROPE_PAIRS)
    ar = row[:, None] * inv
    ac = col[:, None] * inv
    cos64 = np.concatenate([np.cos(ar), np.cos(ar), np.cos(ac), np.cos(ac)], axis=1)
    sin64 = np.concatenate([-np.sin(ar), np.sin(ar), -np.sin(ac), np.sin(ac)], axis=1)
    cos = np.concatenate([cos64, cos64], axis=1)
    sin = np.concatenate([sin64, sin64], axis=1)
    cos = np.concatenate([cos, np.ones((TM, 128))], axis=0)
    sin = np.concatenate([sin, np.zeros((TM, 128))], axis=0)
    return jnp.asarray(cos, F32), jnp.asarray(sin, F32)


def _dft_cs(n):
    k = np.arange(n)
    ang = 2.0 * np.pi * ((k[:, None] * k[None, :]) % n) / n
    return np.cos(ang), np.sin(ang)


def _fourier_tables():
    c64, s64 = _dft_cs(64)
    eye4 = np.eye(4)
    chan = np.concatenate([np.kron(eye4, c64), -np.kron(eye4, s64)], axis=1)
    w1 = np.block([[c64, s64], [-s64, c64]])
    p1 = np.arange(FFT_L1)[:, None]
    l2 = np.arange(FFT_L2)[None, :]
    ang = 2.0 * np.pi * ((p1 * l2) % S) / S
    twr = np.repeat(np.cos(ang)[:, :, None], 128, axis=2)
    twi = np.repeat(np.sin(ang)[:, :, None], 128, axis=2)
    c128, s128 = _dft_cs(128)
    w2 = np.concatenate([c128, s128], axis=1) / math.sqrt(S * 64.0)
    c256, s256 = _dft_cs(CTX)
    wc = np.concatenate([c256, s256], axis=1) / math.sqrt(CTX * 64.0)
    f = lambda a: jnp.asarray(a, F32)
    return (f(chan).astype(BF16), f(w1).astype(BF16), f(twr), f(twi), f(w2).astype(BF16), f(wc).astype(BF16))


def _mod_slice(m, i):
    return m[:, i * D:(i + 1) * D]


def _rms_mod(h, nw, shift, scale):
    ms = jnp.mean(h * h, axis=-1, keepdims=True)
    y = h * lax.rsqrt(ms + EPS) * nw
    return y * (1.0 + scale) + shift


def _swiglu(u, w_in_ref, w_out_ref):
    y = None
    for c in range(D_FF // FF_CHUNK):
        lo = c * FF_CHUNK
        g = jnp.dot(u, w_in_ref[:, lo:lo + FF_CHUNK], preferred_element_type=F32)
        up = jnp.dot(u, w_in_ref[:, D_FF + lo:D_FF + lo + FF_CHUNK], preferred_element_type=F32)
        a = (g * jax.nn.sigmoid(g) * up).astype(BF16)
        t = jnp.dot(a, w_out_ref[lo:lo + FF_CHUNK, :], preferred_element_type=F32)
        y = t if y is None else y + t
    return y


def _ffn_half_step(h, m, nw, mod0, w_in_ref, w_out_ref):
    u = _rms_mod(h, nw, _mod_slice(m, mod0), _mod_slice(m, mod0 + 1)).astype(BF16)
    return h + 0.5 * _mod_slice(m, mod0 + 2) * _swiglu(u, w_in_ref, w_out_ref)


def _log_sigmoid(x):
    return jnp.minimum(x, 0.0) - jnp.log1p(jnp.exp(-jnp.abs(x)))


def _lane_head(width):
    return lax.broadcasted_iota(jnp.int32, (1, width), 1) // HD


def _lane_swap64(x):
    return pltpu.roll(x.astype(F32), 64, axis=1).astype(BF16)


ADA_CB = 1152


def _ada_body(ct_ref, w_ref, b_ref, o_ref):
    ct = ct_ref[...]
    s = ct * jax.nn.sigmoid(ct)
    w = w_ref[0]
    rows = [jnp.sum(w * s[:, r:r + 1], axis=0, keepdims=True) for r in range(3)]
    rows.append(jnp.zeros((5, ADA_CB), F32))
    o_ref[0] = jnp.concatenate(rows, axis=0) + b_ref[0]


def _ada(cond_t, w_ada, b_ada):
    nb = (N_MOD * D) // ADA_CB
    return pl.pallas_call(
        _ada_body,
        grid=(DEPTH, nb),
        in_specs=[
            pl.BlockSpec((D, 8), lambda l, j: (0, 0)),
            pl.BlockSpec((1, D, ADA_CB), lambda l, j: (l, 0, j)),
            pl.BlockSpec((1, 1, ADA_CB), lambda l, j: (l, 0, j)),
        ],
        out_specs=pl.BlockSpec((1, 8, ADA_CB), lambda l, j: (l, 0, j)),
        out_shape=jax.ShapeDtypeStruct((DEPTH, 8, N_MOD * D), F32),
        compiler_params=_params(("arbitrary", "arbitrary")),
        name="ada_mod",
    )(cond_t, w_ada, b_ada.reshape(DEPTH, 1, N_MOD * D))


def _mod_spec(layer):
    return pl.BlockSpec((None, None, 1, N_MOD * D),
                        lambda i: (layer, jnp.minimum(i // (NT_LAT // B), B), 0, 0))


def _norm_spec(layer, k):
    return _const_spec((None, None, 1, D), (layer, k, 0, 0))


def _ffn_w_specs(layer, j):
    return [_const_spec((None, None, D, 2 * D_FF), (layer, j, 0, 0)),
            _const_spec((None, None, D_FF, D), (layer, j, 0, 0))]


def _row_spec(width):
    return pl.BlockSpec((TM, width), lambda i: (i, 0))


def _lat_or_ctx_specs(width):
    return [pl.BlockSpec((TM, width), lambda i: (jnp.minimum(i, NT_LAT - 1), 0)),
            pl.BlockSpec((TM, width), lambda i: (0, 0))]


def _pick(lat_ref, ctx_ref):
    return jnp.where(pl.program_id(0) < NT_LAT, lat_ref[...], ctx_ref[...])


def _ffn_first_body(x_ref, c_ref, m_ref, nw_ref, w_in_ref, w_out_ref, o_ref):
    o_ref[...] = _ffn_half_step(_pick(x_ref, c_ref), m_ref[...], nw_ref[...], 0, w_in_ref, w_out_ref)


def _ffn_first(x2d, c2d, mod_all, norm_w, ffn_in, ffn_out, layer):
    return pl.pallas_call(
        _ffn_first_body,
        grid=(NT_ALL,),
        in_specs=_lat_or_ctx_specs(D) + [_mod_spec(layer), _norm_spec(layer, 0)] + _ffn_w_specs(layer, 0),
        out_specs=_row_spec(D),
        out_shape=jax.ShapeDtypeStruct((R_ALL, D), F32),
        compiler_params=_params(("arbitrary",)),
        name="ffn_first",
    )(x2d, c2d, mod_all, norm_w, ffn_in, ffn_out)


def _ffn_body(h_ref, m_ref, nw_ref, w_in_ref, w_out_ref, o_ref):
    o_ref[...] = _ffn_half_step(h_ref[...], m_ref[...], nw_ref[...], 0, w_in_ref, w_out_ref)


def _ffn(h, mod_all, norm_w, ffn_in, ffn_out, layer):
    return pl.pallas_call(
        _ffn_body,
        grid=(NT_ALL,),
        in_specs=[_row_spec(D), _mod_spec(layer), _norm_spec(layer, 0)] + _ffn_w_specs(layer, 0),
        out_specs=_row_spec(D),
        out_shape=jax.ShapeDtypeStruct((R_ALL, D), F32),
        compiler_params=_params(("arbitrary",)),
        name="ffn_pre",
    )(h, mod_all, norm_w, ffn_in, ffn_out)


def _rope(x, cos, sin, lane_first_half):
    outs = []
    for j in range(x.shape[1] // 128):
        xb = x[:, j * 128:(j + 1) * 128]
        fwd = pltpu.roll(xb, 112, axis=1)
        bwd = pltpu.roll(xb, 16, axis=1)
        partner = jnp.where(lane_first_half, fwd, bwd)
        outs.append(xb * cos + partner * sin)
    return outs[0] if len(outs) == 1 else jnp.concatenate(outs, axis=1)


def _inproj_body(h_ref, m_ref, nw_ref, w_ref, cos_ref, sin_ref, chan_ref,
                 gr_ref, gi_ref, rq_ref, rk_ref, rv_ref, rg_ref, aq_ref, ak_ref, av_ref):
    m = m_ref[...]
    u = _rms_mod(h_ref[...], nw_ref[...], _mod_slice(m, 3), _mod_slice(m, 4)).astype(BF16)
    z = jnp.dot(u, w_ref[...], preferred_element_type=F32)
    cos = cos_ref[...]
    sin = sin_ref[...]
    lane = lax.broadcasted_iota(jnp.int32, (1, 128), 1)
    first_half = (lane // ROPE_PAIRS) % 2 == 0
    g = jnp.dot(z[:, 0:256].astype(BF16), chan_ref[...], preferred_element_type=F32)
    gr_ref[...] = g[:, 0:256]
    gi_ref[...] = g[:, 256:512]
    rq_ref[...] = _rope(z[:, 256:512], cos, sin, first_half).astype(BF16)
    rk_ref[...] = (_rope(z[:, 512:768], cos, sin, first_half) * (HD ** -0.5)).astype(BF16)
    rv_ref[...] = z[:, 768:1024].astype(BF16)
    rg_ref[...] = z[:, 1024:1280].astype(BF16)
    aq_ref[...] = (_rope(z[:, 1280:1792], cos, sin, first_half) * (HD ** -0.5)).astype(BF16)
    ak_ref[...] = _rope(z[:, 1792:1920], cos, sin, first_half).astype(BF16)
    av_ref[...] = z[:, 1920:2048].astype(BF16)


def _rope_block(i):
    return jnp.where(i < NT_LAT, i % (S // TM), S // TM)


def _inproj(h, mod_all, norm_w, w_in, cos_t, sin_t, chan, layer):
    widths = (256, 256, 256, 256, 256, 256, 512, 128, 128)
    return pl.pallas_call(
        _inproj_body,
        grid=(NT_ALL,),
        in_specs=[
            _row_spec(D),
            _mod_spec(layer),
            _norm_spec(layer, 1),
            _const_spec((None, D, D_IN), (layer, 0, 0)),
            pl.BlockSpec((TM, 128), lambda i: (_rope_block(i), 0)),
            pl.BlockSpec((TM, 128), lambda i: (_rope_block(i), 0)),
            _const_spec((256, 512)),
        ],
        out_specs=[_row_spec(w) for w in widths],
        out_shape=[jax.ShapeDtypeStruct((R_ALL, w), F32 if i < 2 else BF16) for i, w in enumerate(widths)],
        compiler_params=_params(("arbitrary",)),
        name="in_proj",
    )(h, mod_all, norm_w, w_in, cos_t, sin_t, chan)


FFT_SUB = 8


def _fft1_body(w1_ref, xr_ref, xi_ref, z_ref):
    w1 = w1_ref[...]
    for s in range(FFT_SUB):
        x = jnp.concatenate([xr_ref[:, s, :], xi_ref[:, s, :]], axis=0).astype(BF16)
        z = jnp.dot(w1, x, preferred_element_type=F32)
        z_ref[0, :, s, :] = z[0:FFT_L1]
        z_ref[1, :, s, :] = z[FFT_L1:2 * FFT_L1]


def _fft1(w1, gr, gi):
    view = (R_ALL // FFT_L2, FFT_L2 // FFT_SUB, FFT_SUB, 256)
    x_spec = pl.BlockSpec((FFT_L1, None, FFT_SUB, 256), lambda b, j: (b, j, 0, 0))
    return pl.pallas_call(
        _fft1_body,
        grid=(B, FFT_L2 // FFT_SUB),
        in_specs=[_const_spec((128, 128)), x_spec, x_spec],
        out_specs=pl.BlockSpec((None, 2, FFT_L1, FFT_SUB, 256), lambda b, j: (b, 0, 0, j, 0)),
        out_shape=jax.ShapeDtypeStruct((B, 2, FFT_L1, FFT_L2, 256), F32),
        compiler_params=_params(("arbitrary", "arbitrary")),
        name="fft_stage1",
    )(w1, gr.reshape(view), gi.reshape(view))


def _fft2_body(w2_ref, twr_ref, twi_ref, z_ref, o_ref):
    w2 = w2_ref[...]
    for p in range(FFT_SUB):
        zr = z_ref[0, p]
        zi = z_ref[1, p]
        tr = jnp.concatenate([twr_ref[p]] * 2, axis=1)
        ti = jnp.concatenate([twi_ref[p]] * 2, axis=1)
        yr = tr * zr + ti * zi
        yi = tr * zi - ti * zr
        y = jnp.concatenate([yr, yi], axis=0).astype(BF16)
        o_ref[:, p, :] = jnp.dot(w2, y, preferred_element_type=F32)


def _fft2(w2, twr, twi, z):
    out = pl.pallas_call(
        _fft2_body,
        grid=(B, FFT_L1 // FFT_SUB),
        in_specs=[
            _const_spec((128, 256)),
            pl.BlockSpec((FFT_SUB, FFT_L2, 128), lambda b, j: (j, 0, 0)),
            pl.BlockSpec((FFT_SUB, FFT_L2, 128), lambda b, j: (j, 0, 0)),
            pl.BlockSpec((None, 2, FFT_SUB, FFT_L2, 256), lambda b, j: (b, 0, j, 0, 0)),
        ],
        out_specs=pl.BlockSpec((None, FFT_L2, None, FFT_SUB, 256), lambda b, j: (b, 0, j, 0, 0)),
        out_shape=jax.ShapeDtypeStruct((B, FFT_L2, FFT_L1 // FFT_SUB, FFT_SUB, 256), F32),
        compiler_params=_params(("arbitrary", "arbitrary")),
        name="fft_stage2",
    )(w2, twr, twi, z)
    return out.reshape(R_LAT, 256)


def _fft_ctx_body(wc_ref, gr_ref, gi_ref, o_ref):
    x = jnp.concatenate([gr_ref[...], gi_ref[...]], axis=0).astype(BF16)
    o_ref[...] = jnp.dot(wc_ref[...], x, preferred_element_type=F32)


def _fft_ctx(wc, gr, gi):
    blk0 = R_LAT // CTX
    return pl.pallas_call(
        _fft_ctx_body,
        grid=(B,),
        in_specs=[
            _const_spec((CTX, 2 * CTX)),
            pl.BlockSpec((CTX, 256), lambda b: (blk0 + b, 0)),
            pl.BlockSpec((CTX, 256), lambda b: (blk0 + b, 0)),
        ],
        out_specs=pl.BlockSpec((CTX, 256), lambda b: (b, 0)),
        out_shape=jax.ShapeDtypeStruct((R_CTX, 256), F32),
        compiler_params=_params(("arbitrary",)),
        name="fft_ctx",
    )(wc, gr, gi)


RS_CH = 8
RS_STEPS = NCH_LAT // RS_CH
N_STATE = (RS_STEPS + 1) * RS_CH
RO_CH = 4
CTX_ROWBLK = R_LAT // CTX


def _head_block_mask():
    r = lax.broadcasted_iota(jnp.int32, (256, 256), 0) // HD
    c = lax.broadcasted_iota(jnp.int32, (256, 256), 1) // HD
    return r == c


def _ret_state_body(dec_ref, kf_ref, vf_ref, kb_ref, vb_ref, kx_ref, vx_ref, sf_ref, sb_ref, accf, accb):
    t = pl.program_id(1)
    lg = _log_sigmoid(dec_ref[...])
    lgf = lg[0:1]
    lgb = lg[1:2]
    pos = lax.broadcasted_iota(jnp.int32, (CH, 1), 0).astype(F32)
    wf = jnp.exp((CH - 1.0 - pos) * lgf)
    wb = jnp.exp(pos * lgb)
    cdf = jnp.exp(float(CH) * lgf)
    cdb = jnp.exp(float(CH) * lgb)
    mask = _head_block_mask()
    dn = (((0,), (0,)), ((), ()))

    def step(acc, out_ref, slot, k, v, w, cd):
        out_ref[0, slot] = acc[...].astype(BF16)
        kd = (k.astype(F32) * w).astype(BF16)
        kv = lax.dot_general(kd, v, dn, preferred_element_type=F32)
        acc[...] = acc[...] * cd + jnp.where(mask, kv, 0.0)

    @pl.when(t == 0)
    def _():
        accf[...] = jnp.zeros_like(accf)
        accb[...] = jnp.zeros_like(accb)
        for j in range(NCH_CTX, RS_CH):
            sf_ref[0, j] = jnp.zeros((256, 256), BF16)
            sb_ref[0, j] = jnp.zeros((256, 256), BF16)
        for j in range(NCH_CTX):
            rows = slice(j * CH, (j + 1) * CH)
            step(accf, sf_ref, j, kx_ref[rows, :], vx_ref[rows, :], wf, cdf)
        for j in reversed(range(NCH_CTX)):
            rows = slice(j * CH, (j + 1) * CH)
            step(accb, sb_ref, j, kx_ref[rows, :], vx_ref[rows, :], wb, cdb)

    @pl.when(t > 0)
    def _():
        for j in range(RS_CH):
            rows = slice(j * CH, (j + 1) * CH)
            step(accf, sf_ref, j, kf_ref[rows, :], vf_ref[rows, :], wf, cdf)
        for j in reversed(range(RS_CH)):
            rows = slice(j * CH, (j + 1) * CH)
            step(accb, sb_ref, j, kb_ref[rows, :], vb_ref[rows, :], wb, cdb)


def _ret_state(dec, rk, rv, layer):
    rows = RS_CH * CH
    fblk = lambda t: jnp.maximum(t - 1, 0)
    bblk = lambda t: RS_STEPS - jnp.maximum(t, 1)
    fmap = lambda b, t: (b * RS_STEPS + fblk(t), 0)
    bmap = lambda b, t: (b * RS_STEPS + bblk(t), 0)
    xmap = lambda b, t: (CTX_ROWBLK + b, 0)
    return pl.pallas_call(
        _ret_state_body,
        grid=(B, RS_STEPS + 1),
        in_specs=[
            pl.BlockSpec((None, 2, 256), lambda b, t: (layer, 0, 0)),
            pl.BlockSpec((rows, 256), fmap),
            pl.BlockSpec((rows, 256), fmap),
            pl.BlockSpec((rows, 256), bmap),
            pl.BlockSpec((rows, 256), bmap),
            pl.BlockSpec((CTX, 256), xmap),
            pl.BlockSpec((CTX, 256), xmap),
        ],
        out_specs=[
            pl.BlockSpec((1, RS_CH, 256, 256), lambda b, t: (b, jnp.where(t == 0, RS_STEPS, t - 1), 0, 0)),
            pl.BlockSpec((1, RS_CH, 256, 256), lambda b, t: (b, jnp.where(t == 0, RS_STEPS, RS_STEPS - t), 0, 0)),
        ],
        out_shape=[jax.ShapeDtypeStruct((B, N_STATE, 256, 256), BF16)] * 2,
        scratch_shapes=[pltpu.VMEM((256, 256), F32), pltpu.VMEM((256, 256), F32)],
        compiler_params=_params(("arbitrary", "arbitrary")),
        name="ret_state",
    )(dec, rk, rv, rk, rv, rk, rv)


def _split_dot(x, w):
    hi = x.astype(BF16)
    lo = (x - hi.astype(F32)).astype(BF16)
    return jnp.dot(hi, w, preferred_element_type=F32) + jnp.dot(lo, w, preferred_element_type=F32)


def _ret_out_body(nck, dec_ref, gn_ref, q_ref, k_ref, v_ref, g_ref, sf_ref, sb_ref, o_ref):
    lg = _log_sigmoid(dec_ref[...])
    lgf = lg[0:1]
    lgb = lg[1:2]
    pos = lax.broadcasted_iota(jnp.int32, (CH, 1), 0).astype(F32)
    wqf = jnp.exp((pos + 1.0) * lgf)
    wqb = jnp.exp((float(CH) - pos) * lgb)
    ri = lax.broadcasted_iota(jnp.int32, (CH, CH), 0)
    ci = lax.broadcasted_iota(jnp.int32, (CH, CH), 1)
    diff = (ri - ci).astype(F32)
    decays = []
    for h in range(RET_HEADS):
        lgf_h = lgf[:, h * HD:h * HD + 1]
        lgb_h = lgb[:, h * HD:h * HD + 1]
        decays.append(jnp.where(ri >= ci, jnp.exp(jnp.maximum(diff, 0.0) * lgf_h), 0.0)
                      + jnp.where(ci >= ri, jnp.exp(jnp.maximum(-diff, 0.0) * lgb_h), 0.0))
    decay = jnp.concatenate(decays, axis=0)
    lane_head = _lane_head(256)
    avg = jnp.where(_head_block_mask(), 1.0 / HD, 0.0).astype(BF16)
    gn = gn_ref[...]
    dn = (((1,), (1,)), ((), ()))
    for c in range(nck):
        rows = slice(c * CH, (c + 1) * CH)
        qb16 = q_ref[rows, :]
        q = qb16.astype(F32)
        o = (jnp.dot((q * wqf).astype(BF16), sf_ref[0, c], preferred_element_type=F32)
             + jnp.dot((q * wqb).astype(BF16), sb_ref[0, c], preferred_element_type=F32))
        qs = jnp.concatenate([jnp.where(lane_head == h, qb16, jnp.zeros_like(qb16))
                              for h in range(RET_HEADS)], axis=0)
        sc = lax.dot_general(qs, k_ref[rows, :], dn, preferred_element_type=F32)
        p = (sc * decay).astype(BF16)
        of = jnp.dot(p, v_ref[rows, :], preferred_element_type=F32)
        for h in range(RET_HEADS):
            o = o + jnp.where(lane_head == h, of[h * CH:(h + 1) * CH], 0.0)
        mu = _split_dot(o, avg)
        cen = o - mu
        var = _split_dot(cen * cen, avg)
        on = cen * lax.rsqrt(var + EPS) * gn
        g = g_ref[rows, :].astype(F32)
        o_ref[rows, :] = (g * jax.nn.sigmoid(g) * on).astype(BF16)


def _ret_out(dec, gn, rq, rk, rv, rg, sf, sb, layer, ctx):
    nck = NCH_CTX if ctx else RO_CH
    rows = nck * CH
    if ctx:
        grid = (B, 1)
        rmap = lambda b, u: (CTX_ROWBLK + b, 0)
        omap = lambda b, u: (b, 0)
        smap = lambda b, u: (b, NCH_LAT // NCH_CTX, 0, 0)
        out_rows = R_CTX
    else:
        grid = (B, NCH_LAT // RO_CH)
        rmap = lambda b, u: (b * (NCH_LAT // RO_CH) + u, 0)
        omap = rmap
        smap = lambda b, u: (b, u, 0, 0)
        out_rows = R_LAT
    return pl.pallas_call(
        functools.partial(_ret_out_body, nck),
        grid=grid,
        in_specs=[
            pl.BlockSpec((None, 2, 256), lambda b, u: (layer, 0, 0)),
            pl.BlockSpec((None, 1, 256), lambda b, u: (layer, 0, 0)),
            pl.BlockSpec((rows, 256), rmap),
            pl.BlockSpec((rows, 256), rmap),
            pl.BlockSpec((rows, 256), rmap),
            pl.BlockSpec((rows, 256), rmap),
            pl.BlockSpec((1, nck, 256, 256), smap),
            pl.BlockSpec((1, nck, 256, 256), smap),
        ],
        out_specs=pl.BlockSpec((rows, 256), omap),
        out_shape=jax.ShapeDtypeStruct((out_rows, 256), BF16),
        compiler_params=_params(("arbitrary", "arbitrary")),
        name="ret_out_ctx" if ctx else "ret_out",
    )(dec, gn, rq, rk, rv, rg, sf, sb)


def _tile_head(x_all, hk):
    own = jnp.where(_lane_head(2 * HD) == hk, x_all, jnp.zeros_like(x_all))
    both = own + _lane_swap64(own)
    return jnp.concatenate([both, both], axis=1)


def _gqa_group(q_grp, kt, vt, sinks, mask_fn):
    m = q_grp.shape[0]
    lane_head = _lane_head(256)
    out = None
    for h0 in range(0, ATT_GROUP, ATT_STACK):
        heads = tuple(range(h0, h0 + ATT_STACK))
        qs = jnp.concatenate([jnp.where(lane_head == g, q_grp, jnp.zeros_like(q_grp))
                              for g in heads], axis=0)
        sink = jnp.concatenate([jnp.full((m, 1), sinks[g], F32) for g in heads], axis=0)
        s = lax.dot_general(qs, kt, (((1,), (1,)), ((), ())), preferred_element_type=F32)
        s = mask_fn(s)
        mx = jnp.maximum(jnp.max(s, axis=1, keepdims=True), sink)
        p = jnp.exp(s - mx)
        den = jnp.sum(p, axis=1, keepdims=True) + jnp.exp(sink - mx)
        of = jnp.dot(p.astype(BF16), vt, preferred_element_type=F32) * (1.0 / den)
        for i, g in enumerate(heads):
            t = jnp.where(lane_head == g, of[i * m:(i + 1) * m], 0.0)
            out = t if out is None else out + t
    return out


def _band_mask_fn(lo_cols, lo_ok, hi_cols, hi_ok, width):
    def mask_fn(s):
        parts = []
        col = 0
        while col < width:
            piece = s[:, col:col + CH]
            if col == lo_cols:
                piece = jnp.where(lo_ok, piece, NEG_INF)
            elif col == hi_cols:
                piece = jnp.where(hi_ok, piece, NEG_INF)
            parts.append(piece)
            col += CH
        return jnp.concatenate(parts, axis=1)
    return mask_fn


def _attn_lat_body(layer, sink_ref, q_ref, kp_ref, kc_ref, kn_ref, vp_ref, vc_ref, vn_ref, kx_ref, vx_ref, o_ref):
    n2 = pl.program_id(1)
    k_all = jnp.concatenate([kp_ref[...], kc_ref[...], kx_ref[...], kn_ref[...]], axis=0)
    v_all = jnp.concatenate([vp_ref[...], vc_ref[...], vx_ref[...], vn_ref[...]], axis=0)
    r = lax.broadcasted_iota(jnp.int32, (ATT_STACK * CH, CH), 0) & (CH - 1)
    c = lax.broadcasted_iota(jnp.int32, (ATT_STACK * CH, CH), 1)
    below = c >= r
    above = c <= r
    span = 3 * CH + CTX
    mask_a = _band_mask_fn(0, below & (n2 > 0), 2 * CH, above, span)
    mask_b = _band_mask_fn(0, below, 2 * CH + CTX, above & (n2 < NCH_LAT // 2 - 1), span)
    for hk in range(ATT_KV):
        sinks = [sink_ref[layer, hk * ATT_GROUP + g] for g in range(ATT_GROUP)]
        kt = _tile_head(k_all, hk)
        vt = _tile_head(v_all, hk)
        cols = slice(hk * 256, (hk + 1) * 256)
        oa = _gqa_group(q_ref[0:CH, cols], kt[0:span], vt[0:span], sinks, mask_a)
        ob = _gqa_group(q_ref[CH:2 * CH, cols], kt[CH:CH + span], vt[CH:CH + span], sinks, mask_b)
        o_ref[0:CH, cols] = oa.astype(BF16)
        o_ref[CH:2 * CH, cols] = ob.astype(BF16)


def _attn_lat(sink, aq, ak, av, layer):
    nq = NCH_LAT // 2
    qmap = lambda b, n: (b * nq + n, 0)
    pmap = lambda b, n: (b * NCH_LAT + jnp.maximum(2 * n - 1, 0), 0)
    nmap = lambda b, n: (b * NCH_LAT + jnp.minimum(2 * n + 2, NCH_LAT - 1), 0)
    xmap = lambda b, n: (CTX_ROWBLK + b, 0)
    kv1 = lambda m: pl.BlockSpec((CH, 2 * HD), m)
    kv2 = pl.BlockSpec((2 * CH, 2 * HD), qmap)
    ctx_spec = pl.BlockSpec((CTX, 2 * HD), xmap)
    return pl.pallas_call(
        functools.partial(_attn_lat_body, layer),
        grid=(B, nq),
        in_specs=[
            pl.BlockSpec(memory_space=pltpu.SMEM),
            pl.BlockSpec((2 * CH, ATT_HEADS * HD), qmap),
            kv1(pmap), kv2, kv1(nmap),
            kv1(pmap), kv2, kv1(nmap),
            ctx_spec, ctx_spec,
        ],
        out_specs=pl.BlockSpec((2 * CH, ATT_HEADS * HD), qmap),
        out_shape=jax.ShapeDtypeStruct((R_LAT, ATT_HEADS * HD), BF16),
        compiler_params=_params(("arbitrary", "arbitrary")),
        name="swa_attn",
    )(sink, aq, ak, ak, ak, av, av, av, ak, av)


def _attn_ctx_body(layer, sink_ref, q_ref, kx_ref, vx_ref, o_ref):
    for hk in range(ATT_KV):
        sinks = [sink_ref[layer, hk * ATT_GROUP + g] for g in range(ATT_GROUP)]
        o = _gqa_group(q_ref[:, hk * 256:(hk + 1) * 256], _tile_head(kx_ref[...], hk),
                       _tile_head(vx_ref[...], hk), sinks, lambda s: s)
        o_ref[:, hk * 256:(hk + 1) * 256] = o.astype(BF16)


def _attn_ctx(sink, aq, ak, av, layer):
    xmap = lambda b: (CTX_ROWBLK + b, 0)
    return pl.pallas_call(
        functools.partial(_attn_ctx_body, layer),
        grid=(B,),
        in_specs=[
            pl.BlockSpec(memory_space=pltpu.SMEM),
            pl.BlockSpec((CTX, ATT_HEADS * HD), xmap),
            pl.BlockSpec((CTX, 2 * HD), xmap),
            pl.BlockSpec((CTX, 2 * HD), xmap),
        ],
        out_specs=pl.BlockSpec((CTX, ATT_HEADS * HD), lambda b: (b, 0)),
        out_shape=jax.ShapeDtypeStruct((R_CTX, ATT_HEADS * HD), BF16),
        compiler_params=_params(("arbitrary",)),
        name="ctx_attn",
    )(sink, aq, ak, av)


def _mix_ffn_body(has_ctx, final, *refs):
    refs = list(refs)
    h_ref = refs.pop(0)
    mixers = []
    for _ in range(3):
        if has_ctx:
            lat_ref, ctx_ref = refs.pop(0), refs.pop(0)
            mixers.append(_pick(lat_ref, ctx_ref))
        else:
            mixers.append(refs.pop(0)[...])
    m_ref, nw_ref, wo_ref, w_in_ref, w_out_ref = refs[:5]
    rest = refs[5:]
    yf, yr, ya = mixers
    m = m_ref[...]
    y = (jnp.dot(yf.astype(BF16), wo_ref[0:256, :], preferred_element_type=F32)
         + jnp.dot(yr, wo_ref[256:512, :], preferred_element_type=F32)
         + jnp.dot(ya, wo_ref[512:1024, :], preferred_element_type=F32))
    h = h_ref[...] + _mod_slice(m, 5) * y
    h = _ffn_half_step(h, m, nw_ref[...], 6, w_in_ref, w_out_ref)
    if final:
        fw_ref, o_ref = rest
        ms = jnp.mean(h * h, axis=-1, keepdims=True)
        h = h * lax.rsqrt(ms + EPS) * fw_ref[...]
    else:
        (o_ref,) = rest
    o_ref[...] = h


def _mix_ffn(h, mixers, mod_all, norm_w, wo, ffn_in, ffn_out, final_w, layer):
    has_ctx = mixers[0][1] is not None
    final = final_w is not None
    nt = NT_ALL if has_ctx else NT_LAT
    in_specs = [_row_spec(D)]
    args = [h]
    for (lat, ctx), width in zip(mixers, (256, 256, 512)):
        if has_ctx:
            in_specs += _lat_or_ctx_specs(width)
            args += [lat, ctx]
        else:
            in_specs += [_row_spec(width)]
            args += [lat]
    in_specs += [_mod_spec(layer), _norm_spec(layer, 2), _const_spec((None, D, D), (layer, 0, 0))]
    in_specs += _ffn_w_specs(layer, 1)
    args += [mod_all, norm_w, wo, ffn_in, ffn_out]
    if final:
        in_specs += [_const_spec((1, D))]
        args += [final_w]
    return pl.pallas_call(
        functools.partial(_mix_ffn_body, has_ctx, final),
        grid=(nt,),
        in_specs=in_specs,
        out_specs=_row_spec(D),
        out_shape=jax.ShapeDtypeStruct((nt * TM, D), F32),
        compiler_params=_params(("arbitrary",)),
        name="mix_ffn_final" if final else "mix_ffn",
    )(*args)


def kernel(x, c, ctx, c_ctx, norm_w, w_ada, b_ada, ffn_w_in, ffn_w_out, w_in, w_o, ret_decay, ret_gn_w,
           attn_sink, final_norm_w):
    assert x.shape == (B, S, D) and ctx.shape == (B, CTX, D)
    cos_t, sin_t = _rope_tables()
    chan, w1, twr, twi, w2, wc = _fourier_tables()

    cond_t = jnp.zeros((D, 8), F32).at[:, 0:B].set(c.T).at[:, B].set(c_ctx)
    mod_all = _ada(cond_t, w_ada, b_ada).reshape(DEPTH, 8, 1, N_MOD * D)

    ffn_in = ffn_w_in.astype(BF16)
    ffn_out = ffn_w_out.astype(BF16)
    w_in_b = w_in.astype(BF16)
    w_o_b = w_o.astype(BF16)
    norm4 = norm_w.reshape(DEPTH, 3, 1, D)
    gn3 = ret_gn_w.reshape(DEPTH, 1, 256)
    dec = jnp.repeat(ret_decay.astype(F32), HD, axis=2)
    sink = attn_sink.astype(F32)

    h = None
    out = None
    for layer in range(DEPTH):
        last = layer == DEPTH - 1
        if layer == 0:
            h = _ffn_first(x.reshape(R_LAT, D), ctx.reshape(R_CTX, D), mod_all, norm4, ffn_in, ffn_out, layer)
        else:
            h = _ffn(h, mod_all, norm4, ffn_in, ffn_out, layer)
        gr, gi, rq, rk, rv, rg, aq, ak, av = _inproj(h, mod_all, norm4, w_in_b, cos_t, sin_t, chan, layer)

        yf = _fft2(w2, twr, twi, _fft1(w1, gr, gi))
        sf, sb = _ret_state(dec, rk, rv, layer)
        yr = _ret_out(dec, gn3, rq, rk, rv, rg, sf, sb, layer, ctx=False)
        ya = _attn_lat(sink, aq, ak, av, layer)
        if last:
            yfc = yrc = yac = None
        else:
            yfc = _fft_ctx(wc, gr, gi)
            yrc = _ret_out(dec, gn3, rq, rk, rv, rg, sf, sb, layer, ctx=True)
            yac = _attn_ctx(sink, aq, ak, av, layer)

        res = _mix_ffn(h, [(yf, yfc), (yr, yrc), (ya, yac)], mod_all, norm4, w_o_b, ffn_in, ffn_out,
                       final_norm_w.reshape(1, D) if last else None, layer)
        if last:
            out = res
        else:
            h = res
    return out.reshape(B, S, D)
```

```python
import functools
import math

import numpy as np
import jax
import jax.numpy as jnp
from jax import lax
from jax.experimental import pallas as pl
from jax.experimental.pallas import tpu as pltpu

F32 = jnp.float32
BF16 = jnp.bfloat16

D = 1024
B = 2
S = 8192
DEPTH = 2
GRID_W = 64
CTX = 256
HD = 64
N_MOD = 9
D_FF = 2816
D_IN = 2048
EPS = 1e-6
NEG_INF = -1e30
ROPE_BASE = 10000.0
ROPE_PAIRS = 16
RET_HEADS = 4
ATT_HEADS = 8
ATT_KV = 2
ATT_GROUP = ATT_HEADS // ATT_KV
ATT_STACK = 2
WINDOW = 128

R_LAT = B * S
R_CTX = B * CTX
R_ALL = R_LAT + R_CTX
TM = 512
NT_LAT = R_LAT // TM
NT_ALL = R_ALL // TM
CH = 128
NCH_LAT = S // CH
NCH_CTX = CTX // CH
MXU_N = 256
FF_CHUNKS = ((0, 5 * MXU_N), (5 * MXU_N, D_FF))
FFT_L1 = 64
FFT_L2 = 128
VMEM_LIMIT = 56 * 1024 * 1024


def _params(sem, vmem=VMEM_LIMIT):
    return pltpu.CompilerParams(dimension_semantics=sem, vmem_limit_bytes=vmem)


def _const_spec(shape, index=None):
    if index is None:
        index = (0,) * len(shape)
    return pl.BlockSpec(shape, lambda *_: index, pipeline_mode=pl.Buffered(1))


def _rope_tables():
    rows = S // GRID_W
    row = np.repeat(np.arange(rows, dtype=np.float64), GRID_W)
    col = np.tile(np.arange(GRID_W, dtype=np.float64), rows)
    inv = ROPE_BASE ** (-np.arange(ROPE_PAIRS, dtype=np.float64) / ROPE_PAIRS)
    ar = row[:, None] * inv
    ac = col[:, None] * inv
    cos64 = np.concatenate([np.cos(ar), np.cos(ar), np.cos(ac), np.cos(ac)], axis=1)
    sin64 = np.concatenate([-np.sin(ar), np.sin(ar), -np.sin(ac), np.sin(ac)], axis=1)
    cos = np.concatenate([cos64, cos64], axis=1)
    sin = np.concatenate([sin64, sin64], axis=1)
    cos = np.concatenate([cos, np.ones((TM, 128))], axis=0)
    sin = np.concatenate([sin, np.zeros((TM, 128))], axis=0)
    return jnp.asarray(cos, F32), jnp.asarray(sin, F32)


def _dft_cs(n):
    k = np.arange(n)
    ang = 2.0 * np.pi * ((k[:, None] * k[None, :]) % n) / n
    return np.cos(ang), np.sin(ang)


def _fourier_tables():
    c64, s64 = _dft_cs(64)
    eye4 = np.eye(4)
    chan = np.concatenate([np.kron(eye4, c64), -np.kron(eye4, s64)], axis=1)
    w1 = np.block([[c64, s64], [-s64, c64]])
    p1 = np.arange(FFT_L1)[:, None]
    l2 = np.arange(FFT_L2)[None, :]
    ang = 2.0 * np.pi * ((p1 * l2) % S) / S
    twr = np.repeat(np.cos(ang)[:, :, None], 128, axis=2)
    twi = np.repeat(np.sin(ang)[:, :, None], 128, axis=2)
    c128, s128 = _dft_cs(128)
    w2 = np.concatenate([c128, s128], axis=1) / math.sqrt(S * 64.0)
    c256, s256 = _dft_cs(CTX)
    wc = np.concatenate([c256, s256], axis=1) / math.sqrt(CTX * 64.0)
    f = lambda a: jnp.asarray(a, F32)
    return (f(chan).astype(BF16), f(w1).astype(BF16), f(twr), f(twi), f(w2).astype(BF16), f(wc).astype(BF16))


def _mod_slice(m, i):
    return m[:, i * D:(i + 1) * D]


def _rms_mod(h, nw, shift, scale):
    ms = jnp.mean(h * h, axis=-1, keepdims=True)
    y = h * lax.rsqrt(ms + EPS) * nw
    return y * (1.0 + scale) + shift


def _swiglu(u, w_in_ref, w_out_ref):
    y = None
    for lo, hi in FF_CHUNKS:
        g = jnp.dot(u, w_in_ref[:, lo:hi], preferred_element_type=F32)
        up = jnp.dot(u, w_in_ref[:, D_FF + lo:D_FF + hi], preferred_element_type=F32)
        a = (g * jax.nn.sigmoid(g) * up).astype(BF16)
        t = jnp.dot(a, w_out_ref[lo:hi, :], preferred_element_type=F32)
        y = t if y is None else y + t
    return y


def _ffn_half_step(h, m, nw, mod0, w_in_ref, w_out_ref):
    u = _rms_mod(h, nw, _mod_slice(m, mod0), _mod_slice(m, mod0 + 1)).astype(BF16)
    return h + 0.5 * _mod_slice(m, mod0 + 2) * _swiglu(u, w_in_ref, w_out_ref)


def _log_sigmoid(x):
    return jnp.minimum(x, 0.0) - jnp.log1p(jnp.exp(-jnp.abs(x)))


def _lane_head(width):
    return lax.broadcasted_iota(jnp.int32, (1, width), 1) // HD


def _lane_swap64(x):
    return pltpu.roll(x.astype(F32), 64, axis=1).astype(BF16)


ADA_CB = 1152


def _ada_body(ct_ref, w_ref, b_ref, o_ref):
    ct = ct_ref[...]
    s = ct * jax.nn.sigmoid(ct)
    w = w_ref[0]
    rows = [jnp.sum(w * s[:, r:r + 1], axis=0, keepdims=True) for r in range(3)]
    rows.append(jnp.zeros((5, ADA_CB), F32))
    o_ref[0] = jnp.concatenate(rows, axis=0) + b_ref[0]


def _ada(cond_t, w_ada, b_ada):
    nb = (N_MOD * D) // ADA_CB
    return pl.pallas_call(
        _ada_body,
        grid=(DEPTH, nb),
        in_specs=[
            pl.BlockSpec((D, 8), lambda l, j: (0, 0)),
            pl.BlockSpec((1, D, ADA_CB), lambda l, j: (l, 0, j)),
            pl.BlockSpec((1, 1, ADA_CB), lambda l, j: (l, 0, j)),
        ],
        out_specs=pl.BlockSpec((1, 8, ADA_CB), lambda l, j: (l, 0, j)),
        out_shape=jax.ShapeDtypeStruct((DEPTH, 8, N_MOD * D), F32),
        compiler_params=_params(("arbitrary", "arbitrary")),
        name="ada_mod",
    )(cond_t, w_ada, b_ada.reshape(DEPTH, 1, N_MOD * D))


def _mod_spec(layer):
    return pl.BlockSpec((None, None, 1, N_MOD * D),
                        lambda i: (layer, jnp.minimum(i // (NT_LAT // B), B), 0, 0))


def _norm_spec(layer, k):
    return _const_spec((None, None, 1, D), (layer, k, 0, 0))


def _ffn_w_specs(layer, j):
    return [_const_spec((None, None, D, 2 * D_FF), (layer, j, 0, 0)),
            _const_spec((None, None, D_FF, D), (layer, j, 0, 0))]


def _row_spec(width):
    return pl.BlockSpec((TM, width), lambda i: (i, 0))


def _lat_or_ctx_specs(width):
    return [pl.BlockSpec((TM, width), lambda i: (jnp.minimum(i, NT_LAT - 1), 0)),
            pl.BlockSpec((TM, width), lambda i: (0, 0))]


def _pick(lat_ref, ctx_ref):
    return jnp.where(pl.program_id(0) < NT_LAT, lat_ref[...], ctx_ref[...])


def _ffn_first_body(x_ref, c_ref, m_ref, nw_ref, w_in_ref, w_out_ref, o_ref):
    o_ref[...] = _ffn_half_step(_pick(x_ref, c_ref), m_ref[...], nw_ref[...], 0, w_in_ref, w_out_ref)


def _ffn_first(x2d, c2d, mod_all, norm_w, ffn_in, ffn_out, layer):
    return pl.pallas_call(
        _ffn_first_body,
        grid=(NT_ALL,),
        in_specs=_lat_or_ctx_specs(D) + [_mod_spec(layer), _norm_spec(layer, 0)] + _ffn_w_specs(layer, 0),
        out_specs=_row_spec(D),
        out_shape=jax.ShapeDtypeStruct((R_ALL, D), F32),
        compiler_params=_params(("arbitrary",)),
        name="ffn_first",
    )(x2d, c2d, mod_all, norm_w, ffn_in, ffn_out)


def _ffn_body(h_ref, m_ref, nw_ref, w_in_ref, w_out_ref, o_ref):
    o_ref[...] = _ffn_half_step(h_ref[...], m_ref[...], nw_ref[...], 0, w_in_ref, w_out_ref)


def _ffn(h, mod_all, norm_w, ffn_in, ffn_out, layer):
    return pl.pallas_call(
        _ffn_body,
        grid=(NT_ALL,),
        in_specs=[_row_spec(D), _mod_spec(layer), _norm_spec(layer, 0)] + _ffn_w_specs(layer, 0),
        out_specs=_row_spec(D),
        out_shape=jax.ShapeDtypeStruct((R_ALL, D), F32),
        compiler_params=_params(("arbitrary",)),
        name="ffn_pre",
    )(h, mod_all, norm_w, ffn_in, ffn_out)


def _rope(x, cos, sin, lane_first_half):
    outs = []
    for j in range(x.shape[1] // 128):
        xb = x[:, j * 128:(j + 1) * 128]
        fwd = pltpu.roll(xb, 112, axis=1)
        bwd = pltpu.roll(xb, 16, axis=1)
        partner = jnp.where(lane_first_half, fwd, bwd)
        outs.append(xb * cos + partner * sin)
    return outs[0] if len(outs) == 1 else jnp.concatenate(outs, axis=1)


def _inproj_body(h_ref, m_ref, nw_ref, w_ref, cos_ref, sin_ref, chan_ref,
                 gr_ref, gi_ref, rq_ref, rk_ref, rv_ref, rg_ref, aq_ref, ak_ref, av_ref):
    m = m_ref[...]
    u = _rms_mod(h_ref[...], nw_ref[...], _mod_slice(m, 3), _mod_slice(m, 4)).astype(BF16)
    z = jnp.dot(u, w_ref[...], preferred_element_type=F32)
    cos = cos_ref[...]
    sin = sin_ref[...]
    lane = lax.broadcasted_iota(jnp.int32, (1, 128), 1)
    first_half = (lane // ROPE_PAIRS) % 2 == 0
    g = jnp.dot(z[:, 0:256].astype(BF16), chan_ref[...], preferred_element_type=F32)
    gr_ref[...] = g[:, 0:256]
    gi_ref[...] = g[:, 256:512]
    rq_ref[...] = _rope(z[:, 256:512], cos, sin, first_half).astype(BF16)
    rk_ref[...] = (_rope(z[:, 512:768], cos, sin, first_half) * (HD ** -0.5)).astype(BF16)
    rv_ref[...] = z[:, 768:1024].astype(BF16)
    rg_ref[...] = z[:, 1024:1280].astype(BF16)
    aq_ref[...] = (_rope(z[:, 1280:1792], cos, sin, first_half) * (HD ** -0.5)).astype(BF16)
    ak_ref[...] = _rope(z[:, 1792:1920], cos, sin, first_half).astype(BF16)
    av_ref[...] = z[:, 1920:2048].astype(BF16)


def _rope_block(i):
    return jnp.where(i < NT_LAT, i % (S // TM), S // TM)


def _inproj(h, mod_all, norm_w, w_in, cos_t, sin_t, chan, layer):
    widths = (256, 256, 256, 256, 256, 256, 512, 128, 128)
    return pl.pallas_call(
        _inproj_body,
        grid=(NT_ALL,),
        in_specs=[
            _row_spec(D),
            _mod_spec(layer),
            _norm_spec(layer, 1),
            _const_spec((None, D, D_IN), (layer, 0, 0)),
            pl.BlockSpec((TM, 128), lambda i: (_rope_block(i), 0)),
            pl.BlockSpec((TM, 128), lambda i: (_rope_block(i), 0)),
            _const_spec((256, 512)),
        ],
        out_specs=[_row_spec(w) for w in widths],
        out_shape=[jax.ShapeDtypeStruct((R_ALL, w), F32 if i < 2 else BF16) for i, w in enumerate(widths)],
        compiler_params=_params(("arbitrary",)),
        name="in_proj",
    )(h, mod_all, norm_w, w_in, cos_t, sin_t, chan)


FFT_SUB = 8


def _fft1_body(w1_ref, xr_ref, xi_ref, z_ref):
    w1 = w1_ref[...]
    for s in range(FFT_SUB):
        x = jnp.concatenate([xr_ref[:, s, :], xi_ref[:, s, :]], axis=0).astype(BF16)
        z = jnp.dot(w1, x, preferred_element_type=F32)
        z_ref[0, :, s, :] = z[0:FFT_L1]
        z_ref[1, :, s, :] = z[FFT_L1:2 * FFT_L1]


def _fft1(w1, gr, gi):
    view = (R_ALL // FFT_L2, FFT_L2 // FFT_SUB, FFT_SUB, 256)
    x_spec = pl.BlockSpec((FFT_L1, None, FFT_SUB, 256), lambda b, j: (b, j, 0, 0))
    return pl.pallas_call(
        _fft1_body,
        grid=(B, FFT_L2 // FFT_SUB),
        in_specs=[_const_spec((128, 128)), x_spec, x_spec],
        out_specs=pl.BlockSpec((None, 2, FFT_L1, FFT_SUB, 256), lambda b, j: (b, 0, 0, j, 0)),
        out_shape=jax.ShapeDtypeStruct((B, 2, FFT_L1, FFT_L2, 256), F32),
        compiler_params=_params(("arbitrary", "arbitrary")),
        name="fft_stage1",
    )(w1, gr.reshape(view), gi.reshape(view))


def _fft2_body(w2_ref, twr_ref, twi_ref, z_ref, o_ref):
    w2 = w2_ref[...]
    for p in range(FFT_SUB):
        zr = z_ref[0, p]
        zi = z_ref[1, p]
        tr = jnp.concatenate([twr_ref[p]] * 2, axis=1)
        ti = jnp.concatenate([twi_ref[p]] * 2, axis=1)
        yr = tr * zr + ti * zi
        yi = tr * zi - ti * zr
        y = jnp.concatenate([yr, yi], axis=0).astype(BF16)
        o_ref[:, p, :] = jnp.dot(w2, y, preferred_element_type=F32)


def _fft2(w2, twr, twi, z):
    out = pl.pallas_call(
        _fft2_body,
        grid=(B, FFT_L1 // FFT_SUB),
        in_specs=[
            _const_spec((128, 256)),
            pl.BlockSpec((FFT_SUB, FFT_L2, 128), lambda b, j: (j, 0, 0)),
            pl.BlockSpec((FFT_SUB, FFT_L2, 128), lambda b, j: (j, 0, 0)),
            pl.BlockSpec((None, 2, FFT_SUB, FFT_L2, 256), lambda b, j: (b, 0, j, 0, 0)),
        ],
        out_specs=pl.BlockSpec((None, FFT_L2, None, FFT_SUB, 256), lambda b, j: (b, 0, j, 0, 0)),
        out_shape=jax.ShapeDtypeStruct((B, FFT_L2, FFT_L1 // FFT_SUB, FFT_SUB, 256), F32),
        compiler_params=_params(("arbitrary", "arbitrary")),
        name="fft_stage2",
    )(w2, twr, twi, z)
    return out.reshape(R_LAT, 256)


def _fft_ctx_body(wc_ref, gr_ref, gi_ref, o_ref):
    x = jnp.concatenate([gr_ref[...], gi_ref[...]], axis=0).astype(BF16)
    o_ref[...] = jnp.dot(wc_ref[...], x, preferred_element_type=F32)


def _fft_ctx(wc, gr, gi):
    blk0 = R_LAT // CTX
    return pl.pallas_call(
        _fft_ctx_body,
        grid=(B,),
        in_specs=[
            _const_spec((CTX, 2 * CTX)),
            pl.BlockSpec((CTX, 256), lambda b: (blk0 + b, 0)),
            pl.BlockSpec((CTX, 256), lambda b: (blk0 + b, 0)),
        ],
        out_specs=pl.BlockSpec((CTX, 256), lambda b: (b, 0)),
        out_shape=jax.ShapeDtypeStruct((R_CTX, 256), F32),
        compiler_params=_params(("arbitrary",)),
        name="fft_ctx",
    )(wc, gr, gi)


RS_CH = 8
RS_STEPS = NCH_LAT // RS_CH
N_STATE = (RS_STEPS + 1) * RS_CH
RO_CH = 8
CTX_ROWBLK = R_LAT // CTX


def _head_block_mask():
    r = lax.broadcasted_iota(jnp.int32, (256, 256), 0) // HD
    c = lax.broadcasted_iota(jnp.int32, (256, 256), 1) // HD
    return r == c


def _ret_state_body(dec_ref, kf_ref, vf_ref, kb_ref, vb_ref, kx_ref, vx_ref, sf_ref, sb_ref, accf, accb):
    t = pl.program_id(1)
    lg = _log_sigmoid(dec_ref[...])
    lgf = lg[0:1]
    lgb = lg[1:2]
    pos = lax.broadcasted_iota(jnp.int32, (CH, 1), 0).astype(F32)
    wf = jnp.exp((CH - 1.0 - pos) * lgf)
    wb = jnp.exp(pos * lgb)
    cdf = jnp.exp(float(CH) * lgf)
    cdb = jnp.exp(float(CH) * lgb)
    mask = _head_block_mask()
    dn = (((0,), (0,)), ((), ()))

    def step(acc, out_ref, slot, k, v, w, cd):
        out_ref[0, slot] = acc[...].astype(BF16)
        kd = (k.astype(F32) * w).astype(BF16)
        kv = lax.dot_general(kd, v, dn, preferred_element_type=F32)
        acc[...] = acc[...] * cd + jnp.where(mask, kv, 0.0)

    @pl.when(t == 0)
    def _():
        accf[...] = jnp.zeros_like(accf)
        accb[...] = jnp.zeros_like(accb)
        for j in range(NCH_CTX, RS_CH):
            sf_ref[0, j] = jnp.zeros((256, 256), BF16)
            sb_ref[0, j] = jnp.zeros((256, 256), BF16)
        for j in range(NCH_CTX):
            rows = slice(j * CH, (j + 1) * CH)
            step(accf, sf_ref, j, kx_ref[rows, :], vx_ref[rows, :], wf, cdf)
        for j in reversed(range(NCH_CTX)):
            rows = slice(j * CH, (j + 1) * CH)
            step(accb, sb_ref, j, kx_ref[rows, :], vx_ref[rows, :], wb, cdb)

    @pl.when(t > 0)
    def _():
        for j in range(RS_CH):
            rows = slice(j * CH, (j + 1) * CH)
            step(accf, sf_ref, j, kf_ref[rows, :], vf_ref[rows, :], wf, cdf)
        for j in reversed(range(RS_CH)):
            rows = slice(j * CH, (j + 1) * CH)
            step(accb, sb_ref, j, kb_ref[rows, :], vb_ref[rows, :], wb, cdb)


def _ret_state(dec, rk, rv, layer):
    rows = RS_CH * CH
    fblk = lambda t: jnp.maximum(t - 1, 0)
    bblk = lambda t: RS_STEPS - jnp.maximum(t, 1)
    fmap = lambda b, t: (b * RS_STEPS + fblk(t), 0)
    bmap = lambda b, t: (b * RS_STEPS + bblk(t), 0)
    xmap = lambda b, t: (CTX_ROWBLK + b, 0)
    return pl.pallas_call(
        _ret_state_body,
        grid=(B, RS_STEPS + 1),
        in_specs=[
            pl.BlockSpec((None, 2, 256), lambda b, t: (layer, 0, 0)),
            pl.BlockSpec((rows, 256), fmap),
            pl.BlockSpec((rows, 256), fmap),
            pl.BlockSpec((rows, 256), bmap),
            pl.BlockSpec((rows, 256), bmap),
            pl.BlockSpec((CTX, 256), xmap),
            pl.BlockSpec((CTX, 256), xmap),
        ],
        out_specs=[
            pl.BlockSpec((1, RS_CH, 256, 256), lambda b, t: (b, jnp.where(t == 0, RS_STEPS, t - 1), 0, 0)),
            pl.BlockSpec((1, RS_CH, 256, 256), lambda b, t: (b, jnp.where(t == 0, RS_STEPS, RS_STEPS - t), 0, 0)),
        ],
        out_shape=[jax.ShapeDtypeStruct((B, N_STATE, 256, 256), BF16)] * 2,
        scratch_shapes=[pltpu.VMEM((256, 256), F32), pltpu.VMEM((256, 256), F32)],
        compiler_params=_params(("arbitrary", "arbitrary")),
        name="ret_state",
    )(dec, rk, rv, rk, rv, rk, rv)


def _split_dot(x, w):
    hi = x.astype(BF16)
    lo = (x - hi.astype(F32)).astype(BF16)
    return jnp.dot(hi, w, preferred_element_type=F32) + jnp.dot(lo, w, preferred_element_type=F32)


def _ret_out_body(nck, dec_ref, gn_ref, q_ref, k_ref, v_ref, g_ref, sf_ref, sb_ref, o_ref):
    lg = _log_sigmoid(dec_ref[...])
    lgf = lg[0:1]
    lgb = lg[1:2]
    pos = lax.broadcasted_iota(jnp.int32, (CH, 1), 0).astype(F32)
    wqf = jnp.exp((pos + 1.0) * lgf)
    wqb = jnp.exp((float(CH) - pos) * lgb)
    ri = lax.broadcasted_iota(jnp.int32, (CH, CH), 0)
    ci = lax.broadcasted_iota(jnp.int32, (CH, CH), 1)
    diff = (ri - ci).astype(F32)
    decays = []
    for h in range(RET_HEADS):
        lgf_h = lgf[:, h * HD:h * HD + 1]
        lgb_h = lgb[:, h * HD:h * HD + 1]
        decays.append(jnp.where(ri >= ci, jnp.exp(jnp.maximum(diff, 0.0) * lgf_h), 0.0)
                      + jnp.where(ci >= ri, jnp.exp(jnp.maximum(-diff, 0.0) * lgb_h), 0.0))
    decay = jnp.concatenate(decays, axis=0)
    lane_head = _lane_head(256)
    avg = jnp.where(_head_block_mask(), 1.0 / HD, 0.0).astype(BF16)
    gn = gn_ref[...]
    dn = (((1,), (1,)), ((), ()))
    chunk_rows = [slice(c * CH, (c + 1) * CH) for c in range(nck)]
    cross, scores = [], []
    for c, rows in enumerate(chunk_rows):
        qb16 = q_ref[rows, :]
        q = qb16.astype(F32)
        cross.append(jnp.dot((q * wqf).astype(BF16), sf_ref[0, c], preferred_element_type=F32)
                     + jnp.dot((q * wqb).astype(BF16), sb_ref[0, c], preferred_element_type=F32))
        qs = jnp.concatenate([jnp.where(lane_head == h, qb16, jnp.zeros_like(qb16))
                              for h in range(RET_HEADS)], axis=0)
        scores.append(lax.dot_general(qs, k_ref[rows, :], dn, preferred_element_type=F32))
    outs = []
    for c, rows in enumerate(chunk_rows):
        p = (scores[c] * decay).astype(BF16)
        of = jnp.dot(p, v_ref[rows, :], preferred_element_type=F32)
        o = cross[c]
        for h in range(RET_HEADS):
            o = o + jnp.where(lane_head == h, of[h * CH:(h + 1) * CH], 0.0)
        outs.append(o)
    o = jnp.concatenate(outs, axis=0)
    mu = _split_dot(o, avg)
    cen = o - mu
    var = _split_dot(cen * cen, avg)
    on = cen * lax.rsqrt(var + EPS) * gn
    g = g_ref[...].astype(F32)
    o_ref[...] = (g * jax.nn.sigmoid(g) * on).astype(BF16)


def _ret_out(dec, gn, rq, rk, rv, rg, sf, sb, layer, ctx):
    nck = NCH_CTX if ctx else RO_CH
    rows = nck * CH
    if ctx:
        grid = (B, 1)
        rmap = lambda b, u: (CTX_ROWBLK + b, 0)
        omap = lambda b, u: (b, 0)
        smap = lambda b, u: (b, NCH_LAT // NCH_CTX, 0, 0)
        out_rows = R_CTX
    else:
        grid = (B, NCH_LAT // RO_CH)
        rmap = lambda b, u: (b * (NCH_LAT // RO_CH) + u, 0)
        omap = rmap
        smap = lambda b, u: (b, u, 0, 0)
        out_rows = R_LAT
    return pl.pallas_call(
        functools.partial(_ret_out_body, nck),
        grid=grid,
        in_specs=[
            pl.BlockSpec((None, 2, 256), lambda b, u: (layer, 0, 0)),
            pl.BlockSpec((None, 1, 256), lambda b, u: (layer, 0, 0)),
            pl.BlockSpec((rows, 256), rmap),
            pl.BlockSpec((rows, 256), rmap),
            pl.BlockSpec((rows, 256), rmap),
            pl.BlockSpec((rows, 256), rmap),
            pl.BlockSpec((1, nck, 256, 256), smap),
            pl.BlockSpec((1, nck, 256, 256), smap),
        ],
        out_specs=pl.BlockSpec((rows, 256), omap),
        out_shape=jax.ShapeDtypeStruct((out_rows, 256), BF16),
        compiler_params=_params(("arbitrary", "arbitrary")),
        name="ret_out_ctx" if ctx else "ret_out",
    )(dec, gn, rq, rk, rv, rg, sf, sb)


def _tile_head(x_all, hk):
    own = jnp.where(_lane_head(2 * HD) == hk, x_all, jnp.zeros_like(x_all))
    both = own + _lane_swap64(own)
    return jnp.concatenate([both, both], axis=1)


def _gqa_groups(groups):
    lane_head = _lane_head(256)
    chains = []
    for gi, (q_grp, kt, vt, sinks, mask_fn) in enumerate(groups):
        for h0 in range(0, ATT_GROUP, ATT_STACK):
            chains.append((gi, tuple(range(h0, h0 + ATT_STACK)), q_grp, kt, vt, sinks, mask_fn))
    outs = [None] * len(groups)
    for gi, heads, q_grp, kt, vt, sinks, mask_fn in chains:
        m = q_grp.shape[0]
        qs = jnp.concatenate([jnp.where(lane_head == g, q_grp, jnp.zeros_like(q_grp))
                              for g in heads], axis=0)
        s = lax.dot_general(qs, kt, (((1,), (1,)), ((), ())), preferred_element_type=F32)
        sink = jnp.concatenate([jnp.full((m, 1), sinks[g], F32) for g in heads], axis=0)
        s = mask_fn(s)
        mx = jnp.maximum(jnp.max(s, axis=1, keepdims=True), sink)
        p = jnp.exp(s - mx)
        inv = 1.0 / (jnp.sum(p, axis=1, keepdims=True) + jnp.exp(sink - mx))
        of = jnp.dot(p.astype(BF16), vt, preferred_element_type=F32) * inv
        for i, g in enumerate(heads):
            t = jnp.where(lane_head == g, of[i * m:(i + 1) * m], 0.0)
            outs[gi] = t if outs[gi] is None else outs[gi] + t
    return outs


def _band_mask_fn(lo_cols, lo_ok, hi_cols, hi_ok, width):
    def mask_fn(s):
        parts = []
        col = 0
        while col < width:
            piece = s[:, col:col + CH]
            if col == lo_cols:
                piece = jnp.where(lo_ok, piece, NEG_INF)
            elif col == hi_cols:
                piece = jnp.where(hi_ok, piece, NEG_INF)
            parts.append(piece)
            col += CH
        return jnp.concatenate(parts, axis=1)
    return mask_fn


def _attn_lat_body(layer, sink_ref, q_ref, kp_ref, kc_ref, kn_ref, vp_ref, vc_ref, vn_ref, kx_ref, vx_ref, o_ref):
    n2 = pl.program_id(1)
    k_all = jnp.concatenate([kp_ref[...], kc_ref[...], kx_ref[...], kn_ref[...]], axis=0)
    v_all = jnp.concatenate([vp_ref[...], vc_ref[...], vx_ref[...], vn_ref[...]], axis=0)
    r = lax.broadcasted_iota(jnp.int32, (ATT_STACK * CH, CH), 0) & (CH - 1)
    c = lax.broadcasted_iota(jnp.int32, (ATT_STACK * CH, CH), 1)
    below = c >= r
    above = c <= r
    span = 3 * CH + CTX
    mask_a = _band_mask_fn(0, below & (n2 > 0), 2 * CH, above, span)
    mask_b = _band_mask_fn(0, below, 2 * CH + CTX, above & (n2 < NCH_LAT // 2 - 1), span)
    groups = []
    for hk in range(ATT_KV):
        sinks = [sink_ref[layer, hk * ATT_GROUP + g] for g in range(ATT_GROUP)]
        kt = _tile_head(k_all, hk)
        vt = _tile_head(v_all, hk)
        cols = slice(hk * 256, (hk + 1) * 256)
        groups.append((q_ref[0:CH, cols], kt[0:span], vt[0:span], sinks, mask_a))
        groups.append((q_ref[CH:2 * CH, cols], kt[CH:CH + span], vt[CH:CH + span], sinks, mask_b))
    outs = _gqa_groups(groups)
    for hk in range(ATT_KV):
        cols = slice(hk * 256, (hk + 1) * 256)
        o_ref[0:CH, cols] = outs[2 * hk].astype(BF16)
        o_ref[CH:2 * CH, cols] = outs[2 * hk + 1].astype(BF16)


def _attn_lat(sink, aq, ak, av, layer):
    nq = NCH_LAT // 2
    qmap = lambda b, n: (b * nq + n, 0)
    pmap = lambda b, n: (b * NCH_LAT + jnp.maximum(2 * n - 1, 0), 0)
    nmap = lambda b, n: (b * NCH_LAT + jnp.minimum(2 * n + 2, NCH_LAT - 1), 0)
    xmap = lambda b, n: (CTX_ROWBLK + b, 0)
    kv1 = lambda m: pl.BlockSpec((CH, 2 * HD), m)
    kv2 = pl.BlockSpec((2 * CH, 2 * HD), qmap)
    ctx_spec = pl.BlockSpec((CTX, 2 * HD), xmap)
    return pl.pallas_call(
        functools.partial(_attn_lat_body, layer),
        grid=(B, nq),
        in_specs=[
            pl.BlockSpec(memory_space=pltpu.SMEM),
            pl.BlockSpec((2 * CH, ATT_HEADS * HD), qmap),
            kv1(pmap), kv2, kv1(nmap),
            kv1(pmap), kv2, kv1(nmap),
            ctx_spec, ctx_spec,
        ],
        out_specs=pl.BlockSpec((2 * CH, ATT_HEADS * HD), qmap),
        out_shape=jax.ShapeDtypeStruct((R_LAT, ATT_HEADS * HD), BF16),
        compiler_params=_params(("arbitrary", "arbitrary")),
        name="swa_attn",
    )(sink, aq, ak, ak, ak, av, av, av, ak, av)


def _attn_ctx_body(layer, sink_ref, q_ref, kx_ref, vx_ref, o_ref):
    groups = []
    for hk in range(ATT_KV):
        sinks = [sink_ref[layer, hk * ATT_GROUP + g] for g in range(ATT_GROUP)]
        groups.append((q_ref[:, hk * 256:(hk + 1) * 256], _tile_head(kx_ref[...], hk),
                       _tile_head(vx_ref[...], hk), sinks, lambda s: s))
    outs = _gqa_groups(groups)
    for hk in range(ATT_KV):
        o_ref[:, hk * 256:(hk + 1) * 256] = outs[hk].astype(BF16)


def _attn_ctx(sink, aq, ak, av, layer):
    xmap = lambda b: (CTX_ROWBLK + b, 0)
    return pl.pallas_call(
        functools.partial(_attn_ctx_body, layer),
        grid=(B,),
        in_specs=[
            pl.BlockSpec(memory_space=pltpu.SMEM),
            pl.BlockSpec((CTX, ATT_HEADS * HD), xmap),
            pl.BlockSpec((CTX, 2 * HD), xmap),
            pl.BlockSpec((CTX, 2 * HD), xmap),
        ],
        out_specs=pl.BlockSpec((CTX, ATT_HEADS * HD), lambda b: (b, 0)),
        out_shape=jax.ShapeDtypeStruct((R_CTX, ATT_HEADS * HD), BF16),
        compiler_params=_params(("arbitrary",)),
        name="ctx_attn",
    )(sink, aq, ak, av)


def _mix_ffn_body(has_ctx, final, *refs):
    refs = list(refs)
    h_ref = refs.pop(0)
    mixers = []
    for _ in range(3):
        if has_ctx:
            lat_ref, ctx_ref = refs.pop(0), refs.pop(0)
            mixers.append(_pick(lat_ref, ctx_ref))
        else:
            mixers.append(refs.pop(0)[...])
    m_ref, nw_ref, wo_ref, w_in_ref, w_out_ref = refs[:5]
    rest = refs[5:]
    yf, yr, ya = mixers
    m = m_ref[...]
    y = (jnp.dot(yf.astype(BF16), wo_ref[0:256, :], preferred_element_type=F32)
         + jnp.dot(yr, wo_ref[256:512, :], preferred_element_type=F32)
         + jnp.dot(ya, wo_ref[512:1024, :], preferred_element_type=F32))
    h = h_ref[...] + _mod_slice(m, 5) * y
    h = _ffn_half_step(h, m, nw_ref[...], 6, w_in_ref, w_out_ref)
    if final:
        fw_ref, o_ref = rest
        ms = jnp.mean(h * h, axis=-1, keepdims=True)
        h = h * lax.rsqrt(ms + EPS) * fw_ref[...]
    else:
        (o_ref,) = rest
    o_ref[...] = h


def _mix_ffn(h, mixers, mod_all, norm_w, wo, ffn_in, ffn_out, final_w, layer):
    has_ctx = mixers[0][1] is not None
    final = final_w is not None
    nt = NT_ALL if has_ctx else NT_LAT
    in_specs = [_row_spec(D)]
    args = [h]
    for (lat, ctx), width in zip(mixers, (256, 256, 512)):
        if has_ctx:
            in_specs += _lat_or_ctx_specs(width)
            args += [lat, ctx]
        else:
            in_specs += [_row_spec(width)]
            args += [lat]
    in_specs += [_mod_spec(layer), _norm_spec(layer, 2), _const_spec((None, D, D), (layer, 0, 0))]
    in_specs += _ffn_w_specs(layer, 1)
    args += [mod_all, norm_w, wo, ffn_in, ffn_out]
    if final:
        in_specs += [_const_spec((1, D))]
        args += [final_w]
    return pl.pallas_call(
        functools.partial(_mix_ffn_body, has_ctx, final),
        grid=(nt,),
        in_specs=in_specs,
        out_specs=_row_spec(D),
        out_shape=jax.ShapeDtypeStruct((nt * TM, D), F32),
        compiler_params=_params(("arbitrary",)),
        name="mix_ffn_final" if final else "mix_ffn",
    )(*args)


def kernel(x, c, ctx, c_ctx, norm_w, w_ada, b_ada, ffn_w_in, ffn_w_out, w_in, w_o, ret_decay, ret_gn_w,
           attn_sink, final_norm_w):
    assert x.shape == (B, S, D) and ctx.shape == (B, CTX, D)
    cos_t, sin_t = _rope_tables()
    chan, w1, twr, twi, w2, wc = _fourier_tables()

    cond_t = jnp.zeros((D, 8), F32).at[:, 0:B].set(c.T).at[:, B].set(c_ctx)
    mod_all = _ada(cond_t, w_ada, b_ada).reshape(DEPTH, 8, 1, N_MOD * D)

    ffn_in = ffn_w_in.astype(BF16)
    ffn_out = ffn_w_out.astype(BF16)
    w_in_b = w_in.astype(BF16)
    w_o_b = w_o.astype(BF16)
    norm4 = norm_w.reshape(DEPTH, 3, 1, D)
    gn3 = ret_gn_w.reshape(DEPTH, 1, 256)
    dec = jnp.repeat(ret_decay.astype(F32), HD, axis=2)
    sink = attn_sink.astype(F32)

    h = None
    out = None
    for layer in range(DEPTH):
        last = layer == DEPTH - 1
        if layer == 0:
            h = _ffn_first(x.reshape(R_LAT, D), ctx.reshape(R_CTX, D), mod_all, norm4, ffn_in, ffn_out, layer)
        else:
            h = _ffn(h, mod_all, norm4, ffn_in, ffn_out, layer)
        gr, gi, rq, rk, rv, rg, aq, ak, av = _inproj(h, mod_all, norm4, w_in_b, cos_t, sin_t, chan, layer)

        yf = _fft2(w2, twr, twi, _fft1(w1, gr, gi))
        sf, sb = _ret_state(dec, rk, rv, layer)
        yr = _ret_out(dec, gn3, rq, rk, rv, rg, sf, sb, layer, ctx=False)
        ya = _attn_lat(sink, aq, ak, av, layer)
        if last:
            yfc = yrc = yac = None
        else:
            yfc = _fft_ctx(wc, gr, gi)
            yrc = _ret_out(dec, gn3, rq, rk, rv, rg, sf, sb, layer, ctx=True)
            yac = _attn_ctx(sink, aq, ak, av, layer)

        res = _mix_ffn(h, [(yf, yfc), (yr, yrc), (ya, yac)], mod_all, norm4, w_o_b, ffn_in, ffn_out,
                       final_norm_w.reshape(1, D) if last else None, layer)
        if last:
            out = res
        else:
            h = res
    return out.reshape(B, S, D)
```

```python
import functools
import math

import numpy as np
import jax
import jax.numpy as jnp
from jax import lax
from jax.experimental import pallas as pl
from jax.experimental.pallas import tpu as pltpu

F32 = jnp.float32
BF16 = jnp.bfloat16

D = 1024
B = 2
S = 8192
DEPTH = 2
GRID_W = 64
CTX = 256
HD = 64
N_MOD = 9
D_FF = 2816
D_IN = 2048
EPS = 1e-6
NEG_INF = -1e30
ROPE_BASE = 10000.0
ROPE_PAIRS = 16
RET_HEADS = 4
ATT_HEADS = 8
ATT_KV = 2
ATT_GROUP = ATT_HEADS // ATT_KV
ATT_STACK = 2
KV_W = ATT_KV * HD
LOG2E = math.log2(math.e)
WINDOW = 128

R_LAT = B * S
R_CTX = B * CTX
R_ALL = R_LAT + R_CTX
TM = 512
NT_LAT = R_LAT // TM
NT_ALL = R_ALL // TM
CH = 128
NCH_LAT = S // CH
NCH_CTX = CTX // CH
MXU_N = 256
FF_CHUNKS = ((0, 5 * MXU_N), (5 * MXU_N, D_FF))
FFT_L1 = 64
FFT_L2 = 128
VMEM_LIMIT = 56 * 1024 * 1024


def _params(sem, vmem=VMEM_LIMIT):
    return pltpu.CompilerParams(dimension_semantics=sem, vmem_limit_bytes=vmem)


def _const_spec(shape, index=None):
    if index is None:
        index = (0,) * len(shape)
    return pl.BlockSpec(shape, lambda *_: index, pipeline_mode=pl.Buffered(1))


def _rope_tables():
    rows = S // GRID_W
    row = np.repeat(np.arange(rows, dtype=np.float64), GRID_W)
    col = np.tile(np.arange(GRID_W, dtype=np.float64), rows)
    inv = ROPE_BASE ** (-np.arange(ROPE_PAIRS, dtype=np.float64) / ROPE_PAIRS)
    ar = row[:, None] * inv
    ac = col[:, None] * inv
    cos64 = np.concatenate([np.cos(ar), np.cos(ar), np.cos(ac), np.cos(ac)], axis=1)
    sin64 = np.concatenate([-np.sin(ar), np.sin(ar), -np.sin(ac), np.sin(ac)], axis=1)
    cos = np.concatenate([cos64, cos64], axis=1)
    sin = np.concatenate([sin64, sin64], axis=1)
    cos = np.concatenate([cos, np.ones((TM, 128))], axis=0)
    sin = np.concatenate([sin, np.zeros((TM, 128))], axis=0)
    return jnp.asarray(cos, F32), jnp.asarray(sin, F32)


def _dft_cs(n):
    k = np.arange(n)
    ang = 2.0 * np.pi * ((k[:, None] * k[None, :]) % n) / n
    return np.cos(ang), np.sin(ang)


def _fourier_tables():
    c64, s64 = _dft_cs(64)
    eye4 = np.eye(4)
    chan = np.concatenate([np.kron(eye4, c64), -np.kron(eye4, s64)], axis=1)
    w1 = np.block([[c64, s64], [-s64, c64]])
    p1 = np.arange(FFT_L1)[:, None]
    l2 = np.arange(FFT_L2)[None, :]
    ang = 2.0 * np.pi * ((p1 * l2) % S) / S
    twr = np.repeat(np.cos(ang)[:, :, None], 128, axis=2)
    twi = np.repeat(np.sin(ang)[:, :, None], 128, axis=2)
    c128, s128 = _dft_cs(128)
    w2 = np.concatenate([c128, s128], axis=1) / math.sqrt(S * 64.0)
    c256, s256 = _dft_cs(CTX)
    wc = np.concatenate([c256, s256], axis=1) / math.sqrt(CTX * 64.0)
    f = lambda a: jnp.asarray(a, F32)
    return (f(chan).astype(BF16), f(w1).astype(BF16), f(twr), f(twi), f(w2).astype(BF16), f(wc).astype(BF16))


def _mod_slice(m, i):
    return m[:, i * D:(i + 1) * D]


def _rms_mod(h, nw, shift, scale):
    ms = jnp.mean(h * h, axis=-1, keepdims=True)
    y = h * lax.rsqrt(ms + EPS) * nw
    return y * (1.0 + scale) + shift


def _swiglu(u, w_in_ref, w_out_ref):
    y = None
    for lo, hi in FF_CHUNKS:
        g = jnp.dot(u, w_in_ref[:, lo:hi], preferred_element_type=F32)
        up = jnp.dot(u, w_in_ref[:, D_FF + lo:D_FF + hi], preferred_element_type=F32)
        a = (g * jax.nn.sigmoid(g) * up).astype(BF16)
        t = jnp.dot(a, w_out_ref[lo:hi, :], preferred_element_type=F32)
        y = t if y is None else y + t
    return y


def _ffn_half_step(h, m, nw, mod0, w_in_ref, w_out_ref):
    u = _rms_mod(h, nw, _mod_slice(m, mod0), _mod_slice(m, mod0 + 1)).astype(BF16)
    return h + 0.5 * _mod_slice(m, mod0 + 2) * _swiglu(u, w_in_ref, w_out_ref)


def _log_sigmoid(x):
    return jnp.minimum(x, 0.0) - jnp.log1p(jnp.exp(-jnp.abs(x)))


def _lane_head(width):
    return lax.broadcasted_iota(jnp.int32, (1, width), 1) // HD


ADA_CB = 1152


def _ada_body(ct_ref, w_ref, b_ref, o_ref):
    ct = ct_ref[...]
    s = ct * jax.nn.sigmoid(ct)
    w = w_ref[0]
    rows = [jnp.sum(w * s[:, r:r + 1], axis=0, keepdims=True) for r in range(3)]
    rows.append(jnp.zeros((5, ADA_CB), F32))
    o_ref[0] = jnp.concatenate(rows, axis=0) + b_ref[0]


def _ada(cond_t, w_ada, b_ada):
    nb = (N_MOD * D) // ADA_CB
    return pl.pallas_call(
        _ada_body,
        grid=(DEPTH, nb),
        in_specs=[
            pl.BlockSpec((D, 8), lambda l, j: (0, 0)),
            pl.BlockSpec((1, D, ADA_CB), lambda l, j: (l, 0, j)),
            pl.BlockSpec((1, 1, ADA_CB), lambda l, j: (l, 0, j)),
        ],
        out_specs=pl.BlockSpec((1, 8, ADA_CB), lambda l, j: (l, 0, j)),
        out_shape=jax.ShapeDtypeStruct((DEPTH, 8, N_MOD * D), F32),
        compiler_params=_params(("arbitrary", "arbitrary")),
        name="ada_mod",
    )(cond_t, w_ada, b_ada.reshape(DEPTH, 1, N_MOD * D))


def _mod_spec(layer):
    return pl.BlockSpec((None, None, 1, N_MOD * D),
                        lambda i: (layer, jnp.minimum(i // (NT_LAT // B), B), 0, 0))


def _norm_spec(layer, k):
    return _const_spec((None, None, 1, D), (layer, k, 0, 0))


def _ffn_w_specs(layer, j):
    return [_const_spec((None, None, D, 2 * D_FF), (layer, j, 0, 0)),
            _const_spec((None, None, D_FF, D), (layer, j, 0, 0))]


def _row_spec(width):
    return pl.BlockSpec((TM, width), lambda i: (i, 0))


def _lat_or_ctx_specs(width):
    return [pl.BlockSpec((TM, width), lambda i: (jnp.minimum(i, NT_LAT - 1), 0)),
            pl.BlockSpec((TM, width), lambda i: (0, 0))]


def _pick(lat_ref, ctx_ref):
    return jnp.where(pl.program_id(0) < NT_LAT, lat_ref[...], ctx_ref[...])


def _ffn_first_body(x_ref, c_ref, m_ref, nw_ref, w_in_ref, w_out_ref, o_ref):
    o_ref[...] = _ffn_half_step(_pick(x_ref, c_ref), m_ref[...], nw_ref[...], 0, w_in_ref, w_out_ref)


def _ffn_first(x2d, c2d, mod_all, norm_w, ffn_in, ffn_out, layer):
    return pl.pallas_call(
        _ffn_first_body,
        grid=(NT_ALL,),
        in_specs=_lat_or_ctx_specs(D) + [_mod_spec(layer), _norm_spec(layer, 0)] + _ffn_w_specs(layer, 0),
        out_specs=_row_spec(D),
        out_shape=jax.ShapeDtypeStruct((R_ALL, D), F32),
        compiler_params=_params(("arbitrary",)),
        name="ffn_first",
    )(x2d, c2d, mod_all, norm_w, ffn_in, ffn_out)


def _ffn_body(h_ref, m_ref, nw_ref, w_in_ref, w_out_ref, o_ref):
    o_ref[...] = _ffn_half_step(h_ref[...], m_ref[...], nw_ref[...], 0, w_in_ref, w_out_ref)


def _ffn(h, mod_all, norm_w, ffn_in, ffn_out, layer):
    return pl.pallas_call(
        _ffn_body,
        grid=(NT_ALL,),
        in_specs=[_row_spec(D), _mod_spec(layer), _norm_spec(layer, 0)] + _ffn_w_specs(layer, 0),
        out_specs=_row_spec(D),
        out_shape=jax.ShapeDtypeStruct((R_ALL, D), F32),
        compiler_params=_params(("arbitrary",)),
        name="ffn_pre",
    )(h, mod_all, norm_w, ffn_in, ffn_out)


def _rope(x, cos, sin, lane_first_half):
    outs = []
    for j in range(x.shape[1] // 128):
        xb = x[:, j * 128:(j + 1) * 128]
        fwd = pltpu.roll(xb, 112, axis=1)
        bwd = pltpu.roll(xb, 16, axis=1)
        partner = jnp.where(lane_first_half, fwd, bwd)
        outs.append(xb * cos + partner * sin)
    return outs[0] if len(outs) == 1 else jnp.concatenate(outs, axis=1)


def _inproj_body(h_ref, m_ref, nw_ref, w_ref, cos_ref, sin_ref, chan_ref,
                 gr_ref, gi_ref, rq_ref, rk_ref, rv_ref, rg_ref, aq_ref, ak_ref, av_ref):
    m = m_ref[...]
    u = _rms_mod(h_ref[...], nw_ref[...], _mod_slice(m, 3), _mod_slice(m, 4)).astype(BF16)
    z = jnp.dot(u, w_ref[...], preferred_element_type=F32)
    cos = cos_ref[...]
    sin = sin_ref[...]
    lane = lax.broadcasted_iota(jnp.int32, (1, 128), 1)
    first_half = (lane // ROPE_PAIRS) % 2 == 0
    g = jnp.dot(z[:, 0:256].astype(BF16), chan_ref[...], preferred_element_type=F32)
    gr_ref[...] = g[:, 0:256]
    gi_ref[...] = g[:, 256:512]
    rq_ref[...] = _rope(z[:, 256:512], cos, sin, first_half).astype(BF16)
    rk_ref[...] = (_rope(z[:, 512:768], cos, sin, first_half) * (HD ** -0.5)).astype(BF16)
    rv_ref[...] = z[:, 768:1024].astype(BF16)
    rg_ref[...] = z[:, 1024:1280].astype(BF16)
    aq_ref[...] = (_rope(z[:, 1280:1792], cos, sin, first_half) * (HD ** -0.5 * LOG2E)).astype(BF16)
    ak_ref[...] = _rope(z[:, 1792:1920], cos, sin, first_half).astype(BF16)
    av_ref[...] = z[:, 1920:2048].astype(BF16)


def _rope_block(i):
    return jnp.where(i < NT_LAT, i % (S // TM), S // TM)


def _inproj(h, mod_all, norm_w, w_in, cos_t, sin_t, chan, layer):
    widths = (256, 256, 256, 256, 256, 256, 512, KV_W, KV_W)
    return pl.pallas_call(
        _inproj_body,
        grid=(NT_ALL,),
        in_specs=[
            _row_spec(D),
            _mod_spec(layer),
            _norm_spec(layer, 1),
            _const_spec((None, D, D_IN), (layer, 0, 0)),
            pl.BlockSpec((TM, 128), lambda i: (_rope_block(i), 0)),
            pl.BlockSpec((TM, 128), lambda i: (_rope_block(i), 0)),
            _const_spec((256, 512)),
        ],
        out_specs=[_row_spec(w) for w in widths],
        out_shape=[jax.ShapeDtypeStruct((R_ALL, w), F32 if i < 2 else BF16) for i, w in enumerate(widths)],
        compiler_params=_params(("arbitrary",)),
        name="in_proj",
    )(h, mod_all, norm_w, w_in, cos_t, sin_t, chan)


FFT_SUB = 16


def _fft1_body(w1_ref, xr_ref, xi_ref, z_ref):
    w1 = w1_ref[...]
    for s in range(FFT_SUB):
        x = jnp.concatenate([xr_ref[:, s, :], xi_ref[:, s, :]], axis=0).astype(BF16)
        z = jnp.dot(w1, x, preferred_element_type=F32)
        z_ref[0, :, s, :] = z[0:FFT_L1]
        z_ref[1, :, s, :] = z[FFT_L1:2 * FFT_L1]


def _fft1(w1, gr, gi):
    view = (R_ALL // FFT_L2, FFT_L2 // FFT_SUB, FFT_SUB, 256)
    x_spec = pl.BlockSpec((FFT_L1, None, FFT_SUB, 256), lambda b, j: (b, j, 0, 0))
    return pl.pallas_call(
        _fft1_body,
        grid=(B, FFT_L2 // FFT_SUB),
        in_specs=[_const_spec((128, 128)), x_spec, x_spec],
        out_specs=pl.BlockSpec((None, 2, FFT_L1, FFT_SUB, 256), lambda b, j: (b, 0, 0, j, 0)),
        out_shape=jax.ShapeDtypeStruct((B, 2, FFT_L1, FFT_L2, 256), F32),
        compiler_params=_params(("arbitrary", "arbitrary")),
        name="fft_stage1",
    )(w1, gr.reshape(view), gi.reshape(view))


def _fft2_body(w2_ref, twr_ref, twi_ref, z_ref, o_ref):
    w2 = w2_ref[...]
    for p in range(FFT_SUB):
        zr = z_ref[0, p]
        zi = z_ref[1, p]
        tr = jnp.concatenate([twr_ref[p]] * 2, axis=1)
        ti = jnp.concatenate([twi_ref[p]] * 2, axis=1)
        yr = tr * zr + ti * zi
        yi = tr * zi - ti * zr
        y = jnp.concatenate([yr, yi], axis=0).astype(BF16)
        o_ref[:, p, :] = jnp.dot(w2, y, preferred_element_type=F32)


def _fft2(w2, twr, twi, z):
    out = pl.pallas_call(
        _fft2_body,
        grid=(B, FFT_L1 // FFT_SUB),
        in_specs=[
            _const_spec((128, 256)),
            pl.BlockSpec((FFT_SUB, FFT_L2, 128), lambda b, j: (j, 0, 0)),
            pl.BlockSpec((FFT_SUB, FFT_L2, 128), lambda b, j: (j, 0, 0)),
            pl.BlockSpec((None, 2, FFT_SUB, FFT_L2, 256), lambda b, j: (b, 0, j, 0, 0)),
        ],
        out_specs=pl.BlockSpec((None, FFT_L2, None, FFT_SUB, 256), lambda b, j: (b, 0, j, 0, 0)),
        out_shape=jax.ShapeDtypeStruct((B, FFT_L2, FFT_L1 // FFT_SUB, FFT_SUB, 256), F32),
        compiler_params=_params(("arbitrary", "arbitrary")),
        name="fft_stage2",
    )(w2, twr, twi, z)
    return out.reshape(R_LAT, 256)


def _fft_ctx_body(wc_ref, gr_ref, gi_ref, o_ref):
    x = jnp.concatenate([gr_ref[...], gi_ref[...]], axis=0).astype(BF16)
    o_ref[...] = jnp.dot(wc_ref[...], x, preferred_element_type=F32)


def _fft_ctx(wc, gr, gi):
    blk0 = R_LAT // CTX
    return pl.pallas_call(
        _fft_ctx_body,
        grid=(B,),
        in_specs=[
            _const_spec((CTX, 2 * CTX)),
            pl.BlockSpec((CTX, 256), lambda b: (blk0 + b, 0)),
            pl.BlockSpec((CTX, 256), lambda b: (blk0 + b, 0)),
        ],
        out_specs=pl.BlockSpec((CTX, 256), lambda b: (b, 0)),
        out_shape=jax.ShapeDtypeStruct((R_CTX, 256), F32),
        compiler_params=_params(("arbitrary",)),
        name="fft_ctx",
    )(wc, gr, gi)


RS_CH = 8
RS_STEPS = NCH_LAT // RS_CH
N_STATE = (RS_STEPS + 1) * RS_CH
RO_CH = 8
CTX_ROWBLK = R_LAT // CTX


def _head_block_mask():
    r = lax.broadcasted_iota(jnp.int32, (256, 256), 0) // HD
    c = lax.broadcasted_iota(jnp.int32, (256, 256), 1) // HD
    return r == c


def _ret_state_body(dec_ref, kf_ref, vf_ref, kb_ref, vb_ref, kx_ref, vx_ref, sf_ref, sb_ref, accf, accb):
    t = pl.program_id(1)
    lg = _log_sigmoid(dec_ref[...])
    lgf = lg[0:1]
    lgb = lg[1:2]
    pos = lax.broadcasted_iota(jnp.int32, (CH, 1), 0).astype(F32)
    wf = jnp.exp((CH - 1.0 - pos) * lgf)
    wb = jnp.exp(pos * lgb)
    cdf = jnp.exp(float(CH) * lgf)
    cdb = jnp.exp(float(CH) * lgb)
    mask = _head_block_mask()
    dn = (((0,), (0,)), ((), ()))

    def scan(acc, out_ref, k_ref, v_ref, w, cd, order):
        kvs = {}
        for j in order:
            rows = slice(j * CH, (j + 1) * CH)
            kd = (k_ref[rows, :].astype(F32) * w).astype(BF16)
            kvs[j] = lax.dot_general(kd, v_ref[rows, :], dn, preferred_element_type=F32)
        state = acc[...]
        for j in order:
            out_ref[0, j] = state.astype(BF16)
            state = state * cd + jnp.where(mask, kvs[j], 0.0)
        acc[...] = state

    @pl.when(t == 0)
    def _():
        accf[...] = jnp.zeros_like(accf)
        accb[...] = jnp.zeros_like(accb)
        for j in range(NCH_CTX, RS_CH):
            sf_ref[0, j] = jnp.zeros((256, 256), BF16)
            sb_ref[0, j] = jnp.zeros((256, 256), BF16)
        scan(accf, sf_ref, kx_ref, vx_ref, wf, cdf, list(range(NCH_CTX)))
        scan(accb, sb_ref, kx_ref, vx_ref, wb, cdb, list(reversed(range(NCH_CTX))))

    @pl.when(t > 0)
    def _():
        scan(accf, sf_ref, kf_ref, vf_ref, wf, cdf, list(range(RS_CH)))
        scan(accb, sb_ref, kb_ref, vb_ref, wb, cdb, list(reversed(range(RS_CH))))


def _ret_state(dec, rk, rv, layer):
    rows = RS_CH * CH
    fblk = lambda t: jnp.maximum(t - 1, 0)
    bblk = lambda t: RS_STEPS - jnp.maximum(t, 1)
    fmap = lambda b, t: (b * RS_STEPS + fblk(t), 0)
    bmap = lambda b, t: (b * RS_STEPS + bblk(t), 0)
    xmap = lambda b, t: (CTX_ROWBLK + b, 0)
    return pl.pallas_call(
        _ret_state_body,
        grid=(B, RS_STEPS + 1),
        in_specs=[
            pl.BlockSpec((None, 2, 256), lambda b, t: (layer, 0, 0)),
            pl.BlockSpec((rows, 256), fmap),
            pl.BlockSpec((rows, 256), fmap),
            pl.BlockSpec((rows, 256), bmap),
            pl.BlockSpec((rows, 256), bmap),
            pl.BlockSpec((CTX, 256), xmap),
            pl.BlockSpec((CTX, 256), xmap),
        ],
        out_specs=[
            pl.BlockSpec((1, RS_CH, 256, 256), lambda b, t: (b, jnp.where(t == 0, RS_STEPS, t - 1), 0, 0)),
            pl.BlockSpec((1, RS_CH, 256, 256), lambda b, t: (b, jnp.where(t == 0, RS_STEPS, RS_STEPS - t), 0, 0)),
        ],
        out_shape=[jax.ShapeDtypeStruct((B, N_STATE, 256, 256), BF16)] * 2,
        scratch_shapes=[pltpu.VMEM((256, 256), F32), pltpu.VMEM((256, 256), F32)],
        compiler_params=_params(("arbitrary", "arbitrary")),
        name="ret_state",
    )(dec, rk, rv, rk, rv, rk, rv)


def _split_dot(x, w):
    hi = x.astype(BF16)
    lo = (x - hi.astype(F32)).astype(BF16)
    return jnp.dot(hi, w, preferred_element_type=F32) + jnp.dot(lo, w, preferred_element_type=F32)


def _ret_out_body(nck, dec_ref, gn_ref, q_ref, k_ref, v_ref, g_ref, sf_ref, sb_ref, o_ref):
    lg = _log_sigmoid(dec_ref[...])
    lgf = lg[0:1]
    lgb = lg[1:2]
    pos = lax.broadcasted_iota(jnp.int32, (CH, 1), 0).astype(F32)
    wqf = jnp.exp((pos + 1.0) * lgf)
    wqb = jnp.exp((float(CH) - pos) * lgb)
    ri = lax.broadcasted_iota(jnp.int32, (CH, CH), 0)
    ci = lax.broadcasted_iota(jnp.int32, (CH, CH), 1)
    diff = (ri - ci).astype(F32)
    decays = []
    for h in range(RET_HEADS):
        lgf_h = lgf[:, h * HD:h * HD + 1]
        lgb_h = lgb[:, h * HD:h * HD + 1]
        decays.append(jnp.where(ri >= ci, jnp.exp(jnp.maximum(diff, 0.0) * lgf_h), 0.0)
                      + jnp.where(ci >= ri, jnp.exp(jnp.maximum(-diff, 0.0) * lgb_h), 0.0))
    decay = jnp.concatenate(decays, axis=0)
    lane_head = _lane_head(256)
    avg = jnp.where(_head_block_mask(), 1.0 / HD, 0.0).astype(BF16)
    gn = gn_ref[...]
    dn = (((1,), (1,)), ((), ()))
    chunk_rows = [slice(c * CH, (c + 1) * CH) for c in range(nck)]
    cross, scores = [], []
    for c, rows in enumerate(chunk_rows):
        qb16 = q_ref[rows, :]
        q = qb16.astype(F32)
        cross.append(jnp.dot((q * wqf).astype(BF16), sf_ref[0, c], preferred_element_type=F32)
                     + jnp.dot((q * wqb).astype(BF16), sb_ref[0, c], preferred_element_type=F32))
        qs = jnp.concatenate([jnp.where(lane_head == h, qb16, jnp.zeros_like(qb16))
                              for h in range(RET_HEADS)], axis=0)
        scores.append(lax.dot_general(qs, k_ref[rows, :], dn, preferred_element_type=F32))
    outs = []
    for c, rows in enumerate(chunk_rows):
        p = (scores[c] * decay).astype(BF16)
        of = jnp.dot(p, v_ref[rows, :], preferred_element_type=F32)
        o = cross[c]
        for h in range(RET_HEADS):
            o = o + jnp.where(lane_head == h, of[h * CH:(h + 1) * CH], 0.0)
        outs.append(o)
    o = jnp.concatenate(outs, axis=0)
    mu = _split_dot(o, avg)
    cen = o - mu
    var = _split_dot(cen * cen, avg)
    on = cen * lax.rsqrt(var + EPS) * gn
    g = g_ref[...].astype(F32)
    o_ref[...] = (g * jax.nn.sigmoid(g) * on).astype(BF16)


def _ret_out(dec, gn, rq, rk, rv, rg, sf, sb, layer, ctx):
    nck = NCH_CTX if ctx else RO_CH
    rows = nck * CH
    if ctx:
        grid = (B, 1)
        rmap = lambda b, u: (CTX_ROWBLK + b, 0)
        omap = lambda b, u: (b, 0)
        smap = lambda b, u: (b, NCH_LAT // NCH_CTX, 0, 0)
        out_rows = R_CTX
    else:
        grid = (B, NCH_LAT // RO_CH)
        rmap = lambda b, u: (b * (NCH_LAT // RO_CH) + u, 0)
        omap = rmap
        smap = lambda b, u: (b, u, 0, 0)
        out_rows = R_LAT
    return pl.pallas_call(
        functools.partial(_ret_out_body, nck),
        grid=grid,
        in_specs=[
            pl.BlockSpec((None, 2, 256), lambda b, u: (layer, 0, 0)),
            pl.BlockSpec((None, 1, 256), lambda b, u: (layer, 0, 0)),
            pl.BlockSpec((rows, 256), rmap),
            pl.BlockSpec((rows, 256), rmap),
            pl.BlockSpec((rows, 256), rmap),
            pl.BlockSpec((rows, 256), rmap),
            pl.BlockSpec((1, nck, 256, 256), smap),
            pl.BlockSpec((1, nck, 256, 256), smap),
        ],
        out_specs=pl.BlockSpec((rows, 256), omap),
        out_shape=jax.ShapeDtypeStruct((out_rows, 256), BF16),
        compiler_params=_params(("arbitrary", "arbitrary")),
        name="ret_out_ctx" if ctx else "ret_out",
    )(dec, gn, rq, rk, rv, rg, sf, sb)


def _tile_head(x_all, hk):
    own = jnp.where(_lane_head(KV_W) == hk, x_all, jnp.zeros_like(x_all))
    both = own + pltpu.roll(own.astype(F32), HD, axis=1).astype(BF16)
    return jnp.concatenate([both, both], axis=1)


def _gqa_groups(groups):
    lane_head = _lane_head(256)
    chains = []
    for gi, (q_grp, kt, vt, sinks, mask_fn) in enumerate(groups):
        for h0 in range(0, ATT_GROUP, ATT_STACK):
            chains.append((gi, tuple(range(h0, h0 + ATT_STACK)), q_grp, kt, vt, sinks, mask_fn))
    outs = [None] * len(groups)
    for gi, heads, q_grp, kt, vt, sinks, mask_fn in chains:
        m = q_grp.shape[0]
        qs = jnp.concatenate([jnp.where(lane_head == g, q_grp, jnp.zeros_like(q_grp))
                              for g in heads], axis=0)
        s = lax.dot_general(qs, kt, (((1,), (1,)), ((), ())), preferred_element_type=F32)
        sink = jnp.concatenate([jnp.full((m, 1), sinks[g] * LOG2E, F32) for g in heads], axis=0)
        s = mask_fn(s)
        mx = jnp.maximum(jnp.max(s, axis=1, keepdims=True), sink)
        p = jnp.exp2(s - mx)
        inv = 1.0 / (jnp.sum(p, axis=1, keepdims=True) + jnp.exp2(sink - mx))
        of = jnp.dot(p.astype(BF16), vt, preferred_element_type=F32) * inv
        for i, g in enumerate(heads):
            t = jnp.where(lane_head == g, of[i * m:(i + 1) * m], 0.0)
            outs[gi] = t if outs[gi] is None else outs[gi] + t
    return outs


def _band_mask_fn(lo_cols, lo_ok, hi_cols, hi_ok, width):
    def mask_fn(s):
        parts = []
        col = 0
        while col < width:
            piece = s[:, col:col + CH]
            if col == lo_cols:
                piece = jnp.where(lo_ok, piece, NEG_INF)
            elif col == hi_cols:
                piece = jnp.where(hi_ok, piece, NEG_INF)
            parts.append(piece)
            col += CH
        return jnp.concatenate(parts, axis=1)
    return mask_fn


def _attn_lat_body(layer, sink_ref, q_ref, kp_ref, kc_ref, kn_ref, vp_ref, vc_ref, vn_ref, kx_ref, vx_ref, o_ref):
    n2 = pl.program_id(1)
    k_all = jnp.concatenate([kp_ref[...], kc_ref[...], kx_ref[...], kn_ref[...]], axis=0)
    v_all = jnp.concatenate([vp_ref[...], vc_ref[...], vx_ref[...], vn_ref[...]], axis=0)
    r = lax.broadcasted_iota(jnp.int32, (ATT_STACK * CH, CH), 0) & (CH - 1)
    c = lax.broadcasted_iota(jnp.int32, (ATT_STACK * CH, CH), 1)
    below = c >= r
    above = c <= r
    span = 3 * CH + CTX
    mask_a = _band_mask_fn(0, below & (n2 > 0), 2 * CH, above, span)
    mask_b = _band_mask_fn(0, below, 2 * CH + CTX, above & (n2 < NCH_LAT // 2 - 1), span)
    groups = []
    for hk in range(ATT_KV):
        sinks = [sink_ref[layer, hk * ATT_GROUP + g] for g in range(ATT_GROUP)]
        kt = _tile_head(k_all, hk)
        vt = _tile_head(v_all, hk)
        cols = slice(hk * 256, (hk + 1) * 256)
        groups.append((q_ref[0:CH, cols], kt[0:span], vt[0:span], sinks, mask_a))
        groups.append((q_ref[CH:2 * CH, cols], kt[CH:CH + span], vt[CH:CH + span], sinks, mask_b))
    outs = _gqa_groups(groups)
    for hk in range(ATT_KV):
        cols = slice(hk * 256, (hk + 1) * 256)
        o_ref[0:CH, cols] = outs[2 * hk].astype(BF16)
        o_ref[CH:2 * CH, cols] = outs[2 * hk + 1].astype(BF16)


def _attn_lat(sink, aq, ak, av, layer):
    nq = NCH_LAT // 2
    qmap = lambda b, n: (b * nq + n, 0)
    pmap = lambda b, n: (b * NCH_LAT + jnp.maximum(2 * n - 1, 0), 0)
    nmap = lambda b, n: (b * NCH_LAT + jnp.minimum(2 * n + 2, NCH_LAT - 1), 0)
    xmap = lambda b, n: (CTX_ROWBLK + b, 0)
    kv1 = lambda m: pl.BlockSpec((CH, KV_W), m)
    kv2 = pl.BlockSpec((2 * CH, KV_W), qmap)
    ctx_spec = pl.BlockSpec((CTX, KV_W), xmap)
    return pl.pallas_call(
        functools.partial(_attn_lat_body, layer),
        grid=(B, nq),
        in_specs=[
            pl.BlockSpec(memory_space=pltpu.SMEM),
            pl.BlockSpec((2 * CH, ATT_HEADS * HD), qmap),
            kv1(pmap), kv2, kv1(nmap),
            kv1(pmap), kv2, kv1(nmap),
            ctx_spec, ctx_spec,
        ],
        out_specs=pl.BlockSpec((2 * CH, ATT_HEADS * HD), qmap),
        out_shape=jax.ShapeDtypeStruct((R_LAT, ATT_HEADS * HD), BF16),
        compiler_params=_params(("arbitrary", "arbitrary")),
        name="swa_attn",
    )(sink, aq, ak, ak, ak, av, av, av, ak, av)


def _attn_ctx_body(layer, sink_ref, q_ref, kx_ref, vx_ref, o_ref):
    groups = []
    for hk in range(ATT_KV):
        sinks = [sink_ref[layer, hk * ATT_GROUP + g] for g in range(ATT_GROUP)]
        groups.append((q_ref[:, hk * 256:(hk + 1) * 256], _tile_head(kx_ref[...], hk),
                       _tile_head(vx_ref[...], hk), sinks, lambda s: s))
    outs = _gqa_groups(groups)
    for hk in range(ATT_KV):
        o_ref[:, hk * 256:(hk + 1) * 256] = outs[hk].astype(BF16)


def _attn_ctx(sink, aq, ak, av, layer):
    xmap = lambda b: (CTX_ROWBLK + b, 0)
    return pl.pallas_call(
        functools.partial(_attn_ctx_body, layer),
        grid=(B,),
        in_specs=[
            pl.BlockSpec(memory_space=pltpu.SMEM),
            pl.BlockSpec((CTX, ATT_HEADS * HD), xmap),
            pl.BlockSpec((CTX, KV_W), xmap),
            pl.BlockSpec((CTX, KV_W), xmap),
        ],
        out_specs=pl.BlockSpec((CTX, ATT_HEADS * HD), lambda b: (b, 0)),
        out_shape=jax.ShapeDtypeStruct((R_CTX, ATT_HEADS * HD), BF16),
        compiler_params=_params(("arbitrary",)),
        name="ctx_attn",
    )(sink, aq, ak, av)


def _mix_ffn_body(has_ctx, final, *refs):
    refs = list(refs)
    h_ref = refs.pop(0)
    mixers = []
    for _ in range(3):
        if has_ctx:
            lat_ref, ctx_ref = refs.pop(0), refs.pop(0)
            mixers.append(_pick(lat_ref, ctx_ref))
        else:
            mixers.append(refs.pop(0)[...])
    m_ref, nw_ref, wo_ref, w_in_ref, w_out_ref = refs[:5]
    rest = refs[5:]
    yf, yr, ya = mixers
    m = m_ref[...]
    y = (jnp.dot(yf.astype(BF16), wo_ref[0:256, :], preferred_element_type=F32)
         + jnp.dot(yr, wo_ref[256:512, :], preferred_element_type=F32)
         + jnp.dot(ya, wo_ref[512:1024, :], preferred_element_type=F32))
    h = h_ref[...] + _mod_slice(m, 5) * y
    h = _ffn_half_step(h, m, nw_ref[...], 6, w_in_ref, w_out_ref)
    if final:
        fw_ref, o_ref = rest
        ms = jnp.mean(h * h, axis=-1, keepdims=True)
        h = h * lax.rsqrt(ms + EPS) * fw_ref[...]
    else:
        (o_ref,) = rest
    o_ref[...] = h


def _mix_ffn(h, mixers, mod_all, norm_w, wo, ffn_in, ffn_out, final_w, layer):
    has_ctx = mixers[0][1] is not None
    final = final_w is not None
    nt = NT_ALL if has_ctx else NT_LAT
    in_specs = [_row_spec(D)]
    args = [h]
    for (lat, ctx), width in zip(mixers, (256, 256, 512)):
        if has_ctx:
            in_specs += _lat_or_ctx_specs(width)
            args += [lat, ctx]
        else:
            in_specs += [_row_spec(width)]
            args += [lat]
    in_specs += [_mod_spec(layer), _norm_spec(layer, 2), _const_spec((None, D, D), (layer, 0, 0))]
    in_specs += _ffn_w_specs(layer, 1)
    args += [mod_all, norm_w, wo, ffn_in, ffn_out]
    if final:
        in_specs += [_const_spec((1, D))]
        args += [final_w]
    return pl.pallas_call(
        functools.partial(_mix_ffn_body, has_ctx, final),
        grid=(nt,),
        in_specs=in_specs,
        out_specs=_row_spec(D),
        out_shape=jax.ShapeDtypeStruct((nt * TM, D), F32),
        compiler_params=_params(("arbitrary",)),
        name="mix_ffn_final" if final else "mix_ffn",
    )(*args)


def kernel(x, c, ctx, c_ctx, norm_w, w_ada, b_ada, ffn_w_in, ffn_w_out, w_in, w_o, ret_decay, ret_gn_w,
           attn_sink, final_norm_w):
    assert x.shape == (B, S, D) and ctx.shape == (B, CTX, D)
    cos_t, sin_t = _rope_tables()
    chan, w1, twr, twi, w2, wc = _fourier_tables()

    cond_t = jnp.zeros((D, 8), F32).at[:, 0:B].set(c.T).at[:, B].set(c_ctx)
    mod_all = _ada(cond_t, w_ada, b_ada).reshape(DEPTH, 8, 1, N_MOD * D)

    ffn_in = ffn_w_in.astype(BF16)
    ffn_out = ffn_w_out.astype(BF16)
    w_in_b = w_in.astype(BF16)
    w_o_b = w_o.astype(BF16)
    norm4 = norm_w.reshape(DEPTH, 3, 1, D)
    gn3 = ret_gn_w.reshape(DEPTH, 1, 256)
    dec = jnp.repeat(ret_decay.astype(F32), HD, axis=2)
    sink = attn_sink.astype(F32)

    h = None
    out = None
    for layer in range(DEPTH):
        last = layer == DEPTH - 1
        if layer == 0:
            h = _ffn_first(x.reshape(R_LAT, D), ctx.reshape(R_CTX, D), mod_all, norm4, ffn_in, ffn_out, layer)
        else:
            h = _ffn(h, mod_all, norm4, ffn_in, ffn_out, layer)
        gr, gi, rq, rk, rv, rg, aq, ak, av = _inproj(h, mod_all, norm4, w_in_b, cos_t, sin_t, chan, layer)

        yf = _fft2(w2, twr, twi, _fft1(w1, gr, gi))
        sf, sb = _ret_state(dec, rk, rv, layer)
        yr = _ret_out(dec, gn3, rq, rk, rv, rg, sf, sb, layer, ctx=False)
        ya = _attn_lat(sink, aq, ak, av, layer)
        if last:
            yfc = yrc = yac = None
        else:
            yfc = _fft_ctx(wc, gr, gi)
            yrc = _ret_out(dec, gn3, rq, rk, rv, rg, sf, sb, layer, ctx=True)
            yac = _attn_ctx(sink, aq, ak, av, layer)

        res = _mix_ffn(h, [(yf, yfc), (yr, yrc), (ya, yac)], mod_all, norm4, w_o_b, ffn_in, ffn_out,
                       final_norm_w.reshape(1, D) if last else None, layer)
        if last:
            out = res
        else:
            h = res
    return out.reshape(B, S, D)
```

```python
import functools
import math

import numpy as np
import jax
import jax.numpy as jnp
from jax import lax
from jax.experimental import pallas as pl
from jax.experimental.pallas import tpu as pltpu

F32 = jnp.float32
BF16 = jnp.bfloat16

D = 1024
B = 2
S = 8192
DEPTH = 2
GRID_W = 64
CTX = 256
HD = 64
N_MOD = 9
D_FF = 2816
D_IN = 2048
EPS = 1e-6
NEG_INF = -1e30
ROPE_BASE = 10000.0
ROPE_PAIRS = 16
RET_HEADS = 4
ATT_HEADS = 8
ATT_KV = 2
ATT_GROUP = ATT_HEADS // ATT_KV
ATT_STACK = 4
KV_W = ATT_KV * HD
LOG2E = math.log2(math.e)
WINDOW = 128

R_LAT = B * S
R_CTX = B * CTX
R_ALL = R_LAT + R_CTX
TM = 512
NT_LAT = R_LAT // TM
NT_ALL = R_ALL // TM
CH = 128
NCH_LAT = S // CH
NCH_CTX = CTX // CH
MXU_N = 256
FF_CHUNKS = ((0, 5 * MXU_N), (5 * MXU_N, D_FF))
FFT_L1 = 64
FFT_L2 = 128
VMEM_LIMIT = 56 * 1024 * 1024


def _params(sem, vmem=VMEM_LIMIT):
    return pltpu.CompilerParams(dimension_semantics=sem, vmem_limit_bytes=vmem)


def _const_spec(shape, index=None):
    if index is None:
        index = (0,) * len(shape)
    return pl.BlockSpec(shape, lambda *_: index, pipeline_mode=pl.Buffered(1))


def _rope_tables():
    rows = S // GRID_W
    row = np.repeat(np.arange(rows, dtype=np.float64), GRID_W)
    col = np.tile(np.arange(GRID_W, dtype=np.float64), rows)
    inv = ROPE_BASE ** (-np.arange(ROPE_PAIRS, dtype=np.float64) / ROPE_PAIRS)
    ar = row[:, None] * inv
    ac = col[:, None] * inv
    cos64 = np.concatenate([np.cos(ar), np.cos(ar), np.cos(ac), np.cos(ac)], axis=1)
    sin64 = np.concatenate([-np.sin(ar), np.sin(ar), -np.sin(ac), np.sin(ac)], axis=1)
    cos = np.concatenate([cos64, cos64], axis=1)
    sin = np.concatenate([sin64, sin64], axis=1)
    cos = np.concatenate([cos, np.ones((TM, 128))], axis=0)
    sin = np.concatenate([sin, np.zeros((TM, 128))], axis=0)
    return jnp.asarray(cos, F32), jnp.asarray(sin, F32)


def _dft_cs(n):
    k = np.arange(n)
    ang = 2.0 * np.pi * ((k[:, None] * k[None, :]) % n) / n
    return np.cos(ang), np.sin(ang)


def _fourier_tables():
    c64, s64 = _dft_cs(64)
    eye4 = np.eye(4)
    chan = np.concatenate([np.kron(eye4, c64), -np.kron(eye4, s64)], axis=1)
    w1 = np.block([[c64, s64], [-s64, c64]])
    p1 = np.arange(FFT_L1)[:, None]
    l2 = np.arange(FFT_L2)[None, :]
    ang = 2.0 * np.pi * ((p1 * l2) % S) / S
    twr = np.repeat(np.cos(ang)[:, :, None], 128, axis=2)
    twi = np.repeat(np.sin(ang)[:, :, None], 128, axis=2)
    c128, s128 = _dft_cs(128)
    w2 = np.concatenate([c128, s128], axis=1) / math.sqrt(S * 64.0)
    c256, s256 = _dft_cs(CTX)
    wc = np.concatenate([c256, s256], axis=1) / math.sqrt(CTX * 64.0)
    f = lambda a: jnp.asarray(a, F32)
    return (f(chan).astype(BF16), f(w1).astype(BF16), f(twr), f(twi), f(w2).astype(BF16), f(wc).astype(BF16))


def _mod_slice(m, i):
    return m[:, i * D:(i + 1) * D]


def _rms_mod(h, nw, shift, scale):
    ms = jnp.mean(h * h, axis=-1, keepdims=True)
    y = h * lax.rsqrt(ms + EPS) * nw
    return y * (1.0 + scale) + shift


def _swiglu(u, w_in_ref, w_out_ref):
    y = None
    for lo, hi in FF_CHUNKS:
        g = jnp.dot(u, w_in_ref[:, lo:hi], preferred_element_type=F32)
        up = jnp.dot(u, w_in_ref[:, D_FF + lo:D_FF + hi], preferred_element_type=F32)
        a = (g * jax.nn.sigmoid(g) * up).astype(BF16)
        t = jnp.dot(a, w_out_ref[lo:hi, :], preferred_element_type=F32)
        y = t if y is None else y + t
    return y


def _ffn_half_step(h, m, nw, mod0, w_in_ref, w_out_ref):
    u = _rms_mod(h, nw, _mod_slice(m, mod0), _mod_slice(m, mod0 + 1)).astype(BF16)
    return h + 0.5 * _mod_slice(m, mod0 + 2) * _swiglu(u, w_in_ref, w_out_ref)


def _log_sigmoid(x):
    return jnp.minimum(x, 0.0) - jnp.log1p(jnp.exp(-jnp.abs(x)))


def _lane_head(width):
    return lax.broadcasted_iota(jnp.int32, (1, width), 1) // HD


ADA_CB = 1152


def _ada_body(ct_ref, w_ref, b_ref, o_ref):
    ct = ct_ref[...]
    s = ct * jax.nn.sigmoid(ct)
    w = w_ref[0]
    rows = [jnp.sum(w * s[:, r:r + 1], axis=0, keepdims=True) for r in range(3)]
    rows.append(jnp.zeros((5, ADA_CB), F32))
    o_ref[0] = jnp.concatenate(rows, axis=0) + b_ref[0]


def _ada(cond_t, w_ada, b_ada):
    nb = (N_MOD * D) // ADA_CB
    return pl.pallas_call(
        _ada_body,
        grid=(DEPTH, nb),
        in_specs=[
            pl.BlockSpec((D, 8), lambda l, j: (0, 0)),
            pl.BlockSpec((1, D, ADA_CB), lambda l, j: (l, 0, j)),
            pl.BlockSpec((1, 1, ADA_CB), lambda l, j: (l, 0, j)),
        ],
        out_specs=pl.BlockSpec((1, 8, ADA_CB), lambda l, j: (l, 0, j)),
        out_shape=jax.ShapeDtypeStruct((DEPTH, 8, N_MOD * D), F32),
        compiler_params=_params(("arbitrary", "arbitrary")),
        name="ada_mod",
    )(cond_t, w_ada, b_ada.reshape(DEPTH, 1, N_MOD * D))


def _mod_spec(layer):
    return pl.BlockSpec((None, None, 1, N_MOD * D),
                        lambda i: (layer, jnp.minimum(i // (NT_LAT // B), B), 0, 0))


def _norm_spec(layer, k):
    return _const_spec((None, None, 1, D), (layer, k, 0, 0))


def _ffn_w_specs(layer, j):
    return [_const_spec((None, None, D, 2 * D_FF), (layer, j, 0, 0)),
            _const_spec((None, None, D_FF, D), (layer, j, 0, 0))]


def _row_spec(width):
    return pl.BlockSpec((TM, width), lambda i: (i, 0))


def _lat_or_ctx_specs(width):
    return [pl.BlockSpec((TM, width), lambda i: (jnp.minimum(i, NT_LAT - 1), 0)),
            pl.BlockSpec((TM, width), lambda i: (0, 0))]


def _pick(lat_ref, ctx_ref):
    return jnp.where(pl.program_id(0) < NT_LAT, lat_ref[...], ctx_ref[...])


def _ffn_first_body(x_ref, c_ref, m_ref, nw_ref, w_in_ref, w_out_ref, o_ref):
    o_ref[...] = _ffn_half_step(_pick(x_ref, c_ref), m_ref[...], nw_ref[...], 0, w_in_ref, w_out_ref)


def _ffn_first(x2d, c2d, mod_all, norm_w, ffn_in, ffn_out, layer):
    return pl.pallas_call(
        _ffn_first_body,
        grid=(NT_ALL,),
        in_specs=_lat_or_ctx_specs(D) + [_mod_spec(layer), _norm_spec(layer, 0)] + _ffn_w_specs(layer, 0),
        out_specs=_row_spec(D),
        out_shape=jax.ShapeDtypeStruct((R_ALL, D), F32),
        compiler_params=_params(("arbitrary",)),
        name="ffn_first",
    )(x2d, c2d, mod_all, norm_w, ffn_in, ffn_out)


def _ffn_body(h_ref, m_ref, nw_ref, w_in_ref, w_out_ref, o_ref):
    o_ref[...] = _ffn_half_step(h_ref[...], m_ref[...], nw_ref[...], 0, w_in_ref, w_out_ref)


def _ffn(h, mod_all, norm_w, ffn_in, ffn_out, layer):
    return pl.pallas_call(
        _ffn_body,
        grid=(NT_ALL,),
        in_specs=[_row_spec(D), _mod_spec(layer), _norm_spec(layer, 0)] + _ffn_w_specs(layer, 0),
        out_specs=_row_spec(D),
        out_shape=jax.ShapeDtypeStruct((R_ALL, D), F32),
        compiler_params=_params(("arbitrary",)),
        name="ffn_pre",
    )(h, mod_all, norm_w, ffn_in, ffn_out)


def _rope(x, cos, sin, lane_first_half):
    outs = []
    for j in range(x.shape[1] // 128):
        xb = x[:, j * 128:(j + 1) * 128]
        fwd = pltpu.roll(xb, 112, axis=1)
        bwd = pltpu.roll(xb, 16, axis=1)
        partner = jnp.where(lane_first_half, fwd, bwd)
        outs.append(xb * cos + partner * sin)
    return outs[0] if len(outs) == 1 else jnp.concatenate(outs, axis=1)


def _inproj_body(h_ref, m_ref, nw_ref, w_ref, cos_ref, sin_ref, chan_ref,
                 gr_ref, gi_ref, rq_ref, rk_ref, rv_ref, rg_ref, aq_ref, ak_ref, av_ref):
    m = m_ref[...]
    u = _rms_mod(h_ref[...], nw_ref[...], _mod_slice(m, 3), _mod_slice(m, 4)).astype(BF16)
    z = jnp.dot(u, w_ref[...], preferred_element_type=F32)
    cos = cos_ref[...]
    sin = sin_ref[...]
    lane = lax.broadcasted_iota(jnp.int32, (1, 128), 1)
    first_half = (lane // ROPE_PAIRS) % 2 == 0
    g = jnp.dot(z[:, 0:256].astype(BF16), chan_ref[...], preferred_element_type=F32)
    gr_ref[...] = g[:, 0:256]
    gi_ref[...] = g[:, 256:512]
    rq_ref[...] = _rope(z[:, 256:512], cos, sin, first_half).astype(BF16)
    rk_ref[...] = (_rope(z[:, 512:768], cos, sin, first_half) * (HD ** -0.5)).astype(BF16)
    rv_ref[...] = z[:, 768:1024].astype(BF16)
    rg_ref[...] = z[:, 1024:1280].astype(BF16)
    aq_ref[...] = (_rope(z[:, 1280:1792], cos, sin, first_half) * (HD ** -0.5 * LOG2E)).astype(BF16)
    ak_ref[...] = _rope(z[:, 1792:1920], cos, sin, first_half).astype(BF16)
    av_ref[...] = z[:, 1920:2048].astype(BF16)


def _rope_block(i):
    return jnp.where(i < NT_LAT, i % (S // TM), S // TM)


def _inproj(h, mod_all, norm_w, w_in, cos_t, sin_t, chan, layer):
    widths = (256, 256, 256, 256, 256, 256, 512, KV_W, KV_W)
    return pl.pallas_call(
        _inproj_body,
        grid=(NT_ALL,),
        in_specs=[
            _row_spec(D),
            _mod_spec(layer),
            _norm_spec(layer, 1),
            _const_spec((None, D, D_IN), (layer, 0, 0)),
            pl.BlockSpec((TM, 128), lambda i: (_rope_block(i), 0)),
            pl.BlockSpec((TM, 128), lambda i: (_rope_block(i), 0)),
            _const_spec((256, 512)),
        ],
        out_specs=[_row_spec(w) for w in widths],
        out_shape=[jax.ShapeDtypeStruct((R_ALL, w), F32 if i < 2 else BF16) for i, w in enumerate(widths)],
        compiler_params=_params(("arbitrary",)),
        name="in_proj",
    )(h, mod_all, norm_w, w_in, cos_t, sin_t, chan)


FFT_SUB = 16


def _fft1_body(w1_ref, xr_ref, xi_ref, z_ref):
    w1 = w1_ref[...]
    for s in range(FFT_SUB):
        x = jnp.concatenate([xr_ref[:, s, :], xi_ref[:, s, :]], axis=0).astype(BF16)
        z = jnp.dot(w1, x, preferred_element_type=F32)
        z_ref[0, :, s, :] = z[0:FFT_L1]
        z_ref[1, :, s, :] = z[FFT_L1:2 * FFT_L1]


def _fft1(w1, gr, gi):
    view = (R_ALL // FFT_L2, FFT_L2 // FFT_SUB, FFT_SUB, 256)
    x_spec = pl.BlockSpec((FFT_L1, None, FFT_SUB, 256), lambda b, j: (b, j, 0, 0))
    return pl.pallas_call(
        _fft1_body,
        grid=(B, FFT_L2 // FFT_SUB),
        in_specs=[_const_spec((128, 128)), x_spec, x_spec],
        out_specs=pl.BlockSpec((None, 2, FFT_L1, FFT_SUB, 256), lambda b, j: (b, 0, 0, j, 0)),
        out_shape=jax.ShapeDtypeStruct((B, 2, FFT_L1, FFT_L2, 256), F32),
        compiler_params=_params(("arbitrary", "arbitrary")),
        name="fft_stage1",
    )(w1, gr.reshape(view), gi.reshape(view))


def _fft2_body(w2_ref, twr_ref, twi_ref, z_ref, o_ref):
    w2 = w2_ref[...]
    for p in range(FFT_SUB):
        zr = z_ref[0, p]
        zi = z_ref[1, p]
        tr = jnp.concatenate([twr_ref[p]] * 2, axis=1)
        ti = jnp.concatenate([twi_ref[p]] * 2, axis=1)
        yr = tr * zr + ti * zi
        yi = tr * zi - ti * zr
        y = jnp.concatenate([yr, yi], axis=0).astype(BF16)
        o_ref[:, p, :] = jnp.dot(w2, y, preferred_element_type=F32)


def _fft2(w2, twr, twi, z):
    out = pl.pallas_call(
        _fft2_body,
        grid=(B, FFT_L1 // FFT_SUB),
        in_specs=[
            _const_spec((128, 256)),
            pl.BlockSpec((FFT_SUB, FFT_L2, 128), lambda b, j: (j, 0, 0)),
            pl.BlockSpec((FFT_SUB, FFT_L2, 128), lambda b, j: (j, 0, 0)),
            pl.BlockSpec((None, 2, FFT_SUB, FFT_L2, 256), lambda b, j: (b, 0, j, 0, 0)),
        ],
        out_specs=pl.BlockSpec((None, FFT_L2, None, FFT_SUB, 256), lambda b, j: (b, 0, j, 0, 0)),
        out_shape=jax.ShapeDtypeStruct((B, FFT_L2, FFT_L1 // FFT_SUB, FFT_SUB, 256), F32),
        compiler_params=_params(("arbitrary", "arbitrary")),
        name="fft_stage2",
    )(w2, twr, twi, z)
    return out.reshape(R_LAT, 256)


def _fft_ctx_body(wc_ref, gr_ref, gi_ref, o_ref):
    x = jnp.concatenate([gr_ref[...], gi_ref[...]], axis=0).astype(BF16)
    o_ref[...] = jnp.dot(wc_ref[...], x, preferred_element_type=F32)


def _fft_ctx(wc, gr, gi):
    blk0 = R_LAT // CTX
    return pl.pallas_call(
        _fft_ctx_body,
        grid=(B,),
        in_specs=[
            _const_spec((CTX, 2 * CTX)),
            pl.BlockSpec((CTX, 256), lambda b: (blk0 + b, 0)),
            pl.BlockSpec((CTX, 256), lambda b: (blk0 + b, 0)),
        ],
        out_specs=pl.BlockSpec((CTX, 256), lambda b: (b, 0)),
        out_shape=jax.ShapeDtypeStruct((R_CTX, 256), F32),
        compiler_params=_params(("arbitrary",)),
        name="fft_ctx",
    )(wc, gr, gi)


RS_CH = 8
RS_STEPS = NCH_LAT // RS_CH
N_STATE = (RS_STEPS + 1) * RS_CH
RO_CH = 8
CTX_ROWBLK = R_LAT // CTX


def _head_block_mask():
    r = lax.broadcasted_iota(jnp.int32, (256, 256), 0) // HD
    c = lax.broadcasted_iota(jnp.int32, (256, 256), 1) // HD
    return r == c


def _ret_state_body(dec_ref, kf_ref, vf_ref, kb_ref, vb_ref, kx_ref, vx_ref, sf_ref, sb_ref, accf, accb):
    t = pl.program_id(1)
    lg = _log_sigmoid(dec_ref[...])
    lgf = lg[0:1]
    lgb = lg[1:2]
    pos = lax.broadcasted_iota(jnp.int32, (CH, 1), 0).astype(F32)
    wf = jnp.exp((CH - 1.0 - pos) * lgf)
    wb = jnp.exp(pos * lgb)
    cdf = jnp.exp(float(CH) * lgf)
    cdb = jnp.exp(float(CH) * lgb)
    mask = _head_block_mask()
    dn = (((0,), (0,)), ((), ()))

    def scan(acc, out_ref, k_ref, v_ref, w, cd, order):
        kvs = {}
        for j in order:
            rows = slice(j * CH, (j + 1) * CH)
            kd = (k_ref[rows, :].astype(F32) * w).astype(BF16)
            kvs[j] = lax.dot_general(kd, v_ref[rows, :], dn, preferred_element_type=F32)
        state = acc[...]
        for j in order:
            out_ref[0, j] = state.astype(BF16)
            state = state * cd + jnp.where(mask, kvs[j], 0.0)
        acc[...] = state

    @pl.when(t == 0)
    def _():
        accf[...] = jnp.zeros_like(accf)
        accb[...] = jnp.zeros_like(accb)
        for j in range(NCH_CTX, RS_CH):
            sf_ref[0, j] = jnp.zeros((256, 256), BF16)
            sb_ref[0, j] = jnp.zeros((256, 256), BF16)
        scan(accf, sf_ref, kx_ref, vx_ref, wf, cdf, list(range(NCH_CTX)))
        scan(accb, sb_ref, kx_ref, vx_ref, wb, cdb, list(reversed(range(NCH_CTX))))

    @pl.when(t > 0)
    def _():
        scan(accf, sf_ref, kf_ref, vf_ref, wf, cdf, list(range(RS_CH)))
        scan(accb, sb_ref, kb_ref, vb_ref, wb, cdb, list(reversed(range(RS_CH))))


def _ret_state(dec, rk, rv, layer):
    rows = RS_CH * CH
    fblk = lambda t: jnp.maximum(t - 1, 0)
    bblk = lambda t: RS_STEPS - jnp.maximum(t, 1)
    fmap = lambda b, t: (b * RS_STEPS + fblk(t), 0)
    bmap = lambda b, t: (b * RS_STEPS + bblk(t), 0)
    xmap = lambda b, t: (CTX_ROWBLK + b, 0)
    return pl.pallas_call(
        _ret_state_body,
        grid=(B, RS_STEPS + 1),
        in_specs=[
            pl.BlockSpec((None, 2, 256), lambda b, t: (layer, 0, 0)),
            pl.BlockSpec((rows, 256), fmap),
            pl.BlockSpec((rows, 256), fmap),
            pl.BlockSpec((rows, 256), bmap),
            pl.BlockSpec((rows, 256), bmap),
            pl.BlockSpec((CTX, 256), xmap),
            pl.BlockSpec((CTX, 256), xmap),
        ],
        out_specs=[
            pl.BlockSpec((1, RS_CH, 256, 256), lambda b, t: (b, jnp.where(t == 0, RS_STEPS, t - 1), 0, 0)),
            pl.BlockSpec((1, RS_CH, 256, 256), lambda b, t: (b, jnp.where(t == 0, RS_STEPS, RS_STEPS - t), 0, 0)),
        ],
        out_shape=[jax.ShapeDtypeStruct((B, N_STATE, 256, 256), BF16)] * 2,
        scratch_shapes=[pltpu.VMEM((256, 256), F32), pltpu.VMEM((256, 256), F32)],
        compiler_params=_params(("arbitrary", "arbitrary")),
        name="ret_state",
    )(dec, rk, rv, rk, rv, rk, rv)


def _split_dot(x, w):
    hi = x.astype(BF16)
    lo = (x - hi.astype(F32)).astype(BF16)
    return jnp.dot(hi, w, preferred_element_type=F32) + jnp.dot(lo, w, preferred_element_type=F32)


def _ret_out_body(nck, dec_ref, gn_ref, q_ref, k_ref, v_ref, g_ref, sf_ref, sb_ref, o_ref):
    lg = _log_sigmoid(dec_ref[...])
    lgf = lg[0:1]
    lgb = lg[1:2]
    pos = lax.broadcasted_iota(jnp.int32, (CH, 1), 0).astype(F32)
    wqf = jnp.exp((pos + 1.0) * lgf)
    wqb = jnp.exp((float(CH) - pos) * lgb)
    ri = lax.broadcasted_iota(jnp.int32, (CH, CH), 0)
    ci = lax.broadcasted_iota(jnp.int32, (CH, CH), 1)
    diff = (ri - ci).astype(F32)
    decays = []
    for h in range(RET_HEADS):
        lgf_h = lgf[:, h * HD:h * HD + 1]
        lgb_h = lgb[:, h * HD:h * HD + 1]
        decays.append(jnp.where(ri >= ci, jnp.exp(jnp.maximum(diff, 0.0) * lgf_h), 0.0)
                      + jnp.where(ci >= ri, jnp.exp(jnp.maximum(-diff, 0.0) * lgb_h), 0.0))
    decay = jnp.concatenate(decays, axis=0)
    lane_head = _lane_head(256)
    avg = jnp.where(_head_block_mask(), 1.0 / HD, 0.0).astype(BF16)
    gn = gn_ref[...]
    dn = (((1,), (1,)), ((), ()))
    chunk_rows = [slice(c * CH, (c + 1) * CH) for c in range(nck)]
    cross, scores = [], []
    for c, rows in enumerate(chunk_rows):
        qb16 = q_ref[rows, :]
        q = qb16.astype(F32)
        cross.append(jnp.dot((q * wqf).astype(BF16), sf_ref[0, c], preferred_element_type=F32)
                     + jnp.dot((q * wqb).astype(BF16), sb_ref[0, c], preferred_element_type=F32))
        qs = jnp.concatenate([jnp.where(lane_head == h, qb16, jnp.zeros_like(qb16))
                              for h in range(RET_HEADS)], axis=0)
        scores.append(lax.dot_general(qs, k_ref[rows, :], dn, preferred_element_type=F32))
    outs = []
    for c, rows in enumerate(chunk_rows):
        p = (scores[c] * decay).astype(BF16)
        of = jnp.dot(p, v_ref[rows, :], preferred_element_type=F32)
        o = cross[c]
        for h in range(RET_HEADS):
            o = o + jnp.where(lane_head == h, of[h * CH:(h + 1) * CH], 0.0)
        outs.append(o)
    o = jnp.concatenate(outs, axis=0)
    mu = _split_dot(o, avg)
    cen = o - mu
    var = _split_dot(cen * cen, avg)
    on = cen * lax.rsqrt(var + EPS) * gn
    g = g_ref[...].astype(F32)
    o_ref[...] = (g * jax.nn.sigmoid(g) * on).astype(BF16)


def _ret_out(dec, gn, rq, rk, rv, rg, sf, sb, layer, ctx):
    nck = NCH_CTX if ctx else RO_CH
    rows = nck * CH
    if ctx:
        grid = (B, 1)
        rmap = lambda b, u: (CTX_ROWBLK + b, 0)
        omap = lambda b, u: (b, 0)
        smap = lambda b, u: (b, NCH_LAT // NCH_CTX, 0, 0)
        out_rows = R_CTX
    else:
        grid = (B, NCH_LAT // RO_CH)
        rmap = lambda b, u: (b * (NCH_LAT // RO_CH) + u, 0)
        omap = rmap
        smap = lambda b, u: (b, u, 0, 0)
        out_rows = R_LAT
    return pl.pallas_call(
        functools.partial(_ret_out_body, nck),
        grid=grid,
        in_specs=[
            pl.BlockSpec((None, 2, 256), lambda b, u: (layer, 0, 0)),
            pl.BlockSpec((None, 1, 256), lambda b, u: (layer, 0, 0)),
            pl.BlockSpec((rows, 256), rmap),
            pl.BlockSpec((rows, 256), rmap),
            pl.BlockSpec((rows, 256), rmap),
            pl.BlockSpec((rows, 256), rmap),
            pl.BlockSpec((1, nck, 256, 256), smap),
            pl.BlockSpec((1, nck, 256, 256), smap),
        ],
        out_specs=pl.BlockSpec((rows, 256), omap),
        out_shape=jax.ShapeDtypeStruct((out_rows, 256), BF16),
        compiler_params=_params(("arbitrary", "arbitrary")),
        name="ret_out_ctx" if ctx else "ret_out",
    )(dec, gn, rq, rk, rv, rg, sf, sb)


def _tile_head(x_all, hk):
    own = jnp.where(_lane_head(KV_W) == hk, x_all, jnp.zeros_like(x_all))
    both = own + pltpu.roll(own.astype(F32), HD, axis=1).astype(BF16)
    return jnp.concatenate([both, both], axis=1)


def _gqa_groups(groups):
    lane_head = _lane_head(256)
    chains = []
    for gi, (q_grp, kt, v_t, sinks, mask_fn) in enumerate(groups):
        for h0 in range(0, ATT_GROUP, ATT_STACK):
            chains.append((gi, tuple(range(h0, h0 + ATT_STACK)), q_grp, kt, v_t, sinks, mask_fn))
    scores = []
    for _, heads, q_grp, kt, _, _, _ in chains:
        qs = jnp.concatenate([jnp.where(lane_head == g, q_grp, jnp.zeros_like(q_grp))
                              for g in heads], axis=0)
        scores.append(lax.dot_general(kt, qs, (((1,), (1,)), ((), ())), preferred_element_type=F32))
    probs, inv_den = [], []
    for (_, heads, q_grp, _, _, sinks, mask_fn), st in zip(chains, scores):
        m = q_grp.shape[0]
        sink = jnp.concatenate([jnp.full((1, m), sinks[g] * LOG2E, F32) for g in heads], axis=1)
        st = mask_fn(st)
        mx = jnp.maximum(jnp.max(st, axis=0, keepdims=True), sink)
        p = jnp.exp2(st - mx)
        inv_den.append(1.0 / (jnp.sum(p, axis=0, keepdims=True) + jnp.exp2(sink - mx)))
        probs.append(p.astype(BF16))
    pieces = [[] for _ in groups]
    for (gi, _, q_grp, _, v_t, _, _), p, inv in zip(chains, probs, inv_den):
        m = q_grp.shape[0]
        ot = jnp.dot(v_t, p, preferred_element_type=F32) * inv
        stacked = jnp.concatenate([ot[:, i * m:(i + 1) * m] for i in range(ATT_STACK)], axis=0)
        pieces[gi].append(stacked.T)
    return [jnp.concatenate(ps, axis=1) for ps in pieces]


def _band_mask_fn(lo_rows, lo_ok, hi_rows, hi_ok, n_keys):
    def mask_fn(st):
        parts = []
        row = 0
        while row < n_keys:
            piece = st[row:row + CH]
            if row == lo_rows:
                piece = jnp.where(lo_ok, piece, NEG_INF)
            elif row == hi_rows:
                piece = jnp.where(hi_ok, piece, NEG_INF)
            parts.append(piece)
            row += CH
        return jnp.concatenate(parts, axis=0)
    return mask_fn


def _attn_lat_body(layer, sink_ref, q_ref, kp_ref, kc_ref, kn_ref, vp_ref, vc_ref, vn_ref, kx_ref, vx_ref, o_ref):
    n2 = pl.program_id(1)
    k_all = jnp.concatenate([kp_ref[...], kc_ref[...], kx_ref[...], kn_ref[...]], axis=0)
    v_all = jnp.concatenate([vp_ref[...], vc_ref[...], vx_ref[...], vn_ref[...]], axis=0)
    v_all_t = v_all.astype(F32).T.astype(BF16)
    key = lax.broadcasted_iota(jnp.int32, (CH, ATT_STACK * CH), 0)
    qry = lax.broadcasted_iota(jnp.int32, (CH, ATT_STACK * CH), 1) & (CH - 1)
    below = key >= qry
    above = key <= qry
    span = 3 * CH + CTX
    mask_a = _band_mask_fn(0, below & (n2 > 0), 2 * CH, above, span)
    mask_b = _band_mask_fn(0, below, 2 * CH + CTX, above & (n2 < NCH_LAT // 2 - 1), span)
    groups = []
    for hk in range(ATT_KV):
        sinks = [sink_ref[layer, hk * ATT_GROUP + g] for g in range(ATT_GROUP)]
        kt = _tile_head(k_all, hk)
        v_t = v_all_t[hk * HD:(hk + 1) * HD]
        cols = slice(hk * 256, (hk + 1) * 256)
        groups.append((q_ref[0:CH, cols], kt[0:span], v_t[:, 0:span], sinks, mask_a))
        groups.append((q_ref[CH:2 * CH, cols], kt[CH:CH + span], v_t[:, CH:CH + span], sinks, mask_b))
    outs = _gqa_groups(groups)
    for hk in range(ATT_KV):
        cols = slice(hk * 256, (hk + 1) * 256)
        o_ref[0:CH, cols] = outs[2 * hk].astype(BF16)
        o_ref[CH:2 * CH, cols] = outs[2 * hk + 1].astype(BF16)


def _attn_lat(sink, aq, ak, av, layer):
    nq = NCH_LAT // 2
    qmap = lambda b, n: (b * nq + n, 0)
    pmap = lambda b, n: (b * NCH_LAT + jnp.maximum(2 * n - 1, 0), 0)
    nmap = lambda b, n: (b * NCH_LAT + jnp.minimum(2 * n + 2, NCH_LAT - 1), 0)
    xmap = lambda b, n: (CTX_ROWBLK + b, 0)
    kv1 = lambda m: pl.BlockSpec((CH, KV_W), m)
    kv2 = pl.BlockSpec((2 * CH, KV_W), qmap)
    ctx_spec = pl.BlockSpec((CTX, KV_W), xmap)
    return pl.pallas_call(
        functools.partial(_attn_lat_body, layer),
        grid=(B, nq),
        in_specs=[
            pl.BlockSpec(memory_space=pltpu.SMEM),
            pl.BlockSpec((2 * CH, ATT_HEADS * HD), qmap),
            kv1(pmap), kv2, kv1(nmap),
            kv1(pmap), kv2, kv1(nmap),
            ctx_spec, ctx_spec,
        ],
        out_specs=pl.BlockSpec((2 * CH, ATT_HEADS * HD), qmap),
        out_shape=jax.ShapeDtypeStruct((R_LAT, ATT_HEADS * HD), BF16),
        compiler_params=_params(("arbitrary", "arbitrary")),
        name="swa_attn",
    )(sink, aq, ak, ak, ak, av, av, av, ak, av)


def _attn_ctx_body(layer, sink_ref, q_ref, kx_ref, vx_ref, o_ref):
    groups = []
    v_all_t = vx_ref[...].astype(F32).T.astype(BF16)
    for hk in range(ATT_KV):
        sinks = [sink_ref[layer, hk * ATT_GROUP + g] for g in range(ATT_GROUP)]
        groups.append((q_ref[:, hk * 256:(hk + 1) * 256], _tile_head(kx_ref[...], hk),
                       v_all_t[hk * HD:(hk + 1) * HD], sinks, lambda s: s))
    outs = _gqa_groups(groups)
    for hk in range(ATT_KV):
        o_ref[:, hk * 256:(hk + 1) * 256] = outs[hk].astype(BF16)


def _attn_ctx(sink, aq, ak, av, layer):
    xmap = lambda b: (CTX_ROWBLK + b, 0)
    return pl.pallas_call(
        functools.partial(_attn_ctx_body, layer),
        grid=(B,),
        in_specs=[
            pl.BlockSpec(memory_space=pltpu.SMEM),
            pl.BlockSpec((CTX, ATT_HEADS * HD), xmap),
            pl.BlockSpec((CTX, KV_W), xmap),
            pl.BlockSpec((CTX, KV_W), xmap),
        ],
        out_specs=pl.BlockSpec((CTX, ATT_HEADS * HD), lambda b: (b, 0)),
        out_shape=jax.ShapeDtypeStruct((R_CTX, ATT_HEADS * HD), BF16),
        compiler_params=_params(("arbitrary",)),
        name="ctx_attn",
    )(sink, aq, ak, av)


def _mix_ffn_body(has_ctx, final, *refs):
    refs = list(refs)
    h_ref = refs.pop(0)
    mixers = []
    for _ in range(3):
        if has_ctx:
            lat_ref, ctx_ref = refs.pop(0), refs.pop(0)
            mixers.append(_pick(lat_ref, ctx_ref))
        else:
            mixers.append(refs.pop(0)[...])
    m_ref, nw_ref, wo_ref, w_in_ref, w_out_ref = refs[:5]
    rest = refs[5:]
    yf, yr, ya = mixers
    m = m_ref[...]
    y = (jnp.dot(yf.astype(BF16), wo_ref[0:256, :], preferred_element_type=F32)
         + jnp.dot(yr, wo_ref[256:512, :], preferred_element_type=F32)
         + jnp.dot(ya, wo_ref[512:1024, :], preferred_element_type=F32))
    h = h_ref[...] + _mod_slice(m, 5) * y
    h = _ffn_half_step(h, m, nw_ref[...], 6, w_in_ref, w_out_ref)
    if final:
        fw_ref, o_ref = rest
        ms = jnp.mean(h * h, axis=-1, keepdims=True)
        h = h * lax.rsqrt(ms + EPS) * fw_ref[...]
    else:
        (o_ref,) = rest
    o_ref[...] = h


def _mix_ffn(h, mixers, mod_all, norm_w, wo, ffn_in, ffn_out, final_w, layer):
    has_ctx = mixers[0][1] is not None
    final = final_w is not None
    nt = NT_ALL if has_ctx else NT_LAT
    in_specs = [_row_spec(D)]
    args = [h]
    for (lat, ctx), width in zip(mixers, (256, 256, 512)):
        if has_ctx:
            in_specs += _lat_or_ctx_specs(width)
            args += [lat, ctx]
        else:
            in_specs += [_row_spec(width)]
            args += [lat]
    in_specs += [_mod_spec(layer), _norm_spec(layer, 2), _const_spec((None, D, D), (layer, 0, 0))]
    in_specs += _ffn_w_specs(layer, 1)
    args += [mod_all, norm_w, wo, ffn_in, ffn_out]
    if final:
        in_specs += [_const_spec((1, D))]
        args += [final_w]
    return pl.pallas_call(
        functools.partial(_mix_ffn_body, has_ctx, final),
        grid=(nt,),
        in_specs=in_specs,
        out_specs=_row_spec(D),
        out_shape=jax.ShapeDtypeStruct((nt * TM, D), F32),
        compiler_params=_params(("arbitrary",)),
        name="mix_ffn_final" if final else "mix_ffn",
    )(*args)


def kernel(x, c, ctx, c_ctx, norm_w, w_ada, b_ada, ffn_w_in, ffn_w_out, w_in, w_o, ret_decay, ret_gn_w,
           attn_sink, final_norm_w):
    assert x.shape == (B, S, D) and ctx.shape == (B, CTX, D)
    cos_t, sin_t = _rope_tables()
    chan, w1, twr, twi, w2, wc = _fourier_tables()

    cond_t = jnp.zeros((D, 8), F32).at[:, 0:B].set(c.T).at[:, B].set(c_ctx)
    mod_all = _ada(cond_t, w_ada, b_ada).reshape(DEPTH, 8, 1, N_MOD * D)

    ffn_in = ffn_w_in.astype(BF16)
    ffn_out = ffn_w_out.astype(BF16)
    w_in_b = w_in.astype(BF16)
    w_o_b = w_o.astype(BF16)
    norm4 = norm_w.reshape(DEPTH, 3, 1, D)
    gn3 = ret_gn_w.reshape(DEPTH, 1, 256)
    dec = jnp.repeat(ret_decay.astype(F32), HD, axis=2)
    sink = attn_sink.astype(F32)

    h = None
    out = None
    for layer in range(DEPTH):
        last = layer == DEPTH - 1
        if layer == 0:
            h = _ffn_first(x.reshape(R_LAT, D), ctx.reshape(R_CTX, D), mod_all, norm4, ffn_in, ffn_out, layer)
        else:
            h = _ffn(h, mod_all, norm4, ffn_in, ffn_out, layer)
        gr, gi, rq, rk, rv, rg, aq, ak, av = _inproj(h, mod_all, norm4, w_in_b, cos_t, sin_t, chan, layer)

        yf = _fft2(w2, twr, twi, _fft1(w1, gr, gi))
        sf, sb = _ret_state(dec, rk, rv, layer)
        yr = _ret_out(dec, gn3, rq, rk, rv, rg, sf, sb, layer, ctx=False)
        ya = _attn_lat(sink, aq, ak, av, layer)
        if last:
            yfc = yrc = yac = None
        else:
            yfc = _fft_ctx(wc, gr, gi)
            yrc = _ret_out(dec, gn3, rq, rk, rv, rg, sf, sb, layer, ctx=True)
            yac = _attn_ctx(sink, aq, ak, av, layer)

        res = _mix_ffn(h, [(yf, yfc), (yr, yrc), (ya, yac)], mod_all, norm4, w_o_b, ffn_in, ffn_out,
                       final_norm_w.reshape(1, D) if last else None, layer)
        if last:
            out = res
        else:
            h = res
    return out.reshape(B, S, D)
```

```python
import functools
import math

import numpy as np
import jax
import jax.numpy as jnp
from jax import lax
from jax.experimental import pallas as pl
from jax.experimental.pallas import tpu as pltpu

F32 = jnp.float32
BF16 = jnp.bfloat16

D = 1024
B = 2
S = 8192
DEPTH = 2
GRID_W = 64
CTX = 256
HD = 64
N_MOD = 9
D_FF = 2816
D_IN = 2048
EPS = 1e-6
NEG_INF = -1e30
ROPE_BASE = 10000.0
ROPE_PAIRS = 16
RET_HEADS = 4
ATT_HEADS = 8
ATT_KV = 2
ATT_GROUP = ATT_HEADS // ATT_KV
ATT_STACK = 4
KV_W = ATT_KV * HD
LOG2E = math.log2(math.e)
WINDOW = 128

R_LAT = B * S
R_CTX = B * CTX
R_ALL = R_LAT + R_CTX
TM = 512
NT_LAT = R_LAT // TM
NT_ALL = R_ALL // TM
CH = 128
NCH_LAT = S // CH
NCH_CTX = CTX // CH
MXU_N = 256
FF_CHUNKS = ((0, 5 * MXU_N), (5 * MXU_N, D_FF))
FFT_L1 = 64
FFT_L2 = 128
VMEM_LIMIT = 56 * 1024 * 1024


def _params(sem, vmem=VMEM_LIMIT):
    return pltpu.CompilerParams(dimension_semantics=sem, vmem_limit_bytes=vmem)


def _const_spec(shape, index=None):
    if index is None:
        index = (0,) * len(shape)
    return pl.BlockSpec(shape, lambda *_: index, pipeline_mode=pl.Buffered(1))


def _rope_tables():
    rows = S // GRID_W
    row = np.repeat(np.arange(rows, dtype=np.float64), GRID_W)
    col = np.tile(np.arange(GRID_W, dtype=np.float64), rows)
    inv = ROPE_BASE ** (-np.arange(ROPE_PAIRS, dtype=np.float64) / ROPE_PAIRS)
    ar = row[:, None] * inv
    ac = col[:, None] * inv
    cos64 = np.concatenate([np.cos(ar), np.cos(ar), np.cos(ac), np.cos(ac)], axis=1)
    sin64 = np.concatenate([-np.sin(ar), np.sin(ar), -np.sin(ac), np.sin(ac)], axis=1)
    cos = np.concatenate([cos64, cos64], axis=1)
    sin = np.concatenate([sin64, sin64], axis=1)
    cos = np.concatenate([cos, np.ones((TM, 128))], axis=0)
    sin = np.concatenate([sin, np.zeros((TM, 128))], axis=0)
    return jnp.asarray(cos, F32), jnp.asarray(sin, F32)


def _dft_cs(n):
    k = np.arange(n)
    ang = 2.0 * np.pi * ((k[:, None] * k[None, :]) % n) / n
    return np.cos(ang), np.sin(ang)


def _fourier_tables():
    c64, s64 = _dft_cs(64)
    eye4 = np.eye(4)
    chan = np.concatenate([np.kron(eye4, c64), -np.kron(eye4, s64)], axis=1)
    w1 = np.block([[c64, s64], [-s64, c64]])
    p1 = np.arange(FFT_L1)[:, None]
    l2 = np.arange(FFT_L2)[None, :]
    ang = 2.0 * np.pi * ((p1 * l2) % S) / S
    twr = np.repeat(np.cos(ang)[:, :, None], 128, axis=2)
    twi = np.repeat(np.sin(ang)[:, :, None], 128, axis=2)
    c128, s128 = _dft_cs(128)
    w2 = np.concatenate([c128, s128], axis=1) / math.sqrt(S * 64.0)
    c256, s256 = _dft_cs(CTX)
    wc = np.concatenate([c256, s256], axis=1) / math.sqrt(CTX * 64.0)
    f = lambda a: jnp.asarray(a, F32)
    return (f(chan).astype(BF16), f(w1).astype(BF16), f(twr), f(twi), f(w2).astype(BF16), f(wc).astype(BF16))


def _mod_slice(m, i):
    return m[:, i * D:(i + 1) * D]


def _rms_mod(h, nw, shift, scale):
    ms = jnp.mean(h * h, axis=-1, keepdims=True)
    y = h * lax.rsqrt(ms + EPS) * nw
    return y * (1.0 + scale) + shift


def _swiglu(u, w_in_ref, w_out_ref):
    y = None
    for lo, hi in FF_CHUNKS:
        g = jnp.dot(u, w_in_ref[:, lo:hi], preferred_element_type=F32)
        up = jnp.dot(u, w_in_ref[:, D_FF + lo:D_FF + hi], preferred_element_type=F32)
        a = (g * jax.nn.sigmoid(g) * up).astype(BF16)
        t = jnp.dot(a, w_out_ref[lo:hi, :], preferred_element_type=F32)
        y = t if y is None else y + t
    return y


def _ffn_half_step(h, m, nw, mod0, w_in_ref, w_out_ref):
    u = _rms_mod(h, nw, _mod_slice(m, mod0), _mod_slice(m, mod0 + 1)).astype(BF16)
    return h + 0.5 * _mod_slice(m, mod0 + 2) * _swiglu(u, w_in_ref, w_out_ref)


CAST_ROWS = 256
CAST_COLS = 512


def _stage_scratch():
    return [pltpu.VMEM((2, CAST_ROWS, CAST_COLS), F32), pltpu.SemaphoreType.DMA((2,))]


def _stage_weights(plan, stage_ref, sem_ref):
    chunks = []
    for src, dst in plan:
        rows, cols = dst.shape
        assert rows % CAST_ROWS == 0 and cols % CAST_COLS == 0
        for r0 in range(0, rows, CAST_ROWS):
            for c0 in range(0, cols, CAST_COLS):
                chunks.append((src, dst, r0, c0))

    def copy(i):
        src, _, r0, c0 = chunks[i]
        return pltpu.make_async_copy(src.at[pl.ds(r0, CAST_ROWS), pl.ds(c0, CAST_COLS)],
                                     stage_ref.at[i % 2], sem_ref.at[i % 2])

    copy(0).start()
    for i, (_, dst, r0, c0) in enumerate(chunks):
        if i + 1 < len(chunks):
            copy(i + 1).start()
        copy(i).wait()
        dst[r0:r0 + CAST_ROWS, c0:c0 + CAST_COLS] = stage_ref[i % 2].astype(BF16)


def _log_sigmoid(x):
    return jnp.minimum(x, 0.0) - jnp.log1p(jnp.exp(-jnp.abs(x)))


def _lane_head(width):
    return lax.broadcasted_iota(jnp.int32, (1, width), 1) // HD


ADA_CB = 1152


def _ada_body(ct_ref, w_ref, b_ref, o_ref):
    ct = ct_ref[...]
    s = ct * jax.nn.sigmoid(ct)
    w = w_ref[0]
    rows = [jnp.sum(w * s[:, r:r + 1], axis=0, keepdims=True) for r in range(3)]
    rows.append(jnp.zeros((5, ADA_CB), F32))
    o_ref[0] = jnp.concatenate(rows, axis=0) + b_ref[0]


def _ada(cond_t, w_ada, b_ada):
    nb = (N_MOD * D) // ADA_CB
    return pl.pallas_call(
        _ada_body,
        grid=(DEPTH, nb),
        in_specs=[
            pl.BlockSpec((D, 8), lambda l, j: (0, 0)),
            pl.BlockSpec((1, D, ADA_CB), lambda l, j: (l, 0, j)),
            pl.BlockSpec((1, 1, ADA_CB), lambda l, j: (l, 0, j)),
        ],
        out_specs=pl.BlockSpec((1, 8, ADA_CB), lambda l, j: (l, 0, j)),
        out_shape=jax.ShapeDtypeStruct((DEPTH, 8, N_MOD * D), F32),
        compiler_params=_params(("arbitrary", "arbitrary")),
        name="ada_mod",
    )(cond_t, w_ada, b_ada.reshape(DEPTH, 1, N_MOD * D))


def _mod_spec(layer):
    return pl.BlockSpec((None, None, 1, N_MOD * D),
                        lambda i: (layer, jnp.minimum(i // (NT_LAT // B), B), 0, 0))


def _norm_spec(layer, k):
    return _const_spec((None, None, 1, D), (layer, k, 0, 0))


def _row_spec(width):
    return pl.BlockSpec((TM, width), lambda i: (i, 0))


def _lat_or_ctx_specs(width):
    return [pl.BlockSpec((TM, width), lambda i: (jnp.minimum(i, NT_LAT - 1), 0)),
            pl.BlockSpec((TM, width), lambda i: (0, 0))]


def _pick(lat_ref, ctx_ref):
    return jnp.where(pl.program_id(0) < NT_LAT, lat_ref[...], ctx_ref[...])


def _rope(x, cos, sin, lane_first_half):
    outs = []
    for j in range(x.shape[1] // 128):
        xb = x[:, j * 128:(j + 1) * 128]
        fwd = pltpu.roll(xb, 112, axis=1)
        bwd = pltpu.roll(xb, 16, axis=1)
        partner = jnp.where(lane_first_half, fwd, bwd)
        outs.append(xb * cos + partner * sin)
    return outs[0] if len(outs) == 1 else jnp.concatenate(outs, axis=1)


def _ffn_inproj_body(first, layer, *refs):
    refs = list(refs)
    if first:
        h = _pick(refs.pop(0), refs.pop(0))
    else:
        h = refs.pop(0)[...]
    (m_ref, nw0_ref, nw1_ref, fw_in_hbm, fw_out_hbm, w_hbm, cos_ref, sin_ref, chan_ref,
     h_out_ref, gr_ref, gi_ref, rq_ref, rk_ref, rv_ref, rg_ref, aq_ref, ak_ref, av_ref,
     fw_in_ref, fw_out_ref, w_ref, stage_ref, sem_ref) = refs

    @pl.when(pl.program_id(0) == 0)
    def _():
        _stage_weights([(fw_in_hbm.at[layer, 0], fw_in_ref), (fw_out_hbm.at[layer, 0], fw_out_ref),
                        (w_hbm.at[layer], w_ref)], stage_ref, sem_ref)

    m = m_ref[...]
    h = _ffn_half_step(h, m, nw0_ref[...], 0, fw_in_ref, fw_out_ref)
    h_out_ref[...] = h
    u = _rms_mod(h, nw1_ref[...], _mod_slice(m, 3), _mod_slice(m, 4)).astype(BF16)
    z = jnp.dot(u, w_ref[...], preferred_element_type=F32)
    cos = cos_ref[...]
    sin = sin_ref[...]
    lane = lax.broadcasted_iota(jnp.int32, (1, 128), 1)
    first_half = (lane // ROPE_PAIRS) % 2 == 0
    g = jnp.dot(z[:, 0:256].astype(BF16), chan_ref[...], preferred_element_type=F32)
    gr_ref[...] = g[:, 0:256]
    gi_ref[...] = g[:, 256:512]
    rq_ref[...] = _rope(z[:, 256:512], cos, sin, first_half).astype(BF16)
    rk_ref[...] = (_rope(z[:, 512:768], cos, sin, first_half) * (HD ** -0.5)).astype(BF16)
    rv_ref[...] = z[:, 768:1024].astype(BF16)
    rg_ref[...] = z[:, 1024:1280].astype(BF16)
    aq_ref[...] = (_rope(z[:, 1280:1792], cos, sin, first_half) * (HD ** -0.5 * LOG2E)).astype(BF16)
    ak_ref[...] = _rope(z[:, 1792:1920], cos, sin, first_half).astype(BF16)
    av_ref[...] = z[:, 1920:2048].astype(BF16)


def _rope_block(i):
    return jnp.where(i < NT_LAT, i % (S // TM), S // TM)


def _ffn_inproj(rows_in, mod_all, norm_w, ffn_in, ffn_out, w_in, cos_t, sin_t, chan, layer):
    first = len(rows_in) == 2
    widths = (256, 256, 256, 256, 256, 256, 512, KV_W, KV_W)
    dtypes = [F32, F32] + [BF16] * (len(widths) - 2)
    hbm = pl.BlockSpec(memory_space=pl.ANY)
    return pl.pallas_call(
        functools.partial(_ffn_inproj_body, first, layer),
        grid=(NT_ALL,),
        in_specs=(_lat_or_ctx_specs(D) if first else [_row_spec(D)]) + [
            _mod_spec(layer),
            _norm_spec(layer, 0),
            _norm_spec(layer, 1),
            hbm, hbm, hbm,
            pl.BlockSpec((TM, 128), lambda i: (_rope_block(i), 0)),
            pl.BlockSpec((TM, 128), lambda i: (_rope_block(i), 0)),
            _const_spec((256, 512)),
        ],
        out_specs=[_row_spec(D)] + [_row_spec(w) for w in widths],
        out_shape=[jax.ShapeDtypeStruct((R_ALL, D), F32)]
        + [jax.ShapeDtypeStruct((R_ALL, w), dt) for w, dt in zip(widths, dtypes)],
        scratch_shapes=[pltpu.VMEM((D, 2 * D_FF), BF16), pltpu.VMEM((D_FF, D), BF16),
                        pltpu.VMEM((D, D_IN), BF16)] + _stage_scratch(),
        compiler_params=_params(("arbitrary",)),
        name="ffn_inproj_first" if first else "ffn_inproj",
    )(*rows_in, mod_all, norm_w, norm_w, ffn_in, ffn_out, w_in, cos_t, sin_t, chan)


FFT_SUB = 16


def _fft1_body(w1_ref, xr_ref, xi_ref, z_ref):
    w1 = w1_ref[...]
    for s in range(FFT_SUB):
        x = jnp.concatenate([xr_ref[:, s, :], xi_ref[:, s, :]], axis=0).astype(BF16)
        z = jnp.dot(w1, x, preferred_element_type=F32)
        z_ref[0, :, s, :] = z[0:FFT_L1]
        z_ref[1, :, s, :] = z[FFT_L1:2 * FFT_L1]


def _fft1(w1, gr, gi):
    view = (R_ALL // FFT_L2, FFT_L2 // FFT_SUB, FFT_SUB, 256)
    x_spec = pl.BlockSpec((FFT_L1, None, FFT_SUB, 256), lambda b, j: (b, j, 0, 0))
    return pl.pallas_call(
        _fft1_body,
        grid=(B, FFT_L2 // FFT_SUB),
        in_specs=[_const_spec((128, 128)), x_spec, x_spec],
        out_specs=pl.BlockSpec((None, 2, FFT_L1, FFT_SUB, 256), lambda b, j: (b, 0, 0, j, 0)),
        out_shape=jax.ShapeDtypeStruct((B, 2, FFT_L1, FFT_L2, 256), F32),
        compiler_params=_params(("arbitrary", "arbitrary")),
        name="fft_stage1",
    )(w1, gr.reshape(view), gi.reshape(view))


def _fft2_body(w2_ref, twr_ref, twi_ref, z_ref, o_ref):
    w2 = w2_ref[...]
    for p in range(FFT_SUB):
        zr = z_ref[0, p]
        zi = z_ref[1, p]
        tr = jnp.concatenate([twr_ref[p]] * 2, axis=1)
        ti = jnp.concatenate([twi_ref[p]] * 2, axis=1)
        yr = tr * zr + ti * zi
        yi = tr * zi - ti * zr
        y = jnp.concatenate([yr, yi], axis=0).astype(BF16)
        o_ref[:, p, :] = jnp.dot(w2, y, preferred_element_type=F32)


def _fft2(w2, twr, twi, z):
    out = pl.pallas_call(
        _fft2_body,
        grid=(B, FFT_L1 // FFT_SUB),
        in_specs=[
            _const_spec((128, 256)),
            pl.BlockSpec((FFT_SUB, FFT_L2, 128), lambda b, j: (j, 0, 0)),
            pl.BlockSpec((FFT_SUB, FFT_L2, 128), lambda b, j: (j, 0, 0)),
            pl.BlockSpec((None, 2, FFT_SUB, FFT_L2, 256), lambda b, j: (b, 0, j, 0, 0)),
        ],
        out_specs=pl.BlockSpec((None, FFT_L2, None, FFT_SUB, 256), lambda b, j: (b, 0, j, 0, 0)),
        out_shape=jax.ShapeDtypeStruct((B, FFT_L2, FFT_L1 // FFT_SUB, FFT_SUB, 256), F32),
        compiler_params=_params(("arbitrary", "arbitrary")),
        name="fft_stage2",
    )(w2, twr, twi, z)
    return out.reshape(R_LAT, 256)


def _fft_ctx_body(wc_ref, gr_ref, gi_ref, o_ref):
    x = jnp.concatenate([gr_ref[...], gi_ref[...]], axis=0).astype(BF16)
    o_ref[...] = jnp.dot(wc_ref[...], x, preferred_element_type=F32)


def _fft_ctx(wc, gr, gi):
    blk0 = R_LAT // CTX
    return pl.pallas_call(
        _fft_ctx_body,
        grid=(B,),
        in_specs=[
            _const_spec((CTX, 2 * CTX)),
            pl.BlockSpec((CTX, 256), lambda b: (blk0 + b, 0)),
            pl.BlockSpec((CTX, 256), lambda b: (blk0 + b, 0)),
        ],
        out_specs=pl.BlockSpec((CTX, 256), lambda b: (b, 0)),
        out_shape=jax.ShapeDtypeStruct((R_CTX, 256), F32),
        compiler_params=_params(("arbitrary",)),
        name="fft_ctx",
    )(wc, gr, gi)


RS_CH = 8
RS_STEPS = NCH_LAT // RS_CH
N_STATE = (RS_STEPS + 1) * RS_CH
RO_CH = 8
CTX_ROWBLK = R_LAT // CTX


def _head_block_mask():
    r = lax.broadcasted_iota(jnp.int32, (256, 256), 0) // HD
    c = lax.broadcasted_iota(jnp.int32, (256, 256), 1) // HD
    return r == c


def _ret_state_body(dec_ref, kf_ref, vf_ref, kb_ref, vb_ref, kx_ref, vx_ref, sf_ref, sb_ref, accf, accb):
    t = pl.program_id(1)
    lg = _log_sigmoid(dec_ref[...])
    lgf = lg[0:1]
    lgb = lg[1:2]
    pos = lax.broadcasted_iota(jnp.int32, (CH, 1), 0).astype(F32)
    wf = jnp.exp((CH - 1.0 - pos) * lgf)
    wb = jnp.exp(pos * lgb)
    cdf = jnp.exp(float(CH) * lgf)
    cdb = jnp.exp(float(CH) * lgb)
    mask = _head_block_mask()
    dn = (((0,), (0,)), ((), ()))

    def scan(acc, out_ref, k_ref, v_ref, w, cd, order):
        kvs = {}
        for j in order:
            rows = slice(j * CH, (j + 1) * CH)
            kd = (k_ref[rows, :].astype(F32) * w).astype(BF16)
            kvs[j] = lax.dot_general(kd, v_ref[rows, :], dn, preferred_element_type=F32)
        state = acc[...]
        for j in order:
            out_ref[0, j] = state.astype(BF16)
            state = state * cd + jnp.where(mask, kvs[j], 0.0)
        acc[...] = state

    @pl.when(t == 0)
    def _():
        accf[...] = jnp.zeros_like(accf)
        accb[...] = jnp.zeros_like(accb)
        for j in range(NCH_CTX, RS_CH):
            sf_ref[0, j] = jnp.zeros((256, 256), BF16)
            sb_ref[0, j] = jnp.zeros((256, 256), BF16)
        scan(accf, sf_ref, kx_ref, vx_ref, wf, cdf, list(range(NCH_CTX)))
        scan(accb, sb_ref, kx_ref, vx_ref, wb, cdb, list(reversed(range(NCH_CTX))))

    @pl.when(t > 0)
    def _():
        scan(accf, sf_ref, kf_ref, vf_ref, wf, cdf, list(range(RS_CH)))
        scan(accb, sb_ref, kb_ref, vb_ref, wb, cdb, list(reversed(range(RS_CH))))


def _ret_state(dec, rk, rv, layer):
    rows = RS_CH * CH
    fblk = lambda t: jnp.maximum(t - 1, 0)
    bblk = lambda t: RS_STEPS - jnp.maximum(t, 1)
    fmap = lambda b, t: (b * RS_STEPS + fblk(t), 0)
    bmap = lambda b, t: (b * RS_STEPS + bblk(t), 0)
    xmap = lambda b, t: (CTX_ROWBLK + b, 0)
    return pl.pallas_call(
        _ret_state_body,
        grid=(B, RS_STEPS + 1),
        in_specs=[
            pl.BlockSpec((None, 2, 256), lambda b, t: (layer, 0, 0)),
            pl.BlockSpec((rows, 256), fmap),
            pl.BlockSpec((rows, 256), fmap),
            pl.BlockSpec((rows, 256), bmap),
            pl.BlockSpec((rows, 256), bmap),
            pl.BlockSpec((CTX, 256), xmap),
            pl.BlockSpec((CTX, 256), xmap),
        ],
        out_specs=[
            pl.BlockSpec((1, RS_CH, 256, 256), lambda b, t: (b, jnp.where(t == 0, RS_STEPS, t - 1), 0, 0)),
            pl.BlockSpec((1, RS_CH, 256, 256), lambda b, t: (b, jnp.where(t == 0, RS_STEPS, RS_STEPS - t), 0, 0)),
        ],
        out_shape=[jax.ShapeDtypeStruct((B, N_STATE, 256, 256), BF16)] * 2,
        scratch_shapes=[pltpu.VMEM((256, 256), F32), pltpu.VMEM((256, 256), F32)],
        compiler_params=_params(("arbitrary", "arbitrary")),
        name="ret_state",
    )(dec, rk, rv, rk, rv, rk, rv)


def _split_dot(x, w):
    hi = x.astype(BF16)
    lo = (x - hi.astype(F32)).astype(BF16)
    return jnp.dot(hi, w, preferred_element_type=F32) + jnp.dot(lo, w, preferred_element_type=F32)


def _ret_out_body(nck, dec_ref, gn_ref, q_ref, k_ref, v_ref, g_ref, sf_ref, sb_ref, o_ref):
    lg = _log_sigmoid(dec_ref[...])
    lgf = lg[0:1]
    lgb = lg[1:2]
    pos = lax.broadcasted_iota(jnp.int32, (CH, 1), 0).astype(F32)
    wqf = jnp.exp((pos + 1.0) * lgf)
    wqb = jnp.exp((float(CH) - pos) * lgb)
    ri = lax.broadcasted_iota(jnp.int32, (CH, CH), 0)
    ci = lax.broadcasted_iota(jnp.int32, (CH, CH), 1)
    diff = (ri - ci).astype(F32)
    decays = []
    for h in range(RET_HEADS):
        lgf_h = lgf[:, h * HD:h * HD + 1]
        lgb_h = lgb[:, h * HD:h * HD + 1]
        decays.append(jnp.where(ri >= ci, jnp.exp(jnp.maximum(diff, 0.0) * lgf_h), 0.0)
                      + jnp.where(ci >= ri, jnp.exp(jnp.maximum(-diff, 0.0) * lgb_h), 0.0))
    decay = jnp.concatenate(decays, axis=0)
    lane_head = _lane_head(256)
    avg = jnp.where(_head_block_mask(), 1.0 / HD, 0.0).astype(BF16)
    gn = gn_ref[...]
    dn = (((1,), (1,)), ((), ()))
    chunk_rows = [slice(c * CH, (c + 1) * CH) for c in range(nck)]
    cross, scores = [], []
    for c, rows in enumerate(chunk_rows):
        qb16 = q_ref[rows, :]
        q = qb16.astype(F32)
        cross.append(jnp.dot((q * wqf).astype(BF16), sf_ref[0, c], preferred_element_type=F32)
                     + jnp.dot((q * wqb).astype(BF16), sb_ref[0, c], preferred_element_type=F32))
        qs = jnp.concatenate([jnp.where(lane_head == h, qb16, jnp.zeros_like(qb16))
                              for h in range(RET_HEADS)], axis=0)
        scores.append(lax.dot_general(qs, k_ref[rows, :], dn, preferred_element_type=F32))
    outs = []
    for c, rows in enumerate(chunk_rows):
        p = (scores[c] * decay).astype(BF16)
        of = jnp.dot(p, v_ref[rows, :], preferred_element_type=F32)
        o = cross[c]
        for h in range(RET_HEADS):
            o = o + jnp.where(lane_head == h, of[h * CH:(h + 1) * CH], 0.0)
        outs.append(o)
    o = jnp.concatenate(outs, axis=0)
    mu = _split_dot(o, avg)
    cen = o - mu
    var = _split_dot(cen * cen, avg)
    on = cen * lax.rsqrt(var + EPS) * gn
    g = g_ref[...].astype(F32)
    o_ref[...] = (g * jax.nn.sigmoid(g) * on).astype(BF16)


def _ret_out(dec, gn, rq, rk, rv, rg, sf, sb, layer, ctx):
    nck = NCH_CTX if ctx else RO_CH
    rows = nck * CH
    if ctx:
        grid = (B, 1)
        rmap = lambda b, u: (CTX_ROWBLK + b, 0)
        omap = lambda b, u: (b, 0)
        smap = lambda b, u: (b, NCH_LAT // NCH_CTX, 0, 0)
        out_rows = R_CTX
    else:
        grid = (B, NCH_LAT // RO_CH)
        rmap = lambda b, u: (b * (NCH_LAT // RO_CH) + u, 0)
        omap = rmap
        smap = lambda b, u: (b, u, 0, 0)
        out_rows = R_LAT
    return pl.pallas_call(
        functools.partial(_ret_out_body, nck),
        grid=grid,
        in_specs=[
            pl.BlockSpec((None, 2, 256), lambda b, u: (layer, 0, 0)),
            pl.BlockSpec((None, 1, 256), lambda b, u: (layer, 0, 0)),
            pl.BlockSpec((rows, 256), rmap),
            pl.BlockSpec((rows, 256), rmap),
            pl.BlockSpec((rows, 256), rmap),
            pl.BlockSpec((rows, 256), rmap),
            pl.BlockSpec((1, nck, 256, 256), smap),
            pl.BlockSpec((1, nck, 256, 256), smap),
        ],
        out_specs=pl.BlockSpec((rows, 256), omap),
        out_shape=jax.ShapeDtypeStruct((out_rows, 256), BF16),
        compiler_params=_params(("arbitrary", "arbitrary")),
        name="ret_out_ctx" if ctx else "ret_out",
    )(dec, gn, rq, rk, rv, rg, sf, sb)


def _tile_head(x_all, hk):
    own = jnp.where(_lane_head(KV_W) == hk, x_all, jnp.zeros_like(x_all))
    both = own + pltpu.roll(own.astype(F32), HD, axis=1).astype(BF16)
    return jnp.concatenate([both, both], axis=1)


def _gqa_groups(groups):
    lane_head = _lane_head(256)
    chains = []
    for gi, (q_grp, kt, v_t, sinks, mask_fn) in enumerate(groups):
        for h0 in range(0, ATT_GROUP, ATT_STACK):
            chains.append((gi, tuple(range(h0, h0 + ATT_STACK)), q_grp, kt, v_t, sinks, mask_fn))
    scores = []
    for _, heads, q_grp, kt, _, _, _ in chains:
        qs = jnp.concatenate([jnp.where(lane_head == g, q_grp, jnp.zeros_like(q_grp))
                              for g in heads], axis=0)
        scores.append(lax.dot_general(kt, qs, (((1,), (1,)), ((), ())), preferred_element_type=F32))
    probs, inv_den = [], []
    for (_, heads, q_grp, _, _, sinks, mask_fn), st in zip(chains, scores):
        m = q_grp.shape[0]
        sink = jnp.concatenate([jnp.full((1, m), sinks[g] * LOG2E, F32) for g in heads], axis=1)
        st = mask_fn(st)
        mx = jnp.maximum(jnp.max(st, axis=0, keepdims=True), sink)
        p = jnp.exp2(st - mx)
        inv_den.append(1.0 / (jnp.sum(p, axis=0, keepdims=True) + jnp.exp2(sink - mx)))
        probs.append(p.astype(BF16))
    pieces = [[] for _ in groups]
    for (gi, _, q_grp, _, v_t, _, _), p, inv in zip(chains, probs, inv_den):
        m = q_grp.shape[0]
        ot = jnp.dot(v_t, p, preferred_element_type=F32) * inv
        stacked = jnp.concatenate([ot[:, i * m:(i + 1) * m] for i in range(ATT_STACK)], axis=0)
        pieces[gi].append(stacked.T)
    return [jnp.concatenate(ps, axis=1) for ps in pieces]


def _band_mask_fn(lo_rows, lo_ok, hi_rows, hi_ok, n_keys):
    def mask_fn(st):
        parts = []
        row = 0
        while row < n_keys:
            piece = st[row:row + CH]
            if row == lo_rows:
                piece = jnp.where(lo_ok, piece, NEG_INF)
            elif row == hi_rows:
                piece = jnp.where(hi_ok, piece, NEG_INF)
            parts.append(piece)
            row += CH
        return jnp.concatenate(parts, axis=0)
    return mask_fn


def _attn_lat_body(layer, sink_ref, q_ref, kp_ref, kc_ref, kn_ref, vp_ref, vc_ref, vn_ref, kx_ref, vx_ref, o_ref):
    n2 = pl.program_id(1)
    k_all = jnp.concatenate([kp_ref[...], kc_ref[...], kx_ref[...], kn_ref[...]], axis=0)
    v_all = jnp.concatenate([vp_ref[...], vc_ref[...], vx_ref[...], vn_ref[...]], axis=0)
    v_all_t = v_all.astype(F32).T.astype(BF16)
    key = lax.broadcasted_iota(jnp.int32, (CH, ATT_STACK * CH), 0)
    qry = lax.broadcasted_iota(jnp.int32, (CH, ATT_STACK * CH), 1) & (CH - 1)
    below = key >= qry
    above = key <= qry
    span = 3 * CH + CTX
    mask_a = _band_mask_fn(0, below & (n2 > 0), 2 * CH, above, span)
    mask_b = _band_mask_fn(0, below, 2 * CH + CTX, above & (n2 < NCH_LAT // 2 - 1), span)
    groups = []
    for hk in range(ATT_KV):
        sinks = [sink_ref[layer, hk * ATT_GROUP + g] for g in range(ATT_GROUP)]
        kt = _tile_head(k_all, hk)
        v_t = v_all_t[hk * HD:(hk + 1) * HD]
        cols = slice(hk * 256, (hk + 1) * 256)
        groups.append((q_ref[0:CH, cols], kt[0:span], v_t[:, 0:span], sinks, mask_a))
        groups.append((q_ref[CH:2 * CH, cols], kt[CH:CH + span], v_t[:, CH:CH + span], sinks, mask_b))
    outs = _gqa_groups(groups)
    for hk in range(ATT_KV):
        cols = slice(hk * 256, (hk + 1) * 256)
        o_ref[0:CH, cols] = outs[2 * hk].astype(BF16)
        o_ref[CH:2 * CH, cols] = outs[2 * hk + 1].astype(BF16)


def _attn_lat(sink, aq, ak, av, layer):
    nq = NCH_LAT // 2
    qmap = lambda b, n: (b * nq + n, 0)
    pmap = lambda b, n: (b * NCH_LAT + jnp.maximum(2 * n - 1, 0), 0)
    nmap = lambda b, n: (b * NCH_LAT + jnp.minimum(2 * n + 2, NCH_LAT - 1), 0)
    xmap = lambda b, n: (CTX_ROWBLK + b, 0)
    kv1 = lambda m: pl.BlockSpec((CH, KV_W), m)
    kv2 = pl.BlockSpec((2 * CH, KV_W), qmap)
    ctx_spec = pl.BlockSpec((CTX, KV_W), xmap)
    return pl.pallas_call(
        functools.partial(_attn_lat_body, layer),
        grid=(B, nq),
        in_specs=[
            pl.BlockSpec(memory_space=pltpu.SMEM),
            pl.BlockSpec((2 * CH, ATT_HEADS * HD), qmap),
            kv1(pmap), kv2, kv1(nmap),
            kv1(pmap), kv2, kv1(nmap),
            ctx_spec, ctx_spec,
        ],
        out_specs=pl.BlockSpec((2 * CH, ATT_HEADS * HD), qmap),
        out_shape=jax.ShapeDtypeStruct((R_LAT, ATT_HEADS * HD), BF16),
        compiler_params=_params(("arbitrary", "arbitrary")),
        name="swa_attn",
    )(sink, aq, ak, ak, ak, av, av, av, ak, av)


def _attn_ctx_body(layer, sink_ref, q_ref, kx_ref, vx_ref, o_ref):
    groups = []
    v_all_t = vx_ref[...].astype(F32).T.astype(BF16)
    for hk in range(ATT_KV):
        sinks = [sink_ref[layer, hk * ATT_GROUP + g] for g in range(ATT_GROUP)]
        groups.append((q_ref[:, hk * 256:(hk + 1) * 256], _tile_head(kx_ref[...], hk),
                       v_all_t[hk * HD:(hk + 1) * HD], sinks, lambda s: s))
    outs = _gqa_groups(groups)
    for hk in range(ATT_KV):
        o_ref[:, hk * 256:(hk + 1) * 256] = outs[hk].astype(BF16)


def _attn_ctx(sink, aq, ak, av, layer):
    xmap = lambda b: (CTX_ROWBLK + b, 0)
    return pl.pallas_call(
        functools.partial(_attn_ctx_body, layer),
        grid=(B,),
        in_specs=[
            pl.BlockSpec(memory_space=pltpu.SMEM),
            pl.BlockSpec((CTX, ATT_HEADS * HD), xmap),
            pl.BlockSpec((CTX, KV_W), xmap),
            pl.BlockSpec((CTX, KV_W), xmap),
        ],
        out_specs=pl.BlockSpec((CTX, ATT_HEADS * HD), lambda b: (b, 0)),
        out_shape=jax.ShapeDtypeStruct((R_CTX, ATT_HEADS * HD), BF16),
        compiler_params=_params(("arbitrary",)),
        name="ctx_attn",
    )(sink, aq, ak, av)


def _mix_ffn_body(has_ctx, final, layer, *refs):
    refs = list(refs)
    h_ref = refs.pop(0)
    mixers = []
    for _ in range(3):
        if has_ctx:
            lat_ref, ctx_ref = refs.pop(0), refs.pop(0)
            mixers.append(_pick(lat_ref, ctx_ref))
        else:
            mixers.append(refs.pop(0)[...])
    m_ref, nw_ref, wo_hbm, w_in_hbm, w_out_hbm = refs[:5]
    wo_ref, w_in_ref, w_out_ref, stage_ref, sem_ref = refs[-5:]
    rest = refs[5:-5]

    @pl.when(pl.program_id(0) == 0)
    def _():
        _stage_weights([(wo_hbm.at[layer], wo_ref), (w_in_hbm.at[layer, 1], w_in_ref),
                        (w_out_hbm.at[layer, 1], w_out_ref)], stage_ref, sem_ref)

    yf, yr, ya = mixers
    m = m_ref[...]
    y = (jnp.dot(yf.astype(BF16), wo_ref[0:256, :], preferred_element_type=F32)
         + jnp.dot(yr, wo_ref[256:512, :], preferred_element_type=F32)
         + jnp.dot(ya, wo_ref[512:1024, :], preferred_element_type=F32))
    h = h_ref[...] + _mod_slice(m, 5) * y
    h = _ffn_half_step(h, m, nw_ref[...], 6, w_in_ref, w_out_ref)
    if final:
        fw_ref, o_ref = rest
        ms = jnp.mean(h * h, axis=-1, keepdims=True)
        h = h * lax.rsqrt(ms + EPS) * fw_ref[...]
    else:
        (o_ref,) = rest
    o_ref[...] = h


def _mix_ffn(h, mixers, mod_all, norm_w, wo, ffn_in, ffn_out, final_w, layer):
    has_ctx = mixers[0][1] is not None
    final = final_w is not None
    nt = NT_ALL if has_ctx else NT_LAT
    in_specs = [_row_spec(D)]
    args = [h]
    for (lat, ctx), width in zip(mixers, (256, 256, 512)):
        if has_ctx:
            in_specs += _lat_or_ctx_specs(width)
            args += [lat, ctx]
        else:
            in_specs += [_row_spec(width)]
            args += [lat]
    hbm = pl.BlockSpec(memory_space=pl.ANY)
    in_specs += [_mod_spec(layer), _norm_spec(layer, 2), hbm, hbm, hbm]
    args += [mod_all, norm_w, wo, ffn_in, ffn_out]
    if final:
        in_specs += [_const_spec((1, D))]
        args += [final_w]
    return pl.pallas_call(
        functools.partial(_mix_ffn_body, has_ctx, final, layer),
        grid=(nt,),
        in_specs=in_specs,
        out_specs=_row_spec(D),
        out_shape=jax.ShapeDtypeStruct((nt * TM, D), F32),
        scratch_shapes=[pltpu.VMEM((D, D), BF16), pltpu.VMEM((D, 2 * D_FF), BF16),
                        pltpu.VMEM((D_FF, D), BF16)] + _stage_scratch(),
        compiler_params=_params(("arbitrary",)),
        name="mix_ffn_final" if final else "mix_ffn",
    )(*args)


def kernel(x, c, ctx, c_ctx, norm_w, w_ada, b_ada, ffn_w_in, ffn_w_out, w_in, w_o, ret_decay, ret_gn_w,
           attn_sink, final_norm_w):
    assert x.shape == (B, S, D) and ctx.shape == (B, CTX, D)
    cos_t, sin_t = _rope_tables()
    chan, w1, twr, twi, w2, wc = _fourier_tables()

    cond_t = jnp.zeros((D, 8), F32).at[:, 0:B].set(c.T).at[:, B].set(c_ctx)
    mod_all = _ada(cond_t, w_ada, b_ada).reshape(DEPTH, 8, 1, N_MOD * D)

    ffn_in, ffn_out, w_in_b, w_o_b = ffn_w_in, ffn_w_out, w_in, w_o
    norm4 = norm_w.reshape(DEPTH, 3, 1, D)
    gn3 = ret_gn_w.reshape(DEPTH, 1, 256)
    dec = jnp.repeat(ret_decay.astype(F32), HD, axis=2)
    sink = attn_sink.astype(F32)

    h = None
    out = None
    for layer in range(DEPTH):
        last = layer == DEPTH - 1
        rows_in = (x.reshape(R_LAT, D), ctx.reshape(R_CTX, D)) if layer == 0 else (h,)
        h, gr, gi, rq, rk, rv, rg, aq, ak, av = _ffn_inproj(rows_in, mod_all, norm4, ffn_in, ffn_out, w_in_b,
                                                            cos_t, sin_t, chan, layer)

        yf = _fft2(w2, twr, twi, _fft1(w1, gr, gi))
        sf, sb = _ret_state(dec, rk, rv, layer)
        yr = _ret_out(dec, gn3, rq, rk, rv, rg, sf, sb, layer, ctx=False)
        ya = _attn_lat(sink, aq, ak, av, layer)
        if last:
            yfc = yrc = yac = None
        else:
            yfc = _fft_ctx(wc, gr, gi)
            yrc = _ret_out(dec, gn3, rq, rk, rv, rg, sf, sb, layer, ctx=True)
            yac = _attn_ctx(sink, aq, ak, av, layer)

        res = _mix_ffn(h, [(yf, yfc), (yr, yrc), (ya, yac)], mod_all, norm4, w_o_b, ffn_in, ffn_out,
                       final_norm_w.reshape(1, D) if last else None, layer)
        if last:
            out = res
        else:
            h = res
    return out.reshape(B, S, D)
```

```python
import functools
import math

import numpy as np
import jax
import jax.numpy as jnp
from jax import lax
from jax.experimental import pallas as pl
from jax.experimental.pallas import tpu as pltpu

F32 = jnp.float32
BF16 = jnp.bfloat16

D = 1024
B = 2
S = 8192
DEPTH = 2
GRID_W = 64
CTX = 256
HD = 64
N_MOD = 9
D_FF = 2816
D_IN = 2048
EPS = 1e-6
NEG_INF = -1e30
ROPE_BASE = 10000.0
ROPE_PAIRS = 16
RET_HEADS = 4
ATT_HEADS = 8
ATT_KV = 2
ATT_GROUP = ATT_HEADS // ATT_KV
ATT_STACK = 4
KV_W = ATT_KV * HD
LOG2E = math.log2(math.e)
WINDOW = 128

R_LAT = B * S
R_CTX = B * CTX
R_ALL = R_LAT + R_CTX
TM = 512
NT_LAT = R_LAT // TM
NT_ALL = R_ALL // TM
CH = 128
NCH_LAT = S // CH
NCH_CTX = CTX // CH
MXU_N = 256
FF_CHUNKS = ((0, 5 * MXU_N), (5 * MXU_N, D_FF))
FFT_L1 = 64
FFT_L2 = 128
VMEM_LIMIT = 56 * 1024 * 1024


def _params(sem, vmem=VMEM_LIMIT):
    return pltpu.CompilerParams(dimension_semantics=sem, vmem_limit_bytes=vmem)


def _const_spec(shape, index=None):
    if index is None:
        index = (0,) * len(shape)
    return pl.BlockSpec(shape, lambda *_: index, pipeline_mode=pl.Buffered(1))


def _rope_tables():
    rows = S // GRID_W
    row = np.repeat(np.arange(rows, dtype=np.float64), GRID_W)
    col = np.tile(np.arange(GRID_W, dtype=np.float64), rows)
    inv = ROPE_BASE ** (-np.arange(ROPE_PAIRS, dtype=np.float64) / ROPE_PAIRS)
    ar = row[:, None] * inv
    ac = col[:, None] * inv
    cos64 = np.concatenate([np.cos(ar), np.cos(ar), np.cos(ac), np.cos(ac)], axis=1)
    sin64 = np.concatenate([-np.sin(ar), np.sin(ar), -np.sin(ac), np.sin(ac)], axis=1)
    cos = np.concatenate([cos64, cos64], axis=1)
    sin = np.concatenate([sin64, sin64], axis=1)
    cos = np.concatenate([cos, np.ones((TM, 128))], axis=0)
    sin = np.concatenate([sin, np.zeros((TM, 128))], axis=0)
    return jnp.asarray(cos, F32), jnp.asarray(sin, F32)


def _dft_cs(n):
    k = np.arange(n)
    ang = 2.0 * np.pi * ((k[:, None] * k[None, :]) % n) / n
    return np.cos(ang), np.sin(ang)


def _fourier_tables():
    c64, s64 = _dft_cs(64)
    eye4 = np.eye(4)
    chan = np.concatenate([np.kron(eye4, c64), -np.kron(eye4, s64)], axis=1)
    w1 = np.block([[c64, s64], [-s64, c64]])
    p1 = np.arange(FFT_L1)[:, None]
    l2 = np.arange(FFT_L2)[None, :]
    ang = 2.0 * np.pi * ((p1 * l2) % S) / S
    twr = np.repeat(np.cos(ang)[:, :, None], 128, axis=2)
    twi = np.repeat(np.sin(ang)[:, :, None], 128, axis=2)
    c128, s128 = _dft_cs(128)
    w2 = np.concatenate([c128, s128], axis=1) / math.sqrt(S * 64.0)
    c256, s256 = _dft_cs(CTX)
    wc = np.concatenate([c256, s256], axis=1) / math.sqrt(CTX * 64.0)
    f = lambda a: jnp.asarray(a, F32)
    return (f(chan).astype(BF16), f(w1).astype(BF16), f(twr), f(twi), f(w2).astype(BF16), f(wc).astype(BF16))


def _mod_slice(m, i):
    return m[:, i * D:(i + 1) * D]


def _rms_mod(h, nw, shift, scale):
    ms = jnp.mean(h * h, axis=-1, keepdims=True)
    y = h * lax.rsqrt(ms + EPS) * nw
    return y * (1.0 + scale) + shift


def _swiglu(u, w_in_ref, w_out_ref):
    y = None
    for lo, hi in FF_CHUNKS:
        g = jnp.dot(u, w_in_ref[:, lo:hi], preferred_element_type=F32)
        up = jnp.dot(u, w_in_ref[:, D_FF + lo:D_FF + hi], preferred_element_type=F32)
        a = (g * jax.nn.sigmoid(g) * up).astype(BF16)
        t = jnp.dot(a, w_out_ref[lo:hi, :], preferred_element_type=F32)
        y = t if y is None else y + t
    return y


def _ffn_half_step(h, m, nw, mod0, w_in_ref, w_out_ref):
    u = _rms_mod(h, nw, _mod_slice(m, mod0), _mod_slice(m, mod0 + 1)).astype(BF16)
    return h + 0.5 * _mod_slice(m, mod0 + 2) * _swiglu(u, w_in_ref, w_out_ref)


BF16_ROWS = 16


def _cast_specs(sources, n_steps):
    in_specs, out_specs, out_shapes = [], [], []
    for arr, lead, rows, cols in sources:
        n_blk = min(n_steps, rows // BF16_ROWS)
        while rows % (BF16_ROWS * n_blk):
            n_blk -= 1
        rb = rows // n_blk
        in_specs.append(pl.BlockSpec((None,) * len(lead) + (rb, cols),
                                     lambda i, lead=lead, n_blk=n_blk: (*lead, jnp.minimum(i, n_blk - 1), 0)))
        out_specs.append(pl.BlockSpec((rb, cols), lambda i, n_blk=n_blk: (jnp.minimum(i, n_blk - 1), 0)))
        out_shapes.append(jax.ShapeDtypeStruct((rows, cols), BF16))
    return in_specs, out_specs, out_shapes


def _cast_blocks(src_refs, dst_refs):
    for src, dst in zip(src_refs, dst_refs):
        dst[...] = src[...].astype(BF16)


def _log_sigmoid(x):
    return jnp.minimum(x, 0.0) - jnp.log1p(jnp.exp(-jnp.abs(x)))


def _lane_head(width):
    return lax.broadcasted_iota(jnp.int32, (1, width), 1) // HD


ADA_CB = 1152


def _ada_body(ct_ref, w_ref, b_ref, o_ref):
    ct = ct_ref[...]
    s = ct * jax.nn.sigmoid(ct)
    w = w_ref[0]
    rows = [jnp.sum(w * s[:, r:r + 1], axis=0, keepdims=True) for r in range(3)]
    rows.append(jnp.zeros((5, ADA_CB), F32))
    o_ref[0] = jnp.concatenate(rows, axis=0) + b_ref[0]


def _ada(cond_t, w_ada, b_ada):
    nb = (N_MOD * D) // ADA_CB
    return pl.pallas_call(
        _ada_body,
        grid=(DEPTH, nb),
        in_specs=[
            pl.BlockSpec((D, 8), lambda l, j: (0, 0)),
            pl.BlockSpec((1, D, ADA_CB), lambda l, j: (l, 0, j)),
            pl.BlockSpec((1, 1, ADA_CB), lambda l, j: (l, 0, j)),
        ],
        out_specs=pl.BlockSpec((1, 8, ADA_CB), lambda l, j: (l, 0, j)),
        out_shape=jax.ShapeDtypeStruct((DEPTH, 8, N_MOD * D), F32),
        compiler_params=_params(("arbitrary", "arbitrary")),
        name="ada_mod",
    )(cond_t, w_ada, b_ada.reshape(DEPTH, 1, N_MOD * D))


def _mod_spec(layer):
    return pl.BlockSpec((None, None, 1, N_MOD * D),
                        lambda i: (layer, jnp.minimum(i // (NT_LAT // B), B), 0, 0))


def _norm_spec(layer, k):
    return _const_spec((None, None, 1, D), (layer, k, 0, 0))


def _row_spec(width):
    return pl.BlockSpec((TM, width), lambda i: (i, 0))


def _lat_or_ctx_specs(width):
    return [pl.BlockSpec((TM, width), lambda i: (jnp.minimum(i, NT_LAT - 1), 0)),
            pl.BlockSpec((TM, width), lambda i: (0, 0))]


def _pick(lat_ref, ctx_ref):
    return jnp.where(pl.program_id(0) < NT_LAT, lat_ref[...], ctx_ref[...])


def _rope(x, cos, sin, lane_first_half):
    outs = []
    for j in range(x.shape[1] // 128):
        xb = x[:, j * 128:(j + 1) * 128]
        fwd = pltpu.roll(xb, 112, axis=1)
        bwd = pltpu.roll(xb, 16, axis=1)
        partner = jnp.where(lane_first_half, fwd, bwd)
        outs.append(xb * cos + partner * sin)
    return outs[0] if len(outs) == 1 else jnp.concatenate(outs, axis=1)


N_CAST = 3


def _ffn_inproj_body(first, *refs):
    refs = list(refs)
    if first:
        h = _pick(refs.pop(0), refs.pop(0))
    else:
        h = refs.pop(0)[...]
    (m_ref, nw0_ref, nw1_ref, fw_in_ref, fw_out_ref, w_ref, cos_ref, sin_ref, chan_ref) = refs[:9]
    cast_src = refs[9:9 + N_CAST]
    (h_out_ref, gr_ref, gi_ref, rq_ref, rk_ref, rv_ref, rg_ref, aq_ref, ak_ref, av_ref) = refs[9 + N_CAST:19 + N_CAST]
    _cast_blocks(cast_src, refs[19 + N_CAST:])
    m = m_ref[...]
    h = _ffn_half_step(h, m, nw0_ref[...], 0, fw_in_ref, fw_out_ref)
    h_out_ref[...] = h
    u = _rms_mod(h, nw1_ref[...], _mod_slice(m, 3), _mod_slice(m, 4)).astype(BF16)
    z = jnp.dot(u, w_ref[...], preferred_element_type=F32)
    cos = cos_ref[...]
    sin = sin_ref[...]
    lane = lax.broadcasted_iota(jnp.int32, (1, 128), 1)
    first_half = (lane // ROPE_PAIRS) % 2 == 0
    g = jnp.dot(z[:, 0:256].astype(BF16), chan_ref[...], preferred_element_type=F32)
    gr_ref[...] = g[:, 0:256]
    gi_ref[...] = g[:, 256:512]
    rq_ref[...] = _rope(z[:, 256:512], cos, sin, first_half).astype(BF16)
    rk_ref[...] = (_rope(z[:, 512:768], cos, sin, first_half) * (HD ** -0.5)).astype(BF16)
    rv_ref[...] = z[:, 768:1024].astype(BF16)
    rg_ref[...] = z[:, 1024:1280].astype(BF16)
    aq_ref[...] = (_rope(z[:, 1280:1792], cos, sin, first_half) * (HD ** -0.5 * LOG2E)).astype(BF16)
    ak_ref[...] = _rope(z[:, 1792:1920], cos, sin, first_half).astype(BF16)
    av_ref[...] = z[:, 1920:2048].astype(BF16)


def _rope_block(i):
    return jnp.where(i < NT_LAT, i % (S // TM), S // TM)


def _ffn_inproj(rows_in, mod_all, norm_w, weights, cast_sources, cos_t, sin_t, chan, layer):
    first = len(rows_in) == 2
    widths = (256, 256, 256, 256, 256, 256, 512, KV_W, KV_W)
    dtypes = [F32, F32] + [BF16] * (len(widths) - 2)
    cast_in, cast_out, cast_shapes = _cast_specs(cast_sources, NT_ALL)
    return pl.pallas_call(
        functools.partial(_ffn_inproj_body, first),
        grid=(NT_ALL,),
        in_specs=(_lat_or_ctx_specs(D) if first else [_row_spec(D)]) + [
            _mod_spec(layer),
            _norm_spec(layer, 0),
            _norm_spec(layer, 1),
            _const_spec((D, 2 * D_FF)), _const_spec((D_FF, D)), _const_spec((D, D_IN)),
            pl.BlockSpec((TM, 128), lambda i: (_rope_block(i), 0)),
            pl.BlockSpec((TM, 128), lambda i: (_rope_block(i), 0)),
            _const_spec((256, 512)),
        ] + cast_in,
        out_specs=[_row_spec(D)] + [_row_spec(w) for w in widths] + cast_out,
        out_shape=[jax.ShapeDtypeStruct((R_ALL, D), F32)]
        + [jax.ShapeDtypeStruct((R_ALL, w), dt) for w, dt in zip(widths, dtypes)] + cast_shapes,
        compiler_params=_params(("arbitrary",)),
        name="ffn_inproj_first" if first else "ffn_inproj",
    )(*rows_in, mod_all, norm_w, norm_w, *weights, cos_t, sin_t, chan, *[s[0] for s in cast_sources])


FFT_SUB = 16


def _fft1_body(w1_ref, xr_ref, xi_ref, z_ref):
    w1 = w1_ref[...]
    for s in range(FFT_SUB):
        x = jnp.concatenate([xr_ref[:, s, :], xi_ref[:, s, :]], axis=0).astype(BF16)
        z = jnp.dot(w1, x, preferred_element_type=F32)
        z_ref[0, :, s, :] = z[0:FFT_L1]
        z_ref[1, :, s, :] = z[FFT_L1:2 * FFT_L1]


def _fft1(w1, gr, gi):
    view = (R_ALL // FFT_L2, FFT_L2 // FFT_SUB, FFT_SUB, 256)
    x_spec = pl.BlockSpec((FFT_L1, None, FFT_SUB, 256), lambda b, j: (b, j, 0, 0))
    return pl.pallas_call(
        _fft1_body,
        grid=(B, FFT_L2 // FFT_SUB),
        in_specs=[_const_spec((128, 128)), x_spec, x_spec],
        out_specs=pl.BlockSpec((None, 2, FFT_L1, FFT_SUB, 256), lambda b, j: (b, 0, 0, j, 0)),
        out_shape=jax.ShapeDtypeStruct((B, 2, FFT_L1, FFT_L2, 256), F32),
        compiler_params=_params(("arbitrary", "arbitrary")),
        name="fft_stage1",
    )(w1, gr.reshape(view), gi.reshape(view))


def _fft2_body(w2_ref, twr_ref, twi_ref, z_ref, o_ref):
    w2 = w2_ref[...]
    for p in range(FFT_SUB):
        zr = z_ref[0, p]
        zi = z_ref[1, p]
        tr = jnp.concatenate([twr_ref[p]] * 2, axis=1)
        ti = jnp.concatenate([twi_ref[p]] * 2, axis=1)
        yr = tr * zr + ti * zi
        yi = tr * zi - ti * zr
        y = jnp.concatenate([yr, yi], axis=0).astype(BF16)
        o_ref[:, p, :] = jnp.dot(w2, y, preferred_element_type=F32)


def _fft2(w2, twr, twi, z):
    out = pl.pallas_call(
        _fft2_body,
        grid=(B, FFT_L1 // FFT_SUB),
        in_specs=[
            _const_spec((128, 256)),
            pl.BlockSpec((FFT_SUB, FFT_L2, 128), lambda b, j: (j, 0, 0)),
            pl.BlockSpec((FFT_SUB, FFT_L2, 128), lambda b, j: (j, 0, 0)),
            pl.BlockSpec((None, 2, FFT_SUB, FFT_L2, 256), lambda b, j: (b, 0, j, 0, 0)),
        ],
        out_specs=pl.BlockSpec((None, FFT_L2, None, FFT_SUB, 256), lambda b, j: (b, 0, j, 0, 0)),
        out_shape=jax.ShapeDtypeStruct((B, FFT_L2, FFT_L1 // FFT_SUB, FFT_SUB, 256), F32),
        compiler_params=_params(("arbitrary", "arbitrary")),
        name="fft_stage2",
    )(w2, twr, twi, z)
    return out.reshape(R_LAT, 256)


def _fft_ctx_body(wc_ref, gr_ref, gi_ref, o_ref):
    x = jnp.concatenate([gr_ref[...], gi_ref[...]], axis=0).astype(BF16)
    o_ref[...] = jnp.dot(wc_ref[...], x, preferred_element_type=F32)


def _fft_ctx(wc, gr, gi):
    blk0 = R_LAT // CTX
    return pl.pallas_call(
        _fft_ctx_body,
        grid=(B,),
        in_specs=[
            _const_spec((CTX, 2 * CTX)),
            pl.BlockSpec((CTX, 256), lambda b: (blk0 + b, 0)),
            pl.BlockSpec((CTX, 256), lambda b: (blk0 + b, 0)),
        ],
        out_specs=pl.BlockSpec((CTX, 256), lambda b: (b, 0)),
        out_shape=jax.ShapeDtypeStruct((R_CTX, 256), F32),
        compiler_params=_params(("arbitrary",)),
        name="fft_ctx",
    )(wc, gr, gi)


RS_CH = 8
RS_STEPS = NCH_LAT // RS_CH
N_STATE = (RS_STEPS + 1) * RS_CH
RO_CH = 8
CTX_ROWBLK = R_LAT // CTX


def _head_block_mask():
    r = lax.broadcasted_iota(jnp.int32, (256, 256), 0) // HD
    c = lax.broadcasted_iota(jnp.int32, (256, 256), 1) // HD
    return r == c


def _ret_state_body(dec_ref, kf_ref, vf_ref, kb_ref, vb_ref, kx_ref, vx_ref, sf_ref, sb_ref, accf, accb):
    t = pl.program_id(1)
    lg = _log_sigmoid(dec_ref[...])
    lgf = lg[0:1]
    lgb = lg[1:2]
    pos = lax.broadcasted_iota(jnp.int32, (CH, 1), 0).astype(F32)
    wf = jnp.exp((CH - 1.0 - pos) * lgf)
    wb = jnp.exp(pos * lgb)
    cdf = jnp.exp(float(CH) * lgf)
    cdb = jnp.exp(float(CH) * lgb)
    mask = _head_block_mask()
    dn = (((0,), (0,)), ((), ()))

    def scan(acc, out_ref, k_ref, v_ref, w, cd, order):
        kvs = {}
        for j in order:
            rows = slice(j * CH, (j + 1) * CH)
            kd = (k_ref[rows, :].astype(F32) * w).astype(BF16)
            kvs[j] = lax.dot_general(kd, v_ref[rows, :], dn, preferred_element_type=F32)
        state = acc[...]
        for j in order:
            out_ref[0, j] = state.astype(BF16)
            state = state * cd + jnp.where(mask, kvs[j], 0.0)
        acc[...] = state

    @pl.when(t == 0)
    def _():
        accf[...] = jnp.zeros_like(accf)
        accb[...] = jnp.zeros_like(accb)
        for j in range(NCH_CTX, RS_CH):
            sf_ref[0, j] = jnp.zeros((256, 256), BF16)
            sb_ref[0, j] = jnp.zeros((256, 256), BF16)
        scan(accf, sf_ref, kx_ref, vx_ref, wf, cdf, list(range(NCH_CTX)))
        scan(accb, sb_ref, kx_ref, vx_ref, wb, cdb, list(reversed(range(NCH_CTX))))

    @pl.when(t > 0)
    def _():
        scan(accf, sf_ref, kf_ref, vf_ref, wf, cdf, list(range(RS_CH)))
        scan(accb, sb_ref, kb_ref, vb_ref, wb, cdb, list(reversed(range(RS_CH))))


def _ret_state(dec, rk, rv, layer):
    rows = RS_CH * CH
    fblk = lambda t: jnp.maximum(t - 1, 0)
    bblk = lambda t: RS_STEPS - jnp.maximum(t, 1)
    fmap = lambda b, t: (b * RS_STEPS + fblk(t), 0)
    bmap = lambda b, t: (b * RS_STEPS + bblk(t), 0)
    xmap = lambda b, t: (CTX_ROWBLK + b, 0)
    return pl.pallas_call(
        _ret_state_body,
        grid=(B, RS_STEPS + 1),
        in_specs=[
            pl.BlockSpec((None, 2, 256), lambda b, t: (layer, 0, 0)),
            pl.BlockSpec((rows, 256), fmap),
            pl.BlockSpec((rows, 256), fmap),
            pl.BlockSpec((rows, 256), bmap),
            pl.BlockSpec((rows, 256), bmap),
            pl.BlockSpec((CTX, 256), xmap),
            pl.BlockSpec((CTX, 256), xmap),
        ],
        out_specs=[
            pl.BlockSpec((1, RS_CH, 256, 256), lambda b, t: (b, jnp.where(t == 0, RS_STEPS, t - 1), 0, 0)),
            pl.BlockSpec((1, RS_CH, 256, 256), lambda b, t: (b, jnp.where(t == 0, RS_STEPS, RS_STEPS - t), 0, 0)),
        ],
        out_shape=[jax.ShapeDtypeStruct((B, N_STATE, 256, 256), BF16)] * 2,
        scratch_shapes=[pltpu.VMEM((256, 256), F32), pltpu.VMEM((256, 256), F32)],
        compiler_params=_params(("arbitrary", "arbitrary")),
        name="ret_state",
    )(dec, rk, rv, rk, rv, rk, rv)


def _split_dot(x, w):
    hi = x.astype(BF16)
    lo = (x - hi.astype(F32)).astype(BF16)
    return jnp.dot(hi, w, preferred_element_type=F32) + jnp.dot(lo, w, preferred_element_type=F32)


def _ret_out_body(nck, dec_ref, gn_ref, q_ref, k_ref, v_ref, g_ref, sf_ref, sb_ref, o_ref):
    lg = _log_sigmoid(dec_ref[...])
    lgf = lg[0:1]
    lgb = lg[1:2]
    pos = lax.broadcasted_iota(jnp.int32, (CH, 1), 0).astype(F32)
    wqf = jnp.exp((pos + 1.0) * lgf)
    wqb = jnp.exp((float(CH) - pos) * lgb)
    ri = lax.broadcasted_iota(jnp.int32, (CH, CH), 0)
    ci = lax.broadcasted_iota(jnp.int32, (CH, CH), 1)
    diff = (ri - ci).astype(F32)
    decays = []
    for h in range(RET_HEADS):
        lgf_h = lgf[:, h * HD:h * HD + 1]
        lgb_h = lgb[:, h * HD:h * HD + 1]
        decays.append(jnp.where(ri >= ci, jnp.exp(jnp.maximum(diff, 0.0) * lgf_h), 0.0)
                      + jnp.where(ci >= ri, jnp.exp(jnp.maximum(-diff, 0.0) * lgb_h), 0.0))
    decay = jnp.concatenate(decays, axis=0)
    lane_head = _lane_head(256)
    avg = jnp.where(_head_block_mask(), 1.0 / HD, 0.0).astype(BF16)
    gn = gn_ref[...]
    dn = (((1,), (1,)), ((), ()))
    chunk_rows = [slice(c * CH, (c + 1) * CH) for c in range(nck)]
    cross, scores = [], []
    for c, rows in enumerate(chunk_rows):
        qb16 = q_ref[rows, :]
        q = qb16.astype(F32)
        cross.append(jnp.dot((q * wqf).astype(BF16), sf_ref[0, c], preferred_element_type=F32)
                     + jnp.dot((q * wqb).astype(BF16), sb_ref[0, c], preferred_element_type=F32))
        qs = jnp.concatenate([jnp.where(lane_head == h, qb16, jnp.zeros_like(qb16))
                              for h in range(RET_HEADS)], axis=0)
        scores.append(lax.dot_general(qs, k_ref[rows, :], dn, preferred_element_type=F32))
    outs = []
    for c, rows in enumerate(chunk_rows):
        p = (scores[c] * decay).astype(BF16)
        of = jnp.dot(p, v_ref[rows, :], preferred_element_type=F32)
        o = cross[c]
        for h in range(RET_HEADS):
            o = o + jnp.where(lane_head == h, of[h * CH:(h + 1) * CH], 0.0)
        outs.append(o)
    o = jnp.concatenate(outs, axis=0)
    mu = _split_dot(o, avg)
    cen = o - mu
    var = _split_dot(cen * cen, avg)
    on = cen * lax.rsqrt(var + EPS) * gn
    g = g_ref[...].astype(F32)
    o_ref[...] = (g * jax.nn.sigmoid(g) * on).astype(BF16)


def _ret_out(dec, gn, rq, rk, rv, rg, sf, sb, layer, ctx):
    nck = NCH_CTX if ctx else RO_CH
    rows = nck * CH
    if ctx:
        grid = (B, 1)
        rmap = lambda b, u: (CTX_ROWBLK + b, 0)
        omap = lambda b, u: (b, 0)
        smap = lambda b, u: (b, NCH_LAT // NCH_CTX, 0, 0)
        out_rows = R_CTX
    else:
        grid = (B, NCH_LAT // RO_CH)
        rmap = lambda b, u: (b * (NCH_LAT // RO_CH) + u, 0)
        omap = rmap
        smap = lambda b, u: (b, u, 0, 0)
        out_rows = R_LAT
    return pl.pallas_call(
        functools.partial(_ret_out_body, nck),
        grid=grid,
        in_specs=[
            pl.BlockSpec((None, 2, 256), lambda b, u: (layer, 0, 0)),
            pl.BlockSpec((None, 1, 256), lambda b, u: (layer, 0, 0)),
            pl.BlockSpec((rows, 256), rmap),
            pl.BlockSpec((rows, 256), rmap),
            pl.BlockSpec((rows, 256), rmap),
            pl.BlockSpec((rows, 256), rmap),
            pl.BlockSpec((1, nck, 256, 256), smap),
            pl.BlockSpec((1, nck, 256, 256), smap),
        ],
        out_specs=pl.BlockSpec((rows, 256), omap),
        out_shape=jax.ShapeDtypeStruct((out_rows, 256), BF16),
        compiler_params=_params(("arbitrary", "arbitrary")),
        name="ret_out_ctx" if ctx else "ret_out",
    )(dec, gn, rq, rk, rv, rg, sf, sb)


def _tile_head(x_all, hk):
    own = jnp.where(_lane_head(KV_W) == hk, x_all, jnp.zeros_like(x_all))
    both = own + pltpu.roll(own.astype(F32), HD, axis=1).astype(BF16)
    return jnp.concatenate([both, both], axis=1)


def _gqa_groups(groups):
    lane_head = _lane_head(256)
    chains = []
    for gi, (q_grp, kt, v_t, sinks, mask_fn) in enumerate(groups):
        for h0 in range(0, ATT_GROUP, ATT_STACK):
            chains.append((gi, tuple(range(h0, h0 + ATT_STACK)), q_grp, kt, v_t, sinks, mask_fn))
    scores = []
    for _, heads, q_grp, kt, _, _, _ in chains:
        qs = jnp.concatenate([jnp.where(lane_head == g, q_grp, jnp.zeros_like(q_grp))
                              for g in heads], axis=0)
        scores.append(lax.dot_general(kt, qs, (((1,), (1,)), ((), ())), preferred_element_type=F32))
    probs, inv_den = [], []
    for (_, heads, q_grp, _, _, sinks, mask_fn), st in zip(chains, scores):
        m = q_grp.shape[0]
        sink = jnp.concatenate([jnp.full((1, m), sinks[g] * LOG2E, F32) for g in heads], axis=1)
        st = mask_fn(st)
        mx = jnp.maximum(jnp.max(st, axis=0, keepdims=True), sink)
        p = jnp.exp2(st - mx)
        inv_den.append(1.0 / (jnp.sum(p, axis=0, keepdims=True) + jnp.exp2(sink - mx)))
        probs.append(p.astype(BF16))
    pieces = [[] for _ in groups]
    for (gi, _, q_grp, _, v_t, _, _), p, inv in zip(chains, probs, inv_den):
        m = q_grp.shape[0]
        ot = jnp.dot(v_t, p, preferred_element_type=F32) * inv
        stacked = jnp.concatenate([ot[:, i * m:(i + 1) * m] for i in range(ATT_STACK)], axis=0)
        pieces[gi].append(stacked.T)
    return [jnp.concatenate(ps, axis=1) for ps in pieces]


def _band_mask_fn(lo_rows, lo_ok, hi_rows, hi_ok, n_keys):
    def mask_fn(st):
        parts = []
        row = 0
        while row < n_keys:
            piece = st[row:row + CH]
            if row == lo_rows:
                piece = jnp.where(lo_ok, piece, NEG_INF)
            elif row == hi_rows:
                piece = jnp.where(hi_ok, piece, NEG_INF)
            parts.append(piece)
            row += CH
        return jnp.concatenate(parts, axis=0)
    return mask_fn


def _attn_lat_body(layer, sink_ref, q_ref, kp_ref, kc_ref, kn_ref, vp_ref, vc_ref, vn_ref, kx_ref, vx_ref, o_ref):
    n2 = pl.program_id(1)
    k_all = jnp.concatenate([kp_ref[...], kc_ref[...], kx_ref[...], kn_ref[...]], axis=0)
    v_all = jnp.concatenate([vp_ref[...], vc_ref[...], vx_ref[...], vn_ref[...]], axis=0)
    v_all_t = v_all.astype(F32).T.astype(BF16)
    key = lax.broadcasted_iota(jnp.int32, (CH, ATT_STACK * CH), 0)
    qry = lax.broadcasted_iota(jnp.int32, (CH, ATT_STACK * CH), 1) & (CH - 1)
    below = key >= qry
    above = key <= qry
    span = 3 * CH + CTX
    mask_a = _band_mask_fn(0, below & (n2 > 0), 2 * CH, above, span)
    mask_b = _band_mask_fn(0, below, 2 * CH + CTX, above & (n2 < NCH_LAT // 2 - 1), span)
    groups = []
    for hk in range(ATT_KV):
        sinks = [sink_ref[layer, hk * ATT_GROUP + g] for g in range(ATT_GROUP)]
        kt = _tile_head(k_all, hk)
        v_t = v_all_t[hk * HD:(hk + 1) * HD]
        cols = slice(hk * 256, (hk + 1) * 256)
        groups.append((q_ref[0:CH, cols], kt[0:span], v_t[:, 0:span], sinks, mask_a))
        groups.append((q_ref[CH:2 * CH, cols], kt[CH:CH + span], v_t[:, CH:CH + span], sinks, mask_b))
    outs = _gqa_groups(groups)
    for hk in range(ATT_KV):
        cols = slice(hk * 256, (hk + 1) * 256)
        o_ref[0:CH, cols] = outs[2 * hk].astype(BF16)
        o_ref[CH:2 * CH, cols] = outs[2 * hk + 1].astype(BF16)


def _attn_lat(sink, aq, ak, av, layer):
    nq = NCH_LAT // 2
    qmap = lambda b, n: (b * nq + n, 0)
    pmap = lambda b, n: (b * NCH_LAT + jnp.maximum(2 * n - 1, 0), 0)
    nmap = lambda b, n: (b * NCH_LAT + jnp.minimum(2 * n + 2, NCH_LAT - 1), 0)
    xmap = lambda b, n: (CTX_ROWBLK + b, 0)
    kv1 = lambda m: pl.BlockSpec((CH, KV_W), m)
    kv2 = pl.BlockSpec((2 * CH, KV_W), qmap)
    ctx_spec = pl.BlockSpec((CTX, KV_W), xmap)
    return pl.pallas_call(
        functools.partial(_attn_lat_body, layer),
        grid=(B, nq),
        in_specs=[
            pl.BlockSpec(memory_space=pltpu.SMEM),
            pl.BlockSpec((2 * CH, ATT_HEADS * HD), qmap),
            kv1(pmap), kv2, kv1(nmap),
            kv1(pmap), kv2, kv1(nmap),
            ctx_spec, ctx_spec,
        ],
        out_specs=pl.BlockSpec((2 * CH, ATT_HEADS * HD), qmap),
        out_shape=jax.ShapeDtypeStruct((R_LAT, ATT_HEADS * HD), BF16),
        compiler_params=_params(("arbitrary", "arbitrary")),
        name="swa_attn",
    )(sink, aq, ak, ak, ak, av, av, av, ak, av)


def _attn_ctx_body(layer, sink_ref, q_ref, kx_ref, vx_ref, o_ref):
    groups = []
    v_all_t = vx_ref[...].astype(F32).T.astype(BF16)
    for hk in range(ATT_KV):
        sinks = [sink_ref[layer, hk * ATT_GROUP + g] for g in range(ATT_GROUP)]
        groups.append((q_ref[:, hk * 256:(hk + 1) * 256], _tile_head(kx_ref[...], hk),
                       v_all_t[hk * HD:(hk + 1) * HD], sinks, lambda s: s))
    outs = _gqa_groups(groups)
    for hk in range(ATT_KV):
        o_ref[:, hk * 256:(hk + 1) * 256] = outs[hk].astype(BF16)


def _attn_ctx(sink, aq, ak, av, layer):
    xmap = lambda b: (CTX_ROWBLK + b, 0)
    return pl.pallas_call(
        functools.partial(_attn_ctx_body, layer),
        grid=(B,),
        in_specs=[
            pl.BlockSpec(memory_space=pltpu.SMEM),
            pl.BlockSpec((CTX, ATT_HEADS * HD), xmap),
            pl.BlockSpec((CTX, KV_W), xmap),
            pl.BlockSpec((CTX, KV_W), xmap),
        ],
        out_specs=pl.BlockSpec((CTX, ATT_HEADS * HD), lambda b: (b, 0)),
        out_shape=jax.ShapeDtypeStruct((R_CTX, ATT_HEADS * HD), BF16),
        compiler_params=_params(("arbitrary",)),
        name="ctx_attn",
    )(sink, aq, ak, av)


def _mix_ffn_body(has_ctx, final, n_cast, *refs):
    refs = list(refs)
    h_ref = refs.pop(0)
    mixers = []
    for _ in range(3):
        if has_ctx:
            lat_ref, ctx_ref = refs.pop(0), refs.pop(0)
            mixers.append(_pick(lat_ref, ctx_ref))
        else:
            mixers.append(refs.pop(0)[...])
    m_ref, nw_ref, wo_ref, w_in_ref, w_out_ref = refs[:5]
    refs = refs[5:]
    if final:
        fw_ref = refs.pop(0)
    cast_src, refs = refs[:n_cast], refs[n_cast:]
    o_ref = refs.pop(0)
    _cast_blocks(cast_src, refs)
    yf, yr, ya = mixers
    m = m_ref[...]
    y = (jnp.dot(yf.astype(BF16), wo_ref[0:256, :], preferred_element_type=F32)
         + jnp.dot(yr, wo_ref[256:512, :], preferred_element_type=F32)
         + jnp.dot(ya, wo_ref[512:1024, :], preferred_element_type=F32))
    h = h_ref[...] + _mod_slice(m, 5) * y
    h = _ffn_half_step(h, m, nw_ref[...], 6, w_in_ref, w_out_ref)
    if final:
        ms = jnp.mean(h * h, axis=-1, keepdims=True)
        h = h * lax.rsqrt(ms + EPS) * fw_ref[...]
    o_ref[...] = h


def _mix_ffn(h, mixers, mod_all, norm_w, weights, cast_sources, final_w, layer):
    has_ctx = mixers[0][1] is not None
    final = final_w is not None
    nt = NT_ALL if has_ctx else NT_LAT
    cast_in, cast_out, cast_shapes = _cast_specs(cast_sources, nt)
    in_specs = [_row_spec(D)]
    args = [h]
    for (lat, ctx), width in zip(mixers, (256, 256, 512)):
        if has_ctx:
            in_specs += _lat_or_ctx_specs(width)
            args += [lat, ctx]
        else:
            in_specs += [_row_spec(width)]
            args += [lat]
    in_specs += [_mod_spec(layer), _norm_spec(layer, 2),
                 _const_spec((D, D)), _const_spec((D, 2 * D_FF)), _const_spec((D_FF, D))]
    args += [mod_all, norm_w, *weights]
    if final:
        in_specs += [_const_spec((1, D))]
        args += [final_w]
    in_specs += cast_in
    args += [s[0] for s in cast_sources]
    return pl.pallas_call(
        functools.partial(_mix_ffn_body, has_ctx, final, len(cast_sources)),
        grid=(nt,),
        in_specs=in_specs,
        out_specs=[_row_spec(D)] + cast_out,
        out_shape=[jax.ShapeDtypeStruct((nt * TM, D), F32)] + cast_shapes,
        compiler_params=_params(("arbitrary",)),
        name="mix_ffn_final" if final else "mix_ffn",
    )(*args)


def kernel(x, c, ctx, c_ctx, norm_w, w_ada, b_ada, ffn_w_in, ffn_w_out, w_in, w_o, ret_decay, ret_gn_w,
           attn_sink, final_norm_w):
    assert x.shape == (B, S, D) and ctx.shape == (B, CTX, D)
    cos_t, sin_t = _rope_tables()
    chan, w1, twr, twi, w2, wc = _fourier_tables()

    cond_t = jnp.zeros((D, 8), F32).at[:, 0:B].set(c.T).at[:, B].set(c_ctx)
    mod_all = _ada(cond_t, w_ada, b_ada).reshape(DEPTH, 8, 1, N_MOD * D)

    def ffn_sources(layer, j):
        return [(ffn_w_in, (layer, j), D, 2 * D_FF), (ffn_w_out, (layer, j), D_FF, D)]

    pre_w = (ffn_w_in[0, 0].astype(BF16), ffn_w_out[0, 0].astype(BF16), w_in[0].astype(BF16))
    norm4 = norm_w.reshape(DEPTH, 3, 1, D)
    gn3 = ret_gn_w.reshape(DEPTH, 1, 256)
    dec = jnp.repeat(ret_decay.astype(F32), HD, axis=2)
    sink = attn_sink.astype(F32)

    h = None
    out = None
    for layer in range(DEPTH):
        last = layer == DEPTH - 1
        rows_in = (x.reshape(R_LAT, D), ctx.reshape(R_CTX, D)) if layer == 0 else (h,)
        mix_sources = [(w_o, (layer,), D, D)] + ffn_sources(layer, 1)
        (h, gr, gi, rq, rk, rv, rg, aq, ak, av, *mix_w) = _ffn_inproj(
            rows_in, mod_all, norm4, pre_w, mix_sources, cos_t, sin_t, chan, layer)

        yf = _fft2(w2, twr, twi, _fft1(w1, gr, gi))
        sf, sb = _ret_state(dec, rk, rv, layer)
        yr = _ret_out(dec, gn3, rq, rk, rv, rg, sf, sb, layer, ctx=False)
        ya = _attn_lat(sink, aq, ak, av, layer)
        if last:
            yfc = yrc = yac = None
        else:
            yfc = _fft_ctx(wc, gr, gi)
            yrc = _ret_out(dec, gn3, rq, rk, rv, rg, sf, sb, layer, ctx=True)
            yac = _attn_ctx(sink, aq, ak, av, layer)

        next_sources = [] if last else ffn_sources(layer + 1, 0) + [(w_in, (layer + 1,), D, D_IN)]
        res, *pre_w = _mix_ffn(h, [(yf, yfc), (yr, yrc), (ya, yac)], mod_all, norm4, mix_w, next_sources,
                               final_norm_w.reshape(1, D) if last else None, layer)
        if last:
            out = res
        else:
            h = res
    return out.reshape(B, S, D)
```

```python
import functools
import math

import numpy as np
import jax
import jax.numpy as jnp
from jax import lax
from jax.experimental import pallas as pl
from jax.experimental.pallas import tpu as pltpu

F32 = jnp.float32
BF16 = jnp.bfloat16

D = 1024
B = 2
S = 8192
DEPTH = 2
GRID_W = 64
CTX = 256
HD = 64
N_MOD = 9
D_FF = 2816
D_IN = 2048
EPS = 1e-6
NEG_INF = -1e30
ROPE_BASE = 10000.0
ROPE_PAIRS = 16
RET_HEADS = 4
ATT_HEADS = 8
ATT_KV = 2
ATT_GROUP = ATT_HEADS // ATT_KV
ATT_STACK = 4
ATT_QB = 8
KV_W = ATT_KV * HD
LOG2E = math.log2(math.e)
WINDOW = 128

R_LAT = B * S
R_CTX = B * CTX
R_ALL = R_LAT + R_CTX
TM = 512
NT_LAT = R_LAT // TM
NT_ALL = R_ALL // TM
CH = 128
NCH_LAT = S // CH
NCH_CTX = CTX // CH
MXU_N = 256
FF_CHUNKS = ((0, 5 * MXU_N), (5 * MXU_N, D_FF))
FFT_L1 = 64
FFT_L2 = 128
VMEM_LIMIT = 56 * 1024 * 1024


def _params(sem, vmem=VMEM_LIMIT):
    return pltpu.CompilerParams(dimension_semantics=sem, vmem_limit_bytes=vmem)


def _const_spec(shape, index=None):
    if index is None:
        index = (0,) * len(shape)
    return pl.BlockSpec(shape, lambda *_: index, pipeline_mode=pl.Buffered(1))


def _rope_tables():
    rows = S // GRID_W
    row = np.repeat(np.arange(rows, dtype=np.float64), GRID_W)
    col = np.tile(np.arange(GRID_W, dtype=np.float64), rows)
    inv = ROPE_BASE ** (-np.arange(ROPE_PAIRS, dtype=np.float64) / ROPE_PAIRS)
    ar = row[:, None] * inv
    ac = col[:, None] * inv
    cos64 = np.concatenate([np.cos(ar), np.cos(ar), np.cos(ac), np.cos(ac)], axis=1)
    sin64 = np.concatenate([-np.sin(ar), np.sin(ar), -np.sin(ac), np.sin(ac)], axis=1)
    cos = np.concatenate([cos64, cos64], axis=1)
    sin = np.concatenate([sin64, sin64], axis=1)
    cos = np.concatenate([cos, np.ones((TM, 128))], axis=0)
    sin = np.concatenate([sin, np.zeros((TM, 128))], axis=0)
    return jnp.asarray(cos, F32), jnp.asarray(sin, F32)


def _dft_cs(n):
    k = np.arange(n)
    ang = 2.0 * np.pi * ((k[:, None] * k[None, :]) % n) / n
    return np.cos(ang), np.sin(ang)


def _fourier_tables():
    c64, s64 = _dft_cs(64)
    eye4 = np.eye(4)
    chan = np.concatenate([np.kron(eye4, c64), -np.kron(eye4, s64)], axis=1)
    w1 = np.block([[c64, s64], [-s64, c64]])
    p1 = np.arange(FFT_L1)[:, None]
    l2 = np.arange(FFT_L2)[None, :]
    ang = 2.0 * np.pi * ((p1 * l2) % S) / S
    twr = np.repeat(np.cos(ang)[:, :, None], 128, axis=2)
    twi = np.repeat(np.sin(ang)[:, :, None], 128, axis=2)
    c128, s128 = _dft_cs(128)
    w2 = np.concatenate([c128, s128], axis=1) / math.sqrt(S * 64.0)
    c256, s256 = _dft_cs(CTX)
    wc = np.concatenate([c256, s256], axis=1) / math.sqrt(CTX * 64.0)
    f = lambda a: jnp.asarray(a, F32)
    return (f(chan).astype(BF16), f(w1).astype(BF16), f(twr), f(twi), f(w2).astype(BF16), f(wc).astype(BF16))


def _mod_slice(m, i):
    return m[:, i * D:(i + 1) * D]


def _rms_mod(h, nw, shift, scale):
    ms = jnp.mean(h * h, axis=-1, keepdims=True)
    y = h * lax.rsqrt(ms + EPS) * nw
    return y * (1.0 + scale) + shift


def _swiglu(u, w_in_ref, w_out_ref):
    y = None
    for lo, hi in FF_CHUNKS:
        g = jnp.dot(u, w_in_ref[:, lo:hi], preferred_element_type=F32)
        up = jnp.dot(u, w_in_ref[:, D_FF + lo:D_FF + hi], preferred_element_type=F32)
        a = (g * jax.nn.sigmoid(g) * up).astype(BF16)
        t = jnp.dot(a, w_out_ref[lo:hi, :], preferred_element_type=F32)
        y = t if y is None else y + t
    return y


def _ffn_half_step(h, m, nw, mod0, w_in_ref, w_out_ref):
    u = _rms_mod(h, nw, _mod_slice(m, mod0), _mod_slice(m, mod0 + 1)).astype(BF16)
    return h + 0.5 * _mod_slice(m, mod0 + 2) * _swiglu(u, w_in_ref, w_out_ref)


BF16_ROWS = 16


def _cast_specs(sources, n_steps):
    in_specs, out_specs, out_shapes = [], [], []
    for arr, lead, rows, cols in sources:
        n_blk = min(n_steps, rows // BF16_ROWS)
        while rows % (BF16_ROWS * n_blk):
            n_blk -= 1
        rb = rows // n_blk
        in_specs.append(pl.BlockSpec((None,) * len(lead) + (rb, cols),
                                     lambda i, lead=lead, n_blk=n_blk: (*lead, jnp.minimum(i, n_blk - 1), 0)))
        out_specs.append(pl.BlockSpec((rb, cols), lambda i, n_blk=n_blk: (jnp.minimum(i, n_blk - 1), 0)))
        out_shapes.append(jax.ShapeDtypeStruct((rows, cols), BF16))
    return in_specs, out_specs, out_shapes


def _cast_blocks(src_refs, dst_refs):
    for src, dst in zip(src_refs, dst_refs):
        dst[...] = src[...].astype(BF16)


def _log_sigmoid(x):
    return jnp.minimum(x, 0.0) - jnp.log1p(jnp.exp(-jnp.abs(x)))


def _lane_head(width):
    return lax.broadcasted_iota(jnp.int32, (1, width), 1) // HD


ADA_CB = 1152


def _ada_body(ct_ref, w_ref, b_ref, o_ref):
    ct = ct_ref[...]
    s = ct * jax.nn.sigmoid(ct)
    w = w_ref[0]
    rows = [jnp.sum(w * s[:, r:r + 1], axis=0, keepdims=True) for r in range(3)]
    rows.append(jnp.zeros((5, ADA_CB), F32))
    o_ref[0] = jnp.concatenate(rows, axis=0) + b_ref[0]


def _ada(cond_t, w_ada, b_ada):
    nb = (N_MOD * D) // ADA_CB
    return pl.pallas_call(
        _ada_body,
        grid=(DEPTH, nb),
        in_specs=[
            pl.BlockSpec((D, 8), lambda l, j: (0, 0)),
            pl.BlockSpec((1, D, ADA_CB), lambda l, j: (l, 0, j)),
            pl.BlockSpec((1, 1, ADA_CB), lambda l, j: (l, 0, j)),
        ],
        out_specs=pl.BlockSpec((1, 8, ADA_CB), lambda l, j: (l, 0, j)),
        out_shape=jax.ShapeDtypeStruct((DEPTH, 8, N_MOD * D), F32),
        compiler_params=_params(("arbitrary", "arbitrary")),
        name="ada_mod",
    )(cond_t, w_ada, b_ada.reshape(DEPTH, 1, N_MOD * D))


def _mod_spec(layer):
    return pl.BlockSpec((None, None, 1, N_MOD * D),
                        lambda i: (layer, jnp.minimum(i // (NT_LAT // B), B), 0, 0))


def _norm_spec(layer, k):
    return _const_spec((None, None, 1, D), (layer, k, 0, 0))


def _row_spec(width):
    return pl.BlockSpec((TM, width), lambda i: (i, 0))


def _lat_or_ctx_specs(width):
    return [pl.BlockSpec((TM, width), lambda i: (jnp.minimum(i, NT_LAT - 1), 0)),
            pl.BlockSpec((TM, width), lambda i: (0, 0))]


def _pick(lat_ref, ctx_ref):
    return jnp.where(pl.program_id(0) < NT_LAT, lat_ref[...], ctx_ref[...])


def _rope(x, cos, sin, lane_first_half):
    outs = []
    for j in range(x.shape[1] // 128):
        xb = x[:, j * 128:(j + 1) * 128]
        fwd = pltpu.roll(xb, 112, axis=1)
        bwd = pltpu.roll(xb, 16, axis=1)
        partner = jnp.where(lane_first_half, fwd, bwd)
        outs.append(xb * cos + partner * sin)
    return outs[0] if len(outs) == 1 else jnp.concatenate(outs, axis=1)


N_CAST = 3


def _ffn_inproj_body(first, *refs):
    refs = list(refs)
    if first:
        h = _pick(refs.pop(0), refs.pop(0))
    else:
        h = refs.pop(0)[...]
    (m_ref, nw0_ref, nw1_ref, fw_in_ref, fw_out_ref, w_ref, cos_ref, sin_ref, chan_ref) = refs[:9]
    cast_src = refs[9:9 + N_CAST]
    (h_out_ref, gr_ref, gi_ref, rq_ref, rk_ref, rv_ref, rg_ref, aq_ref, ak_ref, av_ref) = refs[9 + N_CAST:19 + N_CAST]
    _cast_blocks(cast_src, refs[19 + N_CAST:])
    m = m_ref[...]
    h = _ffn_half_step(h, m, nw0_ref[...], 0, fw_in_ref, fw_out_ref)
    h_out_ref[...] = h
    u = _rms_mod(h, nw1_ref[...], _mod_slice(m, 3), _mod_slice(m, 4)).astype(BF16)
    z = jnp.dot(u, w_ref[...], preferred_element_type=F32)
    cos = cos_ref[...]
    sin = sin_ref[...]
    lane = lax.broadcasted_iota(jnp.int32, (1, 128), 1)
    first_half = (lane // ROPE_PAIRS) % 2 == 0
    g = jnp.dot(z[:, 0:256].astype(BF16), chan_ref[...], preferred_element_type=F32)
    gr_ref[...] = g[:, 0:256]
    gi_ref[...] = g[:, 256:512]
    rq_ref[...] = _rope(z[:, 256:512], cos, sin, first_half).astype(BF16)
    rk_ref[...] = (_rope(z[:, 512:768], cos, sin, first_half) * (HD ** -0.5)).astype(BF16)
    rv_ref[...] = z[:, 768:1024].astype(BF16)
    rg_ref[...] = z[:, 1024:1280].astype(BF16)
    aq_ref[...] = (_rope(z[:, 1280:1792], cos, sin, first_half) * (HD ** -0.5 * LOG2E)).astype(BF16)
    ak_ref[...] = _rope(z[:, 1792:1920], cos, sin, first_half).astype(BF16)
    av_ref[...] = z[:, 1920:2048].astype(BF16)


def _rope_block(i):
    return jnp.where(i < NT_LAT, i % (S // TM), S // TM)


def _ffn_inproj(rows_in, mod_all, norm_w, weights, cast_sources, cos_t, sin_t, chan, layer):
    first = len(rows_in) == 2
    widths = (256, 256, 256, 256, 256, 256, 512, KV_W, KV_W)
    dtypes = [F32, F32] + [BF16] * (len(widths) - 2)
    cast_in, cast_out, cast_shapes = _cast_specs(cast_sources, NT_ALL)
    return pl.pallas_call(
        functools.partial(_ffn_inproj_body, first),
        grid=(NT_ALL,),
        in_specs=(_lat_or_ctx_specs(D) if first else [_row_spec(D)]) + [
            _mod_spec(layer),
            _norm_spec(layer, 0),
            _norm_spec(layer, 1),
            _const_spec((D, 2 * D_FF)), _const_spec((D_FF, D)), _const_spec((D, D_IN)),
            pl.BlockSpec((TM, 128), lambda i: (_rope_block(i), 0)),
            pl.BlockSpec((TM, 128), lambda i: (_rope_block(i), 0)),
            _const_spec((256, 512)),
        ] + cast_in,
        out_specs=[_row_spec(D)] + [_row_spec(w) for w in widths] + cast_out,
        out_shape=[jax.ShapeDtypeStruct((R_ALL, D), F32)]
        + [jax.ShapeDtypeStruct((R_ALL, w), dt) for w, dt in zip(widths, dtypes)] + cast_shapes,
        compiler_params=_params(("arbitrary",)),
        name="ffn_inproj_first" if first else "ffn_inproj",
    )(*rows_in, mod_all, norm_w, norm_w, *weights, cos_t, sin_t, chan, *[s[0] for s in cast_sources])


FFT_SUB = 16


def _fft1_body(w1_ref, xr_ref, xi_ref, z_ref):
    w1 = w1_ref[...]
    for s in range(FFT_SUB):
        x = jnp.concatenate([xr_ref[:, s, :], xi_ref[:, s, :]], axis=0).astype(BF16)
        z = jnp.dot(w1, x, preferred_element_type=F32)
        z_ref[0, :, s, :] = z[0:FFT_L1]
        z_ref[1, :, s, :] = z[FFT_L1:2 * FFT_L1]


def _fft1(w1, gr, gi):
    view = (R_ALL // FFT_L2, FFT_L2 // FFT_SUB, FFT_SUB, 256)
    x_spec = pl.BlockSpec((FFT_L1, None, FFT_SUB, 256), lambda b, j: (b, j, 0, 0))
    return pl.pallas_call(
        _fft1_body,
        grid=(B, FFT_L2 // FFT_SUB),
        in_specs=[_const_spec((128, 128)), x_spec, x_spec],
        out_specs=pl.BlockSpec((None, 2, FFT_L1, FFT_SUB, 256), lambda b, j: (b, 0, 0, j, 0)),
        out_shape=jax.ShapeDtypeStruct((B, 2, FFT_L1, FFT_L2, 256), F32),
        compiler_params=_params(("arbitrary", "arbitrary")),
        name="fft_stage1",
    )(w1, gr.reshape(view), gi.reshape(view))


def _fft2_body(w2_ref, twr_ref, twi_ref, z_ref, o_ref):
    w2 = w2_ref[...]
    for p in range(FFT_SUB):
        zr = z_ref[0, p]
        zi = z_ref[1, p]
        tr = jnp.concatenate([twr_ref[p]] * 2, axis=1)
        ti = jnp.concatenate([twi_ref[p]] * 2, axis=1)
        yr = tr * zr + ti * zi
        yi = tr * zi - ti * zr
        y = jnp.concatenate([yr, yi], axis=0).astype(BF16)
        o_ref[:, p, :] = jnp.dot(w2, y, preferred_element_type=F32)


def _fft2(w2, twr, twi, z):
    out = pl.pallas_call(
        _fft2_body,
        grid=(B, FFT_L1 // FFT_SUB),
        in_specs=[
            _const_spec((128, 256)),
            pl.BlockSpec((FFT_SUB, FFT_L2, 128), lambda b, j: (j, 0, 0)),
            pl.BlockSpec((FFT_SUB, FFT_L2, 128), lambda b, j: (j, 0, 0)),
            pl.BlockSpec((None, 2, FFT_SUB, FFT_L2, 256), lambda b, j: (b, 0, j, 0, 0)),
        ],
        out_specs=pl.BlockSpec((None, FFT_L2, None, FFT_SUB, 256), lambda b, j: (b, 0, j, 0, 0)),
        out_shape=jax.ShapeDtypeStruct((B, FFT_L2, FFT_L1 // FFT_SUB, FFT_SUB, 256), F32),
        compiler_params=_params(("arbitrary", "arbitrary")),
        name="fft_stage2",
    )(w2, twr, twi, z)
    return out.reshape(R_LAT, 256)


def _fft_ctx_body(wc_ref, gr_ref, gi_ref, o_ref):
    x = jnp.concatenate([gr_ref[...], gi_ref[...]], axis=0).astype(BF16)
    o_ref[...] = jnp.dot(wc_ref[...], x, preferred_element_type=F32)


def _fft_ctx(wc, gr, gi):
    blk0 = R_LAT // CTX
    return pl.pallas_call(
        _fft_ctx_body,
        grid=(B,),
        in_specs=[
            _const_spec((CTX, 2 * CTX)),
            pl.BlockSpec((CTX, 256), lambda b: (blk0 + b, 0)),
            pl.BlockSpec((CTX, 256), lambda b: (blk0 + b, 0)),
        ],
        out_specs=pl.BlockSpec((CTX, 256), lambda b: (b, 0)),
        out_shape=jax.ShapeDtypeStruct((R_CTX, 256), F32),
        compiler_params=_params(("arbitrary",)),
        name="fft_ctx",
    )(wc, gr, gi)


RS_CH = 8
RS_STEPS = NCH_LAT // RS_CH
N_STATE = (RS_STEPS + 1) * RS_CH
RO_CH = 8
CTX_ROWBLK = R_LAT // CTX


def _head_block_mask():
    r = lax.broadcasted_iota(jnp.int32, (256, 256), 0) // HD
    c = lax.broadcasted_iota(jnp.int32, (256, 256), 1) // HD
    return r == c


def _ret_state_body(dec_ref, kf_ref, vf_ref, kb_ref, vb_ref, kx_ref, vx_ref, sf_ref, sb_ref, accf, accb):
    t = pl.program_id(1)
    lg = _log_sigmoid(dec_ref[...])
    lgf = lg[0:1]
    lgb = lg[1:2]
    pos = lax.broadcasted_iota(jnp.int32, (CH, 1), 0).astype(F32)
    wf = jnp.exp((CH - 1.0 - pos) * lgf)
    wb = jnp.exp(pos * lgb)
    cdf = jnp.exp(float(CH) * lgf)
    cdb = jnp.exp(float(CH) * lgb)
    mask = _head_block_mask()
    dn = (((0,), (0,)), ((), ()))

    def scan(acc, out_ref, k_ref, v_ref, w, cd, order):
        kvs = {}
        for j in order:
            rows = slice(j * CH, (j + 1) * CH)
            kd = (k_ref[rows, :].astype(F32) * w).astype(BF16)
            kvs[j] = lax.dot_general(kd, v_ref[rows, :], dn, preferred_element_type=F32)
        state = acc[...]
        for j in order:
            out_ref[0, j] = state.astype(BF16)
            state = state * cd + jnp.where(mask, kvs[j], 0.0)
        acc[...] = state

    @pl.when(t == 0)
    def _():
        accf[...] = jnp.zeros_like(accf)
        accb[...] = jnp.zeros_like(accb)
        for j in range(NCH_CTX, RS_CH):
            sf_ref[0, j] = jnp.zeros((256, 256), BF16)
            sb_ref[0, j] = jnp.zeros((256, 256), BF16)
        scan(accf, sf_ref, kx_ref, vx_ref, wf, cdf, list(range(NCH_CTX)))
        scan(accb, sb_ref, kx_ref, vx_ref, wb, cdb, list(reversed(range(NCH_CTX))))

    @pl.when(t > 0)
    def _():
        scan(accf, sf_ref, kf_ref, vf_ref, wf, cdf, list(range(RS_CH)))
        scan(accb, sb_ref, kb_ref, vb_ref, wb, cdb, list(reversed(range(RS_CH))))


def _ret_state(dec, rk, rv, layer):
    rows = RS_CH * CH
    fblk = lambda t: jnp.maximum(t - 1, 0)
    bblk = lambda t: RS_STEPS - jnp.maximum(t, 1)
    fmap = lambda b, t: (b * RS_STEPS + fblk(t), 0)
    bmap = lambda b, t: (b * RS_STEPS + bblk(t), 0)
    xmap = lambda b, t: (CTX_ROWBLK + b, 0)
    return pl.pallas_call(
        _ret_state_body,
        grid=(B, RS_STEPS + 1),
        in_specs=[
            pl.BlockSpec((None, 2, 256), lambda b, t: (layer, 0, 0)),
            pl.BlockSpec((rows, 256), fmap),
            pl.BlockSpec((rows, 256), fmap),
            pl.BlockSpec((rows, 256), bmap),
            pl.BlockSpec((rows, 256), bmap),
            pl.BlockSpec((CTX, 256), xmap),
            pl.BlockSpec((CTX, 256), xmap),
        ],
        out_specs=[
            pl.BlockSpec((1, RS_CH, 256, 256), lambda b, t: (b, jnp.where(t == 0, RS_STEPS, t - 1), 0, 0)),
            pl.BlockSpec((1, RS_CH, 256, 256), lambda b, t: (b, jnp.where(t == 0, RS_STEPS, RS_STEPS - t), 0, 0)),
        ],
        out_shape=[jax.ShapeDtypeStruct((B, N_STATE, 256, 256), BF16)] * 2,
        scratch_shapes=[pltpu.VMEM((256, 256), F32), pltpu.VMEM((256, 256), F32)],
        compiler_params=_params(("arbitrary", "arbitrary")),
        name="ret_state",
    )(dec, rk, rv, rk, rv, rk, rv)


def _split_dot(x, w):
    hi = x.astype(BF16)
    lo = (x - hi.astype(F32)).astype(BF16)
    return jnp.dot(hi, w, preferred_element_type=F32) + jnp.dot(lo, w, preferred_element_type=F32)


def _ret_out_body(nck, dec_ref, gn_ref, q_ref, k_ref, v_ref, g_ref, sf_ref, sb_ref, o_ref):
    lg = _log_sigmoid(dec_ref[...])
    lgf = lg[0:1]
    lgb = lg[1:2]
    pos = lax.broadcasted_iota(jnp.int32, (CH, 1), 0).astype(F32)
    wqf = jnp.exp((pos + 1.0) * lgf)
    wqb = jnp.exp((float(CH) - pos) * lgb)
    ri = lax.broadcasted_iota(jnp.int32, (CH, CH), 0)
    ci = lax.broadcasted_iota(jnp.int32, (CH, CH), 1)
    diff = (ri - ci).astype(F32)
    decays = []
    for h in range(RET_HEADS):
        lgf_h = lgf[:, h * HD:h * HD + 1]
        lgb_h = lgb[:, h * HD:h * HD + 1]
        decays.append(jnp.where(ri >= ci, jnp.exp(jnp.maximum(diff, 0.0) * lgf_h), 0.0)
                      + jnp.where(ci >= ri, jnp.exp(jnp.maximum(-diff, 0.0) * lgb_h), 0.0))
    decay = jnp.concatenate(decays, axis=0)
    lane_head = _lane_head(256)
    avg = jnp.where(_head_block_mask(), 1.0 / HD, 0.0).astype(BF16)
    gn = gn_ref[...]
    dn = (((1,), (1,)), ((), ()))
    chunk_rows = [slice(c * CH, (c + 1) * CH) for c in range(nck)]
    cross, scores = [], []
    for c, rows in enumerate(chunk_rows):
        qb16 = q_ref[rows, :]
        q = qb16.astype(F32)
        cross.append(jnp.dot((q * wqf).astype(BF16), sf_ref[0, c], preferred_element_type=F32)
                     + jnp.dot((q * wqb).astype(BF16), sb_ref[0, c], preferred_element_type=F32))
        qs = jnp.concatenate([jnp.where(lane_head == h, qb16, jnp.zeros_like(qb16))
                              for h in range(RET_HEADS)], axis=0)
        scores.append(lax.dot_general(qs, k_ref[rows, :], dn, preferred_element_type=F32))
    outs = []
    for c, rows in enumerate(chunk_rows):
        p = (scores[c] * decay).astype(BF16)
        of = jnp.dot(p, v_ref[rows, :], preferred_element_type=F32)
        o = cross[c]
        for h in range(RET_HEADS):
            o = o + jnp.where(lane_head == h, of[h * CH:(h + 1) * CH], 0.0)
        outs.append(o)
    o = jnp.concatenate(outs, axis=0)
    mu = _split_dot(o, avg)
    cen = o - mu
    var = _split_dot(cen * cen, avg)
    on = cen * lax.rsqrt(var + EPS) * gn
    g = g_ref[...].astype(F32)
    o_ref[...] = (g * jax.nn.sigmoid(g) * on).astype(BF16)


def _ret_out(dec, gn, rq, rk, rv, rg, sf, sb, layer, ctx):
    nck = NCH_CTX if ctx else RO_CH
    rows = nck * CH
    if ctx:
        grid = (B, 1)
        rmap = lambda b, u: (CTX_ROWBLK + b, 0)
        omap = lambda b, u: (b, 0)
        smap = lambda b, u: (b, NCH_LAT // NCH_CTX, 0, 0)
        out_rows = R_CTX
    else:
        grid = (B, NCH_LAT // RO_CH)
        rmap = lambda b, u: (b * (NCH_LAT // RO_CH) + u, 0)
        omap = rmap
        smap = lambda b, u: (b, u, 0, 0)
        out_rows = R_LAT
    return pl.pallas_call(
        functools.partial(_ret_out_body, nck),
        grid=grid,
        in_specs=[
            pl.BlockSpec((None, 2, 256), lambda b, u: (layer, 0, 0)),
            pl.BlockSpec((None, 1, 256), lambda b, u: (layer, 0, 0)),
            pl.BlockSpec((rows, 256), rmap),
            pl.BlockSpec((rows, 256), rmap),
            pl.BlockSpec((rows, 256), rmap),
            pl.BlockSpec((rows, 256), rmap),
            pl.BlockSpec((1, nck, 256, 256), smap),
            pl.BlockSpec((1, nck, 256, 256), smap),
        ],
        out_specs=pl.BlockSpec((rows, 256), omap),
        out_shape=jax.ShapeDtypeStruct((out_rows, 256), BF16),
        compiler_params=_params(("arbitrary", "arbitrary")),
        name="ret_out_ctx" if ctx else "ret_out",
    )(dec, gn, rq, rk, rv, rg, sf, sb)


def _tile_head(x_all, hk):
    own = jnp.where(_lane_head(KV_W) == hk, x_all, jnp.zeros_like(x_all))
    both = own + pltpu.roll(own.astype(F32), HD, axis=1).astype(BF16)
    return jnp.concatenate([both, both], axis=1)


def _gqa_groups(groups):
    lane_head = _lane_head(256)
    dn = (((1,), (1,)), ((), ()))
    chains = []
    for gi, (q_grp, k_parts, v_parts, sinks, mask_fn) in enumerate(groups):
        for h0 in range(0, ATT_GROUP, ATT_STACK):
            chains.append((gi, tuple(range(h0, h0 + ATT_STACK)), q_grp, k_parts, v_parts, sinks, mask_fn))
    scores = []
    for _, heads, q_grp, k_parts, _, _, _ in chains:
        qs = jnp.concatenate([jnp.where(lane_head == g, q_grp, jnp.zeros_like(q_grp))
                              for g in heads], axis=0)
        scores.append(jnp.concatenate(
            [lax.dot_general(kp, qs, dn, preferred_element_type=F32) for kp in k_parts], axis=0))
    probs, inv_den = [], []
    for (_, heads, q_grp, _, _, sinks, mask_fn), st in zip(chains, scores):
        m = q_grp.shape[0]
        sink = jnp.concatenate([jnp.full((1, m), sinks[g] * LOG2E, F32) for g in heads], axis=1)
        st = mask_fn(st)
        mx = jnp.maximum(jnp.max(st, axis=0, keepdims=True), sink)
        p = jnp.exp2(st - mx)
        inv_den.append(1.0 / (jnp.sum(p, axis=0, keepdims=True) + jnp.exp2(sink - mx)))
        probs.append(p.astype(BF16))
    pieces = [[] for _ in groups]
    for (gi, _, q_grp, _, v_parts, _, _), p, inv in zip(chains, probs, inv_den):
        m = q_grp.shape[0]
        ot, row = None, 0
        for vp in v_parts:
            n = vp.shape[1]
            t = jnp.dot(vp, p[row:row + n], preferred_element_type=F32)
            ot = t if ot is None else ot + t
            row += n
        ot = ot * inv
        stacked = jnp.concatenate([ot[:, i * m:(i + 1) * m] for i in range(ATT_STACK)], axis=0)
        pieces[gi].append(stacked.T)
    return [jnp.concatenate(ps, axis=1) for ps in pieces]


def _band_mask_fn(lo_rows, lo_ok, hi_rows, hi_ok, n_keys):
    def mask_fn(st):
        parts = []
        row = 0
        while row < n_keys:
            piece = st[row:row + CH]
            if row == lo_rows:
                piece = jnp.where(lo_ok, piece, NEG_INF)
            elif row == hi_rows:
                piece = jnp.where(hi_ok, piece, NEG_INF)
            parts.append(piece)
            row += CH
        return jnp.concatenate(parts, axis=0)
    return mask_fn


def _attn_lat_body(layer, sink_ref, q_ref, kp_ref, kc_ref, kn_ref, vp_ref, vc_ref, vn_ref, kx_ref, vx_ref, o_ref):
    step = pl.program_id(1)
    k_loc = jnp.concatenate([kp_ref[...], kc_ref[...], kn_ref[...]], axis=0)
    v_loc = jnp.concatenate([vp_ref[...], vc_ref[...], vn_ref[...]], axis=0)
    v_loc_t = v_loc.astype(F32).T.astype(BF16)
    v_ctx_t = vx_ref[...].astype(F32).T.astype(BF16)
    key = lax.broadcasted_iota(jnp.int32, (CH, ATT_STACK * CH), 0)
    qry = lax.broadcasted_iota(jnp.int32, (CH, ATT_STACK * CH), 1) & (CH - 1)
    below = key >= qry
    above = key <= qry
    n_keys = 3 * CH + CTX
    masks = []
    for i in range(ATT_QB):
        lo_ok = below & (step > 0) if i == 0 else below
        hi_ok = above & (step < NCH_LAT // ATT_QB - 1) if i == ATT_QB - 1 else above
        masks.append(_band_mask_fn(0, lo_ok, 2 * CH, hi_ok, n_keys))
    groups = []
    for hk in range(ATT_KV):
        sinks = [sink_ref[layer, hk * ATT_GROUP + g] for g in range(ATT_GROUP)]
        kt_loc = _tile_head(k_loc, hk)
        kt_ctx = _tile_head(kx_ref[...], hk)
        vt_loc = v_loc_t[hk * HD:(hk + 1) * HD]
        vt_ctx = v_ctx_t[hk * HD:(hk + 1) * HD]
        cols = slice(hk * 256, (hk + 1) * 256)
        for i in range(ATT_QB):
            win = slice(i * CH, (i + 3) * CH)
            groups.append((q_ref[i * CH:(i + 1) * CH, cols], [kt_loc[win], kt_ctx],
                           [vt_loc[:, win], vt_ctx], sinks, masks[i]))
    outs = _gqa_groups(groups)
    for hk in range(ATT_KV):
        cols = slice(hk * 256, (hk + 1) * 256)
        for i in range(ATT_QB):
            o_ref[i * CH:(i + 1) * CH, cols] = outs[hk * ATT_QB + i].astype(BF16)


def _attn_lat(sink, aq, ak, av, layer):
    nq = NCH_LAT // ATT_QB
    qmap = lambda b, n: (b * nq + n, 0)
    pmap = lambda b, n: (b * NCH_LAT + jnp.maximum(ATT_QB * n - 1, 0), 0)
    nmap = lambda b, n: (b * NCH_LAT + jnp.minimum(ATT_QB * (n + 1), NCH_LAT - 1), 0)
    xmap = lambda b, n: (CTX_ROWBLK + b, 0)
    kv1 = lambda m: pl.BlockSpec((CH, KV_W), m)
    kv2 = pl.BlockSpec((ATT_QB * CH, KV_W), qmap)
    ctx_spec = pl.BlockSpec((CTX, KV_W), xmap)
    return pl.pallas_call(
        functools.partial(_attn_lat_body, layer),
        grid=(B, nq),
        in_specs=[
            pl.BlockSpec(memory_space=pltpu.SMEM),
            pl.BlockSpec((ATT_QB * CH, ATT_HEADS * HD), qmap),
            kv1(pmap), kv2, kv1(nmap),
            kv1(pmap), kv2, kv1(nmap),
            ctx_spec, ctx_spec,
        ],
        out_specs=pl.BlockSpec((ATT_QB * CH, ATT_HEADS * HD), qmap),
        out_shape=jax.ShapeDtypeStruct((R_LAT, ATT_HEADS * HD), BF16),
        compiler_params=_params(("arbitrary", "arbitrary")),
        name="swa_attn",
    )(sink, aq, ak, ak, ak, av, av, av, ak, av)


def _attn_ctx_body(layer, sink_ref, q_ref, kx_ref, vx_ref, o_ref):
    groups = []
    v_all_t = vx_ref[...].astype(F32).T.astype(BF16)
    for hk in range(ATT_KV):
        sinks = [sink_ref[layer, hk * ATT_GROUP + g] for g in range(ATT_GROUP)]
        groups.append((q_ref[:, hk * 256:(hk + 1) * 256], [_tile_head(kx_ref[...], hk)],
                       [v_all_t[hk * HD:(hk + 1) * HD]], sinks, lambda s: s))
    outs = _gqa_groups(groups)
    for hk in range(ATT_KV):
        o_ref[:, hk * 256:(hk + 1) * 256] = outs[hk].astype(BF16)


def _attn_ctx(sink, aq, ak, av, layer):
    xmap = lambda b: (CTX_ROWBLK + b, 0)
    return pl.pallas_call(
        functools.partial(_attn_ctx_body, layer),
        grid=(B,),
        in_specs=[
            pl.BlockSpec(memory_space=pltpu.SMEM),
            pl.BlockSpec((CTX, ATT_HEADS * HD), xmap),
            pl.BlockSpec((CTX, KV_W), xmap),
            pl.BlockSpec((CTX, KV_W), xmap),
        ],
        out_specs=pl.BlockSpec((CTX, ATT_HEADS * HD), lambda b: (b, 0)),
        out_shape=jax.ShapeDtypeStruct((R_CTX, ATT_HEADS * HD), BF16),
        compiler_params=_params(("arbitrary",)),
        name="ctx_attn",
    )(sink, aq, ak, av)


def _mix_ffn_body(has_ctx, final, n_cast, *refs):
    refs = list(refs)
    h_ref = refs.pop(0)
    mixers = []
    for _ in range(3):
        if has_ctx:
            lat_ref, ctx_ref = refs.pop(0), refs.pop(0)
            mixers.append(_pick(lat_ref, ctx_ref))
        else:
            mixers.append(refs.pop(0)[...])
    m_ref, nw_ref, wo_ref, w_in_ref, w_out_ref = refs[:5]
    refs = refs[5:]
    if final:
        fw_ref = refs.pop(0)
    cast_src, refs = refs[:n_cast], refs[n_cast:]
    o_ref = refs.pop(0)
    _cast_blocks(cast_src, refs)
    yf, yr, ya = mixers
    m = m_ref[...]
    y = (jnp.dot(yf.astype(BF16), wo_ref[0:256, :], preferred_element_type=F32)
         + jnp.dot(yr, wo_ref[256:512, :], preferred_element_type=F32)
         + jnp.dot(ya, wo_ref[512:1024, :], preferred_element_type=F32))
    h = h_ref[...] + _mod_slice(m, 5) * y
    h = _ffn_half_step(h, m, nw_ref[...], 6, w_in_ref, w_out_ref)
    if final:
        ms = jnp.mean(h * h, axis=-1, keepdims=True)
        h = h * lax.rsqrt(ms + EPS) * fw_ref[...]
    o_ref[...] = h


def _mix_ffn(h, mixers, mod_all, norm_w, weights, cast_sources, final_w, layer):
    has_ctx = mixers[0][1] is not None
    final = final_w is not None
    nt = NT_ALL if has_ctx else NT_LAT
    cast_in, cast_out, cast_shapes = _cast_specs(cast_sources, nt)
    in_specs = [_row_spec(D)]
    args = [h]
    for (lat, ctx), width in zip(mixers, (256, 256, 512)):
        if has_ctx:
            in_specs += _lat_or_ctx_specs(width)
            args += [lat, ctx]
        else:
            in_specs += [_row_spec(width)]
            args += [lat]
    in_specs += [_mod_spec(layer), _norm_spec(layer, 2),
                 _const_spec((D, D)), _const_spec((D, 2 * D_FF)), _const_spec((D_FF, D))]
    args += [mod_all, norm_w, *weights]
    if final:
        in_specs += [_const_spec((1, D))]
        args += [final_w]
    in_specs += cast_in
    args += [s[0] for s in cast_sources]
    return pl.pallas_call(
        functools.partial(_mix_ffn_body, has_ctx, final, len(cast_sources)),
        grid=(nt,),
        in_specs=in_specs,
        out_specs=[_row_spec(D)] + cast_out,
        out_shape=[jax.ShapeDtypeStruct((nt * TM, D), F32)] + cast_shapes,
        compiler_params=_params(("arbitrary",)),
        name="mix_ffn_final" if final else "mix_ffn",
    )(*args)


def kernel(x, c, ctx, c_ctx, norm_w, w_ada, b_ada, ffn_w_in, ffn_w_out, w_in, w_o, ret_decay, ret_gn_w,
           attn_sink, final_norm_w):
    assert x.shape == (B, S, D) and ctx.shape == (B, CTX, D)
    cos_t, sin_t = _rope_tables()
    chan, w1, twr, twi, w2, wc = _fourier_tables()

    cond_t = jnp.zeros((D, 8), F32).at[:, 0:B].set(c.T).at[:, B].set(c_ctx)
    mod_all = _ada(cond_t, w_ada, b_ada).reshape(DEPTH, 8, 1, N_MOD * D)

    def ffn_sources(layer, j):
        return [(ffn_w_in, (layer, j), D, 2 * D_FF), (ffn_w_out, (layer, j), D_FF, D)]

    pre_w = (ffn_w_in[0, 0].astype(BF16), ffn_w_out[0, 0].astype(BF16), w_in[0].astype(BF16))
    norm4 = norm_w.reshape(DEPTH, 3, 1, D)
    gn3 = ret_gn_w.reshape(DEPTH, 1, 256)
    dec = jnp.repeat(ret_decay.astype(F32), HD, axis=2)
    sink = attn_sink.astype(F32)

    h = None
    out = None
    for layer in range(DEPTH):
        last = layer == DEPTH - 1
        rows_in = (x.reshape(R_LAT, D), ctx.reshape(R_CTX, D)) if layer == 0 else (h,)
        mix_sources = [(w_o, (layer,), D, D)] + ffn_sources(layer, 1)
        (h, gr, gi, rq, rk, rv, rg, aq, ak, av, *mix_w) = _ffn_inproj(
            rows_in, mod_all, norm4, pre_w, mix_sources, cos_t, sin_t, chan, layer)

        yf = _fft2(w2, twr, twi, _fft1(w1, gr, gi))
        sf, sb = _ret_state(dec, rk, rv, layer)
        yr = _ret_out(dec, gn3, rq, rk, rv, rg, sf, sb, layer, ctx=False)
        ya = _attn_lat(sink, aq, ak, av, layer)
        if last:
            yfc = yrc = yac = None
        else:
            yfc = _fft_ctx(wc, gr, gi)
            yrc = _ret_out(dec, gn3, rq, rk, rv, rg, sf, sb, layer, ctx=True)
            yac = _attn_ctx(sink, aq, ak, av, layer)

        next_sources = [] if last else ffn_sources(layer + 1, 0) + [(w_in, (layer + 1,), D, D_IN)]
        res, *pre_w = _mix_ffn(h, [(yf, yfc), (yr, yrc), (ya, yac)], mod_all, norm4, mix_w, next_sources,
                               final_norm_w.reshape(1, D) if last else None, layer)
        if last:
            out = res
        else:
            h = res
    return out.reshape(B, S, D)
```

```python
import functools
import math

import numpy as np
import jax
import jax.numpy as jnp
from jax import lax
from jax.experimental import pallas as pl
from jax.experimental.pallas import tpu as pltpu

F32 = jnp.float32
BF16 = jnp.bfloat16

D = 1024
B = 2
S = 8192
DEPTH = 2
GRID_W = 64
CTX = 256
HD = 64
N_MOD = 9
D_FF = 2816
D_IN = 2048
EPS = 1e-6
NEG_INF = -1e30
ROPE_BASE = 10000.0
ROPE_PAIRS = 16
RET_HEADS = 4
ATT_HEADS = 8
ATT_KV = 2
ATT_GROUP = ATT_HEADS // ATT_KV
ATT_STACK = 4
ATT_QB = 8
ONES_ROWS = 16
KV_W = ATT_KV * HD
LOG2E = math.log2(math.e)
WINDOW = 128

R_LAT = B * S
R_CTX = B * CTX
R_ALL = R_LAT + R_CTX
TM = 512
NT_LAT = R_LAT // TM
NT_ALL = R_ALL // TM
CH = 128
NCH_LAT = S // CH
NCH_CTX = CTX // CH
MXU_N = 256
FF_CHUNKS = ((0, 5 * MXU_N), (5 * MXU_N, D_FF))
FFT_L1 = 64
FFT_L2 = 128
VMEM_LIMIT = 56 * 1024 * 1024


def _params(sem, vmem=VMEM_LIMIT):
    return pltpu.CompilerParams(dimension_semantics=sem, vmem_limit_bytes=vmem)


def _const_spec(shape, index=None):
    if index is None:
        index = (0,) * len(shape)
    return pl.BlockSpec(shape, lambda *_: index, pipeline_mode=pl.Buffered(1))


def _rope_tables():
    rows = S // GRID_W
    row = np.repeat(np.arange(rows, dtype=np.float64), GRID_W)
    col = np.tile(np.arange(GRID_W, dtype=np.float64), rows)
    inv = ROPE_BASE ** (-np.arange(ROPE_PAIRS, dtype=np.float64) / ROPE_PAIRS)
    ar = row[:, None] * inv
    ac = col[:, None] * inv
    cos64 = np.concatenate([np.cos(ar), np.cos(ar), np.cos(ac), np.cos(ac)], axis=1)
    sin64 = np.concatenate([-np.sin(ar), np.sin(ar), -np.sin(ac), np.sin(ac)], axis=1)
    cos = np.concatenate([cos64, cos64], axis=1)
    sin = np.concatenate([sin64, sin64], axis=1)
    cos = np.concatenate([cos, np.ones((TM, 128))], axis=0)
    sin = np.concatenate([sin, np.zeros((TM, 128))], axis=0)
    return jnp.asarray(cos, F32), jnp.asarray(sin, F32)


def _dft_cs(n):
    k = np.arange(n)
    ang = 2.0 * np.pi * ((k[:, None] * k[None, :]) % n) / n
    return np.cos(ang), np.sin(ang)


def _fourier_tables():
    c64, s64 = _dft_cs(64)
    eye4 = np.eye(4)
    chan = np.concatenate([np.kron(eye4, c64), -np.kron(eye4, s64)], axis=1)
    w1 = np.block([[c64, s64], [-s64, c64]])
    p1 = np.arange(FFT_L1)[:, None]
    l2 = np.arange(FFT_L2)[None, :]
    ang = 2.0 * np.pi * ((p1 * l2) % S) / S
    twr = np.repeat(np.cos(ang)[:, :, None], 128, axis=2)
    twi = np.repeat(np.sin(ang)[:, :, None], 128, axis=2)
    c128, s128 = _dft_cs(128)
    w2 = np.concatenate([c128, s128], axis=1) / math.sqrt(S * 64.0)
    c256, s256 = _dft_cs(CTX)
    wc = np.concatenate([c256, s256], axis=1) / math.sqrt(CTX * 64.0)
    f = lambda a: jnp.asarray(a, F32)
    return (f(chan).astype(BF16), f(w1).astype(BF16), f(twr), f(twi), f(w2).astype(BF16), f(wc).astype(BF16))


def _mod_slice(m, i):
    return m[:, i * D:(i + 1) * D]


def _rms_mod(h, nw, shift, scale):
    ms = jnp.mean(h * h, axis=-1, keepdims=True)
    y = h * lax.rsqrt(ms + EPS) * nw
    return y * (1.0 + scale) + shift


def _swiglu(u, w_in_ref, w_out_ref):
    y = None
    for lo, hi in FF_CHUNKS:
        g = jnp.dot(u, w_in_ref[:, lo:hi], preferred_element_type=F32)
        up = jnp.dot(u, w_in_ref[:, D_FF + lo:D_FF + hi], preferred_element_type=F32)
        a = (g * jax.nn.sigmoid(g) * up).astype(BF16)
        t = jnp.dot(a, w_out_ref[lo:hi, :], preferred_element_type=F32)
        y = t if y is None else y + t
    return y


def _ffn_half_step(h, m, nw, mod0, w_in_ref, w_out_ref):
    u = _rms_mod(h, nw, _mod_slice(m, mod0), _mod_slice(m, mod0 + 1)).astype(BF16)
    return h + 0.5 * _mod_slice(m, mod0 + 2) * _swiglu(u, w_in_ref, w_out_ref)


BF16_ROWS = 16


def _cast_specs(sources, n_steps):
    in_specs, out_specs, out_shapes = [], [], []
    for arr, lead, rows, cols in sources:
        n_blk = min(n_steps, rows // BF16_ROWS)
        while rows % (BF16_ROWS * n_blk):
            n_blk -= 1
        rb = rows // n_blk
        in_specs.append(pl.BlockSpec((None,) * len(lead) + (rb, cols),
                                     lambda i, lead=lead, n_blk=n_blk: (*lead, jnp.minimum(i, n_blk - 1), 0)))
        out_specs.append(pl.BlockSpec((rb, cols), lambda i, n_blk=n_blk: (jnp.minimum(i, n_blk - 1), 0)))
        out_shapes.append(jax.ShapeDtypeStruct((rows, cols), BF16))
    return in_specs, out_specs, out_shapes


def _cast_blocks(src_refs, dst_refs):
    for src, dst in zip(src_refs, dst_refs):
        dst[...] = src[...].astype(BF16)


def _log_sigmoid(x):
    return jnp.minimum(x, 0.0) - jnp.log1p(jnp.exp(-jnp.abs(x)))


def _lane_head(width):
    return lax.broadcasted_iota(jnp.int32, (1, width), 1) // HD


ADA_CB = 1152


def _ada_body(ct_ref, w_ref, b_ref, o_ref):
    ct = ct_ref[...]
    s = ct * jax.nn.sigmoid(ct)
    w = w_ref[0]
    rows = [jnp.sum(w * s[:, r:r + 1], axis=0, keepdims=True) for r in range(3)]
    rows.append(jnp.zeros((5, ADA_CB), F32))
    o_ref[0] = jnp.concatenate(rows, axis=0) + b_ref[0]


def _ada(cond_t, w_ada, b_ada):
    nb = (N_MOD * D) // ADA_CB
    return pl.pallas_call(
        _ada_body,
        grid=(DEPTH, nb),
        in_specs=[
            pl.BlockSpec((D, 8), lambda l, j: (0, 0)),
            pl.BlockSpec((1, D, ADA_CB), lambda l, j: (l, 0, j)),
            pl.BlockSpec((1, 1, ADA_CB), lambda l, j: (l, 0, j)),
        ],
        out_specs=pl.BlockSpec((1, 8, ADA_CB), lambda l, j: (l, 0, j)),
        out_shape=jax.ShapeDtypeStruct((DEPTH, 8, N_MOD * D), F32),
        compiler_params=_params(("arbitrary", "arbitrary")),
        name="ada_mod",
    )(cond_t, w_ada, b_ada.reshape(DEPTH, 1, N_MOD * D))


def _mod_spec(layer):
    return pl.BlockSpec((None, None, 1, N_MOD * D),
                        lambda i: (layer, jnp.minimum(i // (NT_LAT // B), B), 0, 0))


def _norm_spec(layer, k):
    return _const_spec((None, None, 1, D), (layer, k, 0, 0))


def _row_spec(width):
    return pl.BlockSpec((TM, width), lambda i: (i, 0))


def _lat_or_ctx_specs(width):
    return [pl.BlockSpec((TM, width), lambda i: (jnp.minimum(i, NT_LAT - 1), 0)),
            pl.BlockSpec((TM, width), lambda i: (0, 0))]


def _pick(lat_ref, ctx_ref):
    return jnp.where(pl.program_id(0) < NT_LAT, lat_ref[...], ctx_ref[...])


def _rope(x, cos, sin, lane_first_half):
    outs = []
    for j in range(x.shape[1] // 128):
        xb = x[:, j * 128:(j + 1) * 128]
        fwd = pltpu.roll(xb, 112, axis=1)
        bwd = pltpu.roll(xb, 16, axis=1)
        partner = jnp.where(lane_first_half, fwd, bwd)
        outs.append(xb * cos + partner * sin)
    return outs[0] if len(outs) == 1 else jnp.concatenate(outs, axis=1)


N_CAST = 3


def _ffn_inproj_body(first, *refs):
    refs = list(refs)
    if first:
        h = _pick(refs.pop(0), refs.pop(0))
    else:
        h = refs.pop(0)[...]
    (m_ref, nw0_ref, nw1_ref, fw_in_ref, fw_out_ref, w_ref, cos_ref, sin_ref, chan_ref) = refs[:9]
    cast_src = refs[9:9 + N_CAST]
    (h_out_ref, gr_ref, gi_ref, rq_ref, rk_ref, rv_ref, rg_ref, aq_ref, ak_ref, av_ref) = refs[9 + N_CAST:19 + N_CAST]
    _cast_blocks(cast_src, refs[19 + N_CAST:])
    m = m_ref[...]
    h = _ffn_half_step(h, m, nw0_ref[...], 0, fw_in_ref, fw_out_ref)
    h_out_ref[...] = h
    u = _rms_mod(h, nw1_ref[...], _mod_slice(m, 3), _mod_slice(m, 4)).astype(BF16)
    z = jnp.dot(u, w_ref[...], preferred_element_type=F32)
    cos = cos_ref[...]
    sin = sin_ref[...]
    lane = lax.broadcasted_iota(jnp.int32, (1, 128), 1)
    first_half = (lane // ROPE_PAIRS) % 2 == 0
    g = jnp.dot(z[:, 0:256].astype(BF16), chan_ref[...], preferred_element_type=F32)
    gr_ref[...] = g[:, 0:256]
    gi_ref[...] = g[:, 256:512]
    rq_ref[...] = _rope(z[:, 256:512], cos, sin, first_half).astype(BF16)
    rk_ref[...] = (_rope(z[:, 512:768], cos, sin, first_half) * (HD ** -0.5)).astype(BF16)
    rv_ref[...] = z[:, 768:1024].astype(BF16)
    rg_ref[...] = z[:, 1024:1280].astype(BF16)
    aq_ref[...] = (_rope(z[:, 1280:1792], cos, sin, first_half) * (HD ** -0.5 * LOG2E)).astype(BF16)
    ak_ref[...] = _rope(z[:, 1792:1920], cos, sin, first_half).astype(BF16)
    av_ref[...] = z[:, 1920:2048].astype(BF16)


def _rope_block(i):
    return jnp.where(i < NT_LAT, i % (S // TM), S // TM)


def _ffn_inproj(rows_in, mod_all, norm_w, weights, cast_sources, cos_t, sin_t, chan, layer):
    first = len(rows_in) == 2
    widths = (256, 256, 256, 256, 256, 256, 512, KV_W, KV_W)
    dtypes = [F32, F32] + [BF16] * (len(widths) - 2)
    cast_in, cast_out, cast_shapes = _cast_specs(cast_sources, NT_ALL)
    return pl.pallas_call(
        functools.partial(_ffn_inproj_body, first),
        grid=(NT_ALL,),
        in_specs=(_lat_or_ctx_specs(D) if first else [_row_spec(D)]) + [
            _mod_spec(layer),
            _norm_spec(layer, 0),
            _norm_spec(layer, 1),
            _const_spec((D, 2 * D_FF)), _const_spec((D_FF, D)), _const_spec((D, D_IN)),
            pl.BlockSpec((TM, 128), lambda i: (_rope_block(i), 0)),
            pl.BlockSpec((TM, 128), lambda i: (_rope_block(i), 0)),
            _const_spec((256, 512)),
        ] + cast_in,
        out_specs=[_row_spec(D)] + [_row_spec(w) for w in widths] + cast_out,
        out_shape=[jax.ShapeDtypeStruct((R_ALL, D), F32)]
        + [jax.ShapeDtypeStruct((R_ALL, w), dt) for w, dt in zip(widths, dtypes)] + cast_shapes,
        compiler_params=_params(("arbitrary",)),
        name="ffn_inproj_first" if first else "ffn_inproj",
    )(*rows_in, mod_all, norm_w, norm_w, *weights, cos_t, sin_t, chan, *[s[0] for s in cast_sources])


FFT_SUB = 16


FFT_N1 = FFT_L2 // FFT_SUB
FFT_N2 = FFT_L1 // FFT_SUB


def _fft_body(w1_ref, w2_ref, twr_ref, twi_ref, xr_ref, xi_ref, o_ref, z_ref):
    t = pl.program_id(1)

    @pl.when(t < FFT_N1)
    def _():
        w1 = w1_ref[...]
        for s in range(FFT_SUB):
            x = jnp.concatenate([xr_ref[:, s, :], xi_ref[:, s, :]], axis=0).astype(BF16)
            z = jnp.dot(w1, x, preferred_element_type=F32)
            z_ref[0, :, t, s, :] = z[0:FFT_L1]
            z_ref[1, :, t, s, :] = z[FFT_L1:2 * FFT_L1]

    @pl.when(t >= FFT_N1)
    def _():
        w2 = w2_ref[...]
        p0 = (t - FFT_N1) * FFT_SUB
        for p in range(FFT_SUB):
            zr = z_ref[0, p0 + p].reshape(FFT_L2, 256)
            zi = z_ref[1, p0 + p].reshape(FFT_L2, 256)
            tr = jnp.concatenate([twr_ref[p]] * 2, axis=1)
            ti = jnp.concatenate([twi_ref[p]] * 2, axis=1)
            yr = tr * zr + ti * zi
            yi = tr * zi - ti * zr
            y = jnp.concatenate([yr, yi], axis=0).astype(BF16)
            o_ref[:, p, :] = jnp.dot(w2, y, preferred_element_type=F32)


def _fft(w1, w2, twr, twi, gr, gi):
    view = (R_ALL // FFT_L2, FFT_N1, FFT_SUB, 256)
    x_spec = pl.BlockSpec((FFT_L1, None, FFT_SUB, 256), lambda b, t: (b, jnp.minimum(t, FFT_N1 - 1), 0, 0))
    tw_spec = pl.BlockSpec((FFT_SUB, FFT_L2, 128), lambda b, t: (jnp.maximum(t - FFT_N1, 0), 0, 0))
    out = pl.pallas_call(
        _fft_body,
        grid=(B, FFT_N1 + FFT_N2),
        in_specs=[_const_spec((128, 128)), _const_spec((128, 256)), tw_spec, tw_spec, x_spec, x_spec],
        out_specs=pl.BlockSpec((None, FFT_L2, None, FFT_SUB, 256),
                               lambda b, t: (b, 0, jnp.maximum(t - FFT_N1, 0), 0, 0)),
        out_shape=jax.ShapeDtypeStruct((B, FFT_L2, FFT_N2, FFT_SUB, 256), F32),
        scratch_shapes=[pltpu.VMEM((2, FFT_L1, FFT_N1, FFT_SUB, 256), F32)],
        compiler_params=_params(("arbitrary", "arbitrary")),
        name="fft_lat",
    )(w1, w2, twr, twi, gr.reshape(view), gi.reshape(view))
    return out.reshape(R_LAT, 256)


def _fft_ctx_body(wc_ref, gr_ref, gi_ref, o_ref):
    x = jnp.concatenate([gr_ref[...], gi_ref[...]], axis=0).astype(BF16)
    o_ref[...] = jnp.dot(wc_ref[...], x, preferred_element_type=F32)


def _fft_ctx(wc, gr, gi):
    blk0 = R_LAT // CTX
    return pl.pallas_call(
        _fft_ctx_body,
        grid=(B,),
        in_specs=[
            _const_spec((CTX, 2 * CTX)),
            pl.BlockSpec((CTX, 256), lambda b: (blk0 + b, 0)),
            pl.BlockSpec((CTX, 256), lambda b: (blk0 + b, 0)),
        ],
        out_specs=pl.BlockSpec((CTX, 256), lambda b: (b, 0)),
        out_shape=jax.ShapeDtypeStruct((R_CTX, 256), F32),
        compiler_params=_params(("arbitrary",)),
        name="fft_ctx",
    )(wc, gr, gi)


RS_CH = 8
RS_STEPS = NCH_LAT // RS_CH
N_STATE = (RS_STEPS + 1) * RS_CH
RO_CH = 8
CTX_ROWBLK = R_LAT // CTX


def _head_block_mask():
    r = lax.broadcasted_iota(jnp.int32, (256, 256), 0) // HD
    c = lax.broadcasted_iota(jnp.int32, (256, 256), 1) // HD
    return r == c


def _ret_state_body(dec_ref, kf_ref, vf_ref, kb_ref, vb_ref, kx_ref, vx_ref, sf_ref, sb_ref, accf, accb):
    t = pl.program_id(1)
    lg = _log_sigmoid(dec_ref[...])
    lgf = lg[0:1]
    lgb = lg[1:2]
    pos = lax.broadcasted_iota(jnp.int32, (CH, 1), 0).astype(F32)
    wf = jnp.exp((CH - 1.0 - pos) * lgf)
    wb = jnp.exp(pos * lgb)
    cdf = jnp.exp(float(CH) * lgf)
    cdb = jnp.exp(float(CH) * lgb)
    mask = _head_block_mask()
    dn = (((0,), (0,)), ((), ()))

    def scan(acc, out_ref, k_ref, v_ref, w, cd, order):
        kvs = {}
        for j in order:
            rows = slice(j * CH, (j + 1) * CH)
            kd = (k_ref[rows, :].astype(F32) * w).astype(BF16)
            kvs[j] = lax.dot_general(kd, v_ref[rows, :], dn, preferred_element_type=F32)
        state = acc[...]
        for j in order:
            out_ref[0, j] = state.astype(BF16)
            state = state * cd + jnp.where(mask, kvs[j], 0.0)
        acc[...] = state

    @pl.when(t == 0)
    def _():
        accf[...] = jnp.zeros_like(accf)
        accb[...] = jnp.zeros_like(accb)
        for j in range(NCH_CTX, RS_CH):
            sf_ref[0, j] = jnp.zeros((256, 256), BF16)
            sb_ref[0, j] = jnp.zeros((256, 256), BF16)
        scan(accf, sf_ref, kx_ref, vx_ref, wf, cdf, list(range(NCH_CTX)))
        scan(accb, sb_ref, kx_ref, vx_ref, wb, cdb, list(reversed(range(NCH_CTX))))

    @pl.when(t > 0)
    def _():
        scan(accf, sf_ref, kf_ref, vf_ref, wf, cdf, list(range(RS_CH)))
        scan(accb, sb_ref, kb_ref, vb_ref, wb, cdb, list(reversed(range(RS_CH))))


def _ret_state(dec, rk, rv, layer):
    rows = RS_CH * CH
    fblk = lambda t: jnp.maximum(t - 1, 0)
    bblk = lambda t: RS_STEPS - jnp.maximum(t, 1)
    fmap = lambda b, t: (b * RS_STEPS + fblk(t), 0)
    bmap = lambda b, t: (b * RS_STEPS + bblk(t), 0)
    xmap = lambda b, t: (CTX_ROWBLK + b, 0)
    return pl.pallas_call(
        _ret_state_body,
        grid=(B, RS_STEPS + 1),
        in_specs=[
            pl.BlockSpec((None, 2, 256), lambda b, t: (layer, 0, 0)),
            pl.BlockSpec((rows, 256), fmap),
            pl.BlockSpec((rows, 256), fmap),
            pl.BlockSpec((rows, 256), bmap),
            pl.BlockSpec((rows, 256), bmap),
            pl.BlockSpec((CTX, 256), xmap),
            pl.BlockSpec((CTX, 256), xmap),
        ],
        out_specs=[
            pl.BlockSpec((1, RS_CH, 256, 256), lambda b, t: (b, jnp.where(t == 0, RS_STEPS, t - 1), 0, 0)),
            pl.BlockSpec((1, RS_CH, 256, 256), lambda b, t: (b, jnp.where(t == 0, RS_STEPS, RS_STEPS - t), 0, 0)),
        ],
        out_shape=[jax.ShapeDtypeStruct((B, N_STATE, 256, 256), BF16)] * 2,
        scratch_shapes=[pltpu.VMEM((256, 256), F32), pltpu.VMEM((256, 256), F32)],
        compiler_params=_params(("arbitrary", "arbitrary")),
        name="ret_state",
    )(dec, rk, rv, rk, rv, rk, rv)


def _split_dot(x, w):
    hi = x.astype(BF16)
    lo = (x - hi.astype(F32)).astype(BF16)
    return jnp.dot(hi, w, preferred_element_type=F32) + jnp.dot(lo, w, preferred_element_type=F32)


def _ret_out_body(nck, dec_ref, gn_ref, q_ref, k_ref, v_ref, g_ref, sf_ref, sb_ref, o_ref):
    lg = _log_sigmoid(dec_ref[...])
    lgf = lg[0:1]
    lgb = lg[1:2]
    pos = lax.broadcasted_iota(jnp.int32, (CH, 1), 0).astype(F32)
    wqf = jnp.exp((pos + 1.0) * lgf)
    wqb = jnp.exp((float(CH) - pos) * lgb)
    ri = lax.broadcasted_iota(jnp.int32, (CH, CH), 0)
    ci = lax.broadcasted_iota(jnp.int32, (CH, CH), 1)
    diff = (ri - ci).astype(F32)
    decays = []
    for h in range(RET_HEADS):
        lgf_h = lgf[:, h * HD:h * HD + 1]
        lgb_h = lgb[:, h * HD:h * HD + 1]
        decays.append(jnp.where(ri >= ci, jnp.exp(jnp.maximum(diff, 0.0) * lgf_h), 0.0)
                      + jnp.where(ci >= ri, jnp.exp(jnp.maximum(-diff, 0.0) * lgb_h), 0.0))
    decay = jnp.concatenate(decays, axis=0)
    lane_head = _lane_head(256)
    avg = jnp.where(_head_block_mask(), 1.0 / HD, 0.0).astype(BF16)
    gn = gn_ref[...]
    dn = (((1,), (1,)), ((), ()))
    chunk_rows = [slice(c * CH, (c + 1) * CH) for c in range(nck)]
    cross, scores = [], []
    for c, rows in enumerate(chunk_rows):
        qb16 = q_ref[rows, :]
        q = qb16.astype(F32)
        cross.append(jnp.dot((q * wqf).astype(BF16), sf_ref[0, c], preferred_element_type=F32)
                     + jnp.dot((q * wqb).astype(BF16), sb_ref[0, c], preferred_element_type=F32))
        qs = jnp.concatenate([jnp.where(lane_head == h, qb16, jnp.zeros_like(qb16))
                              for h in range(RET_HEADS)], axis=0)
        scores.append(lax.dot_general(qs, k_ref[rows, :], dn, preferred_element_type=F32))
    outs = []
    for c, rows in enumerate(chunk_rows):
        p = (scores[c] * decay).astype(BF16)
        of = jnp.dot(p, v_ref[rows, :], preferred_element_type=F32)
        o = cross[c]
        for h in range(RET_HEADS):
            o = o + jnp.where(lane_head == h, of[h * CH:(h + 1) * CH], 0.0)
        outs.append(o)
    o = jnp.concatenate(outs, axis=0)
    mu = _split_dot(o, avg)
    cen = o - mu
    var = _split_dot(cen * cen, avg)
    on = cen * lax.rsqrt(var + EPS) * gn
    g = g_ref[...].astype(F32)
    o_ref[...] = (g * jax.nn.sigmoid(g) * on).astype(BF16)


def _ret_out(dec, gn, rq, rk, rv, rg, sf, sb, layer, ctx):
    nck = NCH_CTX if ctx else RO_CH
    rows = nck * CH
    if ctx:
        grid = (B, 1)
        rmap = lambda b, u: (CTX_ROWBLK + b, 0)
        omap = lambda b, u: (b, 0)
        smap = lambda b, u: (b, NCH_LAT // NCH_CTX, 0, 0)
        out_rows = R_CTX
    else:
        grid = (B, NCH_LAT // RO_CH)
        rmap = lambda b, u: (b * (NCH_LAT // RO_CH) + u, 0)
        omap = rmap
        smap = lambda b, u: (b, u, 0, 0)
        out_rows = R_LAT
    return pl.pallas_call(
        functools.partial(_ret_out_body, nck),
        grid=grid,
        in_specs=[
            pl.BlockSpec((None, 2, 256), lambda b, u: (layer, 0, 0)),
            pl.BlockSpec((None, 1, 256), lambda b, u: (layer, 0, 0)),
            pl.BlockSpec((rows, 256), rmap),
            pl.BlockSpec((rows, 256), rmap),
            pl.BlockSpec((rows, 256), rmap),
            pl.BlockSpec((rows, 256), rmap),
            pl.BlockSpec((1, nck, 256, 256), smap),
            pl.BlockSpec((1, nck, 256, 256), smap),
        ],
        out_specs=pl.BlockSpec((rows, 256), omap),
        out_shape=jax.ShapeDtypeStruct((out_rows, 256), BF16),
        compiler_params=_params(("arbitrary", "arbitrary")),
        name="ret_out_ctx" if ctx else "ret_out",
    )(dec, gn, rq, rk, rv, rg, sf, sb)


def _tile_head(x_all, hk):
    own = jnp.where(_lane_head(KV_W) == hk, x_all, jnp.zeros_like(x_all))
    both = own + pltpu.roll(own.astype(F32), HD, axis=1).astype(BF16)
    return jnp.concatenate([both, both], axis=1)


def _gqa_groups(groups):
    lane_head = _lane_head(256)
    dn = (((1,), (1,)), ((), ()))
    chains = []
    for gi, (q_grp, k_parts, v_parts, sinks, mask_fn) in enumerate(groups):
        for h0 in range(0, ATT_GROUP, ATT_STACK):
            chains.append((gi, tuple(range(h0, h0 + ATT_STACK)), q_grp, k_parts, v_parts, sinks, mask_fn))
    scores = []
    for _, heads, q_grp, k_parts, _, _, _ in chains:
        qs = jnp.concatenate([jnp.where(lane_head == g, q_grp, jnp.zeros_like(q_grp))
                              for g in heads], axis=0)
        scores.append(jnp.concatenate(
            [lax.dot_general(kp, qs, dn, preferred_element_type=F32) for kp in k_parts], axis=0))
    probs, sink_terms = [], []
    for (_, heads, q_grp, _, _, sinks, mask_fn), st in zip(chains, scores):
        m = q_grp.shape[0]
        sink = jnp.concatenate([jnp.full((1, m), sinks[g] * LOG2E, F32) for g in heads], axis=1)
        st = mask_fn(st)
        mx = jnp.maximum(jnp.max(st, axis=0, keepdims=True), sink)
        probs.append(jnp.exp2(st - mx).astype(BF16))
        sink_terms.append(jnp.exp2(sink - mx))
    pieces = [[] for _ in groups]
    for (gi, _, q_grp, _, v_parts, _, _), p, sink_term in zip(chains, probs, sink_terms):
        m = q_grp.shape[0]
        acc, row = None, 0
        for vp in v_parts:
            n = vp.shape[1]
            t = jnp.dot(vp, p[row:row + n], preferred_element_type=F32)
            acc = t if acc is None else acc + t
            row += n
        ot = acc[0:HD] * (1.0 / (acc[HD:HD + 1] + sink_term))
        stacked = jnp.concatenate([ot[:, i * m:(i + 1) * m] for i in range(ATT_STACK)], axis=0)
        pieces[gi].append(stacked.T)
    return [jnp.concatenate(ps, axis=1) for ps in pieces]


def _band_mask_fn(lo_rows, lo_bias, hi_rows, hi_bias, n_keys):
    def mask_fn(st):
        parts = []
        row = 0
        while row < n_keys:
            piece = st[row:row + CH]
            if row == lo_rows:
                piece = piece + lo_bias
            elif row == hi_rows:
                piece = piece + hi_bias
            parts.append(piece)
            row += CH
        return jnp.concatenate(parts, axis=0)
    return mask_fn


def _with_ones_rows(v_t):
    return jnp.concatenate([v_t, jnp.ones((ONES_ROWS, v_t.shape[1]), BF16)], axis=0)


def _attn_lat_body(layer, sink_ref, q_ref, kp_ref, kc_ref, kn_ref, vp_ref, vc_ref, vn_ref, kx_ref, vx_ref, o_ref):
    step = pl.program_id(1)
    k_loc = jnp.concatenate([kp_ref[...], kc_ref[...], kn_ref[...]], axis=0)
    v_loc = jnp.concatenate([vp_ref[...], vc_ref[...], vn_ref[...]], axis=0)
    v_loc_t = v_loc.astype(F32).T.astype(BF16)
    v_ctx_t = vx_ref[...].astype(F32).T.astype(BF16)
    key = lax.broadcasted_iota(jnp.int32, (CH, ATT_STACK * CH), 0)
    qry = lax.broadcasted_iota(jnp.int32, (CH, ATT_STACK * CH), 1) & (CH - 1)
    below = key >= qry
    above = key <= qry
    bias = lambda ok: jnp.where(ok, 0.0, NEG_INF)
    lo_bias, lo_bias_first = bias(below), bias(below & (step > 0))
    hi_bias, hi_bias_last = bias(above), bias(above & (step < NCH_LAT // ATT_QB - 1))
    n_keys = 3 * CH + CTX
    masks = []
    for i in range(ATT_QB):
        masks.append(_band_mask_fn(0, lo_bias_first if i == 0 else lo_bias,
                                   2 * CH, hi_bias_last if i == ATT_QB - 1 else hi_bias, n_keys))
    groups = []
    for hk in range(ATT_KV):
        sinks = [sink_ref[layer, hk * ATT_GROUP + g] for g in range(ATT_GROUP)]
        kt_loc = _tile_head(k_loc, hk)
        kt_ctx = _tile_head(kx_ref[...], hk)
        vt_loc = _with_ones_rows(v_loc_t[hk * HD:(hk + 1) * HD])
        vt_ctx = _with_ones_rows(v_ctx_t[hk * HD:(hk + 1) * HD])
        cols = slice(hk * 256, (hk + 1) * 256)
        for i in range(ATT_QB):
            win = slice(i * CH, (i + 3) * CH)
            groups.append((q_ref[i * CH:(i + 1) * CH, cols], [kt_loc[win], kt_ctx],
                           [vt_loc[:, win], vt_ctx], sinks, masks[i]))
    outs = _gqa_groups(groups)
    for hk in range(ATT_KV):
        cols = slice(hk * 256, (hk + 1) * 256)
        for i in range(ATT_QB):
            o_ref[i * CH:(i + 1) * CH, cols] = outs[hk * ATT_QB + i].astype(BF16)


def _attn_lat(sink, aq, ak, av, layer):
    nq = NCH_LAT // ATT_QB
    qmap = lambda b, n: (b * nq + n, 0)
    pmap = lambda b, n: (b * NCH_LAT + jnp.maximum(ATT_QB * n - 1, 0), 0)
    nmap = lambda b, n: (b * NCH_LAT + jnp.minimum(ATT_QB * (n + 1), NCH_LAT - 1), 0)
    xmap = lambda b, n: (CTX_ROWBLK + b, 0)
    kv1 = lambda m: pl.BlockSpec((CH, KV_W), m)
    kv2 = pl.BlockSpec((ATT_QB * CH, KV_W), qmap)
    ctx_spec = pl.BlockSpec((CTX, KV_W), xmap)
    return pl.pallas_call(
        functools.partial(_attn_lat_body, layer),
        grid=(B, nq),
        in_specs=[
            pl.BlockSpec(memory_space=pltpu.SMEM),
            pl.BlockSpec((ATT_QB * CH, ATT_HEADS * HD), qmap),
            kv1(pmap), kv2, kv1(nmap),
            kv1(pmap), kv2, kv1(nmap),
            ctx_spec, ctx_spec,
        ],
        out_specs=pl.BlockSpec((ATT_QB * CH, ATT_HEADS * HD), qmap),
        out_shape=jax.ShapeDtypeStruct((R_LAT, ATT_HEADS * HD), BF16),
        compiler_params=_params(("arbitrary", "arbitrary")),
        name="swa_attn",
    )(sink, aq, ak, ak, ak, av, av, av, ak, av)


def _attn_ctx_body(layer, sink_ref, q_ref, kx_ref, vx_ref, o_ref):
    groups = []
    v_all_t = vx_ref[...].astype(F32).T.astype(BF16)
    for hk in range(ATT_KV):
        sinks = [sink_ref[layer, hk * ATT_GROUP + g] for g in range(ATT_GROUP)]
        groups.append((q_ref[:, hk * 256:(hk + 1) * 256], [_tile_head(kx_ref[...], hk)],
                       [_with_ones_rows(v_all_t[hk * HD:(hk + 1) * HD])], sinks, lambda s: s))
    outs = _gqa_groups(groups)
    for hk in range(ATT_KV):
        o_ref[:, hk * 256:(hk + 1) * 256] = outs[hk].astype(BF16)


def _attn_ctx(sink, aq, ak, av, layer):
    xmap = lambda b: (CTX_ROWBLK + b, 0)
    return pl.pallas_call(
        functools.partial(_attn_ctx_body, layer),
        grid=(B,),
        in_specs=[
            pl.BlockSpec(memory_space=pltpu.SMEM),
            pl.BlockSpec((CTX, ATT_HEADS * HD), xmap),
            pl.BlockSpec((CTX, KV_W), xmap),
            pl.BlockSpec((CTX, KV_W), xmap),
        ],
        out_specs=pl.BlockSpec((CTX, ATT_HEADS * HD), lambda b: (b, 0)),
        out_shape=jax.ShapeDtypeStruct((R_CTX, ATT_HEADS * HD), BF16),
        compiler_params=_params(("arbitrary",)),
        name="ctx_attn",
    )(sink, aq, ak, av)


def _mix_ffn_body(has_ctx, final, n_cast, *refs):
    refs = list(refs)
    h_ref = refs.pop(0)
    mixers = []
    for _ in range(3):
        if has_ctx:
            lat_ref, ctx_ref = refs.pop(0), refs.pop(0)
            mixers.append(_pick(lat_ref, ctx_ref))
        else:
            mixers.append(refs.pop(0)[...])
    m_ref, nw_ref, wo_ref, w_in_ref, w_out_ref = refs[:5]
    refs = refs[5:]
    if final:
        fw_ref = refs.pop(0)
    cast_src, refs = refs[:n_cast], refs[n_cast:]
    o_ref = refs.pop(0)
    _cast_blocks(cast_src, refs)
    yf, yr, ya = mixers
    m = m_ref[...]
    y = (jnp.dot(yf.astype(BF16), wo_ref[0:256, :], preferred_element_type=F32)
         + jnp.dot(yr, wo_ref[256:512, :], preferred_element_type=F32)
         + jnp.dot(ya, wo_ref[512:1024, :], preferred_element_type=F32))
    h = h_ref[...] + _mod_slice(m, 5) * y
    h = _ffn_half_step(h, m, nw_ref[...], 6, w_in_ref, w_out_ref)
    if final:
        ms = jnp.mean(h * h, axis=-1, keepdims=True)
        h = h * lax.rsqrt(ms + EPS) * fw_ref[...]
    o_ref[...] = h


def _mix_ffn(h, mixers, mod_all, norm_w, weights, cast_sources, final_w, layer):
    has_ctx = mixers[0][1] is not None
    final = final_w is not None
    nt = NT_ALL if has_ctx else NT_LAT
    cast_in, cast_out, cast_shapes = _cast_specs(cast_sources, nt)
    in_specs = [_row_spec(D)]
    args = [h]
    for (lat, ctx), width in zip(mixers, (256, 256, 512)):
        if has_ctx:
            in_specs += _lat_or_ctx_specs(width)
            args += [lat, ctx]
        else:
            in_specs += [_row_spec(width)]
            args += [lat]
    in_specs += [_mod_spec(layer), _norm_spec(layer, 2),
                 _const_spec((D, D)), _const_spec((D, 2 * D_FF)), _const_spec((D_FF, D))]
    args += [mod_all, norm_w, *weights]
    if final:
        in_specs += [_const_spec((1, D))]
        args += [final_w]
    in_specs += cast_in
    args += [s[0] for s in cast_sources]
    return pl.pallas_call(
        functools.partial(_mix_ffn_body, has_ctx, final, len(cast_sources)),
        grid=(nt,),
        in_specs=in_specs,
        out_specs=[_row_spec(D)] + cast_out,
        out_shape=[jax.ShapeDtypeStruct((nt * TM, D), F32)] + cast_shapes,
        compiler_params=_params(("arbitrary",)),
        name="mix_ffn_final" if final else "mix_ffn",
    )(*args)


def kernel(x, c, ctx, c_ctx, norm_w, w_ada, b_ada, ffn_w_in, ffn_w_out, w_in, w_o, ret_decay, ret_gn_w,
           attn_sink, final_norm_w):
    assert x.shape == (B, S, D) and ctx.shape == (B, CTX, D)
    cos_t, sin_t = _rope_tables()
    chan, w1, twr, twi, w2, wc = _fourier_tables()

    cond_t = jnp.zeros((D, 8), F32).at[:, 0:B].set(c.T).at[:, B].set(c_ctx)
    mod_all = _ada(cond_t, w_ada, b_ada).reshape(DEPTH, 8, 1, N_MOD * D)

    def ffn_sources(layer, j):
        return [(ffn_w_in, (layer, j), D, 2 * D_FF), (ffn_w_out, (layer, j), D_FF, D)]

    pre_w = (ffn_w_in[0, 0].astype(BF16), ffn_w_out[0, 0].astype(BF16), w_in[0].astype(BF16))
    norm4 = norm_w.reshape(DEPTH, 3, 1, D)
    gn3 = ret_gn_w.reshape(DEPTH, 1, 256)
    dec = jnp.repeat(ret_decay.astype(F32), HD, axis=2)
    sink = attn_sink.astype(F32)

    h = None
    out = None
    for layer in range(DEPTH):
        last = layer == DEPTH - 1
        rows_in = (x.reshape(R_LAT, D), ctx.reshape(R_CTX, D)) if layer == 0 else (h,)
        mix_sources = [(w_o, (layer,), D, D)] + ffn_sources(layer, 1)
        (h, gr, gi, rq, rk, rv, rg, aq, ak, av, *mix_w) = _ffn_inproj(
            rows_in, mod_all, norm4, pre_w, mix_sources, cos_t, sin_t, chan, layer)

        yf = _fft(w1, w2, twr, twi, gr, gi)
        sf, sb = _ret_state(dec, rk, rv, layer)
        yr = _ret_out(dec, gn3, rq, rk, rv, rg, sf, sb, layer, ctx=False)
        ya = _attn_lat(sink, aq, ak, av, layer)
        if last:
            yfc = yrc = yac = None
        else:
            yfc = _fft_ctx(wc, gr, gi)
            yrc = _ret_out(dec, gn3, rq, rk, rv, rg, sf, sb, layer, ctx=True)
            yac = _attn_ctx(sink, aq, ak, av, layer)

        next_sources = [] if last else ffn_sources(layer + 1, 0) + [(w_in, (layer + 1,), D, D_IN)]
        res, *pre_w = _mix_ffn(h, [(yf, yfc), (yr, yrc), (ya, yac)], mod_all, norm4, mix_w, next_sources,
                               final_norm_w.reshape(1, D) if last else None, layer)
        if last:
            out = res
        else:
            h = res
    return out.reshape(B, S, D)
```

```python
import functools
import math

import numpy as np
import jax
import jax.numpy as jnp
from jax import lax
from jax.experimental import pallas as pl
from jax.experimental.pallas import tpu as pltpu

F32 = jnp.float32
BF16 = jnp.bfloat16

D = 1024
B = 2
S = 8192
DEPTH = 2
GRID_W = 64
CTX = 256
HD = 64
N_MOD = 9
D_FF = 2816
D_IN = 2048
EPS = 1e-6
NEG_INF = -1e30
ROPE_BASE = 10000.0
ROPE_PAIRS = 16
RET_HEADS = 4
ATT_HEADS = 8
ATT_KV = 2
ATT_GROUP = ATT_HEADS // ATT_KV
ATT_STACK = 2
ATT_QB = 8
ONES_ROWS = 16
KV_W = ATT_KV * HD
LOG2E = math.log2(math.e)
WINDOW = 128

R_LAT = B * S
R_CTX = B * CTX
R_ALL = R_LAT + R_CTX
TM = 512
NT_LAT = R_LAT // TM
NT_ALL = R_ALL // TM
CH = 128
NCH_LAT = S // CH
NCH_CTX = CTX // CH
MXU_N = 256
FF_CHUNKS = ((0, 5 * MXU_N), (5 * MXU_N, D_FF))
FFT_L1 = 64
FFT_L2 = 128
VMEM_LIMIT = 56 * 1024 * 1024


def _params(sem, vmem=VMEM_LIMIT):
    return pltpu.CompilerParams(dimension_semantics=sem, vmem_limit_bytes=vmem)


def _const_spec(shape, index=None):
    if index is None:
        index = (0,) * len(shape)
    return pl.BlockSpec(shape, lambda *_: index, pipeline_mode=pl.Buffered(1))


def _rope_tables():
    rows = S // GRID_W
    row = np.repeat(np.arange(rows, dtype=np.float64), GRID_W)
    col = np.tile(np.arange(GRID_W, dtype=np.float64), rows)
    inv = ROPE_BASE ** (-np.arange(ROPE_PAIRS, dtype=np.float64) / ROPE_PAIRS)
    ar = row[:, None] * inv
    ac = col[:, None] * inv
    cos64 = np.concatenate([np.cos(ar), np.cos(ar), np.cos(ac), np.cos(ac)], axis=1)
    sin64 = np.concatenate([-np.sin(ar), np.sin(ar), -np.sin(ac), np.sin(ac)], axis=1)
    cos = np.concatenate([cos64, cos64], axis=1)
    sin = np.concatenate([sin64, sin64], axis=1)
    cos = np.concatenate([cos, np.ones((TM, 128))], axis=0)
    sin = np.concatenate([sin, np.zeros((TM, 128))], axis=0)
    return jnp.asarray(cos, F32), jnp.asarray(sin, F32)


def _dft_cs(n):
    k = np.arange(n)
    ang = 2.0 * np.pi * ((k[:, None] * k[None, :]) % n) / n
    return np.cos(ang), np.sin(ang)


def _fourier_tables():
    c64, s64 = _dft_cs(64)
    eye4 = np.eye(4)
    chan = np.concatenate([np.kron(eye4, c64), -np.kron(eye4, s64)], axis=1)
    w1 = np.block([[c64, s64], [-s64, c64]])
    p1 = np.arange(FFT_L1)[:, None]
    l2 = np.arange(FFT_L2)[None, :]
    ang = 2.0 * np.pi * ((p1 * l2) % S) / S
    twr = np.repeat(np.cos(ang)[:, :, None], 128, axis=2)
    twi = np.repeat(np.sin(ang)[:, :, None], 128, axis=2)
    c128, s128 = _dft_cs(128)
    w2 = np.concatenate([c128, s128], axis=1) / math.sqrt(S * 64.0)
    c256, s256 = _dft_cs(CTX)
    wc = np.concatenate([c256, s256], axis=1) / math.sqrt(CTX * 64.0)
    f = lambda a: jnp.asarray(a, F32)
    return (f(chan).astype(BF16), f(w1).astype(BF16), f(twr), f(twi), f(w2).astype(BF16), f(wc).astype(BF16))


def _mod_slice(m, i):
    return m[:, i * D:(i + 1) * D]


def _rms_mod(h, nw, shift, scale):
    ms = jnp.mean(h * h, axis=-1, keepdims=True)
    y = h * lax.rsqrt(ms + EPS) * nw
    return y * (1.0 + scale) + shift


def _swiglu(u, w_in_ref, w_out_ref):
    y = None
    for lo, hi in FF_CHUNKS:
        g = jnp.dot(u, w_in_ref[:, lo:hi], preferred_element_type=F32)
        up = jnp.dot(u, w_in_ref[:, D_FF + lo:D_FF + hi], preferred_element_type=F32)
        a = (g * jax.nn.sigmoid(g) * up).astype(BF16)
        t = jnp.dot(a, w_out_ref[lo:hi, :], preferred_element_type=F32)
        y = t if y is None else y + t
    return y


def _ffn_half_step(h, m, nw, mod0, w_in_ref, w_out_ref):
    u = _rms_mod(h, nw, _mod_slice(m, mod0), _mod_slice(m, mod0 + 1)).astype(BF16)
    return h + 0.5 * _mod_slice(m, mod0 + 2) * _swiglu(u, w_in_ref, w_out_ref)


BF16_ROWS = 16


def _cast_specs(sources, n_steps):
    in_specs, out_specs, out_shapes = [], [], []
    for arr, lead, rows, cols in sources:
        n_blk = min(n_steps, rows // BF16_ROWS)
        while rows % (BF16_ROWS * n_blk):
            n_blk -= 1
        rb = rows // n_blk
        in_specs.append(pl.BlockSpec((None,) * len(lead) + (rb, cols),
                                     lambda i, lead=lead, n_blk=n_blk: (*lead, jnp.minimum(i, n_blk - 1), 0)))
        out_specs.append(pl.BlockSpec((rb, cols), lambda i, n_blk=n_blk: (jnp.minimum(i, n_blk - 1), 0)))
        out_shapes.append(jax.ShapeDtypeStruct((rows, cols), BF16))
    return in_specs, out_specs, out_shapes


def _cast_blocks(src_refs, dst_refs):
    for src, dst in zip(src_refs, dst_refs):
        dst[...] = src[...].astype(BF16)


def _log_sigmoid(x):
    return jnp.minimum(x, 0.0) - jnp.log1p(jnp.exp(-jnp.abs(x)))


def _lane_head(width):
    return lax.broadcasted_iota(jnp.int32, (1, width), 1) // HD


ADA_CB = 1152


def _ada_body(ct_ref, w_ref, b_ref, o_ref):
    ct = ct_ref[...]
    s = ct * jax.nn.sigmoid(ct)
    w = w_ref[0]
    rows = [jnp.sum(w * s[:, r:r + 1], axis=0, keepdims=True) for r in range(3)]
    rows.append(jnp.zeros((5, ADA_CB), F32))
    o_ref[0] = jnp.concatenate(rows, axis=0) + b_ref[0]


def _ada(cond_t, w_ada, b_ada):
    nb = (N_MOD * D) // ADA_CB
    return pl.pallas_call(
        _ada_body,
        grid=(DEPTH, nb),
        in_specs=[
            pl.BlockSpec((D, 8), lambda l, j: (0, 0)),
            pl.BlockSpec((1, D, ADA_CB), lambda l, j: (l, 0, j)),
            pl.BlockSpec((1, 1, ADA_CB), lambda l, j: (l, 0, j)),
        ],
        out_specs=pl.BlockSpec((1, 8, ADA_CB), lambda l, j: (l, 0, j)),
        out_shape=jax.ShapeDtypeStruct((DEPTH, 8, N_MOD * D), F32),
        compiler_params=_params(("arbitrary", "arbitrary")),
        name="ada_mod",
    )(cond_t, w_ada, b_ada.reshape(DEPTH, 1, N_MOD * D))


def _mod_spec(layer):
    return pl.BlockSpec((None, None, 1, N_MOD * D),
                        lambda i: (layer, jnp.minimum(i // (NT_LAT // B), B), 0, 0))


def _norm_spec(layer, k):
    return _const_spec((None, None, 1, D), (layer, k, 0, 0))


def _row_spec(width):
    return pl.BlockSpec((TM, width), lambda i: (i, 0))


def _lat_or_ctx_specs(width):
    return [pl.BlockSpec((TM, width), lambda i: (jnp.minimum(i, NT_LAT - 1), 0)),
            pl.BlockSpec((TM, width), lambda i: (0, 0))]


def _pick(lat_ref, ctx_ref):
    return jnp.where(pl.program_id(0) < NT_LAT, lat_ref[...], ctx_ref[...])


def _rope(x, cos, sin, lane_first_half):
    outs = []
    for j in range(x.shape[1] // 128):
        xb = x[:, j * 128:(j + 1) * 128]
        fwd = pltpu.roll(xb, 112, axis=1)
        bwd = pltpu.roll(xb, 16, axis=1)
        partner = jnp.where(lane_first_half, fwd, bwd)
        outs.append(xb * cos + partner * sin)
    return outs[0] if len(outs) == 1 else jnp.concatenate(outs, axis=1)


N_CAST = 3


def _ffn_inproj_body(first, *refs):
    refs = list(refs)
    if first:
        h = _pick(refs.pop(0), refs.pop(0))
    else:
        h = refs.pop(0)[...]
    (m_ref, nw0_ref, nw1_ref, fw_in_ref, fw_out_ref, w_ref, cos_ref, sin_ref, chan_ref) = refs[:9]
    cast_src = refs[9:9 + N_CAST]
    (h_out_ref, gr_ref, gi_ref, rq_ref, rk_ref, rv_ref, rg_ref, aq_ref, ak_ref, av_ref) = refs[9 + N_CAST:19 + N_CAST]
    _cast_blocks(cast_src, refs[19 + N_CAST:])
    m = m_ref[...]
    h = _ffn_half_step(h, m, nw0_ref[...], 0, fw_in_ref, fw_out_ref)
    h_out_ref[...] = h
    lane = lax.broadcasted_iota(jnp.int32, (1, 128), 1)
    first_half = (lane // ROPE_PAIRS) % 2 == 0
    halves = [slice(r0, r0 + TM // 2) for r0 in (0, TM // 2)]
    us = [_rms_mod(h[r], nw1_ref[...], _mod_slice(m, 3), _mod_slice(m, 4)).astype(BF16) for r in halves]
    zs = [jnp.dot(u, w_ref[...], preferred_element_type=F32) for u in us]
    for r, z in zip(halves, zs):
        cos = cos_ref[r, :]
        sin = sin_ref[r, :]
        g = jnp.dot(z[:, 0:256].astype(BF16), chan_ref[...], preferred_element_type=F32)
        gr_ref[r, :] = g[:, 0:256]
        gi_ref[r, :] = g[:, 256:512]
        rq_ref[r, :] = _rope(z[:, 256:512], cos, sin, first_half).astype(BF16)
        rk_ref[r, :] = (_rope(z[:, 512:768], cos, sin, first_half) * (HD ** -0.5)).astype(BF16)
        rv_ref[r, :] = z[:, 768:1024].astype(BF16)
        rg_ref[r, :] = z[:, 1024:1280].astype(BF16)
        aq_ref[r, :] = (_rope(z[:, 1280:1792], cos, sin, first_half) * (HD ** -0.5 * LOG2E)).astype(BF16)
        ak_ref[r, :] = _rope(z[:, 1792:1920], cos, sin, first_half).astype(BF16)
        av_ref[r, :] = z[:, 1920:2048].astype(BF16)


def _rope_block(i):
    return jnp.where(i < NT_LAT, i % (S // TM), S // TM)


def _ffn_inproj(rows_in, mod_all, norm_w, weights, cast_sources, cos_t, sin_t, chan, layer):
    first = len(rows_in) == 2
    widths = (256, 256, 256, 256, 256, 256, 512, KV_W, KV_W)
    dtypes = [F32, F32] + [BF16] * (len(widths) - 2)
    cast_in, cast_out, cast_shapes = _cast_specs(cast_sources, NT_ALL)
    return pl.pallas_call(
        functools.partial(_ffn_inproj_body, first),
        grid=(NT_ALL,),
        in_specs=(_lat_or_ctx_specs(D) if first else [_row_spec(D)]) + [
            _mod_spec(layer),
            _norm_spec(layer, 0),
            _norm_spec(layer, 1),
            _const_spec((D, 2 * D_FF)), _const_spec((D_FF, D)), _const_spec((D, D_IN)),
            pl.BlockSpec((TM, 128), lambda i: (_rope_block(i), 0)),
            pl.BlockSpec((TM, 128), lambda i: (_rope_block(i), 0)),
            _const_spec((256, 512)),
        ] + cast_in,
        out_specs=[_row_spec(D)] + [_row_spec(w) for w in widths] + cast_out,
        out_shape=[jax.ShapeDtypeStruct((R_ALL, D), F32)]
        + [jax.ShapeDtypeStruct((R_ALL, w), dt) for w, dt in zip(widths, dtypes)] + cast_shapes,
        compiler_params=_params(("arbitrary",)),
        name="ffn_inproj_first" if first else "ffn_inproj",
    )(*rows_in, mod_all, norm_w, norm_w, *weights, cos_t, sin_t, chan, *[s[0] for s in cast_sources])


FFT_SUB = 16


FFT_N1 = FFT_L2 // FFT_SUB
FFT_N2 = FFT_L1 // FFT_SUB


def _fft_body(w1_ref, w2_ref, twr_ref, twi_ref, xr_ref, xi_ref, o_ref, z_ref):
    t = pl.program_id(1)

    @pl.when(t < FFT_N1)
    def _():
        w1 = w1_ref[...]
        for s in range(FFT_SUB):
            x = jnp.concatenate([xr_ref[:, s, :], xi_ref[:, s, :]], axis=0).astype(BF16)
            z = jnp.dot(w1, x, preferred_element_type=F32)
            z_ref[0, :, t, s, :] = z[0:FFT_L1]
            z_ref[1, :, t, s, :] = z[FFT_L1:2 * FFT_L1]

    @pl.when(t >= FFT_N1)
    def _():
        w2 = w2_ref[...]
        p0 = (t - FFT_N1) * FFT_SUB
        for p in range(FFT_SUB):
            zr = z_ref[0, p0 + p].reshape(FFT_L2, 256)
            zi = z_ref[1, p0 + p].reshape(FFT_L2, 256)
            tr = jnp.concatenate([twr_ref[p]] * 2, axis=1)
            ti = jnp.concatenate([twi_ref[p]] * 2, axis=1)
            yr = tr * zr + ti * zi
            yi = tr * zi - ti * zr
            y = jnp.concatenate([yr, yi], axis=0).astype(BF16)
            o_ref[:, p, :] = jnp.dot(w2, y, preferred_element_type=F32)


def _fft(w1, w2, twr, twi, gr, gi):
    view = (R_ALL // FFT_L2, FFT_N1, FFT_SUB, 256)
    x_spec = pl.BlockSpec((FFT_L1, None, FFT_SUB, 256), lambda b, t: (b, jnp.minimum(t, FFT_N1 - 1), 0, 0))
    tw_spec = pl.BlockSpec((FFT_SUB, FFT_L2, 128), lambda b, t: (jnp.maximum(t - FFT_N1, 0), 0, 0))
    out = pl.pallas_call(
        _fft_body,
        grid=(B, FFT_N1 + FFT_N2),
        in_specs=[_const_spec((128, 128)), _const_spec((128, 256)), tw_spec, tw_spec, x_spec, x_spec],
        out_specs=pl.BlockSpec((None, FFT_L2, None, FFT_SUB, 256),
                               lambda b, t: (b, 0, jnp.maximum(t - FFT_N1, 0), 0, 0)),
        out_shape=jax.ShapeDtypeStruct((B, FFT_L2, FFT_N2, FFT_SUB, 256), F32),
        scratch_shapes=[pltpu.VMEM((2, FFT_L1, FFT_N1, FFT_SUB, 256), F32)],
        compiler_params=_params(("arbitrary", "arbitrary")),
        name="fft_lat",
    )(w1, w2, twr, twi, gr.reshape(view), gi.reshape(view))
    return out.reshape(R_LAT, 256)


def _fft_ctx_body(wc_ref, gr_ref, gi_ref, o_ref):
    x = jnp.concatenate([gr_ref[...], gi_ref[...]], axis=0).astype(BF16)
    o_ref[...] = jnp.dot(wc_ref[...], x, preferred_element_type=F32)


def _fft_ctx(wc, gr, gi):
    blk0 = R_LAT // CTX
    return pl.pallas_call(
        _fft_ctx_body,
        grid=(B,),
        in_specs=[
            _const_spec((CTX, 2 * CTX)),
            pl.BlockSpec((CTX, 256), lambda b: (blk0 + b, 0)),
            pl.BlockSpec((CTX, 256), lambda b: (blk0 + b, 0)),
        ],
        out_specs=pl.BlockSpec((CTX, 256), lambda b: (b, 0)),
        out_shape=jax.ShapeDtypeStruct((R_CTX, 256), F32),
        compiler_params=_params(("arbitrary",)),
        name="fft_ctx",
    )(wc, gr, gi)


RS_CH = 8
RS_STEPS = NCH_LAT // RS_CH
N_STATE = (RS_STEPS + 1) * RS_CH
RO_CH = 8
CTX_ROWBLK = R_LAT // CTX


def _head_block_mask():
    r = lax.broadcasted_iota(jnp.int32, (256, 256), 0) // HD
    c = lax.broadcasted_iota(jnp.int32, (256, 256), 1) // HD
    return r == c


def _ret_state_body(dec_ref, kf_ref, vf_ref, kb_ref, vb_ref, kx_ref, vx_ref, sf_ref, sb_ref, accf, accb):
    t = pl.program_id(1)
    lg = _log_sigmoid(dec_ref[...])
    lgf = lg[0:1]
    lgb = lg[1:2]
    pos = lax.broadcasted_iota(jnp.int32, (CH, 1), 0).astype(F32)
    wf = jnp.exp((CH - 1.0 - pos) * lgf)
    wb = jnp.exp(pos * lgb)
    cdf = jnp.exp(float(CH) * lgf)
    cdb = jnp.exp(float(CH) * lgb)
    mask = _head_block_mask()
    dn = (((0,), (0,)), ((), ()))

    def scan(acc, out_ref, k_ref, v_ref, w, cd, order):
        kvs = {}
        for j in order:
            rows = slice(j * CH, (j + 1) * CH)
            kd = (k_ref[rows, :].astype(F32) * w).astype(BF16)
            kvs[j] = lax.dot_general(kd, v_ref[rows, :], dn, preferred_element_type=F32)
        state = acc[...]
        for j in order:
            out_ref[0, j] = state.astype(BF16)
            state = state * cd + jnp.where(mask, kvs[j], 0.0)
        acc[...] = state

    @pl.when(t == 0)
    def _():
        accf[...] = jnp.zeros_like(accf)
        accb[...] = jnp.zeros_like(accb)
        for j in range(NCH_CTX, RS_CH):
            sf_ref[0, j] = jnp.zeros((256, 256), BF16)
            sb_ref[0, j] = jnp.zeros((256, 256), BF16)
        scan(accf, sf_ref, kx_ref, vx_ref, wf, cdf, list(range(NCH_CTX)))
        scan(accb, sb_ref, kx_ref, vx_ref, wb, cdb, list(reversed(range(NCH_CTX))))

    @pl.when(t > 0)
    def _():
        scan(accf, sf_ref, kf_ref, vf_ref, wf, cdf, list(range(RS_CH)))
        scan(accb, sb_ref, kb_ref, vb_ref, wb, cdb, list(reversed(range(RS_CH))))


def _ret_state(dec, rk, rv, layer):
    rows = RS_CH * CH
    fblk = lambda t: jnp.maximum(t - 1, 0)
    bblk = lambda t: RS_STEPS - jnp.maximum(t, 1)
    fmap = lambda b, t: (b * RS_STEPS + fblk(t), 0)
    bmap = lambda b, t: (b * RS_STEPS + bblk(t), 0)
    xmap = lambda b, t: (CTX_ROWBLK + b, 0)
    return pl.pallas_call(
        _ret_state_body,
        grid=(B, RS_STEPS + 1),
        in_specs=[
            pl.BlockSpec((None, 2, 256), lambda b, t: (layer, 0, 0)),
            pl.BlockSpec((rows, 256), fmap),
            pl.BlockSpec((rows, 256), fmap),
            pl.BlockSpec((rows, 256), bmap),
            pl.BlockSpec((rows, 256), bmap),
            pl.BlockSpec((CTX, 256), xmap),
            pl.BlockSpec((CTX, 256), xmap),
        ],
        out_specs=[
            pl.BlockSpec((1, RS_CH, 256, 256), lambda b, t: (b, jnp.where(t == 0, RS_STEPS, t - 1), 0, 0)),
            pl.BlockSpec((1, RS_CH, 256, 256), lambda b, t: (b, jnp.where(t == 0, RS_STEPS, RS_STEPS - t), 0, 0)),
        ],
        out_shape=[jax.ShapeDtypeStruct((B, N_STATE, 256, 256), BF16)] * 2,
        scratch_shapes=[pltpu.VMEM((256, 256), F32), pltpu.VMEM((256, 256), F32)],
        compiler_params=_params(("arbitrary", "arbitrary")),
        name="ret_state",
    )(dec, rk, rv, rk, rv, rk, rv)


def _split_dot(x, w):
    hi = x.astype(BF16)
    lo = (x - hi.astype(F32)).astype(BF16)
    return jnp.dot(hi, w, preferred_element_type=F32) + jnp.dot(lo, w, preferred_element_type=F32)


def _ret_out_body(nck, dec_ref, gn_ref, q_ref, k_ref, v_ref, g_ref, sf_ref, sb_ref, o_ref):
    lg = _log_sigmoid(dec_ref[...])
    lgf = lg[0:1]
    lgb = lg[1:2]
    pos = lax.broadcasted_iota(jnp.int32, (CH, 1), 0).astype(F32)
    wqf = jnp.exp((pos + 1.0) * lgf)
    wqb = jnp.exp((float(CH) - pos) * lgb)
    ri = lax.broadcasted_iota(jnp.int32, (CH, CH), 0)
    ci = lax.broadcasted_iota(jnp.int32, (CH, CH), 1)
    diff = (ri - ci).astype(F32)
    decays = []
    for h in range(RET_HEADS):
        lgf_h = lgf[:, h * HD:h * HD + 1]
        lgb_h = lgb[:, h * HD:h * HD + 1]
        decays.append(jnp.where(ri >= ci, jnp.exp(jnp.maximum(diff, 0.0) * lgf_h), 0.0)
                      + jnp.where(ci >= ri, jnp.exp(jnp.maximum(-diff, 0.0) * lgb_h), 0.0))
    decay = jnp.concatenate(decays, axis=0)
    lane_head = _lane_head(256)
    avg = jnp.where(_head_block_mask(), 1.0 / HD, 0.0).astype(BF16)
    gn = gn_ref[...]
    dn = (((1,), (1,)), ((), ()))
    chunk_rows = [slice(c * CH, (c + 1) * CH) for c in range(nck)]
    cross, scores = [], []
    for c, rows in enumerate(chunk_rows):
        qb16 = q_ref[rows, :]
        q = qb16.astype(F32)
        cross.append(jnp.dot((q * wqf).astype(BF16), sf_ref[0, c], preferred_element_type=F32)
                     + jnp.dot((q * wqb).astype(BF16), sb_ref[0, c], preferred_element_type=F32))
        qs = jnp.concatenate([jnp.where(lane_head == h, qb16, jnp.zeros_like(qb16))
                              for h in range(RET_HEADS)], axis=0)
        scores.append(lax.dot_general(qs, k_ref[rows, :], dn, preferred_element_type=F32))
    outs = []
    for c, rows in enumerate(chunk_rows):
        p = (scores[c] * decay).astype(BF16)
        of = jnp.dot(p, v_ref[rows, :], preferred_element_type=F32)
        o = cross[c]
        for h in range(RET_HEADS):
            o = o + jnp.where(lane_head == h, of[h * CH:(h + 1) * CH], 0.0)
        outs.append(o)
    o = jnp.concatenate(outs, axis=0)
    mu = _split_dot(o, avg)
    cen = o - mu
    var = _split_dot(cen * cen, avg)
    on = cen * lax.rsqrt(var + EPS) * gn
    g = g_ref[...].astype(F32)
    o_ref[...] = (g * jax.nn.sigmoid(g) * on).astype(BF16)


def _ret_out(dec, gn, rq, rk, rv, rg, sf, sb, layer, ctx):
    nck = NCH_CTX if ctx else RO_CH
    rows = nck * CH
    if ctx:
        grid = (B, 1)
        rmap = lambda b, u: (CTX_ROWBLK + b, 0)
        omap = lambda b, u: (b, 0)
        smap = lambda b, u: (b, NCH_LAT // NCH_CTX, 0, 0)
        out_rows = R_CTX
    else:
        grid = (B, NCH_LAT // RO_CH)
        rmap = lambda b, u: (b * (NCH_LAT // RO_CH) + u, 0)
        omap = rmap
        smap = lambda b, u: (b, u, 0, 0)
        out_rows = R_LAT
    return pl.pallas_call(
        functools.partial(_ret_out_body, nck),
        grid=grid,
        in_specs=[
            pl.BlockSpec((None, 2, 256), lambda b, u: (layer, 0, 0)),
            pl.BlockSpec((None, 1, 256), lambda b, u: (layer, 0, 0)),
            pl.BlockSpec((rows, 256), rmap),
            pl.BlockSpec((rows, 256), rmap),
            pl.BlockSpec((rows, 256), rmap),
            pl.BlockSpec((rows, 256), rmap),
            pl.BlockSpec((1, nck, 256, 256), smap),
            pl.BlockSpec((1, nck, 256, 256), smap),
        ],
        out_specs=pl.BlockSpec((rows, 256), omap),
        out_shape=jax.ShapeDtypeStruct((out_rows, 256), BF16),
        compiler_params=_params(("arbitrary", "arbitrary")),
        name="ret_out_ctx" if ctx else "ret_out",
    )(dec, gn, rq, rk, rv, rg, sf, sb)


def _tile_head(x_all, hk):
    own = jnp.where(_lane_head(KV_W) == hk, x_all, jnp.zeros_like(x_all))
    both = own + pltpu.roll(own.astype(F32), HD, axis=1).astype(BF16)
    return jnp.concatenate([both, both], axis=1)


def _gqa_groups(groups):
    lane_head = _lane_head(256)
    dn = (((1,), (1,)), ((), ()))
    chains = []
    for gi, (q_grp, k_parts, v_parts, sinks, mask_fn) in enumerate(groups):
        for h0 in range(0, ATT_GROUP, ATT_STACK):
            chains.append((gi, tuple(range(h0, h0 + ATT_STACK)), q_grp, k_parts, v_parts, sinks, mask_fn))
    scores, maxima = [], []
    for _, heads, q_grp, k_parts, _, _, mask_fn in chains:
        qs = jnp.concatenate([jnp.where(lane_head == g, q_grp, jnp.zeros_like(q_grp))
                              for g in heads], axis=0)
        st = mask_fn(jnp.concatenate(
            [lax.dot_general(kp, qs, dn, preferred_element_type=F32) for kp in k_parts], axis=0))
        scores.append(st)
        maxima.append(jnp.max(st, axis=0, keepdims=True))
    probs, sink_terms = [], []
    for (_, heads, q_grp, _, _, sinks, _), st, st_max in zip(chains, scores, maxima):
        m = q_grp.shape[0]
        sink = jnp.concatenate([jnp.full((1, m), sinks[g] * LOG2E, F32) for g in heads], axis=1)
        mx = jnp.maximum(st_max, sink)
        probs.append(jnp.exp2(st - mx).astype(BF16))
        sink_terms.append(jnp.exp2(sink - mx))
    pieces = [[] for _ in groups]
    for (gi, _, q_grp, _, v_parts, _, _), p, sink_term in zip(chains, probs, sink_terms):
        m = q_grp.shape[0]
        acc, row = None, 0
        for vp in v_parts:
            n = vp.shape[1]
            t = jnp.dot(vp, p[row:row + n], preferred_element_type=F32)
            acc = t if acc is None else acc + t
            row += n
        ot = acc[0:HD] * (1.0 / (acc[HD:HD + 1] + sink_term))
        stacked = jnp.concatenate([ot[:, i * m:(i + 1) * m] for i in range(ATT_STACK)], axis=0)
        pieces[gi].append(stacked.T)
    return [jnp.concatenate(ps, axis=1) for ps in pieces]


def _band_mask_fn(lo_rows, lo_bias, hi_rows, hi_bias, n_keys):
    def mask_fn(st):
        parts = []
        row = 0
        while row < n_keys:
            piece = st[row:row + CH]
            if row == lo_rows:
                piece = piece + lo_bias
            elif row == hi_rows:
                piece = piece + hi_bias
            parts.append(piece)
            row += CH
        return jnp.concatenate(parts, axis=0)
    return mask_fn


def _with_ones_rows(v_t):
    return jnp.concatenate([v_t, jnp.ones((ONES_ROWS, v_t.shape[1]), BF16)], axis=0)


def _attn_lat_body(layer, sink_ref, q_ref, kp_ref, kc_ref, kn_ref, vp_ref, vc_ref, vn_ref, kx_ref, vx_ref, o_ref):
    step = pl.program_id(1)
    k_loc = jnp.concatenate([kp_ref[...], kc_ref[...], kn_ref[...]], axis=0)
    v_loc = jnp.concatenate([vp_ref[...], vc_ref[...], vn_ref[...]], axis=0)
    v_loc_t = v_loc.astype(F32).T.astype(BF16)
    v_ctx_t = vx_ref[...].astype(F32).T.astype(BF16)
    key = lax.broadcasted_iota(jnp.int32, (CH, ATT_STACK * CH), 0)
    qry = lax.broadcasted_iota(jnp.int32, (CH, ATT_STACK * CH), 1) & (CH - 1)
    below = key >= qry
    above = key <= qry
    bias = lambda ok: jnp.where(ok, 0.0, NEG_INF)
    lo_bias, lo_bias_first = bias(below), bias(below & (step > 0))
    hi_bias, hi_bias_last = bias(above), bias(above & (step < NCH_LAT // ATT_QB - 1))
    n_keys = 3 * CH + CTX
    masks = []
    for i in range(ATT_QB):
        masks.append(_band_mask_fn(0, lo_bias_first if i == 0 else lo_bias,
                                   2 * CH, hi_bias_last if i == ATT_QB - 1 else hi_bias, n_keys))
    groups = []
    for hk in range(ATT_KV):
        sinks = [sink_ref[layer, hk * ATT_GROUP + g] for g in range(ATT_GROUP)]
        kt_loc = _tile_head(k_loc, hk)
        kt_ctx = _tile_head(kx_ref[...], hk)
        vt_loc = _with_ones_rows(v_loc_t[hk * HD:(hk + 1) * HD])
        vt_ctx = _with_ones_rows(v_ctx_t[hk * HD:(hk + 1) * HD])
        cols = slice(hk * 256, (hk + 1) * 256)
        for i in range(ATT_QB):
            win = slice(i * CH, (i + 3) * CH)
            groups.append((q_ref[i * CH:(i + 1) * CH, cols], [jnp.concatenate([kt_loc[win], kt_ctx], axis=0)],
                           [jnp.concatenate([vt_loc[:, win], vt_ctx], axis=1)], sinks, masks[i]))
    outs = _gqa_groups(groups)
    for hk in range(ATT_KV):
        cols = slice(hk * 256, (hk + 1) * 256)
        for i in range(ATT_QB):
            o_ref[i * CH:(i + 1) * CH, cols] = outs[hk * ATT_QB + i].astype(BF16)


def _attn_lat(sink, aq, ak, av, layer):
    nq = NCH_LAT // ATT_QB
    qmap = lambda b, n: (b * nq + n, 0)
    pmap = lambda b, n: (b * NCH_LAT + jnp.maximum(ATT_QB * n - 1, 0), 0)
    nmap = lambda b, n: (b * NCH_LAT + jnp.minimum(ATT_QB * (n + 1), NCH_LAT - 1), 0)
    xmap = lambda b, n: (CTX_ROWBLK + b, 0)
    kv1 = lambda m: pl.BlockSpec((CH, KV_W), m)
    kv2 = pl.BlockSpec((ATT_QB * CH, KV_W), qmap)
    ctx_spec = pl.BlockSpec((CTX, KV_W), xmap)
    return pl.pallas_call(
        functools.partial(_attn_lat_body, layer),
        grid=(B, nq),
        in_specs=[
            pl.BlockSpec(memory_space=pltpu.SMEM),
            pl.BlockSpec((ATT_QB * CH, ATT_HEADS * HD), qmap),
            kv1(pmap), kv2, kv1(nmap),
            kv1(pmap), kv2, kv1(nmap),
            ctx_spec, ctx_spec,
        ],
        out_specs=pl.BlockSpec((ATT_QB * CH, ATT_HEADS * HD), qmap),
        out_shape=jax.ShapeDtypeStruct((R_LAT, ATT_HEADS * HD), BF16),
        compiler_params=_params(("arbitrary", "arbitrary")),
        name="swa_attn",
    )(sink, aq, ak, ak, ak, av, av, av, ak, av)


def _attn_ctx_body(layer, sink_ref, q_ref, kx_ref, vx_ref, o_ref):
    groups = []
    v_all_t = vx_ref[...].astype(F32).T.astype(BF16)
    for hk in range(ATT_KV):
        sinks = [sink_ref[layer, hk * ATT_GROUP + g] for g in range(ATT_GROUP)]
        groups.append((q_ref[:, hk * 256:(hk + 1) * 256], [_tile_head(kx_ref[...], hk)],
                       [_with_ones_rows(v_all_t[hk * HD:(hk + 1) * HD])], sinks, lambda s: s))
    outs = _gqa_groups(groups)
    for hk in range(ATT_KV):
        o_ref[:, hk * 256:(hk + 1) * 256] = outs[hk].astype(BF16)


def _attn_ctx(sink, aq, ak, av, layer):
    xmap = lambda b: (CTX_ROWBLK + b, 0)
    return pl.pallas_call(
        functools.partial(_attn_ctx_body, layer),
        grid=(B,),
        in_specs=[
            pl.BlockSpec(memory_space=pltpu.SMEM),
            pl.BlockSpec((CTX, ATT_HEADS * HD), xmap),
            pl.BlockSpec((CTX, KV_W), xmap),
            pl.BlockSpec((CTX, KV_W), xmap),
        ],
        out_specs=pl.BlockSpec((CTX, ATT_HEADS * HD), lambda b: (b, 0)),
        out_shape=jax.ShapeDtypeStruct((R_CTX, ATT_HEADS * HD), BF16),
        compiler_params=_params(("arbitrary",)),
        name="ctx_attn",
    )(sink, aq, ak, av)


def _mix_ffn_body(has_ctx, final, n_cast, *refs):
    refs = list(refs)
    h_ref = refs.pop(0)
    mixers = []
    for _ in range(3):
        if has_ctx:
            lat_ref, ctx_ref = refs.pop(0), refs.pop(0)
            mixers.append(_pick(lat_ref, ctx_ref))
        else:
            mixers.append(refs.pop(0)[...])
    m_ref, nw_ref, wo_ref, w_in_ref, w_out_ref = refs[:5]
    refs = refs[5:]
    if final:
        fw_ref = refs.pop(0)
    cast_src, refs = refs[:n_cast], refs[n_cast:]
    o_ref = refs.pop(0)
    _cast_blocks(cast_src, refs)
    yf, yr, ya = mixers
    m = m_ref[...]
    y = (jnp.dot(yf.astype(BF16), wo_ref[0:256, :], preferred_element_type=F32)
         + jnp.dot(yr, wo_ref[256:512, :], preferred_element_type=F32)
         + jnp.dot(ya, wo_ref[512:1024, :], preferred_element_type=F32))
    h = h_ref[...] + _mod_slice(m, 5) * y
    h = _ffn_half_step(h, m, nw_ref[...], 6, w_in_ref, w_out_ref)
    if final:
        ms = jnp.mean(h * h, axis=-1, keepdims=True)
        h = h * lax.rsqrt(ms + EPS) * fw_ref[...]
    o_ref[...] = h


def _mix_ffn(h, mixers, mod_all, norm_w, weights, cast_sources, final_w, layer):
    has_ctx = mixers[0][1] is not None
    final = final_w is not None
    nt = NT_ALL if has_ctx else NT_LAT
    cast_in, cast_out, cast_shapes = _cast_specs(cast_sources, nt)
    in_specs = [_row_spec(D)]
    args = [h]
    for (lat, ctx), width in zip(mixers, (256, 256, 512)):
        if has_ctx:
            in_specs += _lat_or_ctx_specs(width)
            args += [lat, ctx]
        else:
            in_specs += [_row_spec(width)]
            args += [lat]
    in_specs += [_mod_spec(layer), _norm_spec(layer, 2),
                 _const_spec((D, D)), _const_spec((D, 2 * D_FF)), _const_spec((D_FF, D))]
    args += [mod_all, norm_w, *weights]
    if final:
        in_specs += [_const_spec((1, D))]
        args += [final_w]
    in_specs += cast_in
    args += [s[0] for s in cast_sources]
    return pl.pallas_call(
        functools.partial(_mix_ffn_body, has_ctx, final, len(cast_sources)),
        grid=(nt,),
        in_specs=in_specs,
        out_specs=[_row_spec(D)] + cast_out,
        out_shape=[jax.ShapeDtypeStruct((nt * TM, D), F32)] + cast_shapes,
        compiler_params=_params(("arbitrary",)),
        name="mix_ffn_final" if final else "mix_ffn",
    )(*args)


def kernel(x, c, ctx, c_ctx, norm_w, w_ada, b_ada, ffn_w_in, ffn_w_out, w_in, w_o, ret_decay, ret_gn_w,
           attn_sink, final_norm_w):
    assert x.shape == (B, S, D) and ctx.shape == (B, CTX, D)
    cos_t, sin_t = _rope_tables()
    chan, w1, twr, twi, w2, wc = _fourier_tables()

    cond_t = jnp.zeros((D, 8), F32).at[:, 0:B].set(c.T).at[:, B].set(c_ctx)
    mod_all = _ada(cond_t, w_ada, b_ada).reshape(DEPTH, 8, 1, N_MOD * D)

    def ffn_sources(layer, j):
        return [(ffn_w_in, (layer, j), D, 2 * D_FF), (ffn_w_out, (layer, j), D_FF, D)]

    pre_w = (ffn_w_in[0, 0].astype(BF16), ffn_w_out[0, 0].astype(BF16), w_in[0].astype(BF16))
    norm4 = norm_w.reshape(DEPTH, 3, 1, D)
    gn3 = ret_gn_w.reshape(DEPTH, 1, 256)
    dec = jnp.repeat(ret_decay.astype(F32), HD, axis=2)
    sink = attn_sink.astype(F32)

    h = None
    out = None
    for layer in range(DEPTH):
        last = layer == DEPTH - 1
        rows_in = (x.reshape(R_LAT, D), ctx.reshape(R_CTX, D)) if layer == 0 else (h,)
        mix_sources = [(w_o, (layer,), D, D)] + ffn_sources(layer, 1)
        (h, gr, gi, rq, rk, rv, rg, aq, ak, av, *mix_w) = _ffn_inproj(
            rows_in, mod_all, norm4, pre_w, mix_sources, cos_t, sin_t, chan, layer)

        yf = _fft(w1, w2, twr, twi, gr, gi)
        sf, sb = _ret_state(dec, rk, rv, layer)
        yr = _ret_out(dec, gn3, rq, rk, rv, rg, sf, sb, layer, ctx=False)
        ya = _attn_lat(sink, aq, ak, av, layer)
        if last:
            yfc = yrc = yac = None
        else:
            yfc = _fft_ctx(wc, gr, gi)
            yrc = _ret_out(dec, gn3, rq, rk, rv, rg, sf, sb, layer, ctx=True)
            yac = _attn_ctx(sink, aq, ak, av, layer)

        next_sources = [] if last else ffn_sources(layer + 1, 0) + [(w_in, (layer + 1,), D, D_IN)]
        res, *pre_w = _mix_ffn(h, [(yf, yfc), (yr, yrc), (ya, yac)], mod_all, norm4, mix_w, next_sources,
                               final_norm_w.reshape(1, D) if last else None, layer)
        if last:
            out = res
        else:
            h = res
    return out.reshape(B, S, D)
```

```python
import functools
import math

import numpy as np
import jax
import jax.numpy as jnp
from jax import lax
from jax.experimental import pallas as pl
from jax.experimental.pallas import tpu as pltpu

F32 = jnp.float32
BF16 = jnp.bfloat16

D = 1024
B = 2
S = 8192
DEPTH = 2
GRID_W = 64
CTX = 256
HD = 64
N_MOD = 9
D_FF = 2816
D_IN = 2048
EPS = 1e-6
NEG_INF = -1e30
ROPE_BASE = 10000.0
ROPE_PAIRS = 16
RET_HEADS = 4
ATT_HEADS = 8
ATT_KV = 2
ATT_GROUP = ATT_HEADS // ATT_KV
ATT_STACK = 2
ATT_QB = 8
ONES_ROWS = 16
KV_W = ATT_KV * HD
LOG2E = math.log2(math.e)
WINDOW = 128

R_LAT = B * S
R_CTX = B * CTX
R_ALL = R_LAT + R_CTX
TM = 512
NT_LAT = R_LAT // TM
NT_ALL = R_ALL // TM
CH = 128
assert WINDOW == CH
NCH_LAT = S // CH
NCH_CTX = CTX // CH
MXU_N = 256
FF_CHUNKS = ((0, 5 * MXU_N), (5 * MXU_N, D_FF))
FFT_L1 = 64
FFT_L2 = 128
VMEM_LIMIT = 56 * 1024 * 1024


def _params(sem, vmem=VMEM_LIMIT):
    return pltpu.CompilerParams(dimension_semantics=sem, vmem_limit_bytes=vmem)


def _const_spec(shape, index=None):
    if index is None:
        index = (0,) * len(shape)
    return pl.BlockSpec(shape, lambda *_: index, pipeline_mode=pl.Buffered(1))


def _rope_tables():
    rows = S // GRID_W
    row = np.repeat(np.arange(rows, dtype=np.float64), GRID_W)
    col = np.tile(np.arange(GRID_W, dtype=np.float64), rows)
    inv = ROPE_BASE ** (-np.arange(ROPE_PAIRS, dtype=np.float64) / ROPE_PAIRS)
    ar = row[:, None] * inv
    ac = col[:, None] * inv
    cos64 = np.concatenate([np.cos(ar), np.cos(ar), np.cos(ac), np.cos(ac)], axis=1)
    sin64 = np.concatenate([-np.sin(ar), np.sin(ar), -np.sin(ac), np.sin(ac)], axis=1)
    cos = np.concatenate([cos64, cos64], axis=1)
    sin = np.concatenate([sin64, sin64], axis=1)
    cos = np.concatenate([cos, np.ones((TM, 128))], axis=0)
    sin = np.concatenate([sin, np.zeros((TM, 128))], axis=0)
    return jnp.asarray(cos, F32), jnp.asarray(sin, F32)


def _dft_cs(n):
    k = np.arange(n)
    ang = 2.0 * np.pi * ((k[:, None] * k[None, :]) % n) / n
    return np.cos(ang), np.sin(ang)


def _fourier_tables():
    c64, s64 = _dft_cs(64)
    eye4 = np.eye(4)
    chan = np.concatenate([np.kron(eye4, c64), -np.kron(eye4, s64)], axis=1)
    w1 = np.block([[c64, s64], [-s64, c64]])
    p1 = np.arange(FFT_L1)[:, None]
    l2 = np.arange(FFT_L2)[None, :]
    ang = 2.0 * np.pi * ((p1 * l2) % S) / S
    twr = np.repeat(np.cos(ang)[:, :, None], 128, axis=2)
    twi = np.repeat(np.sin(ang)[:, :, None], 128, axis=2)
    c128, s128 = _dft_cs(128)
    w2 = np.concatenate([c128, s128], axis=1) / math.sqrt(S * 64.0)
    c256, s256 = _dft_cs(CTX)
    wc = np.concatenate([c256, s256], axis=1) / math.sqrt(CTX * 64.0)
    f = lambda a: jnp.asarray(a, F32)
    return (f(chan).astype(BF16), f(w1).astype(BF16), f(twr), f(twi), f(w2).astype(BF16), f(wc).astype(BF16))


def _mod_slice(m, i):
    return m[:, i * D:(i + 1) * D]


def _rms_mod(h, nw, shift, scale):
    ms = jnp.mean(h * h, axis=-1, keepdims=True)
    y = h * lax.rsqrt(ms + EPS) * nw
    return y * (1.0 + scale) + shift


def _swiglu(u, w_in_ref, w_out_ref):
    y = None
    for lo, hi in FF_CHUNKS:
        g = jnp.dot(u, w_in_ref[:, lo:hi], preferred_element_type=F32)
        up = jnp.dot(u, w_in_ref[:, D_FF + lo:D_FF + hi], preferred_element_type=F32)
        a = (g * jax.nn.sigmoid(g) * up).astype(BF16)
        t = jnp.dot(a, w_out_ref[lo:hi, :], preferred_element_type=F32)
        y = t if y is None else y + t
    return y


def _ffn_half_step(h, m, nw, mod0, w_in_ref, w_out_ref):
    u = _rms_mod(h, nw, _mod_slice(m, mod0), _mod_slice(m, mod0 + 1)).astype(BF16)
    return h + 0.5 * _mod_slice(m, mod0 + 2) * _swiglu(u, w_in_ref, w_out_ref)


BF16_ROWS = 16


def _cast_specs(sources, n_steps, step_of=lambda i: i):
    in_specs, out_specs, out_shapes = [], [], []
    for arr, lead, rows, cols in sources:
        n_blk = min(n_steps, rows // BF16_ROWS)
        while rows % (BF16_ROWS * n_blk):
            n_blk -= 1
        rb = rows // n_blk
        blk = lambda *idx, n_blk=n_blk: jnp.minimum(step_of(*idx), n_blk - 1)
        in_specs.append(pl.BlockSpec((None,) * len(lead) + (rb, cols),
                                     lambda *idx, lead=lead, blk=blk: (*lead, blk(*idx), 0)))
        out_specs.append(pl.BlockSpec((rb, cols), lambda *idx, blk=blk: (blk(*idx), 0)))
        out_shapes.append(jax.ShapeDtypeStruct((rows, cols), BF16))
    return in_specs, out_specs, out_shapes


def _cast_blocks(src_refs, dst_refs):
    for src, dst in zip(src_refs, dst_refs):
        dst[...] = src[...].astype(BF16)


def _log_sigmoid(x):
    return jnp.minimum(x, 0.0) - jnp.log1p(jnp.exp(-jnp.abs(x)))


def _lane_head(width):
    return lax.broadcasted_iota(jnp.int32, (1, width), 1) // HD


ADA_CB = 1152


ADA_NB = (N_MOD * D) // ADA_CB


def _ada_body(n_cast, ct_ref, w_ref, b_ref, *refs):
    cast_src, o_ref, cast_dst = refs[:n_cast], refs[n_cast], refs[n_cast + 1:]
    _cast_blocks(cast_src, cast_dst)
    ct = ct_ref[...]
    s = ct * jax.nn.sigmoid(ct)
    w = w_ref[0]
    b = b_ref[0]
    for r in range(B + 1):
        o_ref[0, r] = jnp.sum(w * s[:, r:r + 1], axis=0, keepdims=True) + b


def _ada(cond_t, w_ada, b_ada, cast_sources):
    cast_in, cast_out, cast_shapes = _cast_specs(cast_sources, DEPTH * ADA_NB, lambda l, j: l * ADA_NB + j)
    return pl.pallas_call(
        functools.partial(_ada_body, len(cast_sources)),
        grid=(DEPTH, ADA_NB),
        in_specs=[
            pl.BlockSpec((D, 8), lambda l, j: (0, 0)),
            pl.BlockSpec((1, D, ADA_CB), lambda l, j: (l, 0, j)),
            pl.BlockSpec((1, 1, ADA_CB), lambda l, j: (l, 0, j)),
        ] + cast_in,
        out_specs=[pl.BlockSpec((1, B + 1, 1, ADA_CB), lambda l, j: (l, 0, 0, j))] + cast_out,
        out_shape=[jax.ShapeDtypeStruct((DEPTH, B + 1, 1, N_MOD * D), F32)] + cast_shapes,
        compiler_params=_params(("arbitrary", "arbitrary")),
        name="ada_mod",
    )(cond_t, w_ada, b_ada.reshape(DEPTH, 1, N_MOD * D), *[s[0] for s in cast_sources])


def _mod_spec(layer):
    return pl.BlockSpec((None, None, 1, N_MOD * D),
                        lambda i: (layer, jnp.minimum(i // (NT_LAT // B), B), 0, 0))


def _norm_spec(layer, k):
    return _const_spec((None, None, 1, D), (layer, k, 0, 0))


def _row_spec(width):
    return pl.BlockSpec((TM, width), lambda i: (i, 0))


def _lat_or_ctx_specs(width):
    return [pl.BlockSpec((TM, width), lambda i: (jnp.minimum(i, NT_LAT - 1), 0)),
            pl.BlockSpec((TM, width), lambda i: (0, 0))]


def _pick(lat_ref, ctx_ref):
    return jnp.where(pl.program_id(0) < NT_LAT, lat_ref[...], ctx_ref[...])


def _rope(x, cos, sin, lane_first_half):
    outs = []
    for j in range(x.shape[1] // 128):
        xb = x[:, j * 128:(j + 1) * 128]
        fwd = pltpu.roll(xb, 112, axis=1)
        bwd = pltpu.roll(xb, 16, axis=1)
        partner = jnp.where(lane_first_half, fwd, bwd)
        outs.append(xb * cos + partner * sin)
    return outs[0] if len(outs) == 1 else jnp.concatenate(outs, axis=1)


N_CAST = 3


def _ffn_inproj_body(first, *refs):
    refs = list(refs)
    if first:
        h = _pick(refs.pop(0), refs.pop(0))
    else:
        h = refs.pop(0)[...]
    (m_ref, nw0_ref, nw1_ref, fw_in_ref, fw_out_ref, w_ref, cos_ref, sin_ref, chan_ref) = refs[:9]
    cast_src = refs[9:9 + N_CAST]
    (h_out_ref, gr_ref, gi_ref, rq_ref, rk_ref, rv_ref, rg_ref, aq_ref, ak_ref, av_ref) = refs[9 + N_CAST:19 + N_CAST]
    _cast_blocks(cast_src, refs[19 + N_CAST:])
    m = m_ref[...]
    h = _ffn_half_step(h, m, nw0_ref[...], 0, fw_in_ref, fw_out_ref)
    h_out_ref[...] = h
    lane = lax.broadcasted_iota(jnp.int32, (1, 128), 1)
    first_half = (lane // ROPE_PAIRS) % 2 == 0
    halves = [slice(r0, r0 + TM // 2) for r0 in (0, TM // 2)]
    us = [_rms_mod(h[r], nw1_ref[...], _mod_slice(m, 3), _mod_slice(m, 4)).astype(BF16) for r in halves]
    zs = [jnp.dot(u, w_ref[...], preferred_element_type=F32) for u in us]
    for r, z in zip(halves, zs):
        cos = cos_ref[r, :]
        sin = sin_ref[r, :]
        g = jnp.dot(z[:, 0:256].astype(BF16), chan_ref[...], preferred_element_type=F32)
        gr_ref[r, :] = g[:, 0:256]
        gi_ref[r, :] = g[:, 256:512]
        rq_ref[r, :] = _rope(z[:, 256:512], cos, sin, first_half).astype(BF16)
        rk_ref[r, :] = (_rope(z[:, 512:768], cos, sin, first_half) * (HD ** -0.5)).astype(BF16)
        rv_ref[r, :] = z[:, 768:1024].astype(BF16)
        rg_ref[r, :] = z[:, 1024:1280].astype(BF16)
        aq_ref[r, :] = (_rope(z[:, 1280:1792], cos, sin, first_half) * (HD ** -0.5 * LOG2E)).astype(BF16)
        ak_ref[r, :] = _rope(z[:, 1792:1920], cos, sin, first_half).astype(BF16)
        av_ref[r, :] = z[:, 1920:2048].astype(BF16)


def _rope_block(i):
    return jnp.where(i < NT_LAT, i % (S // TM), S // TM)


def _ffn_inproj(rows_in, mod_all, norm_w, weights, cast_sources, cos_t, sin_t, chan, layer):
    first = len(rows_in) == 2
    widths = (256, 256, 256, 256, 256, 256, 512, KV_W, KV_W)
    dtypes = [F32, F32] + [BF16] * (len(widths) - 2)
    cast_in, cast_out, cast_shapes = _cast_specs(cast_sources, NT_ALL)
    return pl.pallas_call(
        functools.partial(_ffn_inproj_body, first),
        grid=(NT_ALL,),
        in_specs=(_lat_or_ctx_specs(D) if first else [_row_spec(D)]) + [
            _mod_spec(layer),
            _norm_spec(layer, 0),
            _norm_spec(layer, 1),
            _const_spec((D, 2 * D_FF)), _const_spec((D_FF, D)), _const_spec((D, D_IN)),
            pl.BlockSpec((TM, 128), lambda i: (_rope_block(i), 0)),
            pl.BlockSpec((TM, 128), lambda i: (_rope_block(i), 0)),
            _const_spec((256, 512)),
        ] + cast_in,
        out_specs=[_row_spec(D)] + [_row_spec(w) for w in widths] + cast_out,
        out_shape=[jax.ShapeDtypeStruct((R_ALL, D), F32)]
        + [jax.ShapeDtypeStruct((R_ALL, w), dt) for w, dt in zip(widths, dtypes)] + cast_shapes,
        compiler_params=_params(("arbitrary",)),
        name="ffn_inproj_first" if first else "ffn_inproj",
    )(*rows_in, mod_all, norm_w, norm_w, *weights, cos_t, sin_t, chan, *[s[0] for s in cast_sources])


FFT_SUB = 16


FFT_N1 = FFT_L2 // FFT_SUB
FFT_N2 = FFT_L1 // FFT_SUB


def _fft_body(w1_ref, w2_ref, twr_ref, twi_ref, xr_ref, xi_ref, o_ref, z_ref):
    t = pl.program_id(1)

    @pl.when(t < FFT_N1)
    def _():
        w1 = w1_ref[...]
        for s in range(FFT_SUB):
            x = jnp.concatenate([xr_ref[:, s, :], xi_ref[:, s, :]], axis=0).astype(BF16)
            z = jnp.dot(w1, x, preferred_element_type=F32)
            z_ref[0, :, t, s, :] = z[0:FFT_L1]
            z_ref[1, :, t, s, :] = z[FFT_L1:2 * FFT_L1]

    @pl.when(t >= FFT_N1)
    def _():
        w2 = w2_ref[...]
        p0 = (t - FFT_N1) * FFT_SUB
        for p in range(FFT_SUB):
            zr = z_ref[0, p0 + p].reshape(FFT_L2, 256)
            zi = z_ref[1, p0 + p].reshape(FFT_L2, 256)
            tr = jnp.concatenate([twr_ref[p]] * 2, axis=1)
            ti = jnp.concatenate([twi_ref[p]] * 2, axis=1)
            yr = tr * zr + ti * zi
            yi = tr * zi - ti * zr
            y = jnp.concatenate([yr, yi], axis=0).astype(BF16)
            o_ref[:, p, :] = jnp.dot(w2, y, preferred_element_type=F32)


def _fft(w1, w2, twr, twi, gr, gi):
    view = (R_ALL // FFT_L2, FFT_N1, FFT_SUB, 256)
    x_spec = pl.BlockSpec((FFT_L1, None, FFT_SUB, 256), lambda b, t: (b, jnp.minimum(t, FFT_N1 - 1), 0, 0))
    tw_spec = pl.BlockSpec((FFT_SUB, FFT_L2, 128), lambda b, t: (jnp.maximum(t - FFT_N1, 0), 0, 0))
    out = pl.pallas_call(
        _fft_body,
        grid=(B, FFT_N1 + FFT_N2),
        in_specs=[_const_spec((128, 128)), _const_spec((128, 256)), tw_spec, tw_spec, x_spec, x_spec],
        out_specs=pl.BlockSpec((None, FFT_L2, None, FFT_SUB, 256),
                               lambda b, t: (b, 0, jnp.maximum(t - FFT_N1, 0), 0, 0)),
        out_shape=jax.ShapeDtypeStruct((B, FFT_L2, FFT_N2, FFT_SUB, 256), F32),
        scratch_shapes=[pltpu.VMEM((2, FFT_L1, FFT_N1, FFT_SUB, 256), F32)],
        compiler_params=_params(("arbitrary", "arbitrary")),
        name="fft_lat",
    )(w1, w2, twr, twi, gr.reshape(view), gi.reshape(view))
    return out.reshape(R_LAT, 256)


def _fft_ctx_body(wc_ref, gr_ref, gi_ref, o_ref):
    x = jnp.concatenate([gr_ref[...], gi_ref[...]], axis=0).astype(BF16)
    o_ref[...] = jnp.dot(wc_ref[...], x, preferred_element_type=F32)


def _fft_ctx(wc, gr, gi):
    blk0 = R_LAT // CTX
    return pl.pallas_call(
        _fft_ctx_body,
        grid=(B,),
        in_specs=[
            _const_spec((CTX, 2 * CTX)),
            pl.BlockSpec((CTX, 256), lambda b: (blk0 + b, 0)),
            pl.BlockSpec((CTX, 256), lambda b: (blk0 + b, 0)),
        ],
        out_specs=pl.BlockSpec((CTX, 256), lambda b: (b, 0)),
        out_shape=jax.ShapeDtypeStruct((R_CTX, 256), F32),
        compiler_params=_params(("arbitrary",)),
        name="fft_ctx",
    )(wc, gr, gi)


RS_CH = 8
RS_STEPS = NCH_LAT // RS_CH
N_STATE = (RS_STEPS + 1) * RS_CH
RO_CH = 8
CTX_ROWBLK = R_LAT // CTX


def _head_block_mask():
    r = lax.broadcasted_iota(jnp.int32, (256, 256), 0) // HD
    c = lax.broadcasted_iota(jnp.int32, (256, 256), 1) // HD
    return r == c


def _ret_state_body(dec_ref, kf_ref, vf_ref, kb_ref, vb_ref, kx_ref, vx_ref, sf_ref, sb_ref, accf, accb):
    t = pl.program_id(1)
    lg = _log_sigmoid(dec_ref[...])
    lgf = lg[0:1]
    lgb = lg[1:2]
    pos = lax.broadcasted_iota(jnp.int32, (CH, 1), 0).astype(F32)
    wf = jnp.exp((CH - 1.0 - pos) * lgf)
    wb = jnp.exp(pos * lgb)
    cdf = jnp.exp(float(CH) * lgf)
    cdb = jnp.exp(float(CH) * lgb)
    mask = _head_block_mask()
    dn = (((0,), (0,)), ((), ()))

    def scan(acc, out_ref, k_ref, v_ref, w, cd, order):
        kvs = {}
        for j in order:
            rows = slice(j * CH, (j + 1) * CH)
            kd = (k_ref[rows, :].astype(F32) * w).astype(BF16)
            kvs[j] = lax.dot_general(kd, v_ref[rows, :], dn, preferred_element_type=F32)
        state = acc[...]
        for j in order:
            out_ref[0, j] = state.astype(BF16)
            state = state * cd + jnp.where(mask, kvs[j], 0.0)
        acc[...] = state

    @pl.when(t == 0)
    def _():
        accf[...] = jnp.zeros_like(accf)
        accb[...] = jnp.zeros_like(accb)
        for j in range(NCH_CTX, RS_CH):
            sf_ref[0, j] = jnp.zeros((256, 256), BF16)
            sb_ref[0, j] = jnp.zeros((256, 256), BF16)
        scan(accf, sf_ref, kx_ref, vx_ref, wf, cdf, list(range(NCH_CTX)))
        scan(accb, sb_ref, kx_ref, vx_ref, wb, cdb, list(reversed(range(NCH_CTX))))

    @pl.when(t > 0)
    def _():
        scan(accf, sf_ref, kf_ref, vf_ref, wf, cdf, list(range(RS_CH)))
        scan(accb, sb_ref, kb_ref, vb_ref, wb, cdb, list(reversed(range(RS_CH))))


def _ret_state(dec, rk, rv, layer):
    rows = RS_CH * CH
    fblk = lambda t: jnp.maximum(t - 1, 0)
    bblk = lambda t: RS_STEPS - jnp.maximum(t, 1)
    fmap = lambda b, t: (b * RS_STEPS + fblk(t), 0)
    bmap = lambda b, t: (b * RS_STEPS + bblk(t), 0)
    xmap = lambda b, t: (CTX_ROWBLK + b, 0)
    return pl.pallas_call(
        _ret_state_body,
        grid=(B, RS_STEPS + 1),
        in_specs=[
            pl.BlockSpec((None, 2, 256), lambda b, t: (layer, 0, 0)),
            pl.BlockSpec((rows, 256), fmap),
            pl.BlockSpec((rows, 256), fmap),
            pl.BlockSpec((rows, 256), bmap),
            pl.BlockSpec((rows, 256), bmap),
            pl.BlockSpec((CTX, 256), xmap),
            pl.BlockSpec((CTX, 256), xmap),
        ],
        out_specs=[
            pl.BlockSpec((1, RS_CH, 256, 256), lambda b, t: (b, jnp.where(t == 0, RS_STEPS, t - 1), 0, 0)),
            pl.BlockSpec((1, RS_CH, 256, 256), lambda b, t: (b, jnp.where(t == 0, RS_STEPS, RS_STEPS - t), 0, 0)),
        ],
        out_shape=[jax.ShapeDtypeStruct((B, N_STATE, 256, 256), BF16)] * 2,
        scratch_shapes=[pltpu.VMEM((256, 256), F32), pltpu.VMEM((256, 256), F32)],
        compiler_params=_params(("arbitrary", "arbitrary")),
        name="ret_state",
    )(dec, rk, rv, rk, rv, rk, rv)


def _split_dot(x, w):
    hi = x.astype(BF16)
    lo = (x - hi.astype(F32)).astype(BF16)
    return jnp.dot(hi, w, preferred_element_type=F32) + jnp.dot(lo, w, preferred_element_type=F32)


def _ret_out_body(nck, dec_ref, gn_ref, q_ref, k_ref, v_ref, g_ref, sf_ref, sb_ref, o_ref):
    lg = _log_sigmoid(dec_ref[...])
    lgf = lg[0:1]
    lgb = lg[1:2]
    pos = lax.broadcasted_iota(jnp.int32, (CH, 1), 0).astype(F32)
    wqf = jnp.exp((pos + 1.0) * lgf)
    wqb = jnp.exp((float(CH) - pos) * lgb)
    ri = lax.broadcasted_iota(jnp.int32, (CH, CH), 0)
    ci = lax.broadcasted_iota(jnp.int32, (CH, CH), 1)
    diff = (ri - ci).astype(F32)
    decays = []
    for h in range(RET_HEADS):
        lgf_h = lgf[:, h * HD:h * HD + 1]
        lgb_h = lgb[:, h * HD:h * HD + 1]
        decays.append(jnp.where(ri >= ci, jnp.exp(jnp.maximum(diff, 0.0) * lgf_h), 0.0)
                      + jnp.where(ci >= ri, jnp.exp(jnp.maximum(-diff, 0.0) * lgb_h), 0.0))
    decay = jnp.concatenate(decays, axis=0)
    lane_head = _lane_head(256)
    avg = jnp.where(_head_block_mask(), 1.0 / HD, 0.0).astype(BF16)
    gn = gn_ref[...]
    dn = (((1,), (1,)), ((), ()))
    chunk_rows = [slice(c * CH, (c + 1) * CH) for c in range(nck)]
    cross, scores = [], []
    for c, rows in enumerate(chunk_rows):
        qb16 = q_ref[rows, :]
        q = qb16.astype(F32)
        cross.append(jnp.dot((q * wqf).astype(BF16), sf_ref[0, c], preferred_element_type=F32)
                     + jnp.dot((q * wqb).astype(BF16), sb_ref[0, c], preferred_element_type=F32))
        qs = jnp.concatenate([jnp.where(lane_head == h, qb16, jnp.zeros_like(qb16))
                              for h in range(RET_HEADS)], axis=0)
        scores.append(lax.dot_general(qs, k_ref[rows, :], dn, preferred_element_type=F32))
    outs = []
    for c, rows in enumerate(chunk_rows):
        p = (scores[c] * decay).astype(BF16)
        of = jnp.dot(p, v_ref[rows, :], preferred_element_type=F32)
        o = cross[c]
        for h in range(RET_HEADS):
            o = o + jnp.where(lane_head == h, of[h * CH:(h + 1) * CH], 0.0)
        outs.append(o)
    o = jnp.concatenate(outs, axis=0)
    mu = _split_dot(o, avg)
    cen = o - mu
    var = _split_dot(cen * cen, avg)
    on = cen * lax.rsqrt(var + EPS) * gn
    g = g_ref[...].astype(F32)
    o_ref[...] = (g * jax.nn.sigmoid(g) * on).astype(BF16)


def _ret_out(dec, gn, rq, rk, rv, rg, sf, sb, layer, ctx):
    nck = NCH_CTX if ctx else RO_CH
    rows = nck * CH
    if ctx:
        grid = (B, 1)
        rmap = lambda b, u: (CTX_ROWBLK + b, 0)
        omap = lambda b, u: (b, 0)
        smap = lambda b, u: (b, NCH_LAT // NCH_CTX, 0, 0)
        out_rows = R_CTX
    else:
        grid = (B, NCH_LAT // RO_CH)
        rmap = lambda b, u: (b * (NCH_LAT // RO_CH) + u, 0)
        omap = rmap
        smap = lambda b, u: (b, u, 0, 0)
        out_rows = R_LAT
    return pl.pallas_call(
        functools.partial(_ret_out_body, nck),
        grid=grid,
        in_specs=[
            pl.BlockSpec((None, 2, 256), lambda b, u: (layer, 0, 0)),
            pl.BlockSpec((None, 1, 256), lambda b, u: (layer, 0, 0)),
            pl.BlockSpec((rows, 256), rmap),
            pl.BlockSpec((rows, 256), rmap),
            pl.BlockSpec((rows, 256), rmap),
            pl.BlockSpec((rows, 256), rmap),
            pl.BlockSpec((1, nck, 256, 256), smap),
            pl.BlockSpec((1, nck, 256, 256), smap),
        ],
        out_specs=pl.BlockSpec((rows, 256), omap),
        out_shape=jax.ShapeDtypeStruct((out_rows, 256), BF16),
        compiler_params=_params(("arbitrary", "arbitrary")),
        name="ret_out_ctx" if ctx else "ret_out",
    )(dec, gn, rq, rk, rv, rg, sf, sb)


def _tile_head(x_all, hk):
    own = jnp.where(_lane_head(KV_W) == hk, x_all, jnp.zeros_like(x_all))
    both = own + pltpu.roll(own.astype(F32), HD, axis=1).astype(BF16)
    return jnp.concatenate([both, both], axis=1)


def _gqa_groups(groups):
    lane_head = _lane_head(256)
    dn = (((1,), (1,)), ((), ()))
    chains = []
    for gi, (q_grp, k_parts, v_parts, sinks, mask_fn) in enumerate(groups):
        for h0 in range(0, ATT_GROUP, ATT_STACK):
            chains.append((gi, tuple(range(h0, h0 + ATT_STACK)), q_grp, k_parts, v_parts, sinks, mask_fn))
    scores, maxima = [], []
    for _, heads, q_grp, k_parts, _, _, mask_fn in chains:
        qs = jnp.concatenate([jnp.where(lane_head == g, q_grp, jnp.zeros_like(q_grp))
                              for g in heads], axis=0)
        st = mask_fn(jnp.concatenate(
            [lax.dot_general(kp, qs, dn, preferred_element_type=F32) for kp in k_parts], axis=0))
        scores.append(st)
        maxima.append(jnp.max(st, axis=0, keepdims=True))
    probs, sink_terms = [], []
    for (_, heads, q_grp, _, _, sinks, _), st, st_max in zip(chains, scores, maxima):
        m = q_grp.shape[0]
        sink = jnp.concatenate([jnp.full((1, m), sinks[g] * LOG2E, F32) for g in heads], axis=1)
        mx = jnp.maximum(st_max, sink)
        probs.append(jnp.exp2(st - mx).astype(BF16))
        sink_terms.append(jnp.exp2(sink - mx))
    pieces = [[] for _ in groups]
    for (gi, _, q_grp, _, v_parts, _, _), p, sink_term in zip(chains, probs, sink_terms):
        m = q_grp.shape[0]
        acc, row = None, 0
        for vp in v_parts:
            n = vp.shape[1]
            t = jnp.dot(vp, p[row:row + n], preferred_element_type=F32)
            acc = t if acc is None else acc + t
            row += n
        ot = acc[0:HD] * (1.0 / (acc[HD:HD + 1] + sink_term))
        stacked = jnp.concatenate([ot[:, i * m:(i + 1) * m] for i in range(ATT_STACK)], axis=0)
        pieces[gi].append(stacked.T)
    return [jnp.concatenate(ps, axis=1) for ps in pieces]


def _band_mask_fn(lo_rows, lo_bias, hi_rows, hi_bias, n_keys):
    def mask_fn(st):
        parts = []
        row = 0
        while row < n_keys:
            piece = st[row:row + CH]
            if row == lo_rows:
                piece = piece + lo_bias
            elif row == hi_rows:
                piece = piece + hi_bias
            parts.append(piece)
            row += CH
        return jnp.concatenate(parts, axis=0)
    return mask_fn


def _with_ones_rows(v_t):
    return jnp.concatenate([v_t, jnp.ones((ONES_ROWS, v_t.shape[1]), BF16)], axis=0)


def _attn_lat_body(layer, sink_ref, q_ref, kp_ref, kc_ref, kn_ref, vp_ref, vc_ref, vn_ref, kx_ref, vx_ref, o_ref):
    step = pl.program_id(1)
    k_loc = jnp.concatenate([kp_ref[...], kc_ref[...], kn_ref[...]], axis=0)
    v_loc = jnp.concatenate([vp_ref[...], vc_ref[...], vn_ref[...]], axis=0)
    v_loc_t = v_loc.astype(F32).T.astype(BF16)
    v_ctx_t = vx_ref[...].astype(F32).T.astype(BF16)
    key = lax.broadcasted_iota(jnp.int32, (CH, ATT_STACK * CH), 0)
    qry = lax.broadcasted_iota(jnp.int32, (CH, ATT_STACK * CH), 1) & (CH - 1)
    below = key >= qry
    above = key <= qry
    bias = lambda ok: jnp.where(ok, 0.0, NEG_INF)
    lo_bias, lo_bias_first = bias(below), bias(below & (step > 0))
    hi_bias, hi_bias_last = bias(above), bias(above & (step < NCH_LAT // ATT_QB - 1))
    n_keys = 3 * CH + CTX
    masks = []
    for i in range(ATT_QB):
        masks.append(_band_mask_fn(0, lo_bias_first if i == 0 else lo_bias,
                                   2 * CH, hi_bias_last if i == ATT_QB - 1 else hi_bias, n_keys))
    groups = []
    for hk in range(ATT_KV):
        sinks = [sink_ref[layer, hk * ATT_GROUP + g] for g in range(ATT_GROUP)]
        kt_loc = _tile_head(k_loc, hk)
        kt_ctx = _tile_head(kx_ref[...], hk)
        vt_loc = _with_ones_rows(v_loc_t[hk * HD:(hk + 1) * HD])
        vt_ctx = _with_ones_rows(v_ctx_t[hk * HD:(hk + 1) * HD])
        cols = slice(hk * 256, (hk + 1) * 256)
        for i in range(ATT_QB):
            win = slice(i * CH, (i + 3) * CH)
            groups.append((q_ref[i * CH:(i + 1) * CH, cols], [jnp.concatenate([kt_loc[win], kt_ctx], axis=0)],
                           [jnp.concatenate([vt_loc[:, win], vt_ctx], axis=1)], sinks, masks[i]))
    outs = _gqa_groups(groups)
    for hk in range(ATT_KV):
        cols = slice(hk * 256, (hk + 1) * 256)
        for i in range(ATT_QB):
            o_ref[i * CH:(i + 1) * CH, cols] = outs[hk * ATT_QB + i].astype(BF16)


def _attn_lat(sink, aq, ak, av, layer):
    nq = NCH_LAT // ATT_QB
    qmap = lambda b, n: (b * nq + n, 0)
    pmap = lambda b, n: (b * NCH_LAT + jnp.maximum(ATT_QB * n - 1, 0), 0)
    nmap = lambda b, n: (b * NCH_LAT + jnp.minimum(ATT_QB * (n + 1), NCH_LAT - 1), 0)
    xmap = lambda b, n: (CTX_ROWBLK + b, 0)
    kv1 = lambda m: pl.BlockSpec((CH, KV_W), m)
    kv2 = pl.BlockSpec((ATT_QB * CH, KV_W), qmap)
    ctx_spec = pl.BlockSpec((CTX, KV_W), xmap)
    return pl.pallas_call(
        functools.partial(_attn_lat_body, layer),
        grid=(B, nq),
        in_specs=[
            pl.BlockSpec(memory_space=pltpu.SMEM),
            pl.BlockSpec((ATT_QB * CH, ATT_HEADS * HD), qmap),
            kv1(pmap), kv2, kv1(nmap),
            kv1(pmap), kv2, kv1(nmap),
            ctx_spec, ctx_spec,
        ],
        out_specs=pl.BlockSpec((ATT_QB * CH, ATT_HEADS * HD), qmap),
        out_shape=jax.ShapeDtypeStruct((R_LAT, ATT_HEADS * HD), BF16),
        compiler_params=_params(("arbitrary", "arbitrary")),
        name="swa_attn",
    )(sink, aq, ak, ak, ak, av, av, av, ak, av)


def _attn_ctx_body(layer, sink_ref, q_ref, kx_ref, vx_ref, o_ref):
    groups = []
    v_all_t = vx_ref[...].astype(F32).T.astype(BF16)
    for hk in range(ATT_KV):
        sinks = [sink_ref[layer, hk * ATT_GROUP + g] for g in range(ATT_GROUP)]
        groups.append((q_ref[:, hk * 256:(hk + 1) * 256], [_tile_head(kx_ref[...], hk)],
                       [_with_ones_rows(v_all_t[hk * HD:(hk + 1) * HD])], sinks, lambda s: s))
    outs = _gqa_groups(groups)
    for hk in range(ATT_KV):
        o_ref[:, hk * 256:(hk + 1) * 256] = outs[hk].astype(BF16)


def _attn_ctx(sink, aq, ak, av, layer):
    xmap = lambda b: (CTX_ROWBLK + b, 0)
    return pl.pallas_call(
        functools.partial(_attn_ctx_body, layer),
        grid=(B,),
        in_specs=[
            pl.BlockSpec(memory_space=pltpu.SMEM),
            pl.BlockSpec((CTX, ATT_HEADS * HD), xmap),
            pl.BlockSpec((CTX, KV_W), xmap),
            pl.BlockSpec((CTX, KV_W), xmap),
        ],
        out_specs=pl.BlockSpec((CTX, ATT_HEADS * HD), lambda b: (b, 0)),
        out_shape=jax.ShapeDtypeStruct((R_CTX, ATT_HEADS * HD), BF16),
        compiler_params=_params(("arbitrary",)),
        name="ctx_attn",
    )(sink, aq, ak, av)


def _mix_ffn_body(has_ctx, final, n_cast, *refs):
    refs = list(refs)
    h_ref = refs.pop(0)
    mixers = []
    for _ in range(3):
        if has_ctx:
            lat_ref, ctx_ref = refs.pop(0), refs.pop(0)
            mixers.append(_pick(lat_ref, ctx_ref))
        else:
            mixers.append(refs.pop(0)[...])
    m_ref, nw_ref, wo_ref, w_in_ref, w_out_ref = refs[:5]
    refs = refs[5:]
    if final:
        fw_ref = refs.pop(0)
    cast_src, refs = refs[:n_cast], refs[n_cast:]
    o_ref = refs.pop(0)
    _cast_blocks(cast_src, refs)
    yf, yr, ya = mixers
    m = m_ref[...]
    y = (jnp.dot(yf.astype(BF16), wo_ref[0:256, :], preferred_element_type=F32)
         + jnp.dot(yr, wo_ref[256:512, :], preferred_element_type=F32)
         + jnp.dot(ya, wo_ref[512:1024, :], preferred_element_type=F32))
    h = h_ref[...] + _mod_slice(m, 5) * y
    h = _ffn_half_step(h, m, nw_ref[...], 6, w_in_ref, w_out_ref)
    if final:
        ms = jnp.mean(h * h, axis=-1, keepdims=True)
        h = h * lax.rsqrt(ms + EPS) * fw_ref[...]
    o_ref[...] = h


def _mix_ffn(h, mixers, mod_all, norm_w, weights, cast_sources, final_w, layer):
    has_ctx = mixers[0][1] is not None
    final = final_w is not None
    nt = NT_ALL if has_ctx else NT_LAT
    cast_in, cast_out, cast_shapes = _cast_specs(cast_sources, nt)
    in_specs = [_row_spec(D)]
    args = [h]
    for (lat, ctx), width in zip(mixers, (256, 256, 512)):
        if has_ctx:
            in_specs += _lat_or_ctx_specs(width)
            args += [lat, ctx]
        else:
            in_specs += [_row_spec(width)]
            args += [lat]
    in_specs += [_mod_spec(layer), _norm_spec(layer, 2),
                 _const_spec((D, D)), _const_spec((D, 2 * D_FF)), _const_spec((D_FF, D))]
    args += [mod_all, norm_w, *weights]
    if final:
        in_specs += [_const_spec((1, D))]
        args += [final_w]
    in_specs += cast_in
    args += [s[0] for s in cast_sources]
    return pl.pallas_call(
        functools.partial(_mix_ffn_body, has_ctx, final, len(cast_sources)),
        grid=(nt,),
        in_specs=in_specs,
        out_specs=[_row_spec(D)] + cast_out,
        out_shape=[jax.ShapeDtypeStruct((nt * TM, D), F32)] + cast_shapes,
        compiler_params=_params(("arbitrary",)),
        name="mix_ffn_final" if final else "mix_ffn",
    )(*args)


def kernel(x, c, ctx, c_ctx, norm_w, w_ada, b_ada, ffn_w_in, ffn_w_out, w_in, w_o, ret_decay, ret_gn_w,
           attn_sink, final_norm_w):
    assert x.shape == (B, S, D) and ctx.shape == (B, CTX, D)
    cos_t, sin_t = _rope_tables()
    chan, w1, twr, twi, w2, wc = _fourier_tables()

    def ffn_sources(layer, j):
        return [(ffn_w_in, (layer, j), D, 2 * D_FF), (ffn_w_out, (layer, j), D_FF, D)]

    cond_t = jnp.zeros((D, 8), F32).at[:, 0:B].set(c.T).at[:, B].set(c_ctx)
    mod_all, *pre_w = _ada(cond_t, w_ada, b_ada, ffn_sources(0, 0) + [(w_in, (0,), D, D_IN)])
    norm4 = norm_w.reshape(DEPTH, 3, 1, D)
    gn3 = ret_gn_w.reshape(DEPTH, 1, 256)
    dec = jnp.repeat(ret_decay.astype(F32), HD, axis=2)
    sink = attn_sink.astype(F32)

    h = None
    out = None
    for layer in range(DEPTH):
        last = layer == DEPTH - 1
        rows_in = (x.reshape(R_LAT, D), ctx.reshape(R_CTX, D)) if layer == 0 else (h,)
        mix_sources = [(w_o, (layer,), D, D)] + ffn_sources(layer, 1)
        (h, gr, gi, rq, rk, rv, rg, aq, ak, av, *mix_w) = _ffn_inproj(
            rows_in, mod_all, norm4, pre_w, mix_sources, cos_t, sin_t, chan, layer)

        yf = _fft(w1, w2, twr, twi, gr, gi)
        sf, sb = _ret_state(dec, rk, rv, layer)
        yr = _ret_out(dec, gn3, rq, rk, rv, rg, sf, sb, layer, ctx=False)
        ya = _attn_lat(sink, aq, ak, av, layer)
        if last:
            yfc = yrc = yac = None
        else:
            yfc = _fft_ctx(wc, gr, gi)
            yrc = _ret_out(dec, gn3, rq, rk, rv, rg, sf, sb, layer, ctx=True)
            yac = _attn_ctx(sink, aq, ak, av, layer)

        next_sources = [] if last else ffn_sources(layer + 1, 0) + [(w_in, (layer + 1,), D, D_IN)]
        res, *pre_w = _mix_ffn(h, [(yf, yfc), (yr, yrc), (ya, yac)], mod_all, norm4, mix_w, next_sources,
                               final_norm_w.reshape(1, D) if last else None, layer)
        if last:
            out = res
        else:
            h = res
    return out.reshape(B, S, D)
```

```python
import functools
import math

import numpy as np
import jax
import jax.numpy as jnp
from jax import lax
from jax.experimental import pallas as pl
from jax.experimental.pallas import tpu as pltpu

F32 = jnp.float32
BF16 = jnp.bfloat16

D = 1024
B = 2
S = 8192
DEPTH = 2
GRID_W = 64
CTX = 256
HD = 64
N_MOD = 9
D_FF = 2816
D_IN = 2048
EPS = 1e-6
NEG_INF = -1e30
ROPE_BASE = 10000.0
ROPE_PAIRS = 16
RET_HEADS = 4
ATT_HEADS = 8
ATT_KV = 2
ATT_GROUP = ATT_HEADS // ATT_KV
ATT_STACK = 2
ATT_QB = 8
ATT_WAVES = 2
ONES_ROWS = 16
KV_W = ATT_KV * HD
LOG2E = math.log2(math.e)
WINDOW = 128

R_LAT = B * S
R_CTX = B * CTX
R_ALL = R_LAT + R_CTX
TM = 512
NT_LAT = R_LAT // TM
NT_ALL = R_ALL // TM
CH = 128
assert WINDOW == CH
NCH_LAT = S // CH
NCH_CTX = CTX // CH
MXU_N = 256
FF_CHUNKS = ((0, 5 * MXU_N), (5 * MXU_N, D_FF))
FFT_L1 = 64
FFT_L2 = 128
VMEM_LIMIT = 56 * 1024 * 1024


def _params(sem, vmem=VMEM_LIMIT):
    return pltpu.CompilerParams(dimension_semantics=sem, vmem_limit_bytes=vmem)


def _const_spec(shape, index=None):
    if index is None:
        index = (0,) * len(shape)
    return pl.BlockSpec(shape, lambda *_: index, pipeline_mode=pl.Buffered(1))


def _rope_tables():
    rows = S // GRID_W
    row = np.repeat(np.arange(rows, dtype=np.float64), GRID_W)
    col = np.tile(np.arange(GRID_W, dtype=np.float64), rows)
    inv = ROPE_BASE ** (-np.arange(ROPE_PAIRS, dtype=np.float64) / ROPE_PAIRS)
    ar = row[:, None] * inv
    ac = col[:, None] * inv
    cos64 = np.concatenate([np.cos(ar), np.cos(ar), np.cos(ac), np.cos(ac)], axis=1)
    sin64 = np.concatenate([-np.sin(ar), np.sin(ar), -np.sin(ac), np.sin(ac)], axis=1)
    cos = np.concatenate([cos64, cos64], axis=1)
    sin = np.concatenate([sin64, sin64], axis=1)
    cos = np.concatenate([cos, np.ones((TM, 128))], axis=0)
    sin = np.concatenate([sin, np.zeros((TM, 128))], axis=0)
    return jnp.asarray(cos, F32), jnp.asarray(sin, F32)


def _dft_cs(n):
    k = np.arange(n)
    ang = 2.0 * np.pi * ((k[:, None] * k[None, :]) % n) / n
    return np.cos(ang), np.sin(ang)


def _fourier_tables():
    c64, s64 = _dft_cs(64)
    eye4 = np.eye(4)
    chan = np.concatenate([np.kron(eye4, c64), -np.kron(eye4, s64)], axis=1)
    w1 = np.block([[c64, s64], [-s64, c64]])
    p1 = np.arange(FFT_L1)[:, None]
    l2 = np.arange(FFT_L2)[None, :]
    ang = 2.0 * np.pi * ((p1 * l2) % S) / S
    twr = np.repeat(np.cos(ang)[:, :, None], 128, axis=2)
    twi = np.repeat(np.sin(ang)[:, :, None], 128, axis=2)
    c128, s128 = _dft_cs(128)
    w2 = np.concatenate([c128, s128], axis=1) / math.sqrt(S * 64.0)
    c256, s256 = _dft_cs(CTX)
    wc = np.concatenate([c256, s256], axis=1) / math.sqrt(CTX * 64.0)
    f = lambda a: jnp.asarray(a, F32)
    return (f(chan).astype(BF16), f(w1).astype(BF16), f(twr), f(twi), f(w2).astype(BF16), f(wc).astype(BF16))


def _mod_slice(m, i):
    return m[:, i * D:(i + 1) * D]


def _rms_mod(h, nw, shift, scale):
    ms = jnp.mean(h * h, axis=-1, keepdims=True)
    y = h * lax.rsqrt(ms + EPS) * nw
    return y * (1.0 + scale) + shift


def _swiglu(u, w_in_ref, w_out_ref):
    y = None
    for lo, hi in FF_CHUNKS:
        g = jnp.dot(u, w_in_ref[:, lo:hi], preferred_element_type=F32)
        up = jnp.dot(u, w_in_ref[:, D_FF + lo:D_FF + hi], preferred_element_type=F32)
        a = (g * jax.nn.sigmoid(g) * up).astype(BF16)
        t = jnp.dot(a, w_out_ref[lo:hi, :], preferred_element_type=F32)
        y = t if y is None else y + t
    return y


def _ffn_half_step(h, m, nw, mod0, w_in_ref, w_out_ref):
    u = _rms_mod(h, nw, _mod_slice(m, mod0), _mod_slice(m, mod0 + 1)).astype(BF16)
    return h + 0.5 * _mod_slice(m, mod0 + 2) * _swiglu(u, w_in_ref, w_out_ref)


BF16_ROWS = 16


def _cast_specs(sources, n_steps, step_of=lambda i: i):
    in_specs, out_specs, out_shapes = [], [], []
    for arr, lead, rows, cols in sources:
        n_blk = min(n_steps, rows // BF16_ROWS)
        while rows % (BF16_ROWS * n_blk):
            n_blk -= 1
        rb = rows // n_blk
        blk = lambda *idx, n_blk=n_blk: jnp.minimum(step_of(*idx), n_blk - 1)
        in_specs.append(pl.BlockSpec((None,) * len(lead) + (rb, cols),
                                     lambda *idx, lead=lead, blk=blk: (*lead, blk(*idx), 0)))
        out_specs.append(pl.BlockSpec((rb, cols), lambda *idx, blk=blk: (blk(*idx), 0)))
        out_shapes.append(jax.ShapeDtypeStruct((rows, cols), BF16))
    return in_specs, out_specs, out_shapes


def _cast_blocks(src_refs, dst_refs):
    for src, dst in zip(src_refs, dst_refs):
        dst[...] = src[...].astype(BF16)


def _log_sigmoid(x):
    return jnp.minimum(x, 0.0) - jnp.log1p(jnp.exp(-jnp.abs(x)))


def _lane_head(width):
    return lax.broadcasted_iota(jnp.int32, (1, width), 1) // HD


ADA_CB = 1152


ADA_NB = (N_MOD * D) // ADA_CB


def _ada_body(n_cast, ct_ref, w_ref, b_ref, *refs):
    cast_src, o_ref, cast_dst = refs[:n_cast], refs[n_cast], refs[n_cast + 1:]
    _cast_blocks(cast_src, cast_dst)
    ct = ct_ref[...]
    s = ct * jax.nn.sigmoid(ct)
    w = w_ref[0]
    b = b_ref[0]
    for r in range(B + 1):
        o_ref[0, r] = jnp.sum(w * s[:, r:r + 1], axis=0, keepdims=True) + b


def _ada(cond_t, w_ada, b_ada, cast_sources):
    cast_in, cast_out, cast_shapes = _cast_specs(cast_sources, DEPTH * ADA_NB, lambda l, j: l * ADA_NB + j)
    return pl.pallas_call(
        functools.partial(_ada_body, len(cast_sources)),
        grid=(DEPTH, ADA_NB),
        in_specs=[
            pl.BlockSpec((D, 8), lambda l, j: (0, 0)),
            pl.BlockSpec((1, D, ADA_CB), lambda l, j: (l, 0, j)),
            pl.BlockSpec((1, 1, ADA_CB), lambda l, j: (l, 0, j)),
        ] + cast_in,
        out_specs=[pl.BlockSpec((1, B + 1, 1, ADA_CB), lambda l, j: (l, 0, 0, j))] + cast_out,
        out_shape=[jax.ShapeDtypeStruct((DEPTH, B + 1, 1, N_MOD * D), F32)] + cast_shapes,
        compiler_params=_params(("arbitrary", "arbitrary")),
        name="ada_mod",
    )(cond_t, w_ada, b_ada.reshape(DEPTH, 1, N_MOD * D), *[s[0] for s in cast_sources])


def _mod_spec(layer):
    return pl.BlockSpec((None, None, 1, N_MOD * D),
                        lambda i: (layer, jnp.minimum(i // (NT_LAT // B), B), 0, 0))


def _norm_spec(layer, k):
    return _const_spec((None, None, 1, D), (layer, k, 0, 0))


def _row_spec(width):
    return pl.BlockSpec((TM, width), lambda i: (i, 0))


def _lat_or_ctx_specs(width):
    return [pl.BlockSpec((TM, width), lambda i: (jnp.minimum(i, NT_LAT - 1), 0)),
            pl.BlockSpec((TM, width), lambda i: (0, 0))]


def _pick(lat_ref, ctx_ref):
    return jnp.where(pl.program_id(0) < NT_LAT, lat_ref[...], ctx_ref[...])


def _rope(x, cos, sin, lane_first_half):
    outs = []
    for j in range(x.shape[1] // 128):
        xb = x[:, j * 128:(j + 1) * 128]
        fwd = pltpu.roll(xb, 112, axis=1)
        bwd = pltpu.roll(xb, 16, axis=1)
        partner = jnp.where(lane_first_half, fwd, bwd)
        outs.append(xb * cos + partner * sin)
    return outs[0] if len(outs) == 1 else jnp.concatenate(outs, axis=1)


N_CAST = 3


def _ffn_inproj_body(first, *refs):
    refs = list(refs)
    if first:
        h = _pick(refs.pop(0), refs.pop(0))
    else:
        h = refs.pop(0)[...]
    (m_ref, nw0_ref, nw1_ref, fw_in_ref, fw_out_ref, w_ref, cos_ref, sin_ref, chan_ref) = refs[:9]
    cast_src = refs[9:9 + N_CAST]
    (h_out_ref, gr_ref, gi_ref, rq_ref, rk_ref, rv_ref, rg_ref, aq_ref, ak_ref, av_ref) = refs[9 + N_CAST:19 + N_CAST]
    _cast_blocks(cast_src, refs[19 + N_CAST:])
    m = m_ref[...]
    h = _ffn_half_step(h, m, nw0_ref[...], 0, fw_in_ref, fw_out_ref)
    h_out_ref[...] = h
    lane = lax.broadcasted_iota(jnp.int32, (1, 128), 1)
    first_half = (lane // ROPE_PAIRS) % 2 == 0
    halves = [slice(r0, r0 + TM // 2) for r0 in (0, TM // 2)]
    us = [_rms_mod(h[r], nw1_ref[...], _mod_slice(m, 3), _mod_slice(m, 4)).astype(BF16) for r in halves]
    zs = [jnp.dot(u, w_ref[...], preferred_element_type=F32) for u in us]
    for r, z in zip(halves, zs):
        cos = cos_ref[r, :]
        sin = sin_ref[r, :]
        g = jnp.dot(z[:, 0:256].astype(BF16), chan_ref[...], preferred_element_type=F32)
        gr_ref[r, :] = g[:, 0:256]
        gi_ref[r, :] = g[:, 256:512]
        rq_ref[r, :] = _rope(z[:, 256:512], cos, sin, first_half).astype(BF16)
        rk_ref[r, :] = (_rope(z[:, 512:768], cos, sin, first_half) * (HD ** -0.5)).astype(BF16)
        rv_ref[r, :] = z[:, 768:1024].astype(BF16)
        rg_ref[r, :] = z[:, 1024:1280].astype(BF16)
        aq_ref[r, :] = (_rope(z[:, 1280:1792], cos, sin, first_half) * (HD ** -0.5 * LOG2E)).astype(BF16)
        ak_ref[r, :] = _rope(z[:, 1792:1920], cos, sin, first_half).astype(BF16)
        av_ref[r, :] = z[:, 1920:2048].astype(BF16)


def _rope_block(i):
    return jnp.where(i < NT_LAT, i % (S // TM), S // TM)


def _ffn_inproj(rows_in, mod_all, norm_w, weights, cast_sources, cos_t, sin_t, chan, layer):
    first = len(rows_in) == 2
    widths = (256, 256, 256, 256, 256, 256, 512, KV_W, KV_W)
    dtypes = [F32, F32] + [BF16] * (len(widths) - 2)
    cast_in, cast_out, cast_shapes = _cast_specs(cast_sources, NT_ALL)
    return pl.pallas_call(
        functools.partial(_ffn_inproj_body, first),
        grid=(NT_ALL,),
        in_specs=(_lat_or_ctx_specs(D) if first else [_row_spec(D)]) + [
            _mod_spec(layer),
            _norm_spec(layer, 0),
            _norm_spec(layer, 1),
            _const_spec((D, 2 * D_FF)), _const_spec((D_FF, D)), _const_spec((D, D_IN)),
            pl.BlockSpec((TM, 128), lambda i: (_rope_block(i), 0)),
            pl.BlockSpec((TM, 128), lambda i: (_rope_block(i), 0)),
            _const_spec((256, 512)),
        ] + cast_in,
        out_specs=[_row_spec(D)] + [_row_spec(w) for w in widths] + cast_out,
        out_shape=[jax.ShapeDtypeStruct((R_ALL, D), F32)]
        + [jax.ShapeDtypeStruct((R_ALL, w), dt) for w, dt in zip(widths, dtypes)] + cast_shapes,
        compiler_params=_params(("arbitrary",)),
        name="ffn_inproj_first" if first else "ffn_inproj",
    )(*rows_in, mod_all, norm_w, norm_w, *weights, cos_t, sin_t, chan, *[s[0] for s in cast_sources])


FFT_SUB = 16


FFT_N1 = FFT_L2 // FFT_SUB
FFT_N2 = FFT_L1 // FFT_SUB


def _fft_body(w1_ref, w2_ref, twr_ref, twi_ref, xr_ref, xi_ref, o_ref, z_ref):
    t = pl.program_id(1)

    @pl.when(t < FFT_N1)
    def _():
        w1 = w1_ref[...]
        for s in range(FFT_SUB):
            x = jnp.concatenate([xr_ref[:, s, :], xi_ref[:, s, :]], axis=0).astype(BF16)
            z = jnp.dot(w1, x, preferred_element_type=F32)
            z_ref[0, :, t, s, :] = z[0:FFT_L1]
            z_ref[1, :, t, s, :] = z[FFT_L1:2 * FFT_L1]

    @pl.when(t >= FFT_N1)
    def _():
        w2 = w2_ref[...]
        p0 = (t - FFT_N1) * FFT_SUB
        for p in range(FFT_SUB):
            zr = z_ref[0, p0 + p].reshape(FFT_L2, 256)
            zi = z_ref[1, p0 + p].reshape(FFT_L2, 256)
            tr = jnp.concatenate([twr_ref[p]] * 2, axis=1)
            ti = jnp.concatenate([twi_ref[p]] * 2, axis=1)
            yr = tr * zr + ti * zi
            yi = tr * zi - ti * zr
            y = jnp.concatenate([yr, yi], axis=0).astype(BF16)
            o_ref[:, p, :] = jnp.dot(w2, y, preferred_element_type=F32)


def _fft(w1, w2, twr, twi, gr, gi):
    view = (R_ALL // FFT_L2, FFT_N1, FFT_SUB, 256)
    x_spec = pl.BlockSpec((FFT_L1, None, FFT_SUB, 256), lambda b, t: (b, jnp.minimum(t, FFT_N1 - 1), 0, 0))
    tw_spec = pl.BlockSpec((FFT_SUB, FFT_L2, 128), lambda b, t: (jnp.maximum(t - FFT_N1, 0), 0, 0))
    out = pl.pallas_call(
        _fft_body,
        grid=(B, FFT_N1 + FFT_N2),
        in_specs=[_const_spec((128, 128)), _const_spec((128, 256)), tw_spec, tw_spec, x_spec, x_spec],
        out_specs=pl.BlockSpec((None, FFT_L2, None, FFT_SUB, 256),
                               lambda b, t: (b, 0, jnp.maximum(t - FFT_N1, 0), 0, 0)),
        out_shape=jax.ShapeDtypeStruct((B, FFT_L2, FFT_N2, FFT_SUB, 256), F32),
        scratch_shapes=[pltpu.VMEM((2, FFT_L1, FFT_N1, FFT_SUB, 256), F32)],
        compiler_params=_params(("arbitrary", "arbitrary")),
        name="fft_lat",
    )(w1, w2, twr, twi, gr.reshape(view), gi.reshape(view))
    return out.reshape(R_LAT, 256)


def _fft_ctx_body(wc_ref, gr_ref, gi_ref, o_ref):
    x = jnp.concatenate([gr_ref[...], gi_ref[...]], axis=0).astype(BF16)
    o_ref[...] = jnp.dot(wc_ref[...], x, preferred_element_type=F32)


def _fft_ctx(wc, gr, gi):
    blk0 = R_LAT // CTX
    return pl.pallas_call(
        _fft_ctx_body,
        grid=(B,),
        in_specs=[
            _const_spec((CTX, 2 * CTX)),
            pl.BlockSpec((CTX, 256), lambda b: (blk0 + b, 0)),
            pl.BlockSpec((CTX, 256), lambda b: (blk0 + b, 0)),
        ],
        out_specs=pl.BlockSpec((CTX, 256), lambda b: (b, 0)),
        out_shape=jax.ShapeDtypeStruct((R_CTX, 256), F32),
        compiler_params=_params(("arbitrary",)),
        name="fft_ctx",
    )(wc, gr, gi)


RS_CH = 8
RS_STEPS = NCH_LAT // RS_CH
N_STATE = (RS_STEPS + 1) * RS_CH
RO_CH = 8
CTX_ROWBLK = R_LAT // CTX


def _head_block_mask():
    r = lax.broadcasted_iota(jnp.int32, (256, 256), 0) // HD
    c = lax.broadcasted_iota(jnp.int32, (256, 256), 1) // HD
    return r == c


def _ret_state_body(dec_ref, kf_ref, vf_ref, kb_ref, vb_ref, kx_ref, vx_ref, sf_ref, sb_ref, accf, accb):
    t = pl.program_id(1)
    lg = _log_sigmoid(dec_ref[...])
    lgf = lg[0:1]
    lgb = lg[1:2]
    pos = lax.broadcasted_iota(jnp.int32, (CH, 1), 0).astype(F32)
    wf = jnp.exp((CH - 1.0 - pos) * lgf)
    wb = jnp.exp(pos * lgb)
    cdf = jnp.exp(float(CH) * lgf)
    cdb = jnp.exp(float(CH) * lgb)
    mask = _head_block_mask()
    dn = (((0,), (0,)), ((), ()))

    def scan(acc, out_ref, k_ref, v_ref, w, cd, order):
        kvs = {}
        for j in order:
            rows = slice(j * CH, (j + 1) * CH)
            kd = (k_ref[rows, :].astype(F32) * w).astype(BF16)
            kvs[j] = lax.dot_general(kd, v_ref[rows, :], dn, preferred_element_type=F32)
        state = acc[...]
        for j in order:
            out_ref[0, j] = state.astype(BF16)
            state = state * cd + jnp.where(mask, kvs[j], 0.0)
        acc[...] = state

    @pl.when(t == 0)
    def _():
        accf[...] = jnp.zeros_like(accf)
        accb[...] = jnp.zeros_like(accb)
        for j in range(NCH_CTX, RS_CH):
            sf_ref[0, j] = jnp.zeros((256, 256), BF16)
            sb_ref[0, j] = jnp.zeros((256, 256), BF16)
        scan(accf, sf_ref, kx_ref, vx_ref, wf, cdf, list(range(NCH_CTX)))
        scan(accb, sb_ref, kx_ref, vx_ref, wb, cdb, list(reversed(range(NCH_CTX))))

    @pl.when(t > 0)
    def _():
        scan(accf, sf_ref, kf_ref, vf_ref, wf, cdf, list(range(RS_CH)))
        scan(accb, sb_ref, kb_ref, vb_ref, wb, cdb, list(reversed(range(RS_CH))))


def _ret_state(dec, rk, rv, layer):
    rows = RS_CH * CH
    fblk = lambda t: jnp.maximum(t - 1, 0)
    bblk = lambda t: RS_STEPS - jnp.maximum(t, 1)
    fmap = lambda b, t: (b * RS_STEPS + fblk(t), 0)
    bmap = lambda b, t: (b * RS_STEPS + bblk(t), 0)
    xmap = lambda b, t: (CTX_ROWBLK + b, 0)
    return pl.pallas_call(
        _ret_state_body,
        grid=(B, RS_STEPS + 1),
        in_specs=[
            pl.BlockSpec((None, 2, 256), lambda b, t: (layer, 0, 0)),
            pl.BlockSpec((rows, 256), fmap),
            pl.BlockSpec((rows, 256), fmap),
            pl.BlockSpec((rows, 256), bmap),
            pl.BlockSpec((rows, 256), bmap),
            pl.BlockSpec((CTX, 256), xmap),
            pl.BlockSpec((CTX, 256), xmap),
        ],
        out_specs=[
            pl.BlockSpec((1, RS_CH, 256, 256), lambda b, t: (b, jnp.where(t == 0, RS_STEPS, t - 1), 0, 0)),
            pl.BlockSpec((1, RS_CH, 256, 256), lambda b, t: (b, jnp.where(t == 0, RS_STEPS, RS_STEPS - t), 0, 0)),
        ],
        out_shape=[jax.ShapeDtypeStruct((B, N_STATE, 256, 256), BF16)] * 2,
        scratch_shapes=[pltpu.VMEM((256, 256), F32), pltpu.VMEM((256, 256), F32)],
        compiler_params=_params(("arbitrary", "arbitrary")),
        name="ret_state",
    )(dec, rk, rv, rk, rv, rk, rv)


def _split_dot(x, w):
    hi = x.astype(BF16)
    lo = (x - hi.astype(F32)).astype(BF16)
    return jnp.dot(hi, w, preferred_element_type=F32) + jnp.dot(lo, w, preferred_element_type=F32)


def _ret_out_body(nck, dec_ref, gn_ref, q_ref, k_ref, v_ref, g_ref, sf_ref, sb_ref, o_ref):
    lg = _log_sigmoid(dec_ref[...])
    lgf = lg[0:1]
    lgb = lg[1:2]
    pos = lax.broadcasted_iota(jnp.int32, (CH, 1), 0).astype(F32)
    wqf = jnp.exp((pos + 1.0) * lgf)
    wqb = jnp.exp((float(CH) - pos) * lgb)
    ri = lax.broadcasted_iota(jnp.int32, (CH, CH), 0)
    ci = lax.broadcasted_iota(jnp.int32, (CH, CH), 1)
    diff = (ri - ci).astype(F32)
    decays = []
    for h in range(RET_HEADS):
        lgf_h = lgf[:, h * HD:h * HD + 1]
        lgb_h = lgb[:, h * HD:h * HD + 1]
        decays.append(jnp.where(ri >= ci, jnp.exp(jnp.maximum(diff, 0.0) * lgf_h), 0.0)
                      + jnp.where(ci >= ri, jnp.exp(jnp.maximum(-diff, 0.0) * lgb_h), 0.0))
    decay = jnp.concatenate(decays, axis=0)
    lane_head = _lane_head(256)
    avg = jnp.where(_head_block_mask(), 1.0 / HD, 0.0).astype(BF16)
    gn = gn_ref[...]
    dn = (((1,), (1,)), ((), ()))
    chunk_rows = [slice(c * CH, (c + 1) * CH) for c in range(nck)]
    cross, scores = [], []
    for c, rows in enumerate(chunk_rows):
        qb16 = q_ref[rows, :]
        q = qb16.astype(F32)
        cross.append(jnp.dot((q * wqf).astype(BF16), sf_ref[0, c], preferred_element_type=F32)
                     + jnp.dot((q * wqb).astype(BF16), sb_ref[0, c], preferred_element_type=F32))
        qs = jnp.concatenate([jnp.where(lane_head == h, qb16, jnp.zeros_like(qb16))
                              for h in range(RET_HEADS)], axis=0)
        scores.append(lax.dot_general(qs, k_ref[rows, :], dn, preferred_element_type=F32))
    outs = []
    for c, rows in enumerate(chunk_rows):
        p = (scores[c] * decay).astype(BF16)
        of = jnp.dot(p, v_ref[rows, :], preferred_element_type=F32)
        o = cross[c]
        for h in range(RET_HEADS):
            o = o + jnp.where(lane_head == h, of[h * CH:(h + 1) * CH], 0.0)
        outs.append(o)
    o = jnp.concatenate(outs, axis=0)
    mu = _split_dot(o, avg)
    cen = o - mu
    var = _split_dot(cen * cen, avg)
    on = cen * lax.rsqrt(var + EPS) * gn
    g = g_ref[...].astype(F32)
    o_ref[...] = (g * jax.nn.sigmoid(g) * on).astype(BF16)


def _ret_out(dec, gn, rq, rk, rv, rg, sf, sb, layer, ctx):
    nck = NCH_CTX if ctx else RO_CH
    rows = nck * CH
    if ctx:
        grid = (B, 1)
        rmap = lambda b, u: (CTX_ROWBLK + b, 0)
        omap = lambda b, u: (b, 0)
        smap = lambda b, u: (b, NCH_LAT // NCH_CTX, 0, 0)
        out_rows = R_CTX
    else:
        grid = (B, NCH_LAT // RO_CH)
        rmap = lambda b, u: (b * (NCH_LAT // RO_CH) + u, 0)
        omap = rmap
        smap = lambda b, u: (b, u, 0, 0)
        out_rows = R_LAT
    return pl.pallas_call(
        functools.partial(_ret_out_body, nck),
        grid=grid,
        in_specs=[
            pl.BlockSpec((None, 2, 256), lambda b, u: (layer, 0, 0)),
            pl.BlockSpec((None, 1, 256), lambda b, u: (layer, 0, 0)),
            pl.BlockSpec((rows, 256), rmap),
            pl.BlockSpec((rows, 256), rmap),
            pl.BlockSpec((rows, 256), rmap),
            pl.BlockSpec((rows, 256), rmap),
            pl.BlockSpec((1, nck, 256, 256), smap),
            pl.BlockSpec((1, nck, 256, 256), smap),
        ],
        out_specs=pl.BlockSpec((rows, 256), omap),
        out_shape=jax.ShapeDtypeStruct((out_rows, 256), BF16),
        compiler_params=_params(("arbitrary", "arbitrary")),
        name="ret_out_ctx" if ctx else "ret_out",
    )(dec, gn, rq, rk, rv, rg, sf, sb)


def _tile_head(x_all, hk):
    own = jnp.where(_lane_head(KV_W) == hk, x_all, jnp.zeros_like(x_all))
    both = own + pltpu.roll(own.astype(F32), HD, axis=1).astype(BF16)
    return jnp.concatenate([both, both], axis=1)


def _gqa_groups(groups):
    lane_head = _lane_head(256)
    dn = (((1,), (1,)), ((), ()))
    chains = []
    for gi, (q_grp, k_parts, v_parts, sinks, mask_fn) in enumerate(groups):
        for h0 in range(0, ATT_GROUP, ATT_STACK):
            chains.append((gi, tuple(range(h0, h0 + ATT_STACK)), q_grp, k_parts, v_parts, sinks, mask_fn))
    pieces = [[] for _ in groups]

    def run_wave(wave):
        scores, maxima = [], []
        for _, heads, q_grp, k_parts, _, _, mask_fn in wave:
            qs = jnp.concatenate([jnp.where(lane_head == g, q_grp, jnp.zeros_like(q_grp))
                                  for g in heads], axis=0)
            st = mask_fn(jnp.concatenate(
                [lax.dot_general(kp, qs, dn, preferred_element_type=F32) for kp in k_parts], axis=0))
            scores.append(st)
            maxima.append(jnp.max(st, axis=0, keepdims=True))
        probs, sink_terms = [], []
        for (_, heads, q_grp, _, _, sinks, _), st, st_max in zip(wave, scores, maxima):
            m = q_grp.shape[0]
            sink = jnp.concatenate([jnp.full((1, m), sinks[g] * LOG2E, F32) for g in heads], axis=1)
            mx = jnp.maximum(st_max, sink)
            probs.append(jnp.exp2(st - mx).astype(BF16))
            sink_terms.append(jnp.exp2(sink - mx))
        for (gi, _, q_grp, _, v_parts, _, _), p, sink_term in zip(wave, probs, sink_terms):
            m = q_grp.shape[0]
            acc, row = None, 0
            for vp in v_parts:
                n = vp.shape[1]
                t = jnp.dot(vp, p[row:row + n], preferred_element_type=F32)
                acc = t if acc is None else acc + t
                row += n
            ot = acc[0:HD] * (1.0 / (acc[HD:HD + 1] + sink_term))
            stacked = jnp.concatenate([ot[:, i * m:(i + 1) * m] for i in range(ATT_STACK)], axis=0)
            pieces[gi].append(stacked.T)

    per_wave = -(-len(chains) // ATT_WAVES)
    for w0 in range(0, len(chains), per_wave):
        run_wave(chains[w0:w0 + per_wave])
    return [jnp.concatenate(ps, axis=1) for ps in pieces]


def _band_mask_fn(lo_rows, lo_bias, hi_rows, hi_bias, n_keys):
    def mask_fn(st):
        parts = []
        row = 0
        while row < n_keys:
            piece = st[row:row + CH]
            if row == lo_rows:
                piece = piece + lo_bias
            elif row == hi_rows:
                piece = piece + hi_bias
            parts.append(piece)
            row += CH
        return jnp.concatenate(parts, axis=0)
    return mask_fn


def _with_ones_rows(v_t):
    return jnp.concatenate([v_t, jnp.ones((ONES_ROWS, v_t.shape[1]), BF16)], axis=0)


def _attn_lat_body(layer, sink_ref, q_ref, kp_ref, kc_ref, kn_ref, vp_ref, vc_ref, vn_ref, kx_ref, vx_ref, o_ref):
    step = pl.program_id(1)
    k_loc = jnp.concatenate([kp_ref[...], kc_ref[...], kn_ref[...]], axis=0)
    v_loc = jnp.concatenate([vp_ref[...], vc_ref[...], vn_ref[...]], axis=0)
    v_loc_t = v_loc.astype(F32).T.astype(BF16)
    v_ctx_t = vx_ref[...].astype(F32).T.astype(BF16)
    key = lax.broadcasted_iota(jnp.int32, (CH, ATT_STACK * CH), 0)
    qry = lax.broadcasted_iota(jnp.int32, (CH, ATT_STACK * CH), 1) & (CH - 1)
    below = key >= qry
    above = key <= qry
    bias = lambda ok: jnp.where(ok, 0.0, NEG_INF)
    lo_bias, lo_bias_first = bias(below), bias(below & (step > 0))
    hi_bias, hi_bias_last = bias(above), bias(above & (step < NCH_LAT // ATT_QB - 1))
    n_keys = 3 * CH + CTX
    masks = []
    for i in range(ATT_QB):
        masks.append(_band_mask_fn(0, lo_bias_first if i == 0 else lo_bias,
                                   2 * CH, hi_bias_last if i == ATT_QB - 1 else hi_bias, n_keys))
    groups = []
    for hk in range(ATT_KV):
        sinks = [sink_ref[layer, hk * ATT_GROUP + g] for g in range(ATT_GROUP)]
        kt_loc = _tile_head(k_loc, hk)
        kt_ctx = _tile_head(kx_ref[...], hk)
        vt_loc = _with_ones_rows(v_loc_t[hk * HD:(hk + 1) * HD])
        vt_ctx = _with_ones_rows(v_ctx_t[hk * HD:(hk + 1) * HD])
        cols = slice(hk * 256, (hk + 1) * 256)
        for i in range(ATT_QB):
            win = slice(i * CH, (i + 3) * CH)
            groups.append((q_ref[i * CH:(i + 1) * CH, cols], [jnp.concatenate([kt_loc[win], kt_ctx], axis=0)],
                           [jnp.concatenate([vt_loc[:, win], vt_ctx], axis=1)], sinks, masks[i]))
    outs = _gqa_groups(groups)
    for hk in range(ATT_KV):
        cols = slice(hk * 256, (hk + 1) * 256)
        for i in range(ATT_QB):
            o_ref[i * CH:(i + 1) * CH, cols] = outs[hk * ATT_QB + i].astype(BF16)


def _attn_lat(sink, aq, ak, av, layer):
    nq = NCH_LAT // ATT_QB
    qmap = lambda b, n: (b * nq + n, 0)
    pmap = lambda b, n: (b * NCH_LAT + jnp.maximum(ATT_QB * n - 1, 0), 0)
    nmap = lambda b, n: (b * NCH_LAT + jnp.minimum(ATT_QB * (n + 1), NCH_LAT - 1), 0)
    xmap = lambda b, n: (CTX_ROWBLK + b, 0)
    kv1 = lambda m: pl.BlockSpec((CH, KV_W), m)
    kv2 = pl.BlockSpec((ATT_QB * CH, KV_W), qmap)
    ctx_spec = pl.BlockSpec((CTX, KV_W), xmap)
    return pl.pallas_call(
        functools.partial(_attn_lat_body, layer),
        grid=(B, nq),
        in_specs=[
            pl.BlockSpec(memory_space=pltpu.SMEM),
            pl.BlockSpec((ATT_QB * CH, ATT_HEADS * HD), qmap),
            kv1(pmap), kv2, kv1(nmap),
            kv1(pmap), kv2, kv1(nmap),
            ctx_spec, ctx_spec,
        ],
        out_specs=pl.BlockSpec((ATT_QB * CH, ATT_HEADS * HD), qmap),
        out_shape=jax.ShapeDtypeStruct((R_LAT, ATT_HEADS * HD), BF16),
        compiler_params=_params(("arbitrary", "arbitrary")),
        name="swa_attn",
    )(sink, aq, ak, ak, ak, av, av, av, ak, av)


def _attn_ctx_body(layer, sink_ref, q_ref, kx_ref, vx_ref, o_ref):
    groups = []
    v_all_t = vx_ref[...].astype(F32).T.astype(BF16)
    for hk in range(ATT_KV):
        sinks = [sink_ref[layer, hk * ATT_GROUP + g] for g in range(ATT_GROUP)]
        groups.append((q_ref[:, hk * 256:(hk + 1) * 256], [_tile_head(kx_ref[...], hk)],
                       [_with_ones_rows(v_all_t[hk * HD:(hk + 1) * HD])], sinks, lambda s: s))
    outs = _gqa_groups(groups)
    for hk in range(ATT_KV):
        o_ref[:, hk * 256:(hk + 1) * 256] = outs[hk].astype(BF16)


def _attn_ctx(sink, aq, ak, av, layer):
    xmap = lambda b: (CTX_ROWBLK + b, 0)
    return pl.pallas_call(
        functools.partial(_attn_ctx_body, layer),
        grid=(B,),
        in_specs=[
            pl.BlockSpec(memory_space=pltpu.SMEM),
            pl.BlockSpec((CTX, ATT_HEADS * HD), xmap),
            pl.BlockSpec((CTX, KV_W), xmap),
            pl.BlockSpec((CTX, KV_W), xmap),
        ],
        out_specs=pl.BlockSpec((CTX, ATT_HEADS * HD), lambda b: (b, 0)),
        out_shape=jax.ShapeDtypeStruct((R_CTX, ATT_HEADS * HD), BF16),
        compiler_params=_params(("arbitrary",)),
        name="ctx_attn",
    )(sink, aq, ak, av)


def _mix_ffn_body(has_ctx, final, n_cast, *refs):
    refs = list(refs)
    h_ref = refs.pop(0)
    mixers = []
    for _ in range(3):
        if has_ctx:
            lat_ref, ctx_ref = refs.pop(0), refs.pop(0)
            mixers.append(_pick(lat_ref, ctx_ref))
        else:
            mixers.append(refs.pop(0)[...])
    m_ref, nw_ref, wo_ref, w_in_ref, w_out_ref = refs[:5]
    refs = refs[5:]
    if final:
        fw_ref = refs.pop(0)
    cast_src, refs = refs[:n_cast], refs[n_cast:]
    o_ref = refs.pop(0)
    _cast_blocks(cast_src, refs)
    yf, yr, ya = mixers
    m = m_ref[...]
    y = (jnp.dot(yf.astype(BF16), wo_ref[0:256, :], preferred_element_type=F32)
         + jnp.dot(yr, wo_ref[256:512, :], preferred_element_type=F32)
         + jnp.dot(ya, wo_ref[512:1024, :], preferred_element_type=F32))
    h = h_ref[...] + _mod_slice(m, 5) * y
    h = _ffn_half_step(h, m, nw_ref[...], 6, w_in_ref, w_out_ref)
    if final:
        ms = jnp.mean(h * h, axis=-1, keepdims=True)
        h = h * lax.rsqrt(ms + EPS) * fw_ref[...]
    o_ref[...] = h


def _mix_ffn(h, mixers, mod_all, norm_w, weights, cast_sources, final_w, layer):
    has_ctx = mixers[0][1] is not None
    final = final_w is not None
    nt = NT_ALL if has_ctx else NT_LAT
    cast_in, cast_out, cast_shapes = _cast_specs(cast_sources, nt)
    in_specs = [_row_spec(D)]
    args = [h]
    for (lat, ctx), width in zip(mixers, (256, 256, 512)):
        if has_ctx:
            in_specs += _lat_or_ctx_specs(width)
            args += [lat, ctx]
        else:
            in_specs += [_row_spec(width)]
            args += [lat]
    in_specs += [_mod_spec(layer), _norm_spec(layer, 2),
                 _const_spec((D, D)), _const_spec((D, 2 * D_FF)), _const_spec((D_FF, D))]
    args += [mod_all, norm_w, *weights]
    if final:
        in_specs += [_const_spec((1, D))]
        args += [final_w]
    in_specs += cast_in
    args += [s[0] for s in cast_sources]
    return pl.pallas_call(
        functools.partial(_mix_ffn_body, has_ctx, final, len(cast_sources)),
        grid=(nt,),
        in_specs=in_specs,
        out_specs=[_row_spec(D)] + cast_out,
        out_shape=[jax.ShapeDtypeStruct((nt * TM, D), F32)] + cast_shapes,
        compiler_params=_params(("arbitrary",)),
        name="mix_ffn_final" if final else "mix_ffn",
    )(*args)


def kernel(x, c, ctx, c_ctx, norm_w, w_ada, b_ada, ffn_w_in, ffn_w_out, w_in, w_o, ret_decay, ret_gn_w,
           attn_sink, final_norm_w):
    assert x.shape == (B, S, D) and ctx.shape == (B, CTX, D)
    cos_t, sin_t = _rope_tables()
    chan, w1, twr, twi, w2, wc = _fourier_tables()

    def ffn_sources(layer, j):
        return [(ffn_w_in, (layer, j), D, 2 * D_FF), (ffn_w_out, (layer, j), D_FF, D)]

    cond_t = jnp.zeros((D, 8), F32).at[:, 0:B].set(c.T).at[:, B].set(c_ctx)
    mod_all, *pre_w = _ada(cond_t, w_ada, b_ada, ffn_sources(0, 0) + [(w_in, (0,), D, D_IN)])
    norm4 = norm_w.reshape(DEPTH, 3, 1, D)
    gn3 = ret_gn_w.reshape(DEPTH, 1, 256)
    dec = jnp.repeat(ret_decay.astype(F32), HD, axis=2)
    sink = attn_sink.astype(F32)

    h = None
    out = None
    for layer in range(DEPTH):
        last = layer == DEPTH - 1
        rows_in = (x.reshape(R_LAT, D), ctx.reshape(R_CTX, D)) if layer == 0 else (h,)
        mix_sources = [(w_o, (layer,), D, D)] + ffn_sources(layer, 1)
        (h, gr, gi, rq, rk, rv, rg, aq, ak, av, *mix_w) = _ffn_inproj(
            rows_in, mod_all, norm4, pre_w, mix_sources, cos_t, sin_t, chan, layer)

        yf = _fft(w1, w2, twr, twi, gr, gi)
        sf, sb = _ret_state(dec, rk, rv, layer)
        yr = _ret_out(dec, gn3, rq, rk, rv, rg, sf, sb, layer, ctx=False)
        ya = _attn_lat(sink, aq, ak, av, layer)
        if last:
            yfc = yrc = yac = None
        else:
            yfc = _fft_ctx(wc, gr, gi)
            yrc = _ret_out(dec, gn3, rq, rk, rv, rg, sf, sb, layer, ctx=True)
            yac = _attn_ctx(sink, aq, ak, av, layer)

        next_sources = [] if last else ffn_sources(layer + 1, 0) + [(w_in, (layer + 1,), D, D_IN)]
        res, *pre_w = _mix_ffn(h, [(yf, yfc), (yr, yrc), (ya, yac)], mod_all, norm4, mix_w, next_sources,
                               final_norm_w.reshape(1, D) if last else None, layer)
        if last:
            out = res
        else:
            h = res
    return out.reshape(B, S, D)
```

```python
import functools
import math

import numpy as np
import jax
import jax.numpy as jnp
from jax import lax
from jax.experimental import pallas as pl
from jax.experimental.pallas import tpu as pltpu

F32 = jnp.float32
BF16 = jnp.bfloat16

D = 1024
B = 2
S = 8192
DEPTH = 2
GRID_W = 64
CTX = 256
HD = 64
N_MOD = 9
D_FF = 2816
D_IN = 2048
EPS = 1e-6
NEG_INF = -1e30
ROPE_BASE = 10000.0
ROPE_PAIRS = 16
RET_HEADS = 4
ATT_HEADS = 8
ATT_KV = 2
ATT_GROUP = ATT_HEADS // ATT_KV
ATT_STACK = 2
ATT_QB = 8
ATT_WAVES = 2
ONES_ROWS = 16
KV_W = ATT_KV * HD
LOG2E = math.log2(math.e)
WINDOW = 128

R_LAT = B * S
R_CTX = B * CTX
R_ALL = R_LAT + R_CTX
TM = 512
NT_LAT = R_LAT // TM
NT_ALL = R_ALL // TM
CH = 128
assert WINDOW == CH
NCH_LAT = S // CH
NCH_CTX = CTX // CH
MXU_N = 256
FF_CHUNKS = ((0, 5 * MXU_N), (5 * MXU_N, D_FF))
FFT_L1 = 64
FFT_L2 = 128
VMEM_LIMIT = 56 * 1024 * 1024


def _params(sem, vmem=VMEM_LIMIT):
    return pltpu.CompilerParams(dimension_semantics=sem, vmem_limit_bytes=vmem)


def _const_spec(shape, index=None):
    if index is None:
        index = (0,) * len(shape)
    return pl.BlockSpec(shape, lambda *_: index, pipeline_mode=pl.Buffered(1))


def _rope_tables():
    rows = S // GRID_W
    row = np.repeat(np.arange(rows, dtype=np.float64), GRID_W)
    col = np.tile(np.arange(GRID_W, dtype=np.float64), rows)
    inv = ROPE_BASE ** (-np.arange(ROPE_PAIRS, dtype=np.float64) / ROPE_PAIRS)
    ar = row[:, None] * inv
    ac = col[:, None] * inv
    cos64 = np.concatenate([np.cos(ar), np.cos(ar), np.cos(ac), np.cos(ac)], axis=1)
    sin64 = np.concatenate([-np.sin(ar), np.sin(ar), -np.sin(ac), np.sin(ac)], axis=1)
    cos = np.concatenate([cos64, cos64], axis=1)
    sin = np.concatenate([sin64, sin64], axis=1)
    cos = np.concatenate([cos, np.ones((TM, 128))], axis=0)
    sin = np.concatenate([sin, np.zeros((TM, 128))], axis=0)
    return jnp.asarray(cos, F32), jnp.asarray(sin, F32)


def _dft_cs(n):
    k = np.arange(n)
    ang = 2.0 * np.pi * ((k[:, None] * k[None, :]) % n) / n
    return np.cos(ang), np.sin(ang)


def _fourier_tables():
    c64, s64 = _dft_cs(64)
    eye4 = np.eye(4)
    chan = np.concatenate([np.kron(eye4, c64), -np.kron(eye4, s64)], axis=1)
    w1 = np.block([[c64, s64], [-s64, c64]])
    p1 = np.arange(FFT_L1)[:, None]
    l2 = np.arange(FFT_L2)[None, :]
    ang = 2.0 * np.pi * ((p1 * l2) % S) / S
    twr = np.repeat(np.cos(ang)[:, :, None], 128, axis=2)
    twi = np.repeat(np.sin(ang)[:, :, None], 128, axis=2)
    c128, s128 = _dft_cs(128)
    w2 = np.concatenate([c128, s128], axis=1) / math.sqrt(S * 64.0)
    c256, s256 = _dft_cs(CTX)
    wc = np.concatenate([c256, s256], axis=1) / math.sqrt(CTX * 64.0)
    f = lambda a: jnp.asarray(a, F32)
    return (f(chan).astype(BF16), f(w1).astype(BF16), f(twr), f(twi), f(w2).astype(BF16), f(wc).astype(BF16))


def _mod_slice(m, i):
    return m[:, i * D:(i + 1) * D]


def _rms_mod(h, nw, shift, scale):
    ms = jnp.mean(h * h, axis=-1, keepdims=True)
    y = h * lax.rsqrt(ms + EPS) * nw
    return y * (1.0 + scale) + shift


def _swiglu(u, w_in_ref, w_out_ref):
    y = None
    for lo, hi in FF_CHUNKS:
        g = jnp.dot(u, w_in_ref[:, lo:hi], preferred_element_type=F32)
        up = jnp.dot(u, w_in_ref[:, D_FF + lo:D_FF + hi], preferred_element_type=F32)
        a = (g * jax.nn.sigmoid(g) * up).astype(BF16)
        t = jnp.dot(a, w_out_ref[lo:hi, :], preferred_element_type=F32)
        y = t if y is None else y + t
    return y


def _ffn_half_step(h, m, nw, mod0, w_in_ref, w_out_ref):
    u = _rms_mod(h, nw, _mod_slice(m, mod0), _mod_slice(m, mod0 + 1)).astype(BF16)
    return h + 0.5 * _mod_slice(m, mod0 + 2) * _swiglu(u, w_in_ref, w_out_ref)


BF16_ROWS = 16


def _cast_specs(sources, n_steps, step_of=lambda i: i):
    in_specs, out_specs, out_shapes = [], [], []
    for arr, lead, rows, cols in sources:
        n_blk = min(n_steps, rows // BF16_ROWS)
        while rows % (BF16_ROWS * n_blk):
            n_blk -= 1
        rb = rows // n_blk
        blk = lambda *idx, n_blk=n_blk: jnp.minimum(step_of(*idx), n_blk - 1)
        in_specs.append(pl.BlockSpec((None,) * len(lead) + (rb, cols),
                                     lambda *idx, lead=lead, blk=blk: (*lead, blk(*idx), 0)))
        out_specs.append(pl.BlockSpec((rb, cols), lambda *idx, blk=blk: (blk(*idx), 0)))
        out_shapes.append(jax.ShapeDtypeStruct((rows, cols), BF16))
    return in_specs, out_specs, out_shapes


def _cast_blocks(src_refs, dst_refs):
    for src, dst in zip(src_refs, dst_refs):
        dst[...] = src[...].astype(BF16)


def _log_sigmoid(x):
    return jnp.minimum(x, 0.0) - jnp.log1p(jnp.exp(-jnp.abs(x)))


def _lane_head(width):
    return lax.broadcasted_iota(jnp.int32, (1, width), 1) // HD


ADA_CB = 1152


ADA_NB = (N_MOD * D) // ADA_CB


def _ada_body(n_cast, ct_ref, w_ref, b_ref, *refs):
    cast_src, o_ref, cast_dst = refs[:n_cast], refs[n_cast], refs[n_cast + 1:]
    _cast_blocks(cast_src, cast_dst)
    ct = ct_ref[...]
    s = ct * jax.nn.sigmoid(ct)
    w = w_ref[0]
    b = b_ref[0]
    for r in range(B + 1):
        o_ref[0, r] = jnp.sum(w * s[:, r:r + 1], axis=0, keepdims=True) + b


def _ada(cond_t, w_ada, b_ada, cast_sources):
    cast_in, cast_out, cast_shapes = _cast_specs(cast_sources, DEPTH * ADA_NB, lambda l, j: l * ADA_NB + j)
    return pl.pallas_call(
        functools.partial(_ada_body, len(cast_sources)),
        grid=(DEPTH, ADA_NB),
        in_specs=[
            pl.BlockSpec((D, 8), lambda l, j: (0, 0)),
            pl.BlockSpec((1, D, ADA_CB), lambda l, j: (l, 0, j)),
            pl.BlockSpec((1, 1, ADA_CB), lambda l, j: (l, 0, j)),
        ] + cast_in,
        out_specs=[pl.BlockSpec((1, B + 1, 1, ADA_CB), lambda l, j: (l, 0, 0, j))] + cast_out,
        out_shape=[jax.ShapeDtypeStruct((DEPTH, B + 1, 1, N_MOD * D), F32)] + cast_shapes,
        compiler_params=_params(("arbitrary", "arbitrary")),
        name="ada_mod",
    )(cond_t, w_ada, b_ada.reshape(DEPTH, 1, N_MOD * D), *[s[0] for s in cast_sources])


def _mod_spec(layer):
    return pl.BlockSpec((None, None, 1, N_MOD * D),
                        lambda i: (layer, jnp.minimum(i // (NT_LAT // B), B), 0, 0))


def _norm_spec(layer, k):
    return _const_spec((None, None, 1, D), (layer, k, 0, 0))


def _row_spec(width):
    return pl.BlockSpec((TM, width), lambda i: (i, 0))


def _lat_or_ctx_specs(width):
    return [pl.BlockSpec((TM, width), lambda i: (jnp.minimum(i, NT_LAT - 1), 0)),
            pl.BlockSpec((TM, width), lambda i: (0, 0))]


def _pick(lat_ref, ctx_ref):
    return jnp.where(pl.program_id(0) < NT_LAT, lat_ref[...], ctx_ref[...])


def _rope(x, cos, sin, lane_first_half):
    outs = []
    for j in range(x.shape[1] // 128):
        xb = x[:, j * 128:(j + 1) * 128]
        fwd = pltpu.roll(xb, 112, axis=1)
        bwd = pltpu.roll(xb, 16, axis=1)
        partner = jnp.where(lane_first_half, fwd, bwd)
        outs.append(xb * cos + partner * sin)
    return outs[0] if len(outs) == 1 else jnp.concatenate(outs, axis=1)


N_CAST = 3


def _ffn_inproj_body(first, *refs):
    refs = list(refs)
    if first:
        h = _pick(refs.pop(0), refs.pop(0))
    else:
        h = refs.pop(0)[...]
    (m_ref, nw0_ref, nw1_ref, fw_in_ref, fw_out_ref, w_ref, cos_ref, sin_ref, chan_ref) = refs[:9]
    cast_src = refs[9:9 + N_CAST]
    (h_out_ref, gr_ref, gi_ref, rq_ref, rk_ref, rv_ref, rg_ref, aq_ref, ak_ref, av_ref) = refs[9 + N_CAST:19 + N_CAST]
    _cast_blocks(cast_src, refs[19 + N_CAST:])
    m = m_ref[...]
    h = _ffn_half_step(h, m, nw0_ref[...], 0, fw_in_ref, fw_out_ref)
    h_out_ref[...] = h
    lane = lax.broadcasted_iota(jnp.int32, (1, 128), 1)
    first_half = (lane // ROPE_PAIRS) % 2 == 0
    halves = [slice(r0, r0 + TM // 2) for r0 in (0, TM // 2)]
    us = [_rms_mod(h[r], nw1_ref[...], _mod_slice(m, 3), _mod_slice(m, 4)).astype(BF16) for r in halves]
    zs = [jnp.dot(u, w_ref[...], preferred_element_type=F32) for u in us]
    for r, z in zip(halves, zs):
        cos = cos_ref[r, :]
        sin = sin_ref[r, :]
        g = jnp.dot(z[:, 0:256].astype(BF16), chan_ref[...], preferred_element_type=F32)
        gr_ref[r, :] = g[:, 0:256]
        gi_ref[r, :] = g[:, 256:512]
        rq_ref[r, :] = _rope(z[:, 256:512], cos, sin, first_half).astype(BF16)
        rk_ref[r, :] = (_rope(z[:, 512:768], cos, sin, first_half) * (HD ** -0.5)).astype(BF16)
        rv_ref[r, :] = z[:, 768:1024].astype(BF16)
        rg_ref[r, :] = z[:, 1024:1280].astype(BF16)
        aq_ref[r, :] = (_rope(z[:, 1280:1792], cos, sin, first_half) * (HD ** -0.5 * LOG2E)).astype(BF16)
        ak_ref[r, :] = _rope(z[:, 1792:1920], cos, sin, first_half).astype(BF16)
        av_ref[r, :] = z[:, 1920:2048].astype(BF16)


def _rope_block(i):
    return jnp.where(i < NT_LAT, i % (S // TM), S // TM)


def _ffn_inproj(rows_in, mod_all, norm_w, weights, cast_sources, cos_t, sin_t, chan, layer):
    first = len(rows_in) == 2
    widths = (256, 256, 256, 256, 256, 256, 512, KV_W, KV_W)
    dtypes = [F32, F32] + [BF16] * (len(widths) - 2)
    cast_in, cast_out, cast_shapes = _cast_specs(cast_sources, NT_ALL)
    return pl.pallas_call(
        functools.partial(_ffn_inproj_body, first),
        grid=(NT_ALL,),
        in_specs=(_lat_or_ctx_specs(D) if first else [_row_spec(D)]) + [
            _mod_spec(layer),
            _norm_spec(layer, 0),
            _norm_spec(layer, 1),
            _const_spec((D, 2 * D_FF)), _const_spec((D_FF, D)), _const_spec((D, D_IN)),
            pl.BlockSpec((TM, 128), lambda i: (_rope_block(i), 0)),
            pl.BlockSpec((TM, 128), lambda i: (_rope_block(i), 0)),
            _const_spec((256, 512)),
        ] + cast_in,
        out_specs=[_row_spec(D)] + [_row_spec(w) for w in widths] + cast_out,
        out_shape=[jax.ShapeDtypeStruct((R_ALL, D), F32)]
        + [jax.ShapeDtypeStruct((R_ALL, w), dt) for w, dt in zip(widths, dtypes)] + cast_shapes,
        compiler_params=_params(("arbitrary",)),
        name="ffn_inproj_first" if first else "ffn_inproj",
    )(*rows_in, mod_all, norm_w, norm_w, *weights, cos_t, sin_t, chan, *[s[0] for s in cast_sources])


FFT_SUB = 16


FFT_N1 = FFT_L2 // FFT_SUB
FFT_N2 = FFT_L1 // FFT_SUB


def _fft_body(w1_ref, w2_ref, twr_ref, twi_ref, xr_ref, xi_ref, o_ref, z_ref):
    t = pl.program_id(1)

    @pl.when(t < FFT_N1)
    def _():
        w1 = w1_ref[...]
        for s in range(FFT_SUB):
            x = jnp.concatenate([xr_ref[:, s, :], xi_ref[:, s, :]], axis=0).astype(BF16)
            z = jnp.dot(w1, x, preferred_element_type=F32)
            z_ref[0, :, t, s, :] = z[0:FFT_L1]
            z_ref[1, :, t, s, :] = z[FFT_L1:2 * FFT_L1]

    @pl.when(t >= FFT_N1)
    def _():
        w2 = w2_ref[...]
        p0 = (t - FFT_N1) * FFT_SUB
        for p in range(FFT_SUB):
            zr = z_ref[0, p0 + p].reshape(FFT_L2, 256)
            zi = z_ref[1, p0 + p].reshape(FFT_L2, 256)
            tr = jnp.concatenate([twr_ref[p]] * 2, axis=1)
            ti = jnp.concatenate([twi_ref[p]] * 2, axis=1)
            yr = tr * zr + ti * zi
            yi = tr * zi - ti * zr
            y = jnp.concatenate([yr, yi], axis=0).astype(BF16)
            o_ref[:, p, :] = jnp.dot(w2, y, preferred_element_type=F32)


def _fft(w1, w2, twr, twi, gr, gi):
    view = (R_ALL // FFT_L2, FFT_N1, FFT_SUB, 256)
    x_spec = pl.BlockSpec((FFT_L1, None, FFT_SUB, 256), lambda b, t: (b, jnp.minimum(t, FFT_N1 - 1), 0, 0))
    tw_spec = pl.BlockSpec((FFT_SUB, FFT_L2, 128), lambda b, t: (jnp.maximum(t - FFT_N1, 0), 0, 0))
    out = pl.pallas_call(
        _fft_body,
        grid=(B, FFT_N1 + FFT_N2),
        in_specs=[_const_spec((128, 128)), _const_spec((128, 256)), tw_spec, tw_spec, x_spec, x_spec],
        out_specs=pl.BlockSpec((None, FFT_L2, None, FFT_SUB, 256),
                               lambda b, t: (b, 0, jnp.maximum(t - FFT_N1, 0), 0, 0)),
        out_shape=jax.ShapeDtypeStruct((B, FFT_L2, FFT_N2, FFT_SUB, 256), F32),
        scratch_shapes=[pltpu.VMEM((2, FFT_L1, FFT_N1, FFT_SUB, 256), F32)],
        compiler_params=_params(("arbitrary", "arbitrary")),
        name="fft_lat",
    )(w1, w2, twr, twi, gr.reshape(view), gi.reshape(view))
    return out.reshape(R_LAT, 256)


def _fft_ctx_body(wc_ref, gr_ref, gi_ref, o_ref):
    x = jnp.concatenate([gr_ref[...], gi_ref[...]], axis=0).astype(BF16)
    o_ref[...] = jnp.dot(wc_ref[...], x, preferred_element_type=F32)


def _fft_ctx(wc, gr, gi):
    blk0 = R_LAT // CTX
    return pl.pallas_call(
        _fft_ctx_body,
        grid=(B,),
        in_specs=[
            _const_spec((CTX, 2 * CTX)),
            pl.BlockSpec((CTX, 256), lambda b: (blk0 + b, 0)),
            pl.BlockSpec((CTX, 256), lambda b: (blk0 + b, 0)),
        ],
        out_specs=pl.BlockSpec((CTX, 256), lambda b: (b, 0)),
        out_shape=jax.ShapeDtypeStruct((R_CTX, 256), F32),
        compiler_params=_params(("arbitrary",)),
        name="fft_ctx",
    )(wc, gr, gi)


RS_CH = 8
RS_STEPS = NCH_LAT // RS_CH
N_STATE = (RS_STEPS + 1) * RS_CH
RO_CH = 8
CTX_ROWBLK = R_LAT // CTX


def _head_block_mask():
    r = lax.broadcasted_iota(jnp.int32, (256, 256), 0) // HD
    c = lax.broadcasted_iota(jnp.int32, (256, 256), 1) // HD
    return r == c


def _ret_scan_step(t, lgf, lgb, kf_ref, vf_ref, kb_ref, vb_ref, kx_ref, vx_ref, sf_scr, sb_scr, accf, accb):
    pos = lax.broadcasted_iota(jnp.int32, (CH, 1), 0).astype(F32)
    wf = jnp.exp((CH - 1.0 - pos) * lgf)
    wb = jnp.exp(pos * lgb)
    cdf = jnp.exp(float(CH) * lgf)
    cdb = jnp.exp(float(CH) * lgb)
    mask = _head_block_mask()
    dn = (((0,), (0,)), ((), ()))

    def scan(acc, scr, slot0, k_ref, v_ref, w, cd, order):
        kvs = {}
        for j in order:
            rows = slice(j * CH, (j + 1) * CH)
            kd = (k_ref[rows, :].astype(F32) * w).astype(BF16)
            kvs[j] = lax.dot_general(kd, v_ref[rows, :], dn, preferred_element_type=F32)
        state = acc[...]
        for j in order:
            scr[slot0 + j] = state.astype(BF16)
            state = state * cd + jnp.where(mask, kvs[j], 0.0)
        acc[...] = state

    @pl.when(t == 0)
    def _():
        accf[...] = jnp.zeros_like(accf)
        accb[...] = jnp.zeros_like(accb)
        for j in range(NCH_CTX, RS_CH):
            sf_scr[NCH_LAT + j] = jnp.zeros((256, 256), BF16)
            sb_scr[NCH_LAT + j] = jnp.zeros((256, 256), BF16)
        scan(accf, sf_scr, NCH_LAT, kx_ref, vx_ref, wf, cdf, list(range(NCH_CTX)))
        scan(accb, sb_scr, NCH_LAT, kx_ref, vx_ref, wb, cdb, list(reversed(range(NCH_CTX))))

    @pl.when(t > 0)
    def _():
        scan(accf, sf_scr, (t - 1) * RS_CH, kf_ref, vf_ref, wf, cdf, list(range(RS_CH)))
        scan(accb, sb_scr, (RS_STEPS - t) * RS_CH, kb_ref, vb_ref, wb, cdb, list(reversed(range(RS_CH))))


def _ret_lat_body(dec_ref, gn_ref, kf_ref, vf_ref, kb_ref, vb_ref, kx_ref, vx_ref,
                  q_ref, k_ref, v_ref, g_ref, o_ref, sfc_ref, sbc_ref, sf_scr, sb_scr, accf, accb):
    t = pl.program_id(1)
    lg = _log_sigmoid(dec_ref[...])
    lgf = lg[0:1]
    lgb = lg[1:2]

    @pl.when(t <= RS_STEPS)
    def _():
        _ret_scan_step(t, lgf, lgb, kf_ref, vf_ref, kb_ref, vb_ref, kx_ref, vx_ref, sf_scr, sb_scr, accf, accb)

    @pl.when(t == 0)
    def _():
        for j in range(RS_CH):
            sfc_ref[0, j] = sf_scr[NCH_LAT + j]
            sbc_ref[0, j] = sb_scr[NCH_LAT + j]

    @pl.when(t > RS_STEPS)
    def _():
        slot0 = (t - RS_STEPS - 1) * RO_CH
        _ret_out_core(RO_CH, lgf, lgb, gn_ref, q_ref, k_ref, v_ref, g_ref,
                      lambda c: (sf_scr[slot0 + c], sb_scr[slot0 + c]), o_ref)


def _ret_lat(dec, gn, rq, rk, rv, rg, layer):
    rows = RS_CH * CH
    n1 = RS_STEPS + 1
    n2 = NCH_LAT // RO_CH
    assert RO_CH == RS_CH
    fmap = lambda b, t: (b * RS_STEPS + jnp.clip(t - 1, 0, RS_STEPS - 1), 0)
    bmap = lambda b, t: (b * RS_STEPS + RS_STEPS - jnp.clip(t, 1, RS_STEPS), 0)
    xmap = lambda b, t: (CTX_ROWBLK + b, 0)
    omap = lambda b, t: (b * n2 + jnp.maximum(t - n1, 0), 0)
    blk = lambda m: pl.BlockSpec((rows, 256), m)
    ctx_state = pl.BlockSpec((1, RS_CH, 256, 256), lambda b, t: (b, 0, 0, 0))
    return pl.pallas_call(
        _ret_lat_body,
        grid=(B, n1 + n2),
        in_specs=[
            pl.BlockSpec((None, 2, 256), lambda b, t: (layer, 0, 0)),
            pl.BlockSpec((None, 1, 256), lambda b, t: (layer, 0, 0)),
            blk(fmap), blk(fmap), blk(bmap), blk(bmap),
            pl.BlockSpec((CTX, 256), xmap), pl.BlockSpec((CTX, 256), xmap),
            blk(omap), blk(omap), blk(omap), blk(omap),
        ],
        out_specs=[blk(omap), ctx_state, ctx_state],
        out_shape=[jax.ShapeDtypeStruct((R_LAT, 256), BF16)]
        + [jax.ShapeDtypeStruct((B, RS_CH, 256, 256), BF16)] * 2,
        scratch_shapes=[pltpu.VMEM((N_STATE, 256, 256), BF16), pltpu.VMEM((N_STATE, 256, 256), BF16),
                        pltpu.VMEM((256, 256), F32), pltpu.VMEM((256, 256), F32)],
        compiler_params=_params(("arbitrary", "arbitrary")),
        name="ret_lat",
    )(dec, gn, rk, rv, rk, rv, rk, rv, rq, rk, rv, rg)


def _split_dot(x, w):
    hi = x.astype(BF16)
    lo = (x - hi.astype(F32)).astype(BF16)
    return jnp.dot(hi, w, preferred_element_type=F32) + jnp.dot(lo, w, preferred_element_type=F32)


def _ret_out_body(nck, dec_ref, gn_ref, q_ref, k_ref, v_ref, g_ref, sf_ref, sb_ref, o_ref):
    lg = _log_sigmoid(dec_ref[...])
    _ret_out_core(nck, lg[0:1], lg[1:2], gn_ref, q_ref, k_ref, v_ref, g_ref,
                  lambda c: (sf_ref[0, c], sb_ref[0, c]), o_ref)


def _ret_out_core(nck, lgf, lgb, gn_ref, q_ref, k_ref, v_ref, g_ref, states, o_ref):
    pos = lax.broadcasted_iota(jnp.int32, (CH, 1), 0).astype(F32)
    wqf = jnp.exp((pos + 1.0) * lgf)
    wqb = jnp.exp((float(CH) - pos) * lgb)
    ri = lax.broadcasted_iota(jnp.int32, (CH, CH), 0)
    ci = lax.broadcasted_iota(jnp.int32, (CH, CH), 1)
    diff = (ri - ci).astype(F32)
    decays = []
    for h in range(RET_HEADS):
        lgf_h = lgf[:, h * HD:h * HD + 1]
        lgb_h = lgb[:, h * HD:h * HD + 1]
        decays.append(jnp.where(ri >= ci, jnp.exp(jnp.maximum(diff, 0.0) * lgf_h), 0.0)
                      + jnp.where(ci >= ri, jnp.exp(jnp.maximum(-diff, 0.0) * lgb_h), 0.0))
    decay = jnp.concatenate(decays, axis=0)
    lane_head = _lane_head(256)
    avg = jnp.where(_head_block_mask(), 1.0 / HD, 0.0).astype(BF16)
    gn = gn_ref[...]
    dn = (((1,), (1,)), ((), ()))
    chunk_rows = [slice(c * CH, (c + 1) * CH) for c in range(nck)]
    cross, scores = [], []
    for c, rows in enumerate(chunk_rows):
        qb16 = q_ref[rows, :]
        q = qb16.astype(F32)
        sf_c, sb_c = states(c)
        cross.append(jnp.dot((q * wqf).astype(BF16), sf_c, preferred_element_type=F32)
                     + jnp.dot((q * wqb).astype(BF16), sb_c, preferred_element_type=F32))
        qs = jnp.concatenate([jnp.where(lane_head == h, qb16, jnp.zeros_like(qb16))
                              for h in range(RET_HEADS)], axis=0)
        scores.append(lax.dot_general(qs, k_ref[rows, :], dn, preferred_element_type=F32))
    outs = []
    for c, rows in enumerate(chunk_rows):
        p = (scores[c] * decay).astype(BF16)
        of = jnp.dot(p, v_ref[rows, :], preferred_element_type=F32)
        o = cross[c]
        for h in range(RET_HEADS):
            o = o + jnp.where(lane_head == h, of[h * CH:(h + 1) * CH], 0.0)
        outs.append(o)
    o = jnp.concatenate(outs, axis=0)
    mu = _split_dot(o, avg)
    cen = o - mu
    var = _split_dot(cen * cen, avg)
    on = cen * lax.rsqrt(var + EPS) * gn
    g = g_ref[...].astype(F32)
    o_ref[...] = (g * jax.nn.sigmoid(g) * on).astype(BF16)


def _ret_out_ctx(dec, gn, rq, rk, rv, rg, sfc, sbc, layer):
    rmap = lambda b: (CTX_ROWBLK + b, 0)
    smap = lambda b: (b, 0, 0, 0)
    return pl.pallas_call(
        functools.partial(_ret_out_body, NCH_CTX),
        grid=(B,),
        in_specs=[
            pl.BlockSpec((None, 2, 256), lambda b: (layer, 0, 0)),
            pl.BlockSpec((None, 1, 256), lambda b: (layer, 0, 0)),
            pl.BlockSpec((CTX, 256), rmap),
            pl.BlockSpec((CTX, 256), rmap),
            pl.BlockSpec((CTX, 256), rmap),
            pl.BlockSpec((CTX, 256), rmap),
            pl.BlockSpec((1, NCH_CTX, 256, 256), smap),
            pl.BlockSpec((1, NCH_CTX, 256, 256), smap),
        ],
        out_specs=pl.BlockSpec((CTX, 256), lambda b: (b, 0)),
        out_shape=jax.ShapeDtypeStruct((R_CTX, 256), BF16),
        compiler_params=_params(("arbitrary",)),
        name="ret_out_ctx",
    )(dec, gn, rq, rk, rv, rg, sfc, sbc)


def _tile_head(x_all, hk):
    own = jnp.where(_lane_head(KV_W) == hk, x_all, jnp.zeros_like(x_all))
    both = own + pltpu.roll(own.astype(F32), HD, axis=1).astype(BF16)
    return jnp.concatenate([both, both], axis=1)


def _gqa_groups(groups):
    lane_head = _lane_head(256)
    dn = (((1,), (1,)), ((), ()))
    chains = []
    for gi, (q_grp, k_parts, v_parts, sinks, mask_fn) in enumerate(groups):
        for h0 in range(0, ATT_GROUP, ATT_STACK):
            chains.append((gi, tuple(range(h0, h0 + ATT_STACK)), q_grp, k_parts, v_parts, sinks, mask_fn))
    pieces = [[] for _ in groups]

    def run_wave(wave):
        scores, maxima = [], []
        for _, heads, q_grp, k_parts, _, _, mask_fn in wave:
            qs = jnp.concatenate([jnp.where(lane_head == g, q_grp, jnp.zeros_like(q_grp))
                                  for g in heads], axis=0)
            st = mask_fn(jnp.concatenate(
                [lax.dot_general(kp, qs, dn, preferred_element_type=F32) for kp in k_parts], axis=0))
            scores.append(st)
            maxima.append(jnp.max(st, axis=0, keepdims=True))
        probs, sink_terms = [], []
        for (_, heads, q_grp, _, _, sinks, _), st, st_max in zip(wave, scores, maxima):
            m = q_grp.shape[0]
            sink = jnp.concatenate([jnp.full((1, m), sinks[g] * LOG2E, F32) for g in heads], axis=1)
            mx = jnp.maximum(st_max, sink)
            probs.append(jnp.exp2(st - mx).astype(BF16))
            sink_terms.append(jnp.exp2(sink - mx))
        for (gi, _, q_grp, _, v_parts, _, _), p, sink_term in zip(wave, probs, sink_terms):
            m = q_grp.shape[0]
            acc, row = None, 0
            for vp in v_parts:
                n = vp.shape[1]
                t = jnp.dot(vp, p[row:row + n], preferred_element_type=F32)
                acc = t if acc is None else acc + t
                row += n
            ot = acc[0:HD] * (1.0 / (acc[HD:HD + 1] + sink_term))
            stacked = jnp.concatenate([ot[:, i * m:(i + 1) * m] for i in range(ATT_STACK)], axis=0)
            pieces[gi].append(stacked.T)

    per_wave = -(-len(chains) // ATT_WAVES)
    for w0 in range(0, len(chains), per_wave):
        run_wave(chains[w0:w0 + per_wave])
    return [jnp.concatenate(ps, axis=1) for ps in pieces]


def _band_mask_fn(lo_rows, lo_cap, hi_rows, hi_cap, n_keys):
    def mask_fn(st):
        parts = []
        row = 0
        while row < n_keys:
            piece = st[row:row + CH]
            if row == lo_rows:
                piece = jnp.minimum(piece, lo_cap)
            elif row == hi_rows:
                piece = jnp.minimum(piece, hi_cap)
            parts.append(piece)
            row += CH
        return jnp.concatenate(parts, axis=0)
    return mask_fn


def _with_ones_rows(v_t):
    return jnp.concatenate([v_t, jnp.ones((ONES_ROWS, v_t.shape[1]), BF16)], axis=0)


def _attn_lat_body(layer, sink_ref, q_ref, kp_ref, kc_ref, kn_ref, vp_ref, vc_ref, vn_ref, kx_ref, vx_ref, o_ref):
    step = pl.program_id(1)
    k_loc = jnp.concatenate([kp_ref[...], kc_ref[...], kn_ref[...]], axis=0)
    v_loc = jnp.concatenate([vp_ref[...], vc_ref[...], vn_ref[...]], axis=0)
    v_loc_t = v_loc.astype(F32).T.astype(BF16)
    v_ctx_t = vx_ref[...].astype(F32).T.astype(BF16)
    key = lax.broadcasted_iota(jnp.int32, (CH, ATT_STACK * CH), 0)
    qry = lax.broadcasted_iota(jnp.int32, (CH, ATT_STACK * CH), 1) & (CH - 1)
    below = key >= qry
    above = key <= qry
    cap = lambda ok: jnp.where(ok, jnp.inf, NEG_INF)
    lo_cap, lo_cap_first = cap(below), cap(below & (step > 0))
    hi_cap, hi_cap_last = cap(above), cap(above & (step < NCH_LAT // ATT_QB - 1))
    n_keys = 3 * CH + CTX
    masks = []
    for i in range(ATT_QB):
        masks.append(_band_mask_fn(0, lo_cap_first if i == 0 else lo_cap,
                                   2 * CH, hi_cap_last if i == ATT_QB - 1 else hi_cap, n_keys))
    groups = []
    for hk in range(ATT_KV):
        sinks = [sink_ref[layer, hk * ATT_GROUP + g] for g in range(ATT_GROUP)]
        kt_loc = _tile_head(k_loc, hk)
        kt_ctx = _tile_head(kx_ref[...], hk)
        vt_loc = _with_ones_rows(v_loc_t[hk * HD:(hk + 1) * HD])
        vt_ctx = _with_ones_rows(v_ctx_t[hk * HD:(hk + 1) * HD])
        cols = slice(hk * 256, (hk + 1) * 256)
        for i in range(ATT_QB):
            win = slice(i * CH, (i + 3) * CH)
            groups.append((q_ref[i * CH:(i + 1) * CH, cols], [jnp.concatenate([kt_loc[win], kt_ctx], axis=0)],
                           [jnp.concatenate([vt_loc[:, win], vt_ctx], axis=1)], sinks, masks[i]))
    outs = _gqa_groups(groups)
    for hk in range(ATT_KV):
        cols = slice(hk * 256, (hk + 1) * 256)
        for i in range(ATT_QB):
            o_ref[i * CH:(i + 1) * CH, cols] = outs[hk * ATT_QB + i].astype(BF16)


def _attn_lat(sink, aq, ak, av, layer):
    nq = NCH_LAT // ATT_QB
    qmap = lambda b, n: (b * nq + n, 0)
    pmap = lambda b, n: (b * NCH_LAT + jnp.maximum(ATT_QB * n - 1, 0), 0)
    nmap = lambda b, n: (b * NCH_LAT + jnp.minimum(ATT_QB * (n + 1), NCH_LAT - 1), 0)
    xmap = lambda b, n: (CTX_ROWBLK + b, 0)
    kv1 = lambda m: pl.BlockSpec((CH, KV_W), m)
    kv2 = pl.BlockSpec((ATT_QB * CH, KV_W), qmap)
    ctx_spec = pl.BlockSpec((CTX, KV_W), xmap)
    return pl.pallas_call(
        functools.partial(_attn_lat_body, layer),
        grid=(B, nq),
        in_specs=[
            pl.BlockSpec(memory_space=pltpu.SMEM),
            pl.BlockSpec((ATT_QB * CH, ATT_HEADS * HD), qmap),
            kv1(pmap), kv2, kv1(nmap),
            kv1(pmap), kv2, kv1(nmap),
            ctx_spec, ctx_spec,
        ],
        out_specs=pl.BlockSpec((ATT_QB * CH, ATT_HEADS * HD), qmap),
        out_shape=jax.ShapeDtypeStruct((R_LAT, ATT_HEADS * HD), BF16),
        compiler_params=_params(("arbitrary", "arbitrary")),
        name="swa_attn",
    )(sink, aq, ak, ak, ak, av, av, av, ak, av)


def _attn_ctx_body(layer, sink_ref, q_ref, kx_ref, vx_ref, o_ref):
    groups = []
    v_all_t = vx_ref[...].astype(F32).T.astype(BF16)
    for hk in range(ATT_KV):
        sinks = [sink_ref[layer, hk * ATT_GROUP + g] for g in range(ATT_GROUP)]
        groups.append((q_ref[:, hk * 256:(hk + 1) * 256], [_tile_head(kx_ref[...], hk)],
                       [_with_ones_rows(v_all_t[hk * HD:(hk + 1) * HD])], sinks, lambda s: s))
    outs = _gqa_groups(groups)
    for hk in range(ATT_KV):
        o_ref[:, hk * 256:(hk + 1) * 256] = outs[hk].astype(BF16)


def _attn_ctx(sink, aq, ak, av, layer):
    xmap = lambda b: (CTX_ROWBLK + b, 0)
    return pl.pallas_call(
        functools.partial(_attn_ctx_body, layer),
        grid=(B,),
        in_specs=[
            pl.BlockSpec(memory_space=pltpu.SMEM),
            pl.BlockSpec((CTX, ATT_HEADS * HD), xmap),
            pl.BlockSpec((CTX, KV_W), xmap),
            pl.BlockSpec((CTX, KV_W), xmap),
        ],
        out_specs=pl.BlockSpec((CTX, ATT_HEADS * HD), lambda b: (b, 0)),
        out_shape=jax.ShapeDtypeStruct((R_CTX, ATT_HEADS * HD), BF16),
        compiler_params=_params(("arbitrary",)),
        name="ctx_attn",
    )(sink, aq, ak, av)


def _mix_ffn_body(has_ctx, final, n_cast, *refs):
    refs = list(refs)
    h_ref = refs.pop(0)
    mixers = []
    for _ in range(3):
        if has_ctx:
            lat_ref, ctx_ref = refs.pop(0), refs.pop(0)
            mixers.append(_pick(lat_ref, ctx_ref))
        else:
            mixers.append(refs.pop(0)[...])
    m_ref, nw_ref, wo_ref, w_in_ref, w_out_ref = refs[:5]
    refs = refs[5:]
    if final:
        fw_ref = refs.pop(0)
    cast_src, refs = refs[:n_cast], refs[n_cast:]
    o_ref = refs.pop(0)
    _cast_blocks(cast_src, refs)
    yf, yr, ya = mixers
    m = m_ref[...]
    y = (jnp.dot(yf.astype(BF16), wo_ref[0:256, :], preferred_element_type=F32)
         + jnp.dot(yr, wo_ref[256:512, :], preferred_element_type=F32)
         + jnp.dot(ya, wo_ref[512:1024, :], preferred_element_type=F32))
    h = h_ref[...] + _mod_slice(m, 5) * y
    h = _ffn_half_step(h, m, nw_ref[...], 6, w_in_ref, w_out_ref)
    if final:
        ms = jnp.mean(h * h, axis=-1, keepdims=True)
        h = h * lax.rsqrt(ms + EPS) * fw_ref[...]
    o_ref[...] = h


def _mix_ffn(h, mixers, mod_all, norm_w, weights, cast_sources, final_w, layer):
    has_ctx = mixers[0][1] is not None
    final = final_w is not None
    nt = NT_ALL if has_ctx else NT_LAT
    cast_in, cast_out, cast_shapes = _cast_specs(cast_sources, nt)
    in_specs = [_row_spec(D)]
    args = [h]
    for (lat, ctx), width in zip(mixers, (256, 256, 512)):
        if has_ctx:
            in_specs += _lat_or_ctx_specs(width)
            args += [lat, ctx]
        else:
            in_specs += [_row_spec(width)]
            args += [lat]
    in_specs += [_mod_spec(layer), _norm_spec(layer, 2),
                 _const_spec((D, D)), _const_spec((D, 2 * D_FF)), _const_spec((D_FF, D))]
    args += [mod_all, norm_w, *weights]
    if final:
        in_specs += [_const_spec((1, D))]
        args += [final_w]
    in_specs += cast_in
    args += [s[0] for s in cast_sources]
    return pl.pallas_call(
        functools.partial(_mix_ffn_body, has_ctx, final, len(cast_sources)),
        grid=(nt,),
        in_specs=in_specs,
        out_specs=[_row_spec(D)] + cast_out,
        out_shape=[jax.ShapeDtypeStruct((nt * TM, D), F32)] + cast_shapes,
        compiler_params=_params(("arbitrary",)),
        name="mix_ffn_final" if final else "mix_ffn",
    )(*args)


def kernel(x, c, ctx, c_ctx, norm_w, w_ada, b_ada, ffn_w_in, ffn_w_out, w_in, w_o, ret_decay, ret_gn_w,
           attn_sink, final_norm_w):
    assert x.shape == (B, S, D) and ctx.shape == (B, CTX, D)
    cos_t, sin_t = _rope_tables()
    chan, w1, twr, twi, w2, wc = _fourier_tables()

    def ffn_sources(layer, j):
        return [(ffn_w_in, (layer, j), D, 2 * D_FF), (ffn_w_out, (layer, j), D_FF, D)]

    cond_t = jnp.zeros((D, 8), F32).at[:, 0:B].set(c.T).at[:, B].set(c_ctx)
    mod_all, *pre_w = _ada(cond_t, w_ada, b_ada, ffn_sources(0, 0) + [(w_in, (0,), D, D_IN)])
    norm4 = norm_w.reshape(DEPTH, 3, 1, D)
    gn3 = ret_gn_w.reshape(DEPTH, 1, 256)
    dec = jnp.repeat(ret_decay.astype(F32), HD, axis=2)
    sink = attn_sink.astype(F32)

    h = None
    out = None
    for layer in range(DEPTH):
        last = layer == DEPTH - 1
        rows_in = (x.reshape(R_LAT, D), ctx.reshape(R_CTX, D)) if layer == 0 else (h,)
        mix_sources = [(w_o, (layer,), D, D)] + ffn_sources(layer, 1)
        (h, gr, gi, rq, rk, rv, rg, aq, ak, av, *mix_w) = _ffn_inproj(
            rows_in, mod_all, norm4, pre_w, mix_sources, cos_t, sin_t, chan, layer)

        yf = _fft(w1, w2, twr, twi, gr, gi)
        yr, sfc, sbc = _ret_lat(dec, gn3, rq, rk, rv, rg, layer)
        ya = _attn_lat(sink, aq, ak, av, layer)
        if last:
            yfc = yrc = yac = None
        else:
            yfc = _fft_ctx(wc, gr, gi)
            yrc = _ret_out_ctx(dec, gn3, rq, rk, rv, rg, sfc, sbc, layer)
            yac = _attn_ctx(sink, aq, ak, av, layer)

        next_sources = [] if last else ffn_sources(layer + 1, 0) + [(w_in, (layer + 1,), D, D_IN)]
        res, *pre_w = _mix_ffn(h, [(yf, yfc), (yr, yrc), (ya, yac)], mod_all, norm4, mix_w, next_sources,
                               final_norm_w.reshape(1, D) if last else None, layer)
        if last:
            out = res
        else:
            h = res
    return out.reshape(B, S, D)
```

```python
import functools
import math

import numpy as np
import jax
import jax.numpy as jnp
from jax import lax
from jax.experimental import pallas as pl
from jax.experimental.pallas import tpu as pltpu

F32 = jnp.float32
BF16 = jnp.bfloat16

D = 1024
B = 2
S = 8192
DEPTH = 2
GRID_W = 64
CTX = 256
HD = 64
N_MOD = 9
D_FF = 2816
D_IN = 2048
EPS = 1e-6
NEG_INF = -1e30
ROPE_BASE = 10000.0
ROPE_PAIRS = 16
RET_HEADS = 4
ATT_HEADS = 8
ATT_KV = 2
ATT_GROUP = ATT_HEADS // ATT_KV
ATT_STACK = 2
ATT_QB = 16
ATT_WAVES = 4
ONES_ROWS = 16
KV_W = ATT_KV * HD
LOG2E = math.log2(math.e)
WINDOW = 128

R_LAT = B * S
R_CTX = B * CTX
R_ALL = R_LAT + R_CTX
TM = 512
NT_LAT = R_LAT // TM
NT_ALL = R_ALL // TM
CH = 128
assert WINDOW == CH
NCH_LAT = S // CH
NCH_CTX = CTX // CH
MXU_N = 256
FF_CHUNKS = ((0, 5 * MXU_N), (5 * MXU_N, D_FF))
FFT_L1 = 64
FFT_L2 = 128
VMEM_LIMIT = 56 * 1024 * 1024


def _params(sem, vmem=VMEM_LIMIT):
    return pltpu.CompilerParams(dimension_semantics=sem, vmem_limit_bytes=vmem)


def _const_spec(shape, index=None):
    if index is None:
        index = (0,) * len(shape)
    return pl.BlockSpec(shape, lambda *_: index, pipeline_mode=pl.Buffered(1))


def _rope_tables():
    rows = S // GRID_W
    row = np.repeat(np.arange(rows, dtype=np.float64), GRID_W)
    col = np.tile(np.arange(GRID_W, dtype=np.float64), rows)
    inv = ROPE_BASE ** (-np.arange(ROPE_PAIRS, dtype=np.float64) / ROPE_PAIRS)
    ar = row[:, None] * inv
    ac = col[:, None] * inv
    cos64 = np.concatenate([np.cos(ar), np.cos(ar), np.cos(ac), np.cos(ac)], axis=1)
    sin64 = np.concatenate([-np.sin(ar), np.sin(ar), -np.sin(ac), np.sin(ac)], axis=1)
    cos = np.concatenate([cos64, cos64], axis=1)
    sin = np.concatenate([sin64, sin64], axis=1)
    cos = np.concatenate([cos, np.ones((TM, 128))], axis=0)
    sin = np.concatenate([sin, np.zeros((TM, 128))], axis=0)
    return jnp.asarray(cos, F32), jnp.asarray(sin, F32)


def _dft_cs(n):
    k = np.arange(n)
    ang = 2.0 * np.pi * ((k[:, None] * k[None, :]) % n) / n
    return np.cos(ang), np.sin(ang)


def _fourier_tables():
    c64, s64 = _dft_cs(64)
    eye4 = np.eye(4)
    chan = np.concatenate([np.kron(eye4, c64), -np.kron(eye4, s64)], axis=1)
    w1 = np.block([[c64, s64], [-s64, c64]])
    p1 = np.arange(FFT_L1)[:, None]
    l2 = np.arange(FFT_L2)[None, :]
    ang = 2.0 * np.pi * ((p1 * l2) % S) / S
    twr = np.repeat(np.cos(ang)[:, :, None], 128, axis=2)
    twi = np.repeat(np.sin(ang)[:, :, None], 128, axis=2)
    c128, s128 = _dft_cs(128)
    w2 = np.concatenate([c128, s128], axis=1) / math.sqrt(S * 64.0)
    c256, s256 = _dft_cs(CTX)
    wc = np.concatenate([c256, s256], axis=1) / math.sqrt(CTX * 64.0)
    f = lambda a: jnp.asarray(a, F32)
    return (f(chan).astype(BF16), f(w1).astype(BF16), f(twr), f(twi), f(w2).astype(BF16), f(wc).astype(BF16))


def _mod_slice(m, i):
    return m[:, i * D:(i + 1) * D]


def _rms_mod(h, nw, shift, scale):
    ms = jnp.mean(h * h, axis=-1, keepdims=True)
    gain = nw * (1.0 + scale)
    return (h * lax.rsqrt(ms + EPS)) * gain + shift


def _swiglu(u, w_in_ref, w_out_ref):
    y = None
    for lo, hi in FF_CHUNKS:
        g = jnp.dot(u, w_in_ref[:, lo:hi], preferred_element_type=F32)
        up = jnp.dot(u, w_in_ref[:, D_FF + lo:D_FF + hi], preferred_element_type=F32)
        a = (g * jax.nn.sigmoid(g) * up).astype(BF16)
        t = jnp.dot(a, w_out_ref[lo:hi, :], preferred_element_type=F32)
        y = t if y is None else y + t
    return y


def _ffn_half_step(h, m, nw, mod0, w_in_ref, w_out_ref):
    u = _rms_mod(h, nw, _mod_slice(m, mod0), _mod_slice(m, mod0 + 1)).astype(BF16)
    return h + 0.5 * _mod_slice(m, mod0 + 2) * _swiglu(u, w_in_ref, w_out_ref)


BF16_ROWS = 16


def _cast_specs(sources, n_steps, step_of=lambda i: i):
    in_specs, out_specs, out_shapes = [], [], []
    for arr, lead, rows, cols in sources:
        n_blk = min(n_steps, rows // BF16_ROWS)
        while rows % (BF16_ROWS * n_blk):
            n_blk -= 1
        rb = rows // n_blk
        blk = lambda *idx, n_blk=n_blk: jnp.minimum(step_of(*idx), n_blk - 1)
        in_specs.append(pl.BlockSpec((None,) * len(lead) + (rb, cols),
                                     lambda *idx, lead=lead, blk=blk: (*lead, blk(*idx), 0)))
        out_specs.append(pl.BlockSpec((rb, cols), lambda *idx, blk=blk: (blk(*idx), 0)))
        out_shapes.append(jax.ShapeDtypeStruct((rows, cols), BF16))
    return in_specs, out_specs, out_shapes


def _cast_blocks(src_refs, dst_refs):
    for src, dst in zip(src_refs, dst_refs):
        dst[...] = src[...].astype(BF16)


def _log_sigmoid(x):
    return jnp.minimum(x, 0.0) - jnp.log1p(jnp.exp(-jnp.abs(x)))


def _lane_head(width):
    return lax.broadcasted_iota(jnp.int32, (1, width), 1) // HD


ADA_CB = 1152


ADA_NB = (N_MOD * D) // ADA_CB


def _ada_body(n_cast, ct_ref, w_ref, b_ref, *refs):
    cast_src, o_ref, cast_dst = refs[:n_cast], refs[n_cast], refs[n_cast + 1:]
    _cast_blocks(cast_src, cast_dst)
    ct = ct_ref[...]
    s = ct * jax.nn.sigmoid(ct)
    w = w_ref[0]
    b = b_ref[0]
    for r in range(B + 1):
        o_ref[0, r] = jnp.sum(w * s[:, r:r + 1], axis=0, keepdims=True) + b


def _ada(cond_t, w_ada, b_ada, cast_sources):
    cast_in, cast_out, cast_shapes = _cast_specs(cast_sources, DEPTH * ADA_NB, lambda l, j: l * ADA_NB + j)
    return pl.pallas_call(
        functools.partial(_ada_body, len(cast_sources)),
        grid=(DEPTH, ADA_NB),
        in_specs=[
            pl.BlockSpec((D, 8), lambda l, j: (0, 0)),
            pl.BlockSpec((1, D, ADA_CB), lambda l, j: (l, 0, j)),
            pl.BlockSpec((1, 1, ADA_CB), lambda l, j: (l, 0, j)),
        ] + cast_in,
        out_specs=[pl.BlockSpec((1, B + 1, 1, ADA_CB), lambda l, j: (l, 0, 0, j))] + cast_out,
        out_shape=[jax.ShapeDtypeStruct((DEPTH, B + 1, 1, N_MOD * D), F32)] + cast_shapes,
        compiler_params=_params(("arbitrary", "arbitrary")),
        name="ada_mod",
    )(cond_t, w_ada, b_ada.reshape(DEPTH, 1, N_MOD * D), *[s[0] for s in cast_sources])


def _mod_spec(layer):
    return pl.BlockSpec((None, None, 1, N_MOD * D),
                        lambda i: (layer, jnp.minimum(i // (NT_LAT // B), B), 0, 0))


def _norm_spec(layer, k):
    return _const_spec((None, None, 1, D), (layer, k, 0, 0))


def _row_spec(width):
    return pl.BlockSpec((TM, width), lambda i: (i, 0))


def _lat_or_ctx_specs(width):
    return [pl.BlockSpec((TM, width), lambda i: (jnp.minimum(i, NT_LAT - 1), 0)),
            pl.BlockSpec((TM, width), lambda i: (0, 0))]


def _pick(lat_ref, ctx_ref):
    return jnp.where(pl.program_id(0) < NT_LAT, lat_ref[...], ctx_ref[...])


def _rope(x, cos, sin, lane_first_half):
    outs = []
    for j in range(x.shape[1] // 128):
        xb = x[:, j * 128:(j + 1) * 128]
        fwd = pltpu.roll(xb, 112, axis=1)
        bwd = pltpu.roll(xb, 16, axis=1)
        partner = jnp.where(lane_first_half, fwd, bwd)
        outs.append(xb * cos + partner * sin)
    return outs[0] if len(outs) == 1 else jnp.concatenate(outs, axis=1)


N_CAST = 3


def _ffn_inproj_body(first, *refs):
    refs = list(refs)
    if first:
        h = _pick(refs.pop(0), refs.pop(0))
    else:
        h = refs.pop(0)[...]
    (m_ref, nw0_ref, nw1_ref, fw_in_ref, fw_out_ref, w_ref, cos_ref, sin_ref, chan_ref) = refs[:9]
    cast_src = refs[9:9 + N_CAST]
    (h_out_ref, gr_ref, gi_ref, rq_ref, rk_ref, rv_ref, rg_ref, aq_ref, ak_ref, av_ref) = refs[9 + N_CAST:19 + N_CAST]
    _cast_blocks(cast_src, refs[19 + N_CAST:])
    m = m_ref[...]
    h = _ffn_half_step(h, m, nw0_ref[...], 0, fw_in_ref, fw_out_ref)
    h_out_ref[...] = h
    lane = lax.broadcasted_iota(jnp.int32, (1, 128), 1)
    first_half = (lane // ROPE_PAIRS) % 2 == 0
    halves = [slice(r0, r0 + TM // 2) for r0 in (0, TM // 2)]
    us = [_rms_mod(h[r], nw1_ref[...], _mod_slice(m, 3), _mod_slice(m, 4)).astype(BF16) for r in halves]
    zs = [jnp.dot(u, w_ref[...], preferred_element_type=F32) for u in us]
    for r, z in zip(halves, zs):
        cos = cos_ref[r, :]
        sin = sin_ref[r, :]
        g = jnp.dot(z[:, 0:256].astype(BF16), chan_ref[...], preferred_element_type=F32)
        gr_ref[r, :] = g[:, 0:256]
        gi_ref[r, :] = g[:, 256:512]
        rq_ref[r, :] = _rope(z[:, 256:512], cos, sin, first_half).astype(BF16)
        rk_ref[r, :] = (_rope(z[:, 512:768], cos, sin, first_half) * (HD ** -0.5)).astype(BF16)
        rv_ref[r, :] = z[:, 768:1024].astype(BF16)
        rg_ref[r, :] = z[:, 1024:1280].astype(BF16)
        aq_ref[r, :] = (_rope(z[:, 1280:1792], cos, sin, first_half) * (HD ** -0.5 * LOG2E)).astype(BF16)
        ak_ref[r, :] = _rope(z[:, 1792:1920], cos, sin, first_half).astype(BF16)
        av_ref[r, :] = z[:, 1920:2048].astype(BF16)


def _rope_block(i):
    return jnp.where(i < NT_LAT, i % (S // TM), S // TM)


def _ffn_inproj(rows_in, mod_all, norm_w, weights, cast_sources, cos_t, sin_t, chan, layer):
    first = len(rows_in) == 2
    widths = (256, 256, 256, 256, 256, 256, 512, KV_W, KV_W)
    dtypes = [F32, F32] + [BF16] * (len(widths) - 2)
    cast_in, cast_out, cast_shapes = _cast_specs(cast_sources, NT_ALL)
    return pl.pallas_call(
        functools.partial(_ffn_inproj_body, first),
        grid=(NT_ALL,),
        in_specs=(_lat_or_ctx_specs(D) if first else [_row_spec(D)]) + [
            _mod_spec(layer),
            _norm_spec(layer, 0),
            _norm_spec(layer, 1),
            _const_spec((D, 2 * D_FF)), _const_spec((D_FF, D)), _const_spec((D, D_IN)),
            pl.BlockSpec((TM, 128), lambda i: (_rope_block(i), 0)),
            pl.BlockSpec((TM, 128), lambda i: (_rope_block(i), 0)),
            _const_spec((256, 512)),
        ] + cast_in,
        out_specs=[_row_spec(D)] + [_row_spec(w) for w in widths] + cast_out,
        out_shape=[jax.ShapeDtypeStruct((R_ALL, D), F32)]
        + [jax.ShapeDtypeStruct((R_ALL, w), dt) for w, dt in zip(widths, dtypes)] + cast_shapes,
        compiler_params=_params(("arbitrary",)),
        name="ffn_inproj_first" if first else "ffn_inproj",
    )(*rows_in, mod_all, norm_w, norm_w, *weights, cos_t, sin_t, chan, *[s[0] for s in cast_sources])


FFT_SUB = 16


FFT_N1 = FFT_L2 // FFT_SUB
FFT_N2 = FFT_L1 // FFT_SUB


def _fft_body(w1_ref, w2_ref, twr_ref, twi_ref, xr_ref, xi_ref, o_ref, z_ref):
    t = pl.program_id(1)

    @pl.when(t < FFT_N1)
    def _():
        w1 = w1_ref[...]
        for s in range(FFT_SUB):
            x = jnp.concatenate([xr_ref[:, s, :], xi_ref[:, s, :]], axis=0).astype(BF16)
            z = jnp.dot(w1, x, preferred_element_type=F32)
            z_ref[0, :, t, s, :] = z[0:FFT_L1]
            z_ref[1, :, t, s, :] = z[FFT_L1:2 * FFT_L1]

    @pl.when(t >= FFT_N1)
    def _():
        w2 = w2_ref[...]
        p0 = (t - FFT_N1) * FFT_SUB
        for p in range(FFT_SUB):
            zr = z_ref[0, p0 + p].reshape(FFT_L2, 256)
            zi = z_ref[1, p0 + p].reshape(FFT_L2, 256)
            tr = jnp.concatenate([twr_ref[p]] * 2, axis=1)
            ti = jnp.concatenate([twi_ref[p]] * 2, axis=1)
            yr = tr * zr + ti * zi
            yi = tr * zi - ti * zr
            y = jnp.concatenate([yr, yi], axis=0).astype(BF16)
            o_ref[:, p, :] = jnp.dot(w2, y, preferred_element_type=F32)


def _fft(w1, w2, twr, twi, gr, gi):
    view = (R_ALL // FFT_L2, FFT_N1, FFT_SUB, 256)
    x_spec = pl.BlockSpec((FFT_L1, None, FFT_SUB, 256), lambda b, t: (b, jnp.minimum(t, FFT_N1 - 1), 0, 0))
    tw_spec = pl.BlockSpec((FFT_SUB, FFT_L2, 128), lambda b, t: (jnp.maximum(t - FFT_N1, 0), 0, 0))
    out = pl.pallas_call(
        _fft_body,
        grid=(B, FFT_N1 + FFT_N2),
        in_specs=[_const_spec((128, 128)), _const_spec((128, 256)), tw_spec, tw_spec, x_spec, x_spec],
        out_specs=pl.BlockSpec((None, FFT_L2, None, FFT_SUB, 256),
                               lambda b, t: (b, 0, jnp.maximum(t - FFT_N1, 0), 0, 0)),
        out_shape=jax.ShapeDtypeStruct((B, FFT_L2, FFT_N2, FFT_SUB, 256), F32),
        scratch_shapes=[pltpu.VMEM((2, FFT_L1, FFT_N1, FFT_SUB, 256), F32)],
        compiler_params=_params(("arbitrary", "arbitrary")),
        name="fft_lat",
    )(w1, w2, twr, twi, gr.reshape(view), gi.reshape(view))
    return out.reshape(R_LAT, 256)


def _fft_ctx_body(wc_ref, gr_ref, gi_ref, o_ref):
    x = jnp.concatenate([gr_ref[...], gi_ref[...]], axis=0).astype(BF16)
    o_ref[...] = jnp.dot(wc_ref[...], x, preferred_element_type=F32)


def _fft_ctx(wc, gr, gi):
    blk0 = R_LAT // CTX
    return pl.pallas_call(
        _fft_ctx_body,
        grid=(B,),
        in_specs=[
            _const_spec((CTX, 2 * CTX)),
            pl.BlockSpec((CTX, 256), lambda b: (blk0 + b, 0)),
            pl.BlockSpec((CTX, 256), lambda b: (blk0 + b, 0)),
        ],
        out_specs=pl.BlockSpec((CTX, 256), lambda b: (b, 0)),
        out_shape=jax.ShapeDtypeStruct((R_CTX, 256), F32),
        compiler_params=_params(("arbitrary",)),
        name="fft_ctx",
    )(wc, gr, gi)


RS_CH = 8
RS_STEPS = NCH_LAT // RS_CH
N_STATE = (RS_STEPS + 1) * RS_CH
RO_CH = 8
CTX_ROWBLK = R_LAT // CTX


def _head_block_mask():
    r = lax.broadcasted_iota(jnp.int32, (256, 256), 0) // HD
    c = lax.broadcasted_iota(jnp.int32, (256, 256), 1) // HD
    return r == c


def _ret_scan_step(t, lgf, lgb, kf_ref, vf_ref, kb_ref, vb_ref, kx_ref, vx_ref, sf_scr, sb_scr, accf, accb):
    pos = lax.broadcasted_iota(jnp.int32, (CH, 1), 0).astype(F32)
    wf = jnp.exp((CH - 1.0 - pos) * lgf)
    wb = jnp.exp(pos * lgb)
    cdf = jnp.exp(float(CH) * lgf)
    cdb = jnp.exp(float(CH) * lgb)
    mask = _head_block_mask()
    dn = (((0,), (0,)), ((), ()))

    def scan(acc, scr, slot0, k_ref, v_ref, w, cd, order):
        kvs = {}
        for j in order:
            rows = slice(j * CH, (j + 1) * CH)
            kd = (k_ref[rows, :].astype(F32) * w).astype(BF16)
            kvs[j] = lax.dot_general(kd, v_ref[rows, :], dn, preferred_element_type=F32)
        state = acc[...]
        for j in order:
            scr[slot0 + j] = state.astype(BF16)
            state = state * cd + jnp.where(mask, kvs[j], 0.0)
        acc[...] = state

    @pl.when(t == 0)
    def _():
        accf[...] = jnp.zeros_like(accf)
        accb[...] = jnp.zeros_like(accb)
        for j in range(NCH_CTX, RS_CH):
            sf_scr[NCH_LAT + j] = jnp.zeros((256, 256), BF16)
            sb_scr[NCH_LAT + j] = jnp.zeros((256, 256), BF16)
        scan(accf, sf_scr, NCH_LAT, kx_ref, vx_ref, wf, cdf, list(range(NCH_CTX)))
        scan(accb, sb_scr, NCH_LAT, kx_ref, vx_ref, wb, cdb, list(reversed(range(NCH_CTX))))

    @pl.when(t > 0)
    def _():
        scan(accf, sf_scr, (t - 1) * RS_CH, kf_ref, vf_ref, wf, cdf, list(range(RS_CH)))
        scan(accb, sb_scr, (RS_STEPS - t) * RS_CH, kb_ref, vb_ref, wb, cdb, list(reversed(range(RS_CH))))


def _ret_lat_body(dec_ref, gn_ref, kf_ref, vf_ref, kb_ref, vb_ref, kx_ref, vx_ref,
                  q_ref, k_ref, v_ref, g_ref, o_ref, sfc_ref, sbc_ref, sf_scr, sb_scr, accf, accb):
    t = pl.program_id(1)
    lg = _log_sigmoid(dec_ref[...])
    lgf = lg[0:1]
    lgb = lg[1:2]

    @pl.when(t <= RS_STEPS)
    def _():
        _ret_scan_step(t, lgf, lgb, kf_ref, vf_ref, kb_ref, vb_ref, kx_ref, vx_ref, sf_scr, sb_scr, accf, accb)

    @pl.when(t == 0)
    def _():
        for j in range(RS_CH):
            sfc_ref[0, j] = sf_scr[NCH_LAT + j]
            sbc_ref[0, j] = sb_scr[NCH_LAT + j]

    @pl.when(t > RS_STEPS)
    def _():
        slot0 = (t - RS_STEPS - 1) * RO_CH
        _ret_out_core(RO_CH, lgf, lgb, gn_ref, q_ref, k_ref, v_ref, g_ref,
                      lambda c: (sf_scr[slot0 + c], sb_scr[slot0 + c]), o_ref)


def _ret_lat(dec, gn, rq, rk, rv, rg, layer):
    rows = RS_CH * CH
    n1 = RS_STEPS + 1
    n2 = NCH_LAT // RO_CH
    assert RO_CH == RS_CH
    fmap = lambda b, t: (b * RS_STEPS + jnp.clip(t - 1, 0, RS_STEPS - 1), 0)
    bmap = lambda b, t: (b * RS_STEPS + RS_STEPS - jnp.clip(t, 1, RS_STEPS), 0)
    xmap = lambda b, t: (CTX_ROWBLK + b, 0)
    omap = lambda b, t: (b * n2 + jnp.maximum(t - n1, 0), 0)
    blk = lambda m: pl.BlockSpec((rows, 256), m)
    ctx_state = pl.BlockSpec((1, RS_CH, 256, 256), lambda b, t: (b, 0, 0, 0))
    return pl.pallas_call(
        _ret_lat_body,
        grid=(B, n1 + n2),
        in_specs=[
            pl.BlockSpec((None, 2, 256), lambda b, t: (layer, 0, 0)),
            pl.BlockSpec((None, 1, 256), lambda b, t: (layer, 0, 0)),
            blk(fmap), blk(fmap), blk(bmap), blk(bmap),
            pl.BlockSpec((CTX, 256), xmap), pl.BlockSpec((CTX, 256), xmap),
            blk(omap), blk(omap), blk(omap), blk(omap),
        ],
        out_specs=[blk(omap), ctx_state, ctx_state],
        out_shape=[jax.ShapeDtypeStruct((R_LAT, 256), BF16)]
        + [jax.ShapeDtypeStruct((B, RS_CH, 256, 256), BF16)] * 2,
        scratch_shapes=[pltpu.VMEM((N_STATE, 256, 256), BF16), pltpu.VMEM((N_STATE, 256, 256), BF16),
                        pltpu.VMEM((256, 256), F32), pltpu.VMEM((256, 256), F32)],
        compiler_params=_params(("arbitrary", "arbitrary")),
        name="ret_lat",
    )(dec, gn, rk, rv, rk, rv, rk, rv, rq, rk, rv, rg)


def _split_dot(x, w):
    hi = x.astype(BF16)
    lo = (x - hi.astype(F32)).astype(BF16)
    return jnp.dot(hi, w, preferred_element_type=F32) + jnp.dot(lo, w, preferred_element_type=F32)


def _ret_out_body(nck, dec_ref, gn_ref, q_ref, k_ref, v_ref, g_ref, sf_ref, sb_ref, o_ref):
    lg = _log_sigmoid(dec_ref[...])
    _ret_out_core(nck, lg[0:1], lg[1:2], gn_ref, q_ref, k_ref, v_ref, g_ref,
                  lambda c: (sf_ref[0, c], sb_ref[0, c]), o_ref)


def _ret_out_core(nck, lgf, lgb, gn_ref, q_ref, k_ref, v_ref, g_ref, states, o_ref):
    pos = lax.broadcasted_iota(jnp.int32, (CH, 1), 0).astype(F32)
    wqf = jnp.exp((pos + 1.0) * lgf)
    wqb = jnp.exp((float(CH) - pos) * lgb)
    ri = lax.broadcasted_iota(jnp.int32, (CH, CH), 0)
    ci = lax.broadcasted_iota(jnp.int32, (CH, CH), 1)
    diff = (ri - ci).astype(F32)
    decays = []
    for h in range(RET_HEADS):
        lgf_h = lgf[:, h * HD:h * HD + 1]
        lgb_h = lgb[:, h * HD:h * HD + 1]
        decays.append(jnp.where(ri >= ci, jnp.exp(jnp.maximum(diff, 0.0) * lgf_h), 0.0)
                      + jnp.where(ci >= ri, jnp.exp(jnp.maximum(-diff, 0.0) * lgb_h), 0.0))
    decay = jnp.concatenate(decays, axis=0)
    lane_head = _lane_head(256)
    avg = jnp.where(_head_block_mask(), 1.0 / HD, 0.0).astype(BF16)
    gn = gn_ref[...]
    dn = (((1,), (1,)), ((), ()))
    chunk_rows = [slice(c * CH, (c + 1) * CH) for c in range(nck)]
    cross, scores = [], []
    for c, rows in enumerate(chunk_rows):
        qb16 = q_ref[rows, :]
        q = qb16.astype(F32)
        sf_c, sb_c = states(c)
        cross.append(jnp.dot((q * wqf).astype(BF16), sf_c, preferred_element_type=F32)
                     + jnp.dot((q * wqb).astype(BF16), sb_c, preferred_element_type=F32))
        qs = jnp.concatenate([jnp.where(lane_head == h, qb16, jnp.zeros_like(qb16))
                              for h in range(RET_HEADS)], axis=0)
        scores.append(lax.dot_general(qs, k_ref[rows, :], dn, preferred_element_type=F32))
    outs = []
    for c, rows in enumerate(chunk_rows):
        p = (scores[c] * decay).astype(BF16)
        of = jnp.dot(p, v_ref[rows, :], preferred_element_type=F32)
        o = cross[c]
        for h in range(RET_HEADS):
            o = o + jnp.where(lane_head == h, of[h * CH:(h + 1) * CH], 0.0)
        outs.append(o)
    o = jnp.concatenate(outs, axis=0)
    mu = _split_dot(o, avg)
    cen = o - mu
    var = _split_dot(cen * cen, avg)
    on = cen * lax.rsqrt(var + EPS) * gn
    g = g_ref[...].astype(F32)
    o_ref[...] = (g * jax.nn.sigmoid(g) * on).astype(BF16)


def _ret_out_ctx(dec, gn, rq, rk, rv, rg, sfc, sbc, layer):
    rmap = lambda b: (CTX_ROWBLK + b, 0)
    smap = lambda b: (b, 0, 0, 0)
    return pl.pallas_call(
        functools.partial(_ret_out_body, NCH_CTX),
        grid=(B,),
        in_specs=[
            pl.BlockSpec((None, 2, 256), lambda b: (layer, 0, 0)),
            pl.BlockSpec((None, 1, 256), lambda b: (layer, 0, 0)),
            pl.BlockSpec((CTX, 256), rmap),
            pl.BlockSpec((CTX, 256), rmap),
            pl.BlockSpec((CTX, 256), rmap),
            pl.BlockSpec((CTX, 256), rmap),
            pl.BlockSpec((1, NCH_CTX, 256, 256), smap),
            pl.BlockSpec((1, NCH_CTX, 256, 256), smap),
        ],
        out_specs=pl.BlockSpec((CTX, 256), lambda b: (b, 0)),
        out_shape=jax.ShapeDtypeStruct((R_CTX, 256), BF16),
        compiler_params=_params(("arbitrary",)),
        name="ret_out_ctx",
    )(dec, gn, rq, rk, rv, rg, sfc, sbc)


def _tile_head(x_all, hk):
    own = jnp.where(_lane_head(KV_W) == hk, x_all, jnp.zeros_like(x_all))
    both = own + pltpu.roll(own.astype(F32), HD, axis=1).astype(BF16)
    return jnp.concatenate([both, both], axis=1)


def _gqa_groups(groups):
    lane_head = _lane_head(256)
    dn = (((1,), (1,)), ((), ()))
    chains = []
    for gi, (q_grp, k_parts, v_parts, sinks, mask_fn) in enumerate(groups):
        for h0 in range(0, ATT_GROUP, ATT_STACK):
            chains.append((gi, tuple(range(h0, h0 + ATT_STACK)), q_grp, k_parts, v_parts, sinks, mask_fn))
    pieces = [[] for _ in groups]

    def run_wave(wave):
        scores, maxima = [], []
        for _, heads, q_grp, k_parts, _, _, mask_fn in wave:
            qs = jnp.concatenate([jnp.where(lane_head == g, q_grp, jnp.zeros_like(q_grp))
                                  for g in heads], axis=0)
            st = mask_fn(jnp.concatenate(
                [lax.dot_general(kp, qs, dn, preferred_element_type=F32) for kp in k_parts], axis=0))
            scores.append(st)
            maxima.append(jnp.max(st, axis=0, keepdims=True))
        probs, sink_terms = [], []
        for (_, heads, q_grp, _, _, sinks, _), st, st_max in zip(wave, scores, maxima):
            m = q_grp.shape[0]
            sink = jnp.concatenate([jnp.full((1, m), sinks[g] * LOG2E, F32) for g in heads], axis=1)
            mx = jnp.maximum(st_max, sink)
            probs.append(jnp.exp2(st - mx).astype(BF16))
            sink_terms.append(jnp.exp2(sink - mx))
        for (gi, _, q_grp, _, v_parts, _, _), p, sink_term in zip(wave, probs, sink_terms):
            m = q_grp.shape[0]
            acc, row = None, 0
            for vp in v_parts:
                n = vp.shape[1]
                t = jnp.dot(vp, p[row:row + n], preferred_element_type=F32)
                acc = t if acc is None else acc + t
                row += n
            ot = acc[0:HD] * (1.0 / (acc[HD:HD + 1] + sink_term))
            stacked = jnp.concatenate([ot[:, i * m:(i + 1) * m] for i in range(ATT_STACK)], axis=0)
            pieces[gi].append(stacked.T)

    per_wave = -(-len(chains) // ATT_WAVES)
    for w0 in range(0, len(chains), per_wave):
        run_wave(chains[w0:w0 + per_wave])
    return [jnp.concatenate(ps, axis=1) for ps in pieces]


def _band_mask_fn(lo_rows, lo_cap, hi_rows, hi_cap, n_keys):
    def mask_fn(st):
        parts = []
        row = 0
        while row < n_keys:
            piece = st[row:row + CH]
            if row == lo_rows:
                piece = jnp.minimum(piece, lo_cap)
            elif row == hi_rows:
                piece = jnp.minimum(piece, hi_cap)
            parts.append(piece)
            row += CH
        return jnp.concatenate(parts, axis=0)
    return mask_fn


def _with_ones_rows(v_t):
    return jnp.concatenate([v_t, jnp.ones((ONES_ROWS, v_t.shape[1]), BF16)], axis=0)


def _attn_lat_body(layer, sink_ref, q_ref, kp_ref, kc_ref, kn_ref, vp_ref, vc_ref, vn_ref, kx_ref, vx_ref, o_ref):
    step = pl.program_id(1)
    k_loc = jnp.concatenate([kp_ref[...], kc_ref[...], kn_ref[...]], axis=0)
    v_loc = jnp.concatenate([vp_ref[...], vc_ref[...], vn_ref[...]], axis=0)
    v_loc_t = v_loc.astype(F32).T.astype(BF16)
    v_ctx_t = vx_ref[...].astype(F32).T.astype(BF16)
    key = lax.broadcasted_iota(jnp.int32, (CH, ATT_STACK * CH), 0)
    qry = lax.broadcasted_iota(jnp.int32, (CH, ATT_STACK * CH), 1) & (CH - 1)
    below = key >= qry
    above = key <= qry
    cap = lambda ok: jnp.where(ok, jnp.inf, NEG_INF)
    lo_cap, lo_cap_first = cap(below), cap(below & (step > 0))
    hi_cap, hi_cap_last = cap(above), cap(above & (step < NCH_LAT // ATT_QB - 1))
    n_keys = 3 * CH + CTX
    masks = []
    for i in range(ATT_QB):
        masks.append(_band_mask_fn(0, lo_cap_first if i == 0 else lo_cap,
                                   2 * CH, hi_cap_last if i == ATT_QB - 1 else hi_cap, n_keys))
    groups = []
    for hk in range(ATT_KV):
        sinks = [sink_ref[layer, hk * ATT_GROUP + g] for g in range(ATT_GROUP)]
        kt_loc = _tile_head(k_loc, hk)
        kt_ctx = _tile_head(kx_ref[...], hk)
        vt_loc = _with_ones_rows(v_loc_t[hk * HD:(hk + 1) * HD])
        vt_ctx = _with_ones_rows(v_ctx_t[hk * HD:(hk + 1) * HD])
        cols = slice(hk * 256, (hk + 1) * 256)
        for i in range(ATT_QB):
            win = slice(i * CH, (i + 3) * CH)
            groups.append((q_ref[i * CH:(i + 1) * CH, cols], [jnp.concatenate([kt_loc[win], kt_ctx], axis=0)],
                           [jnp.concatenate([vt_loc[:, win], vt_ctx], axis=1)], sinks, masks[i]))
    outs = _gqa_groups(groups)
    for hk in range(ATT_KV):
        cols = slice(hk * 256, (hk + 1) * 256)
        for i in range(ATT_QB):
            o_ref[i * CH:(i + 1) * CH, cols] = outs[hk * ATT_QB + i].astype(BF16)


def _attn_lat(sink, aq, ak, av, layer):
    nq = NCH_LAT // ATT_QB
    qmap = lambda b, n: (b * nq + n, 0)
    pmap = lambda b, n: (b * NCH_LAT + jnp.maximum(ATT_QB * n - 1, 0), 0)
    nmap = lambda b, n: (b * NCH_LAT + jnp.minimum(ATT_QB * (n + 1), NCH_LAT - 1), 0)
    xmap = lambda b, n: (CTX_ROWBLK + b, 0)
    kv1 = lambda m: pl.BlockSpec((CH, KV_W), m)
    kv2 = pl.BlockSpec((ATT_QB * CH, KV_W), qmap)
    ctx_spec = pl.BlockSpec((CTX, KV_W), xmap)
    return pl.pallas_call(
        functools.partial(_attn_lat_body, layer),
        grid=(B, nq),
        in_specs=[
            pl.BlockSpec(memory_space=pltpu.SMEM),
            pl.BlockSpec((ATT_QB * CH, ATT_HEADS * HD), qmap),
            kv1(pmap), kv2, kv1(nmap),
            kv1(pmap), kv2, kv1(nmap),
            ctx_spec, ctx_spec,
        ],
        out_specs=pl.BlockSpec((ATT_QB * CH, ATT_HEADS * HD), qmap),
        out_shape=jax.ShapeDtypeStruct((R_LAT, ATT_HEADS * HD), BF16),
        compiler_params=_params(("arbitrary", "arbitrary")),
        name="swa_attn",
    )(sink, aq, ak, ak, ak, av, av, av, ak, av)


def _attn_ctx_body(layer, sink_ref, q_ref, kx_ref, vx_ref, o_ref):
    groups = []
    v_all_t = vx_ref[...].astype(F32).T.astype(BF16)
    for hk in range(ATT_KV):
        sinks = [sink_ref[layer, hk * ATT_GROUP + g] for g in range(ATT_GROUP)]
        groups.append((q_ref[:, hk * 256:(hk + 1) * 256], [_tile_head(kx_ref[...], hk)],
                       [_with_ones_rows(v_all_t[hk * HD:(hk + 1) * HD])], sinks, lambda s: s))
    outs = _gqa_groups(groups)
    for hk in range(ATT_KV):
        o_ref[:, hk * 256:(hk + 1) * 256] = outs[hk].astype(BF16)


def _attn_ctx(sink, aq, ak, av, layer):
    xmap = lambda b: (CTX_ROWBLK + b, 0)
    return pl.pallas_call(
        functools.partial(_attn_ctx_body, layer),
        grid=(B,),
        in_specs=[
            pl.BlockSpec(memory_space=pltpu.SMEM),
            pl.BlockSpec((CTX, ATT_HEADS * HD), xmap),
            pl.BlockSpec((CTX, KV_W), xmap),
            pl.BlockSpec((CTX, KV_W), xmap),
        ],
        out_specs=pl.BlockSpec((CTX, ATT_HEADS * HD), lambda b: (b, 0)),
        out_shape=jax.ShapeDtypeStruct((R_CTX, ATT_HEADS * HD), BF16),
        compiler_params=_params(("arbitrary",)),
        name="ctx_attn",
    )(sink, aq, ak, av)


def _mix_ffn_body(has_ctx, final, n_cast, *refs):
    refs = list(refs)
    h_ref = refs.pop(0)
    mixers = []
    for _ in range(3):
        if has_ctx:
            lat_ref, ctx_ref = refs.pop(0), refs.pop(0)
            mixers.append(_pick(lat_ref, ctx_ref))
        else:
            mixers.append(refs.pop(0)[...])
    m_ref, nw_ref, wo_ref, w_in_ref, w_out_ref = refs[:5]
    refs = refs[5:]
    if final:
        fw_ref = refs.pop(0)
    cast_src, refs = refs[:n_cast], refs[n_cast:]
    o_ref = refs.pop(0)
    _cast_blocks(cast_src, refs)
    yf, yr, ya = mixers
    m = m_ref[...]
    mixed = jnp.concatenate([yf.astype(BF16), yr, ya], axis=1)
    y = jnp.dot(mixed, wo_ref[...], preferred_element_type=F32)
    h = h_ref[...] + _mod_slice(m, 5) * y
    h = _ffn_half_step(h, m, nw_ref[...], 6, w_in_ref, w_out_ref)
    if final:
        ms = jnp.mean(h * h, axis=-1, keepdims=True)
        h = h * lax.rsqrt(ms + EPS) * fw_ref[...]
    o_ref[...] = h


def _mix_ffn(h, mixers, mod_all, norm_w, weights, cast_sources, final_w, layer):
    has_ctx = mixers[0][1] is not None
    final = final_w is not None
    nt = NT_ALL if has_ctx else NT_LAT
    cast_in, cast_out, cast_shapes = _cast_specs(cast_sources, nt)
    in_specs = [_row_spec(D)]
    args = [h]
    for (lat, ctx), width in zip(mixers, (256, 256, 512)):
        if has_ctx:
            in_specs += _lat_or_ctx_specs(width)
            args += [lat, ctx]
        else:
            in_specs += [_row_spec(width)]
            args += [lat]
    in_specs += [_mod_spec(layer), _norm_spec(layer, 2),
                 _const_spec((D, D)), _const_spec((D, 2 * D_FF)), _const_spec((D_FF, D))]
    args += [mod_all, norm_w, *weights]
    if final:
        in_specs += [_const_spec((1, D))]
        args += [final_w]
    in_specs += cast_in
    args += [s[0] for s in cast_sources]
    return pl.pallas_call(
        functools.partial(_mix_ffn_body, has_ctx, final, len(cast_sources)),
        grid=(nt,),
        in_specs=in_specs,
        out_specs=[_row_spec(D)] + cast_out,
        out_shape=[jax.ShapeDtypeStruct((nt * TM, D), F32)] + cast_shapes,
        compiler_params=_params(("arbitrary",)),
        name="mix_ffn_final" if final else "mix_ffn",
    )(*args)


def kernel(x, c, ctx, c_ctx, norm_w, w_ada, b_ada, ffn_w_in, ffn_w_out, w_in, w_o, ret_decay, ret_gn_w,
           attn_sink, final_norm_w):
    assert x.shape == (B, S, D) and ctx.shape == (B, CTX, D)
    cos_t, sin_t = _rope_tables()
    chan, w1, twr, twi, w2, wc = _fourier_tables()

    def ffn_sources(layer, j):
        return [(ffn_w_in, (layer, j), D, 2 * D_FF), (ffn_w_out, (layer, j), D_FF, D)]

    cond_t = jnp.zeros((D, 8), F32).at[:, 0:B].set(c.T).at[:, B].set(c_ctx)
    mod_all, *pre_w = _ada(cond_t, w_ada, b_ada, ffn_sources(0, 0) + [(w_in, (0,), D, D_IN)])
    norm4 = norm_w.reshape(DEPTH, 3, 1, D)
    gn3 = ret_gn_w.reshape(DEPTH, 1, 256)
    dec = jnp.repeat(ret_decay.astype(F32), HD, axis=2)
    sink = attn_sink.astype(F32)

    h = None
    out = None
    for layer in range(DEPTH):
        last = layer == DEPTH - 1
        rows_in = (x.reshape(R_LAT, D), ctx.reshape(R_CTX, D)) if layer == 0 else (h,)
        mix_sources = [(w_o, (layer,), D, D)] + ffn_sources(layer, 1)
        (h, gr, gi, rq, rk, rv, rg, aq, ak, av, *mix_w) = _ffn_inproj(
            rows_in, mod_all, norm4, pre_w, mix_sources, cos_t, sin_t, chan, layer)

        yf = _fft(w1, w2, twr, twi, gr, gi)
        yr, sfc, sbc = _ret_lat(dec, gn3, rq, rk, rv, rg, layer)
        ya = _attn_lat(sink, aq, ak, av, layer)
        if last:
            yfc = yrc = yac = None
        else:
            yfc = _fft_ctx(wc, gr, gi)
            yrc = _ret_out_ctx(dec, gn3, rq, rk, rv, rg, sfc, sbc, layer)
            yac = _attn_ctx(sink, aq, ak, av, layer)

        next_sources = [] if last else ffn_sources(layer + 1, 0) + [(w_in, (layer + 1,), D, D_IN)]
        res, *pre_w = _mix_ffn(h, [(yf, yfc), (yr, yrc), (ya, yac)], mod_all, norm4, mix_w, next_sources,
                               final_norm_w.reshape(1, D) if last else None, layer)
        if last:
            out = res
        else:
            h = res
    return out.reshape(B, S, D)
```

```python
import functools
import math

import numpy as np
import jax
import jax.numpy as jnp
from jax import lax
from jax.experimental import pallas as pl
from jax.experimental.pallas import tpu as pltpu

F32 = jnp.float32
BF16 = jnp.bfloat16

D = 1024
B = 2
S = 8192
DEPTH = 2
GRID_W = 64
CTX = 256
HD = 64
N_MOD = 9
D_FF = 2816
D_IN = 2048
EPS = 1e-6
NEG_INF = -1e30
ROPE_BASE = 10000.0
ROPE_PAIRS = 16
RET_HEADS = 4
ATT_HEADS = 8
ATT_KV = 2
ATT_GROUP = ATT_HEADS // ATT_KV
ATT_STACK = 2
ATT_QB = 16
ATT_WAVES = 4
ONES_ROWS = 16
KV_W = ATT_KV * HD
LOG2E = math.log2(math.e)
WINDOW = 128

R_LAT = B * S
R_CTX = B * CTX
R_ALL = R_LAT + R_CTX
TM = 512
NT_LAT = R_LAT // TM
NT_ALL = R_ALL // TM
CH = 128
assert WINDOW == CH
NCH_LAT = S // CH
NCH_CTX = CTX // CH
MXU_N = 256
FF_CHUNKS = ((0, 5 * MXU_N), (5 * MXU_N, D_FF))
FFT_L1 = 64
FFT_L2 = 128
VMEM_LIMIT = 56 * 1024 * 1024


def _params(sem, vmem=VMEM_LIMIT):
    return pltpu.CompilerParams(dimension_semantics=sem, vmem_limit_bytes=vmem)


def _const_spec(shape, index=None):
    if index is None:
        index = (0,) * len(shape)
    return pl.BlockSpec(shape, lambda *_: index, pipeline_mode=pl.Buffered(1))


def _rope_tables():
    rows = S // GRID_W
    row = np.repeat(np.arange(rows, dtype=np.float64), GRID_W)
    col = np.tile(np.arange(GRID_W, dtype=np.float64), rows)
    inv = ROPE_BASE ** (-np.arange(ROPE_PAIRS, dtype=np.float64) / ROPE_PAIRS)
    ar = row[:, None] * inv
    ac = col[:, None] * inv
    cos64 = np.concatenate([np.cos(ar), np.cos(ar), np.cos(ac), np.cos(ac)], axis=1)
    sin64 = np.concatenate([-np.sin(ar), np.sin(ar), -np.sin(ac), np.sin(ac)], axis=1)
    cos = np.concatenate([cos64, cos64], axis=1)
    sin = np.concatenate([sin64, sin64], axis=1)
    cos = np.concatenate([cos, np.ones((TM, 128))], axis=0)
    sin = np.concatenate([sin, np.zeros((TM, 128))], axis=0)
    return jnp.asarray(cos, F32), jnp.asarray(sin, F32)


def _dft_cs(n):
    k = np.arange(n)
    ang = 2.0 * np.pi * ((k[:, None] * k[None, :]) % n) / n
    return np.cos(ang), np.sin(ang)


def _fourier_tables():
    c64, s64 = _dft_cs(64)
    eye4 = np.eye(4)
    chan = np.concatenate([np.kron(eye4, c64), -np.kron(eye4, s64)], axis=1)
    w1 = np.block([[c64, s64], [-s64, c64]])
    p1 = np.arange(FFT_L1)[:, None]
    l2 = np.arange(FFT_L2)[None, :]
    ang = 2.0 * np.pi * ((p1 * l2) % S) / S
    twr = np.repeat(np.cos(ang)[:, :, None], 128, axis=2)
    twi = np.repeat(np.sin(ang)[:, :, None], 128, axis=2)
    c128, s128 = _dft_cs(128)
    w2 = np.concatenate([c128, s128], axis=1) / math.sqrt(S * 64.0)
    c256, s256 = _dft_cs(CTX)
    wc = np.concatenate([c256, s256], axis=1) / math.sqrt(CTX * 64.0)
    f = lambda a: jnp.asarray(a, F32)
    return (f(chan).astype(BF16), f(w1).astype(BF16), f(twr), f(twi), f(w2).astype(BF16), f(wc).astype(BF16))


def _mod_slice(m, i):
    return m[:, i * D:(i + 1) * D]


def _rms_mod(h, nw, shift, scale):
    ms = jnp.mean(h * h, axis=-1, keepdims=True)
    gain = nw * (1.0 + scale)
    return (h * lax.rsqrt(ms + EPS)) * gain + shift


def _swiglu(u, w_in_ref, w_out_ref):
    y = None
    for lo, hi in FF_CHUNKS:
        g = jnp.dot(u, w_in_ref[:, lo:hi], preferred_element_type=F32)
        up = jnp.dot(u, w_in_ref[:, D_FF + lo:D_FF + hi], preferred_element_type=F32)
        a = (g * jax.nn.sigmoid(g) * up).astype(BF16)
        t = jnp.dot(a, w_out_ref[lo:hi, :], preferred_element_type=F32)
        y = t if y is None else y + t
    return y


def _ffn_half_step(h, m, nw, mod0, w_in_ref, w_out_ref):
    u = _rms_mod(h, nw, _mod_slice(m, mod0), _mod_slice(m, mod0 + 1)).astype(BF16)
    return h + 0.5 * _mod_slice(m, mod0 + 2) * _swiglu(u, w_in_ref, w_out_ref)


BF16_ROWS = 16


def _cast_specs(sources, n_steps, step_of=lambda i: i):
    in_specs, out_specs, out_shapes = [], [], []
    for arr, lead, rows, cols in sources:
        n_blk = min(n_steps, rows // BF16_ROWS)
        while rows % (BF16_ROWS * n_blk):
            n_blk -= 1
        rb = rows // n_blk
        blk = lambda *idx, n_blk=n_blk: jnp.minimum(step_of(*idx), n_blk - 1)
        in_specs.append(pl.BlockSpec((None,) * len(lead) + (rb, cols),
                                     lambda *idx, lead=lead, blk=blk: (*lead, blk(*idx), 0)))
        out_specs.append(pl.BlockSpec((rb, cols), lambda *idx, blk=blk: (blk(*idx), 0)))
        out_shapes.append(jax.ShapeDtypeStruct((rows, cols), BF16))
    return in_specs, out_specs, out_shapes


def _cast_blocks(src_refs, dst_refs):
    for src, dst in zip(src_refs, dst_refs):
        dst[...] = src[...].astype(BF16)


def _log_sigmoid(x):
    return jnp.minimum(x, 0.0) - jnp.log1p(jnp.exp(-jnp.abs(x)))


def _lane_head(width):
    return lax.broadcasted_iota(jnp.int32, (1, width), 1) // HD


ADA_CB = 1152


ADA_NB = (N_MOD * D) // ADA_CB


def _ada_body(n_cast, ct_ref, w_ref, b_ref, *refs):
    cast_src, o_ref, cast_dst = refs[:n_cast], refs[n_cast], refs[n_cast + 1:]
    _cast_blocks(cast_src, cast_dst)
    ct = ct_ref[...]
    s = ct * jax.nn.sigmoid(ct)
    w = w_ref[0]
    b = b_ref[0]
    for r in range(B + 1):
        o_ref[0, r] = jnp.sum(w * s[:, r:r + 1], axis=0, keepdims=True) + b


def _ada(cond_t, w_ada, b_ada, cast_sources):
    cast_in, cast_out, cast_shapes = _cast_specs(cast_sources, DEPTH * ADA_NB, lambda l, j: l * ADA_NB + j)
    return pl.pallas_call(
        functools.partial(_ada_body, len(cast_sources)),
        grid=(DEPTH, ADA_NB),
        in_specs=[
            pl.BlockSpec((D, 8), lambda l, j: (0, 0)),
            pl.BlockSpec((1, D, ADA_CB), lambda l, j: (l, 0, j)),
            pl.BlockSpec((1, 1, ADA_CB), lambda l, j: (l, 0, j)),
        ] + cast_in,
        out_specs=[pl.BlockSpec((1, B + 1, 1, ADA_CB), lambda l, j: (l, 0, 0, j))] + cast_out,
        out_shape=[jax.ShapeDtypeStruct((DEPTH, B + 1, 1, N_MOD * D), F32)] + cast_shapes,
        compiler_params=_params(("arbitrary", "arbitrary")),
        name="ada_mod",
    )(cond_t, w_ada, b_ada.reshape(DEPTH, 1, N_MOD * D), *[s[0] for s in cast_sources])


def _mod_spec(layer):
    return pl.BlockSpec((None, None, 1, N_MOD * D),
                        lambda i: (layer, jnp.minimum(i // (NT_LAT // B), B), 0, 0))


def _norm_spec(layer, k):
    return _const_spec((None, None, 1, D), (layer, k, 0, 0))


def _row_spec(width):
    return pl.BlockSpec((TM, width), lambda i: (i, 0))


def _lat_or_ctx_specs(width):
    return [pl.BlockSpec((TM, width), lambda i: (jnp.minimum(i, NT_LAT - 1), 0)),
            pl.BlockSpec((TM, width), lambda i: (0, 0))]


def _pick(lat_ref, ctx_ref):
    return jnp.where(pl.program_id(0) < NT_LAT, lat_ref[...], ctx_ref[...])


def _rope(x, cos, sin, lane_first_half):
    outs = []
    for j in range(x.shape[1] // 128):
        xb = x[:, j * 128:(j + 1) * 128]
        fwd = pltpu.roll(xb, 112, axis=1)
        bwd = pltpu.roll(xb, 16, axis=1)
        partner = jnp.where(lane_first_half, fwd, bwd)
        outs.append(xb * cos + partner * sin)
    return outs[0] if len(outs) == 1 else jnp.concatenate(outs, axis=1)


N_CAST = 3


def _ffn_inproj_body(first, *refs):
    refs = list(refs)
    if first:
        h = _pick(refs.pop(0), refs.pop(0))
    else:
        h = refs.pop(0)[...]
    (m_ref, nw0_ref, nw1_ref, fw_in_ref, fw_out_ref, w_ref, cos_ref, sin_ref, chan_ref) = refs[:9]
    cast_src = refs[9:9 + N_CAST]
    (h_out_ref, gr_ref, gi_ref, rq_ref, rk_ref, rv_ref, rg_ref, aq_ref, ak_ref, av_ref) = refs[9 + N_CAST:19 + N_CAST]
    _cast_blocks(cast_src, refs[19 + N_CAST:])
    m = m_ref[...]
    h = _ffn_half_step(h, m, nw0_ref[...], 0, fw_in_ref, fw_out_ref)
    h_out_ref[...] = h
    lane = lax.broadcasted_iota(jnp.int32, (1, 128), 1)
    first_half = (lane // ROPE_PAIRS) % 2 == 0
    halves = [slice(r0, r0 + TM // 2) for r0 in (0, TM // 2)]
    us = [_rms_mod(h[r], nw1_ref[...], _mod_slice(m, 3), _mod_slice(m, 4)).astype(BF16) for r in halves]
    zs = [jnp.dot(u, w_ref[...], preferred_element_type=F32) for u in us]
    for r, z in zip(halves, zs):
        cos = cos_ref[r, :]
        sin = sin_ref[r, :]
        g = jnp.dot(z[:, 0:256].astype(BF16), chan_ref[...], preferred_element_type=F32)
        gr_ref[r, :] = g[:, 0:256]
        gi_ref[r, :] = g[:, 256:512]
        rq_ref[r, :] = _rope(z[:, 256:512], cos, sin, first_half).astype(BF16)
        rk_ref[r, :] = (_rope(z[:, 512:768], cos, sin, first_half) * (HD ** -0.5)).astype(BF16)
        rv_ref[r, :] = z[:, 768:1024].astype(BF16)
        rg_ref[r, :] = z[:, 1024:1280].astype(BF16)
        aq_ref[r, :] = (_rope(z[:, 1280:1792], cos, sin, first_half) * (HD ** -0.5 * LOG2E)).astype(BF16)
        ak_ref[r, :] = _rope(z[:, 1792:1920], cos, sin, first_half).astype(BF16)
        av_ref[r, :] = z[:, 1920:2048].astype(BF16)


def _rope_block(i):
    return jnp.where(i < NT_LAT, i % (S // TM), S // TM)


def _ffn_inproj(rows_in, mod_all, norm_w, weights, cast_sources, cos_t, sin_t, chan, layer):
    first = len(rows_in) == 2
    widths = (256, 256, 256, 256, 256, 256, 512, KV_W, KV_W)
    dtypes = [F32, F32] + [BF16] * (len(widths) - 2)
    cast_in, cast_out, cast_shapes = _cast_specs(cast_sources, NT_ALL)
    return pl.pallas_call(
        functools.partial(_ffn_inproj_body, first),
        grid=(NT_ALL,),
        in_specs=(_lat_or_ctx_specs(D) if first else [_row_spec(D)]) + [
            _mod_spec(layer),
            _norm_spec(layer, 0),
            _norm_spec(layer, 1),
            _const_spec((D, 2 * D_FF)), _const_spec((D_FF, D)), _const_spec((D, D_IN)),
            pl.BlockSpec((TM, 128), lambda i: (_rope_block(i), 0)),
            pl.BlockSpec((TM, 128), lambda i: (_rope_block(i), 0)),
            _const_spec((256, 512)),
        ] + cast_in,
        out_specs=[_row_spec(D)] + [_row_spec(w) for w in widths] + cast_out,
        out_shape=[jax.ShapeDtypeStruct((R_ALL, D), F32)]
        + [jax.ShapeDtypeStruct((R_ALL, w), dt) for w, dt in zip(widths, dtypes)] + cast_shapes,
        compiler_params=_params(("arbitrary",)),
        name="ffn_inproj_first" if first else "ffn_inproj",
    )(*rows_in, mod_all, norm_w, norm_w, *weights, cos_t, sin_t, chan, *[s[0] for s in cast_sources])


FFT_SUB = 32


FFT_N1 = FFT_L2 // FFT_SUB
FFT_N2 = FFT_L1 // FFT_SUB


def _fft_body(w1_ref, w2_ref, twr_ref, twi_ref, xr_ref, xi_ref, o_ref, z_ref):
    t = pl.program_id(1)

    @pl.when(t < FFT_N1)
    def _():
        w1 = w1_ref[...]
        for s in range(FFT_SUB):
            x = jnp.concatenate([xr_ref[:, s, :], xi_ref[:, s, :]], axis=0).astype(BF16)
            z = jnp.dot(w1, x, preferred_element_type=F32)
            z_ref[0, :, t, s, :] = z[0:FFT_L1]
            z_ref[1, :, t, s, :] = z[FFT_L1:2 * FFT_L1]

    @pl.when(t >= FFT_N1)
    def _():
        w2 = w2_ref[...]
        p0 = (t - FFT_N1) * FFT_SUB
        for p in range(FFT_SUB):
            zr = z_ref[0, p0 + p].reshape(FFT_L2, 256)
            zi = z_ref[1, p0 + p].reshape(FFT_L2, 256)
            tr = jnp.concatenate([twr_ref[p]] * 2, axis=1)
            ti = jnp.concatenate([twi_ref[p]] * 2, axis=1)
            yr = tr * zr + ti * zi
            yi = tr * zi - ti * zr
            y = jnp.concatenate([yr, yi], axis=0).astype(BF16)
            o_ref[:, p, :] = jnp.dot(w2, y, preferred_element_type=F32)


def _fft(w1, w2, twr, twi, gr, gi):
    view = (R_ALL // FFT_L2, FFT_N1, FFT_SUB, 256)
    x_spec = pl.BlockSpec((FFT_L1, None, FFT_SUB, 256), lambda b, t: (b, jnp.minimum(t, FFT_N1 - 1), 0, 0))
    tw_spec = pl.BlockSpec((FFT_SUB, FFT_L2, 128), lambda b, t: (jnp.maximum(t - FFT_N1, 0), 0, 0))
    out = pl.pallas_call(
        _fft_body,
        grid=(B, FFT_N1 + FFT_N2),
        in_specs=[_const_spec((128, 128)), _const_spec((128, 256)), tw_spec, tw_spec, x_spec, x_spec],
        out_specs=pl.BlockSpec((None, FFT_L2, None, FFT_SUB, 256),
                               lambda b, t: (b, 0, jnp.maximum(t - FFT_N1, 0), 0, 0)),
        out_shape=jax.ShapeDtypeStruct((B, FFT_L2, FFT_N2, FFT_SUB, 256), F32),
        scratch_shapes=[pltpu.VMEM((2, FFT_L1, FFT_N1, FFT_SUB, 256), F32)],
        compiler_params=_params(("arbitrary", "arbitrary")),
        name="fft_lat",
    )(w1, w2, twr, twi, gr.reshape(view), gi.reshape(view))
    return out.reshape(R_LAT, 256)


def _fft_ctx_body(wc_ref, gr_ref, gi_ref, o_ref):
    x = jnp.concatenate([gr_ref[...], gi_ref[...]], axis=0).astype(BF16)
    o_ref[...] = jnp.dot(wc_ref[...], x, preferred_element_type=F32)


def _fft_ctx(wc, gr, gi):
    blk0 = R_LAT // CTX
    return pl.pallas_call(
        _fft_ctx_body,
        grid=(B,),
        in_specs=[
            _const_spec((CTX, 2 * CTX)),
            pl.BlockSpec((CTX, 256), lambda b: (blk0 + b, 0)),
            pl.BlockSpec((CTX, 256), lambda b: (blk0 + b, 0)),
        ],
        out_specs=pl.BlockSpec((CTX, 256), lambda b: (b, 0)),
        out_shape=jax.ShapeDtypeStruct((R_CTX, 256), F32),
        compiler_params=_params(("arbitrary",)),
        name="fft_ctx",
    )(wc, gr, gi)


RS_CH = 8
RS_STEPS = NCH_LAT // RS_CH
N_STATE = (RS_STEPS + 1) * RS_CH
RO_CH = 8
CTX_ROWBLK = R_LAT // CTX


def _head_block_mask():
    r = lax.broadcasted_iota(jnp.int32, (256, 256), 0) // HD
    c = lax.broadcasted_iota(jnp.int32, (256, 256), 1) // HD
    return r == c


def _ret_scan_step(t, lgf, lgb, kf_ref, vf_ref, kb_ref, vb_ref, kx_ref, vx_ref, sf_scr, sb_scr, accf, accb):
    pos = lax.broadcasted_iota(jnp.int32, (CH, 1), 0).astype(F32)
    wf = jnp.exp((CH - 1.0 - pos) * lgf)
    wb = jnp.exp(pos * lgb)
    cdf = jnp.exp(float(CH) * lgf)
    cdb = jnp.exp(float(CH) * lgb)
    mask = _head_block_mask()
    dn = (((0,), (0,)), ((), ()))

    def scan(acc, scr, slot0, k_ref, v_ref, w, cd, order):
        kvs = {}
        for j in order:
            rows = slice(j * CH, (j + 1) * CH)
            kd = (k_ref[rows, :].astype(F32) * w).astype(BF16)
            kvs[j] = lax.dot_general(kd, v_ref[rows, :], dn, preferred_element_type=F32)
        state = acc[...]
        for j in order:
            scr[slot0 + j] = state.astype(BF16)
            state = state * cd + jnp.where(mask, kvs[j], 0.0)
        acc[...] = state

    @pl.when(t == 0)
    def _():
        accf[...] = jnp.zeros_like(accf)
        accb[...] = jnp.zeros_like(accb)
        for j in range(NCH_CTX, RS_CH):
            sf_scr[NCH_LAT + j] = jnp.zeros((256, 256), BF16)
            sb_scr[NCH_LAT + j] = jnp.zeros((256, 256), BF16)
        scan(accf, sf_scr, NCH_LAT, kx_ref, vx_ref, wf, cdf, list(range(NCH_CTX)))
        scan(accb, sb_scr, NCH_LAT, kx_ref, vx_ref, wb, cdb, list(reversed(range(NCH_CTX))))

    @pl.when(t > 0)
    def _():
        scan(accf, sf_scr, (t - 1) * RS_CH, kf_ref, vf_ref, wf, cdf, list(range(RS_CH)))
        scan(accb, sb_scr, (RS_STEPS - t) * RS_CH, kb_ref, vb_ref, wb, cdb, list(reversed(range(RS_CH))))


def _ret_lat_body(dec_ref, gn_ref, kf_ref, vf_ref, kb_ref, vb_ref, kx_ref, vx_ref,
                  q_ref, k_ref, v_ref, g_ref, o_ref, sfc_ref, sbc_ref, sf_scr, sb_scr, accf, accb):
    t = pl.program_id(1)
    lg = _log_sigmoid(dec_ref[...])
    lgf = lg[0:1]
    lgb = lg[1:2]

    @pl.when(t <= RS_STEPS)
    def _():
        _ret_scan_step(t, lgf, lgb, kf_ref, vf_ref, kb_ref, vb_ref, kx_ref, vx_ref, sf_scr, sb_scr, accf, accb)

    @pl.when(t == 0)
    def _():
        for j in range(RS_CH):
            sfc_ref[0, j] = sf_scr[NCH_LAT + j]
            sbc_ref[0, j] = sb_scr[NCH_LAT + j]

    @pl.when(t > RS_STEPS)
    def _():
        slot0 = (t - RS_STEPS - 1) * RO_CH
        _ret_out_core(RO_CH, lgf, lgb, gn_ref, q_ref, k_ref, v_ref, g_ref,
                      lambda c: (sf_scr[slot0 + c], sb_scr[slot0 + c]), o_ref)


def _ret_lat(dec, gn, rq, rk, rv, rg, layer):
    rows = RS_CH * CH
    n1 = RS_STEPS + 1
    n2 = NCH_LAT // RO_CH
    assert RO_CH == RS_CH
    fmap = lambda b, t: (b * RS_STEPS + jnp.clip(t - 1, 0, RS_STEPS - 1), 0)
    bmap = lambda b, t: (b * RS_STEPS + RS_STEPS - jnp.clip(t, 1, RS_STEPS), 0)
    xmap = lambda b, t: (CTX_ROWBLK + b, 0)
    omap = lambda b, t: (b * n2 + jnp.maximum(t - n1, 0), 0)
    blk = lambda m: pl.BlockSpec((rows, 256), m)
    ctx_state = pl.BlockSpec((1, RS_CH, 256, 256), lambda b, t: (b, 0, 0, 0))
    return pl.pallas_call(
        _ret_lat_body,
        grid=(B, n1 + n2),
        in_specs=[
            pl.BlockSpec((None, 2, 256), lambda b, t: (layer, 0, 0)),
            pl.BlockSpec((None, 1, 256), lambda b, t: (layer, 0, 0)),
            blk(fmap), blk(fmap), blk(bmap), blk(bmap),
            pl.BlockSpec((CTX, 256), xmap), pl.BlockSpec((CTX, 256), xmap),
            blk(omap), blk(omap), blk(omap), blk(omap),
        ],
        out_specs=[blk(omap), ctx_state, ctx_state],
        out_shape=[jax.ShapeDtypeStruct((R_LAT, 256), BF16)]
        + [jax.ShapeDtypeStruct((B, RS_CH, 256, 256), BF16)] * 2,
        scratch_shapes=[pltpu.VMEM((N_STATE, 256, 256), BF16), pltpu.VMEM((N_STATE, 256, 256), BF16),
                        pltpu.VMEM((256, 256), F32), pltpu.VMEM((256, 256), F32)],
        compiler_params=_params(("arbitrary", "arbitrary")),
        name="ret_lat",
    )(dec, gn, rk, rv, rk, rv, rk, rv, rq, rk, rv, rg)


def _split_dot(x, w):
    hi = x.astype(BF16)
    lo = (x - hi.astype(F32)).astype(BF16)
    return jnp.dot(hi, w, preferred_element_type=F32) + jnp.dot(lo, w, preferred_element_type=F32)


def _ret_out_body(nck, dec_ref, gn_ref, q_ref, k_ref, v_ref, g_ref, sf_ref, sb_ref, o_ref):
    lg = _log_sigmoid(dec_ref[...])
    _ret_out_core(nck, lg[0:1], lg[1:2], gn_ref, q_ref, k_ref, v_ref, g_ref,
                  lambda c: (sf_ref[0, c], sb_ref[0, c]), o_ref)


def _ret_out_core(nck, lgf, lgb, gn_ref, q_ref, k_ref, v_ref, g_ref, states, o_ref):
    pos = lax.broadcasted_iota(jnp.int32, (CH, 1), 0).astype(F32)
    wqf = jnp.exp((pos + 1.0) * lgf)
    wqb = jnp.exp((float(CH) - pos) * lgb)
    ri = lax.broadcasted_iota(jnp.int32, (CH, CH), 0)
    ci = lax.broadcasted_iota(jnp.int32, (CH, CH), 1)
    diff = (ri - ci).astype(F32)
    decays = []
    for h in range(RET_HEADS):
        lgf_h = lgf[:, h * HD:h * HD + 1]
        lgb_h = lgb[:, h * HD:h * HD + 1]
        decays.append(jnp.where(ri >= ci, jnp.exp(jnp.maximum(diff, 0.0) * lgf_h), 0.0)
                      + jnp.where(ci >= ri, jnp.exp(jnp.maximum(-diff, 0.0) * lgb_h), 0.0))
    decay = jnp.concatenate(decays, axis=0)
    lane_head = _lane_head(256)
    avg = jnp.where(_head_block_mask(), 1.0 / HD, 0.0).astype(BF16)
    gn = gn_ref[...]
    dn = (((1,), (1,)), ((), ()))
    chunk_rows = [slice(c * CH, (c + 1) * CH) for c in range(nck)]
    cross, scores = [], []
    for c, rows in enumerate(chunk_rows):
        qb16 = q_ref[rows, :]
        q = qb16.astype(F32)
        sf_c, sb_c = states(c)
        cross.append(jnp.dot((q * wqf).astype(BF16), sf_c, preferred_element_type=F32)
                     + jnp.dot((q * wqb).astype(BF16), sb_c, preferred_element_type=F32))
        qs = jnp.concatenate([jnp.where(lane_head == h, qb16, jnp.zeros_like(qb16))
                              for h in range(RET_HEADS)], axis=0)
        scores.append(lax.dot_general(qs, k_ref[rows, :], dn, preferred_element_type=F32))
    outs = []
    for c, rows in enumerate(chunk_rows):
        p = (scores[c] * decay).astype(BF16)
        of = jnp.dot(p, v_ref[rows, :], preferred_element_type=F32)
        o = cross[c]
        for h in range(RET_HEADS):
            o = o + jnp.where(lane_head == h, of[h * CH:(h + 1) * CH], 0.0)
        outs.append(o)
    o = jnp.concatenate(outs, axis=0)
    mu = _split_dot(o, avg)
    cen = o - mu
    var = _split_dot(cen * cen, avg)
    on = cen * lax.rsqrt(var + EPS) * gn
    g = g_ref[...].astype(F32)
    o_ref[...] = (g * jax.nn.sigmoid(g) * on).astype(BF16)


def _ret_out_ctx(dec, gn, rq, rk, rv, rg, sfc, sbc, layer):
    rmap = lambda b: (CTX_ROWBLK + b, 0)
    smap = lambda b: (b, 0, 0, 0)
    return pl.pallas_call(
        functools.partial(_ret_out_body, NCH_CTX),
        grid=(B,),
        in_specs=[
            pl.BlockSpec((None, 2, 256), lambda b: (layer, 0, 0)),
            pl.BlockSpec((None, 1, 256), lambda b: (layer, 0, 0)),
            pl.BlockSpec((CTX, 256), rmap),
            pl.BlockSpec((CTX, 256), rmap),
            pl.BlockSpec((CTX, 256), rmap),
            pl.BlockSpec((CTX, 256), rmap),
            pl.BlockSpec((1, NCH_CTX, 256, 256), smap),
            pl.BlockSpec((1, NCH_CTX, 256, 256), smap),
        ],
        out_specs=pl.BlockSpec((CTX, 256), lambda b: (b, 0)),
        out_shape=jax.ShapeDtypeStruct((R_CTX, 256), BF16),
        compiler_params=_params(("arbitrary",)),
        name="ret_out_ctx",
    )(dec, gn, rq, rk, rv, rg, sfc, sbc)


def _tile_head(x_all, hk):
    own = jnp.where(_lane_head(KV_W) == hk, x_all, jnp.zeros_like(x_all))
    both = own + pltpu.roll(own.astype(F32), HD, axis=1).astype(BF16)
    return jnp.concatenate([both, both], axis=1)


def _gqa_groups(groups):
    lane_head = _lane_head(256)
    dn = (((1,), (1,)), ((), ()))
    chains = []
    for gi, (q_grp, k_parts, v_parts, sinks, mask_fn) in enumerate(groups):
        for h0 in range(0, ATT_GROUP, ATT_STACK):
            chains.append((gi, tuple(range(h0, h0 + ATT_STACK)), q_grp, k_parts, v_parts, sinks, mask_fn))
    pieces = [[] for _ in groups]

    def run_wave(wave):
        scores, maxima = [], []
        for _, heads, q_grp, k_parts, _, _, mask_fn in wave:
            qs = jnp.concatenate([jnp.where(lane_head == g, q_grp, jnp.zeros_like(q_grp))
                                  for g in heads], axis=0)
            st = mask_fn(jnp.concatenate(
                [lax.dot_general(kp, qs, dn, preferred_element_type=F32) for kp in k_parts], axis=0))
            scores.append(st)
            maxima.append(jnp.max(st, axis=0, keepdims=True))
        probs, sink_terms = [], []
        for (_, heads, q_grp, _, _, sinks, _), st, st_max in zip(wave, scores, maxima):
            m = q_grp.shape[0]
            sink = jnp.concatenate([jnp.full((1, m), sinks[g] * LOG2E, F32) for g in heads], axis=1)
            mx = jnp.maximum(st_max, sink)
            probs.append(jnp.exp2(st - mx).astype(BF16))
            sink_terms.append(jnp.exp2(sink - mx))
        for (gi, _, q_grp, _, v_parts, _, _), p, sink_term in zip(wave, probs, sink_terms):
            m = q_grp.shape[0]
            acc, row = None, 0
            for vp in v_parts:
                n = vp.shape[1]
                t = jnp.dot(vp, p[row:row + n], preferred_element_type=F32)
                acc = t if acc is None else acc + t
                row += n
            ot = acc[0:HD] * (1.0 / (acc[HD:HD + 1] + sink_term))
            stacked = jnp.concatenate([ot[:, i * m:(i + 1) * m] for i in range(ATT_STACK)], axis=0)
            pieces[gi].append(stacked.T)

    per_wave = -(-len(chains) // ATT_WAVES)
    for w0 in range(0, len(chains), per_wave):
        run_wave(chains[w0:w0 + per_wave])
    return [jnp.concatenate(ps, axis=1) for ps in pieces]


def _band_mask_fn(lo_rows, lo_cap, hi_rows, hi_cap, n_keys):
    def mask_fn(st):
        parts = []
        row = 0
        while row < n_keys:
            piece = st[row:row + CH]
            if row == lo_rows:
                piece = jnp.minimum(piece, lo_cap)
            elif row == hi_rows:
                piece = jnp.minimum(piece, hi_cap)
            parts.append(piece)
            row += CH
        return jnp.concatenate(parts, axis=0)
    return mask_fn


def _with_ones_rows(v_t):
    return jnp.concatenate([v_t, jnp.ones((ONES_ROWS, v_t.shape[1]), BF16)], axis=0)


def _attn_lat_body(layer, sink_ref, q_ref, kp_ref, kc_ref, kn_ref, vp_ref, vc_ref, vn_ref, kx_ref, vx_ref, o_ref):
    step = pl.program_id(1)
    k_loc = jnp.concatenate([kp_ref[...], kc_ref[...], kn_ref[...]], axis=0)
    v_loc = jnp.concatenate([vp_ref[...], vc_ref[...], vn_ref[...]], axis=0)
    v_loc_t = v_loc.astype(F32).T.astype(BF16)
    v_ctx_t = vx_ref[...].astype(F32).T.astype(BF16)
    key = lax.broadcasted_iota(jnp.int32, (CH, ATT_STACK * CH), 0)
    qry = lax.broadcasted_iota(jnp.int32, (CH, ATT_STACK * CH), 1) & (CH - 1)
    below = key >= qry
    above = key <= qry
    cap = lambda ok: jnp.where(ok, jnp.inf, NEG_INF)
    lo_cap, lo_cap_first = cap(below), cap(below & (step > 0))
    hi_cap, hi_cap_last = cap(above), cap(above & (step < NCH_LAT // ATT_QB - 1))
    n_keys = 3 * CH + CTX
    masks = []
    for i in range(ATT_QB):
        masks.append(_band_mask_fn(0, lo_cap_first if i == 0 else lo_cap,
                                   2 * CH, hi_cap_last if i == ATT_QB - 1 else hi_cap, n_keys))
    groups = []
    for hk in range(ATT_KV):
        sinks = [sink_ref[layer, hk * ATT_GROUP + g] for g in range(ATT_GROUP)]
        kt_loc = _tile_head(k_loc, hk)
        kt_ctx = _tile_head(kx_ref[...], hk)
        vt_loc = _with_ones_rows(v_loc_t[hk * HD:(hk + 1) * HD])
        vt_ctx = _with_ones_rows(v_ctx_t[hk * HD:(hk + 1) * HD])
        cols = slice(hk * 256, (hk + 1) * 256)
        for i in range(ATT_QB):
            win = slice(i * CH, (i + 3) * CH)
            groups.append((q_ref[i * CH:(i + 1) * CH, cols], [jnp.concatenate([kt_loc[win], kt_ctx], axis=0)],
                           [jnp.concatenate([vt_loc[:, win], vt_ctx], axis=1)], sinks, masks[i]))
    outs = _gqa_groups(groups)
    for hk in range(ATT_KV):
        cols = slice(hk * 256, (hk + 1) * 256)
        for i in range(ATT_QB):
            o_ref[i * CH:(i + 1) * CH, cols] = outs[hk * ATT_QB + i].astype(BF16)


def _attn_lat(sink, aq, ak, av, layer):
    nq = NCH_LAT // ATT_QB
    qmap = lambda b, n: (b * nq + n, 0)
    pmap = lambda b, n: (b * NCH_LAT + jnp.maximum(ATT_QB * n - 1, 0), 0)
    nmap = lambda b, n: (b * NCH_LAT + jnp.minimum(ATT_QB * (n + 1), NCH_LAT - 1), 0)
    xmap = lambda b, n: (CTX_ROWBLK + b, 0)
    kv1 = lambda m: pl.BlockSpec((CH, KV_W), m)
    kv2 = pl.BlockSpec((ATT_QB * CH, KV_W), qmap)
    ctx_spec = pl.BlockSpec((CTX, KV_W), xmap)
    return pl.pallas_call(
        functools.partial(_attn_lat_body, layer),
        grid=(B, nq),
        in_specs=[
            pl.BlockSpec(memory_space=pltpu.SMEM),
            pl.BlockSpec((ATT_QB * CH, ATT_HEADS * HD), qmap),
            kv1(pmap), kv2, kv1(nmap),
            kv1(pmap), kv2, kv1(nmap),
            ctx_spec, ctx_spec,
        ],
        out_specs=pl.BlockSpec((ATT_QB * CH, ATT_HEADS * HD), qmap),
        out_shape=jax.ShapeDtypeStruct((R_LAT, ATT_HEADS * HD), BF16),
        compiler_params=_params(("arbitrary", "arbitrary")),
        name="swa_attn",
    )(sink, aq, ak, ak, ak, av, av, av, ak, av)


def _attn_ctx_body(layer, sink_ref, q_ref, kx_ref, vx_ref, o_ref):
    groups = []
    v_all_t = vx_ref[...].astype(F32).T.astype(BF16)
    for hk in range(ATT_KV):
        sinks = [sink_ref[layer, hk * ATT_GROUP + g] for g in range(ATT_GROUP)]
        groups.append((q_ref[:, hk * 256:(hk + 1) * 256], [_tile_head(kx_ref[...], hk)],
                       [_with_ones_rows(v_all_t[hk * HD:(hk + 1) * HD])], sinks, lambda s: s))
    outs = _gqa_groups(groups)
    for hk in range(ATT_KV):
        o_ref[:, hk * 256:(hk + 1) * 256] = outs[hk].astype(BF16)


def _attn_ctx(sink, aq, ak, av, layer):
    xmap = lambda b: (CTX_ROWBLK + b, 0)
    return pl.pallas_call(
        functools.partial(_attn_ctx_body, layer),
        grid=(B,),
        in_specs=[
            pl.BlockSpec(memory_space=pltpu.SMEM),
            pl.BlockSpec((CTX, ATT_HEADS * HD), xmap),
            pl.BlockSpec((CTX, KV_W), xmap),
            pl.BlockSpec((CTX, KV_W), xmap),
        ],
        out_specs=pl.BlockSpec((CTX, ATT_HEADS * HD), lambda b: (b, 0)),
        out_shape=jax.ShapeDtypeStruct((R_CTX, ATT_HEADS * HD), BF16),
        compiler_params=_params(("arbitrary",)),
        name="ctx_attn",
    )(sink, aq, ak, av)


def _mix_ffn_body(has_ctx, final, n_cast, *refs):
    refs = list(refs)
    h_ref = refs.pop(0)
    mixers = []
    for _ in range(3):
        if has_ctx:
            lat_ref, ctx_ref = refs.pop(0), refs.pop(0)
            mixers.append(_pick(lat_ref, ctx_ref))
        else:
            mixers.append(refs.pop(0)[...])
    m_ref, nw_ref, wo_ref, w_in_ref, w_out_ref = refs[:5]
    refs = refs[5:]
    if final:
        fw_ref = refs.pop(0)
    cast_src, refs = refs[:n_cast], refs[n_cast:]
    o_ref = refs.pop(0)
    _cast_blocks(cast_src, refs)
    yf, yr, ya = mixers
    m = m_ref[...]
    mixed = jnp.concatenate([yf.astype(BF16), yr, ya], axis=1)
    y = jnp.concatenate([jnp.dot(mixed[r0:r0 + TM // 2], wo_ref[...], preferred_element_type=F32)
                         for r0 in (0, TM // 2)], axis=0)
    h = h_ref[...] + _mod_slice(m, 5) * y
    h = _ffn_half_step(h, m, nw_ref[...], 6, w_in_ref, w_out_ref)
    if final:
        ms = jnp.mean(h * h, axis=-1, keepdims=True)
        h = h * lax.rsqrt(ms + EPS) * fw_ref[...]
    o_ref[...] = h


def _mix_ffn(h, mixers, mod_all, norm_w, weights, cast_sources, final_w, layer):
    has_ctx = mixers[0][1] is not None
    final = final_w is not None
    nt = NT_ALL if has_ctx else NT_LAT
    cast_in, cast_out, cast_shapes = _cast_specs(cast_sources, nt)
    in_specs = [_row_spec(D)]
    args = [h]
    for (lat, ctx), width in zip(mixers, (256, 256, 512)):
        if has_ctx:
            in_specs += _lat_or_ctx_specs(width)
            args += [lat, ctx]
        else:
            in_specs += [_row_spec(width)]
            args += [lat]
    in_specs += [_mod_spec(layer), _norm_spec(layer, 2),
                 _const_spec((D, D)), _const_spec((D, 2 * D_FF)), _const_spec((D_FF, D))]
    args += [mod_all, norm_w, *weights]
    if final:
        in_specs += [_const_spec((1, D))]
        args += [final_w]
    in_specs += cast_in
    args += [s[0] for s in cast_sources]
    return pl.pallas_call(
        functools.partial(_mix_ffn_body, has_ctx, final, len(cast_sources)),
        grid=(nt,),
        in_specs=in_specs,
        out_specs=[_row_spec(D)] + cast_out,
        out_shape=[jax.ShapeDtypeStruct((nt * TM, D), F32)] + cast_shapes,
        compiler_params=_params(("arbitrary",)),
        name="mix_ffn_final" if final else "mix_ffn",
    )(*args)


def kernel(x, c, ctx, c_ctx, norm_w, w_ada, b_ada, ffn_w_in, ffn_w_out, w_in, w_o, ret_decay, ret_gn_w,
           attn_sink, final_norm_w):
    assert x.shape == (B, S, D) and ctx.shape == (B, CTX, D)
    cos_t, sin_t = _rope_tables()
    chan, w1, twr, twi, w2, wc = _fourier_tables()

    def ffn_sources(layer, j):
        return [(ffn_w_in, (layer, j), D, 2 * D_FF), (ffn_w_out, (layer, j), D_FF, D)]

    cond_t = jnp.zeros((D, 8), F32).at[:, 0:B].set(c.T).at[:, B].set(c_ctx)
    mod_all, *pre_w = _ada(cond_t, w_ada, b_ada, ffn_sources(0, 0) + [(w_in, (0,), D, D_IN)])
    norm4 = norm_w.reshape(DEPTH, 3, 1, D)
    gn3 = ret_gn_w.reshape(DEPTH, 1, 256)
    dec = jnp.repeat(ret_decay.astype(F32), HD, axis=2)
    sink = attn_sink.astype(F32)

    h = None
    out = None
    for layer in range(DEPTH):
        last = layer == DEPTH - 1
        rows_in = (x.reshape(R_LAT, D), ctx.reshape(R_CTX, D)) if layer == 0 else (h,)
        mix_sources = [(w_o, (layer,), D, D)] + ffn_sources(layer, 1)
        (h, gr, gi, rq, rk, rv, rg, aq, ak, av, *mix_w) = _ffn_inproj(
            rows_in, mod_all, norm4, pre_w, mix_sources, cos_t, sin_t, chan, layer)

        yf = _fft(w1, w2, twr, twi, gr, gi)
        yr, sfc, sbc = _ret_lat(dec, gn3, rq, rk, rv, rg, layer)
        ya = _attn_lat(sink, aq, ak, av, layer)
        if last:
            yfc = yrc = yac = None
        else:
            yfc = _fft_ctx(wc, gr, gi)
            yrc = _ret_out_ctx(dec, gn3, rq, rk, rv, rg, sfc, sbc, layer)
            yac = _attn_ctx(sink, aq, ak, av, layer)

        next_sources = [] if last else ffn_sources(layer + 1, 0) + [(w_in, (layer + 1,), D, D_IN)]
        res, *pre_w = _mix_ffn(h, [(yf, yfc), (yr, yrc), (ya, yac)], mod_all, norm4, mix_w, next_sources,
                               final_norm_w.reshape(1, D) if last else None, layer)
        if last:
            out = res
        else:
            h = res
    return out.reshape(B, S, D)
```

```python
import functools
import math

import numpy as np
import jax
import jax.numpy as jnp
from jax import lax
from jax.experimental import pallas as pl
from jax.experimental.pallas import tpu as pltpu

F32 = jnp.float32
BF16 = jnp.bfloat16

D = 1024
B = 2
S = 8192
DEPTH = 2
GRID_W = 64
CTX = 256
HD = 64
N_MOD = 9
D_FF = 2816
D_IN = 2048
EPS = 1e-6
NEG_INF = -1e30
ROPE_BASE = 10000.0
ROPE_PAIRS = 16
RET_HEADS = 4
ATT_HEADS = 8
ATT_KV = 2
ATT_GROUP = ATT_HEADS // ATT_KV
ATT_STACK = 2
ATT_QB = 16
ATT_WAVES = 4
ONES_ROWS = 16
KV_W = ATT_KV * HD
LOG2E = math.log2(math.e)
WINDOW = 128

R_LAT = B * S
R_CTX = B * CTX
R_ALL = R_LAT + R_CTX
TM = 512
NT_LAT = R_LAT // TM
NT_ALL = R_ALL // TM
CH = 128
assert WINDOW == CH
NCH_LAT = S // CH
NCH_CTX = CTX // CH
MXU_N = 256
FF_CHUNKS = ((0, 5 * MXU_N), (5 * MXU_N, D_FF))
FFT_L1 = 64
FFT_L2 = 128
VMEM_LIMIT = 56 * 1024 * 1024


def _params(sem, vmem=VMEM_LIMIT):
    return pltpu.CompilerParams(dimension_semantics=sem, vmem_limit_bytes=vmem)


def _const_spec(shape, index=None):
    if index is None:
        index = (0,) * len(shape)
    return pl.BlockSpec(shape, lambda *_: index, pipeline_mode=pl.Buffered(1))


def _rope_tables():
    rows = S // GRID_W
    row = np.repeat(np.arange(rows, dtype=np.float64), GRID_W)
    col = np.tile(np.arange(GRID_W, dtype=np.float64), rows)
    inv = ROPE_BASE ** (-np.arange(ROPE_PAIRS, dtype=np.float64) / ROPE_PAIRS)
    ar = row[:, None] * inv
    ac = col[:, None] * inv
    cos64 = np.concatenate([np.cos(ar), np.cos(ar), np.cos(ac), np.cos(ac)], axis=1)
    sin64 = np.concatenate([-np.sin(ar), np.sin(ar), -np.sin(ac), np.sin(ac)], axis=1)
    cos = np.concatenate([cos64, cos64], axis=1)
    sin = np.concatenate([sin64, sin64], axis=1)
    cos = np.concatenate([cos, np.ones((TM, 128))], axis=0)
    sin = np.concatenate([sin, np.zeros((TM, 128))], axis=0)
    return jnp.asarray(cos, F32), jnp.asarray(sin, F32)


def _dft_cs(n):
    k = np.arange(n)
    ang = 2.0 * np.pi * ((k[:, None] * k[None, :]) % n) / n
    return np.cos(ang), np.sin(ang)


def _fourier_tables():
    c64, s64 = _dft_cs(64)
    eye4 = np.eye(4)
    chan = np.concatenate([np.kron(eye4, c64), -np.kron(eye4, s64)], axis=1)
    w1 = np.block([[c64, s64], [-s64, c64]])
    p1 = np.arange(FFT_L1)[:, None]
    l2 = np.arange(FFT_L2)[None, :]
    ang = 2.0 * np.pi * ((p1 * l2) % S) / S
    twr = np.repeat(np.cos(ang)[:, :, None], 128, axis=2)
    twi = np.repeat(np.sin(ang)[:, :, None], 128, axis=2)
    c128, s128 = _dft_cs(128)
    w2 = np.concatenate([c128, s128], axis=1) / math.sqrt(S * 64.0)
    c256, s256 = _dft_cs(CTX)
    wc = np.concatenate([c256, s256], axis=1) / math.sqrt(CTX * 64.0)
    f = lambda a: jnp.asarray(a, F32)
    return (f(chan).astype(BF16), f(w1).astype(BF16), f(twr), f(twi), f(w2).astype(BF16), f(wc).astype(BF16))


def _mod_slice(m, i):
    return m[:, i * D:(i + 1) * D]


def _rms_mod(h, nw, shift, scale):
    ms = jnp.mean(h * h, axis=-1, keepdims=True)
    gain = nw * (1.0 + scale)
    return (h * lax.rsqrt(ms + EPS)) * gain + shift


def _swiglu(u, w_in_ref, w_out_ref, between=None):
    y = None
    for ci, (lo, hi) in enumerate(FF_CHUNKS):
        g = jnp.dot(u, w_in_ref[:, lo:hi], preferred_element_type=F32)
        up = jnp.dot(u, w_in_ref[:, D_FF + lo:D_FF + hi], preferred_element_type=F32)
        if ci == 0 and between is not None:
            between()
        a = (g * jax.nn.sigmoid(g) * up).astype(BF16)
        t = jnp.dot(a, w_out_ref[lo:hi, :], preferred_element_type=F32)
        y = t if y is None else y + t
    return y


def _ffn_half_step(h, m, nw, mod0, w_in_ref, w_out_ref, between=None):
    u = _rms_mod(h, nw, _mod_slice(m, mod0), _mod_slice(m, mod0 + 1)).astype(BF16)
    return h + 0.5 * _mod_slice(m, mod0 + 2) * _swiglu(u, w_in_ref, w_out_ref, between)


BF16_ROWS = 16


def _cast_specs(sources, n_steps, step_of=lambda i: i):
    in_specs, out_specs, out_shapes = [], [], []
    for arr, lead, rows, cols in sources:
        n_blk = min(n_steps, rows // BF16_ROWS)
        while rows % (BF16_ROWS * n_blk):
            n_blk -= 1
        rb = rows // n_blk
        blk = lambda *idx, n_blk=n_blk: jnp.minimum(step_of(*idx), n_blk - 1)
        in_specs.append(pl.BlockSpec((None,) * len(lead) + (rb, cols),
                                     lambda *idx, lead=lead, blk=blk: (*lead, blk(*idx), 0)))
        out_specs.append(pl.BlockSpec((rb, cols), lambda *idx, blk=blk: (blk(*idx), 0)))
        out_shapes.append(jax.ShapeDtypeStruct((rows, cols), BF16))
    return in_specs, out_specs, out_shapes


def _cast_blocks(src_refs, dst_refs):
    for src, dst in zip(src_refs, dst_refs):
        dst[...] = src[...].astype(BF16)


def _log_sigmoid(x):
    return jnp.minimum(x, 0.0) - jnp.log1p(jnp.exp(-jnp.abs(x)))


def _lane_head(width):
    return lax.broadcasted_iota(jnp.int32, (1, width), 1) // HD


ADA_CB = 1152


ADA_NB = (N_MOD * D) // ADA_CB


def _ada_body(n_cast, ct_ref, w_ref, b_ref, *refs):
    cast_src, o_ref, cast_dst = refs[:n_cast], refs[n_cast], refs[n_cast + 1:]
    _cast_blocks(cast_src, cast_dst)
    ct = ct_ref[...]
    s = ct * jax.nn.sigmoid(ct)
    w = w_ref[0]
    b = b_ref[0]
    for r in range(B + 1):
        o_ref[0, r] = jnp.sum(w * s[:, r:r + 1], axis=0, keepdims=True) + b


def _ada(cond_t, w_ada, b_ada, cast_sources):
    cast_in, cast_out, cast_shapes = _cast_specs(cast_sources, DEPTH * ADA_NB, lambda l, j: l * ADA_NB + j)
    return pl.pallas_call(
        functools.partial(_ada_body, len(cast_sources)),
        grid=(DEPTH, ADA_NB),
        in_specs=[
            pl.BlockSpec((D, 8), lambda l, j: (0, 0)),
            pl.BlockSpec((1, D, ADA_CB), lambda l, j: (l, 0, j)),
            pl.BlockSpec((1, 1, ADA_CB), lambda l, j: (l, 0, j)),
        ] + cast_in,
        out_specs=[pl.BlockSpec((1, B + 1, 1, ADA_CB), lambda l, j: (l, 0, 0, j))] + cast_out,
        out_shape=[jax.ShapeDtypeStruct((DEPTH, B + 1, 1, N_MOD * D), F32)] + cast_shapes,
        compiler_params=_params(("arbitrary", "arbitrary")),
        name="ada_mod",
    )(cond_t, w_ada, b_ada.reshape(DEPTH, 1, N_MOD * D), *[s[0] for s in cast_sources])


def _mod_spec(layer):
    return pl.BlockSpec((None, None, 1, N_MOD * D),
                        lambda i: (layer, jnp.minimum(i // (NT_LAT // B), B), 0, 0))


def _norm_spec(layer, k):
    return _const_spec((None, None, 1, D), (layer, k, 0, 0))


def _row_spec(width):
    return pl.BlockSpec((TM, width), lambda i: (i, 0))


def _lat_or_ctx_specs(width):
    return [pl.BlockSpec((TM, width), lambda i: (jnp.minimum(i, NT_LAT - 1), 0)),
            pl.BlockSpec((TM, width), lambda i: (0, 0))]


def _pick(lat_ref, ctx_ref):
    return jnp.where(pl.program_id(0) < NT_LAT, lat_ref[...], ctx_ref[...])


def _rope(x, cos, sin, lane_first_half):
    outs = []
    for j in range(x.shape[1] // 128):
        xb = x[:, j * 128:(j + 1) * 128]
        fwd = pltpu.roll(xb, 112, axis=1)
        bwd = pltpu.roll(xb, 16, axis=1)
        partner = jnp.where(lane_first_half, fwd, bwd)
        outs.append(xb * cos + partner * sin)
    return outs[0] if len(outs) == 1 else jnp.concatenate(outs, axis=1)


N_CAST = 3


def _ffn_inproj_body(first, *refs):
    refs = list(refs)
    if first:
        h = _pick(refs.pop(0), refs.pop(0))
    else:
        h = refs.pop(0)[...]
    (m_ref, mp_ref, nw0_ref, nw1_ref, fw_in_ref, fw_out_ref, w_ref, cos_ref, sin_ref, chan_ref) = refs[:10]
    cast_src = refs[10:10 + N_CAST]
    (h_out_ref, gr_ref, gi_ref, rq_ref, rk_ref, rv_ref, rg_ref, aq_ref, ak_ref, av_ref) = refs[10 + N_CAST:20 + N_CAST]
    _cast_blocks(cast_src, refs[20 + N_CAST:-1])
    hprev_ref = refs[-1]

    @pl.when(pl.program_id(0) == 0)
    def _():
        hprev_ref[...] = jnp.zeros_like(hprev_ref)

    h_cur = h
    h = hprev_ref[...]
    m = mp_ref[...]
    lane = lax.broadcasted_iota(jnp.int32, (1, 128), 1)
    first_half = (lane // ROPE_PAIRS) % 2 == 0
    halves = [slice(r0, r0 + TM // 2) for r0 in (0, TM // 2)]
    us = [_rms_mod(h[r], nw1_ref[...], _mod_slice(m, 3), _mod_slice(m, 4)).astype(BF16) for r in halves]
    zs = [jnp.dot(u, w_ref[...], preferred_element_type=F32) for u in us]
    def project_epilogue():
        for r, z in zip(halves, zs):
            cos = cos_ref[r, :]
            sin = sin_ref[r, :]
            g = jnp.dot(z[:, 0:256].astype(BF16), chan_ref[...], preferred_element_type=F32)
            gr_ref[r, :] = g[:, 0:256]
            gi_ref[r, :] = g[:, 256:512]
            rq_ref[r, :] = _rope(z[:, 256:512], cos, sin, first_half).astype(BF16)
            rk_ref[r, :] = (_rope(z[:, 512:768], cos, sin, first_half) * (HD ** -0.5)).astype(BF16)
            rv_ref[r, :] = z[:, 768:1024].astype(BF16)
            rg_ref[r, :] = z[:, 1024:1280].astype(BF16)
            aq_ref[r, :] = (_rope(z[:, 1280:1792], cos, sin, first_half) * (HD ** -0.5 * LOG2E)).astype(BF16)
            ak_ref[r, :] = _rope(z[:, 1792:1920], cos, sin, first_half).astype(BF16)
            av_ref[r, :] = z[:, 1920:2048].astype(BF16)

    h_new = _ffn_half_step(h_cur, m_ref[...], nw0_ref[...], 0, fw_in_ref, fw_out_ref, project_epilogue)
    h_out_ref[...] = h_new
    hprev_ref[...] = h_new


def _rope_block(i):
    return jnp.where(i < NT_LAT, i % (S // TM), S // TM)


def _ffn_inproj(rows_in, mod_all, norm_w, weights, cast_sources, cos_t, sin_t, chan, layer):
    first = len(rows_in) == 2
    widths = (256, 256, 256, 256, 256, 256, 512, KV_W, KV_W)
    dtypes = [F32, F32] + [BF16] * (len(widths) - 2)
    n_steps = NT_ALL + 1
    cur = lambda i: jnp.minimum(i, NT_ALL - 1)
    prev = lambda i: jnp.maximum(i - 1, 0)
    cast_in, cast_out, cast_shapes = _cast_specs(cast_sources, n_steps)
    mod_prev = pl.BlockSpec((None, None, 1, N_MOD * D),
                            lambda i: (layer, jnp.minimum(prev(i) // (NT_LAT // B), B), 0, 0))
    return pl.pallas_call(
        functools.partial(_ffn_inproj_body, first),
        grid=(n_steps,),
        in_specs=(_lat_or_ctx_specs(D) if first else [pl.BlockSpec((TM, D), lambda i: (cur(i), 0))]) + [
            _mod_spec(layer),
            mod_prev,
            _norm_spec(layer, 0),
            _norm_spec(layer, 1),
            _const_spec((D, 2 * D_FF)), _const_spec((D_FF, D)), _const_spec((D, D_IN)),
            pl.BlockSpec((TM, 128), lambda i: (_rope_block(prev(i)), 0)),
            pl.BlockSpec((TM, 128), lambda i: (_rope_block(prev(i)), 0)),
            _const_spec((256, 512)),
        ] + cast_in,
        out_specs=[pl.BlockSpec((TM, D), lambda i: (cur(i), 0))]
        + [pl.BlockSpec((TM, w), lambda i: (prev(i), 0)) for w in widths] + cast_out,
        out_shape=[jax.ShapeDtypeStruct((R_ALL, D), F32)]
        + [jax.ShapeDtypeStruct((R_ALL, w), dt) for w, dt in zip(widths, dtypes)] + cast_shapes,
        scratch_shapes=[pltpu.VMEM((TM, D), F32)],
        compiler_params=_params(("arbitrary",)),
        name="ffn_inproj_first" if first else "ffn_inproj",
    )(*rows_in, mod_all, mod_all, norm_w, norm_w, *weights, cos_t, sin_t, chan, *[s[0] for s in cast_sources])


FFT_SUB = 16


FFT_N1 = FFT_L2 // FFT_SUB
FFT_N2 = FFT_L1 // FFT_SUB


def _fft_body(w1_ref, w2_ref, twr_ref, twi_ref, xr_ref, xi_ref, o_ref, z_ref):
    t = pl.program_id(1)

    @pl.when(t < FFT_N1)
    def _():
        w1 = w1_ref[...]
        for s in range(FFT_SUB):
            x = jnp.concatenate([xr_ref[:, s, :], xi_ref[:, s, :]], axis=0).astype(BF16)
            z = jnp.dot(w1, x, preferred_element_type=F32)
            z_ref[0, :, t, s, :] = z[0:FFT_L1]
            z_ref[1, :, t, s, :] = z[FFT_L1:2 * FFT_L1]

    @pl.when(t >= FFT_N1)
    def _():
        w2 = w2_ref[...]
        p0 = (t - FFT_N1) * FFT_SUB
        for p in range(FFT_SUB):
            zr = z_ref[0, p0 + p].reshape(FFT_L2, 256)
            zi = z_ref[1, p0 + p].reshape(FFT_L2, 256)
            tr = jnp.concatenate([twr_ref[p]] * 2, axis=1)
            ti = jnp.concatenate([twi_ref[p]] * 2, axis=1)
            yr = tr * zr + ti * zi
            yi = tr * zi - ti * zr
            y = jnp.concatenate([yr, yi], axis=0).astype(BF16)
            o_ref[:, p, :] = jnp.dot(w2, y, preferred_element_type=F32)


def _fft(w1, w2, twr, twi, gr, gi):
    view = (R_ALL // FFT_L2, FFT_N1, FFT_SUB, 256)
    x_spec = pl.BlockSpec((FFT_L1, None, FFT_SUB, 256), lambda b, t: (b, jnp.minimum(t, FFT_N1 - 1), 0, 0))
    tw_spec = pl.BlockSpec((FFT_SUB, FFT_L2, 128), lambda b, t: (jnp.maximum(t - FFT_N1, 0), 0, 0))
    out = pl.pallas_call(
        _fft_body,
        grid=(B, FFT_N1 + FFT_N2),
        in_specs=[_const_spec((128, 128)), _const_spec((128, 256)), tw_spec, tw_spec, x_spec, x_spec],
        out_specs=pl.BlockSpec((None, FFT_L2, None, FFT_SUB, 256),
                               lambda b, t: (b, 0, jnp.maximum(t - FFT_N1, 0), 0, 0)),
        out_shape=jax.ShapeDtypeStruct((B, FFT_L2, FFT_N2, FFT_SUB, 256), F32),
        scratch_shapes=[pltpu.VMEM((2, FFT_L1, FFT_N1, FFT_SUB, 256), F32)],
        compiler_params=_params(("arbitrary", "arbitrary")),
        name="fft_lat",
    )(w1, w2, twr, twi, gr.reshape(view), gi.reshape(view))
    return out.reshape(R_LAT, 256)


def _fft_ctx_body(wc_ref, gr_ref, gi_ref, o_ref):
    x = jnp.concatenate([gr_ref[...], gi_ref[...]], axis=0).astype(BF16)
    o_ref[...] = jnp.dot(wc_ref[...], x, preferred_element_type=F32)


def _fft_ctx(wc, gr, gi):
    blk0 = R_LAT // CTX
    return pl.pallas_call(
        _fft_ctx_body,
        grid=(B,),
        in_specs=[
            _const_spec((CTX, 2 * CTX)),
            pl.BlockSpec((CTX, 256), lambda b: (blk0 + b, 0)),
            pl.BlockSpec((CTX, 256), lambda b: (blk0 + b, 0)),
        ],
        out_specs=pl.BlockSpec((CTX, 256), lambda b: (b, 0)),
        out_shape=jax.ShapeDtypeStruct((R_CTX, 256), F32),
        compiler_params=_params(("arbitrary",)),
        name="fft_ctx",
    )(wc, gr, gi)


RS_CH = 8
RS_STEPS = NCH_LAT // RS_CH
N_STATE = (RS_STEPS + 1) * RS_CH
RO_CH = 8
CTX_ROWBLK = R_LAT // CTX


def _head_block_mask():
    r = lax.broadcasted_iota(jnp.int32, (256, 256), 0) // HD
    c = lax.broadcasted_iota(jnp.int32, (256, 256), 1) // HD
    return r == c


def _ret_scan_step(t, lgf, lgb, kf_ref, vf_ref, kb_ref, vb_ref, kx_ref, vx_ref, sf_scr, sb_scr, accf, accb):
    pos = lax.broadcasted_iota(jnp.int32, (CH, 1), 0).astype(F32)
    wf = jnp.exp((CH - 1.0 - pos) * lgf)
    wb = jnp.exp(pos * lgb)
    cdf = jnp.exp(float(CH) * lgf)
    cdb = jnp.exp(float(CH) * lgb)
    mask = _head_block_mask()
    dn = (((0,), (0,)), ((), ()))

    def scan(acc, scr, slot0, k_ref, v_ref, w, cd, order):
        kvs = {}
        for j in order:
            rows = slice(j * CH, (j + 1) * CH)
            kd = (k_ref[rows, :].astype(F32) * w).astype(BF16)
            kvs[j] = lax.dot_general(kd, v_ref[rows, :], dn, preferred_element_type=F32)
        state = acc[...]
        for j in order:
            scr[slot0 + j] = state.astype(BF16)
            state = state * cd + jnp.where(mask, kvs[j], 0.0)
        acc[...] = state

    @pl.when(t == 0)
    def _():
        accf[...] = jnp.zeros_like(accf)
        accb[...] = jnp.zeros_like(accb)
        for j in range(NCH_CTX, RS_CH):
            sf_scr[NCH_LAT + j] = jnp.zeros((256, 256), BF16)
            sb_scr[NCH_LAT + j] = jnp.zeros((256, 256), BF16)
        scan(accf, sf_scr, NCH_LAT, kx_ref, vx_ref, wf, cdf, list(range(NCH_CTX)))
        scan(accb, sb_scr, NCH_LAT, kx_ref, vx_ref, wb, cdb, list(reversed(range(NCH_CTX))))

    @pl.when(t > 0)
    def _():
        scan(accf, sf_scr, (t - 1) * RS_CH, kf_ref, vf_ref, wf, cdf, list(range(RS_CH)))
        scan(accb, sb_scr, (RS_STEPS - t) * RS_CH, kb_ref, vb_ref, wb, cdb, list(reversed(range(RS_CH))))


def _ret_lat_body(dec_ref, gn_ref, kf_ref, vf_ref, kb_ref, vb_ref, kx_ref, vx_ref,
                  q_ref, k_ref, v_ref, g_ref, o_ref, sfc_ref, sbc_ref, sf_scr, sb_scr, accf, accb):
    t = pl.program_id(1)
    lg = _log_sigmoid(dec_ref[...])
    lgf = lg[0:1]
    lgb = lg[1:2]

    @pl.when(t <= RS_STEPS)
    def _():
        _ret_scan_step(t, lgf, lgb, kf_ref, vf_ref, kb_ref, vb_ref, kx_ref, vx_ref, sf_scr, sb_scr, accf, accb)

    @pl.when(t == 0)
    def _():
        for j in range(RS_CH):
            sfc_ref[0, j] = sf_scr[NCH_LAT + j]
            sbc_ref[0, j] = sb_scr[NCH_LAT + j]

    @pl.when(t > RS_STEPS)
    def _():
        slot0 = (t - RS_STEPS - 1) * RO_CH
        _ret_out_core(RO_CH, lgf, lgb, gn_ref, q_ref, k_ref, v_ref, g_ref,
                      lambda c: (sf_scr[slot0 + c], sb_scr[slot0 + c]), o_ref)


def _ret_lat(dec, gn, rq, rk, rv, rg, layer):
    rows = RS_CH * CH
    n1 = RS_STEPS + 1
    n2 = NCH_LAT // RO_CH
    assert RO_CH == RS_CH
    fmap = lambda b, t: (b * RS_STEPS + jnp.clip(t - 1, 0, RS_STEPS - 1), 0)
    bmap = lambda b, t: (b * RS_STEPS + RS_STEPS - jnp.clip(t, 1, RS_STEPS), 0)
    xmap = lambda b, t: (CTX_ROWBLK + b, 0)
    omap = lambda b, t: (b * n2 + jnp.maximum(t - n1, 0), 0)
    blk = lambda m: pl.BlockSpec((rows, 256), m)
    ctx_state = pl.BlockSpec((1, RS_CH, 256, 256), lambda b, t: (b, 0, 0, 0))
    return pl.pallas_call(
        _ret_lat_body,
        grid=(B, n1 + n2),
        in_specs=[
            pl.BlockSpec((None, 2, 256), lambda b, t: (layer, 0, 0)),
            pl.BlockSpec((None, 1, 256), lambda b, t: (layer, 0, 0)),
            blk(fmap), blk(fmap), blk(bmap), blk(bmap),
            pl.BlockSpec((CTX, 256), xmap), pl.BlockSpec((CTX, 256), xmap),
            blk(omap), blk(omap), blk(omap), blk(omap),
        ],
        out_specs=[blk(omap), ctx_state, ctx_state],
        out_shape=[jax.ShapeDtypeStruct((R_LAT, 256), BF16)]
        + [jax.ShapeDtypeStruct((B, RS_CH, 256, 256), BF16)] * 2,
        scratch_shapes=[pltpu.VMEM((N_STATE, 256, 256), BF16), pltpu.VMEM((N_STATE, 256, 256), BF16),
                        pltpu.VMEM((256, 256), F32), pltpu.VMEM((256, 256), F32)],
        compiler_params=_params(("arbitrary", "arbitrary")),
        name="ret_lat",
    )(dec, gn, rk, rv, rk, rv, rk, rv, rq, rk, rv, rg)


def _split_dot(x, w):
    hi = x.astype(BF16)
    lo = (x - hi.astype(F32)).astype(BF16)
    return jnp.dot(hi, w, preferred_element_type=F32) + jnp.dot(lo, w, preferred_element_type=F32)


def _ret_out_body(nck, dec_ref, gn_ref, q_ref, k_ref, v_ref, g_ref, sf_ref, sb_ref, o_ref):
    lg = _log_sigmoid(dec_ref[...])
    _ret_out_core(nck, lg[0:1], lg[1:2], gn_ref, q_ref, k_ref, v_ref, g_ref,
                  lambda c: (sf_ref[0, c], sb_ref[0, c]), o_ref)


def _ret_out_core(nck, lgf, lgb, gn_ref, q_ref, k_ref, v_ref, g_ref, states, o_ref):
    pos = lax.broadcasted_iota(jnp.int32, (CH, 1), 0).astype(F32)
    wqf = jnp.exp((pos + 1.0) * lgf)
    wqb = jnp.exp((float(CH) - pos) * lgb)
    ri = lax.broadcasted_iota(jnp.int32, (CH, CH), 0)
    ci = lax.broadcasted_iota(jnp.int32, (CH, CH), 1)
    diff = (ri - ci).astype(F32)
    decays = []
    for h in range(RET_HEADS):
        lgf_h = lgf[:, h * HD:h * HD + 1]
        lgb_h = lgb[:, h * HD:h * HD + 1]
        decays.append(jnp.where(ri >= ci, jnp.exp(jnp.maximum(diff, 0.0) * lgf_h), 0.0)
                      + jnp.where(ci >= ri, jnp.exp(jnp.maximum(-diff, 0.0) * lgb_h), 0.0))
    decay = jnp.concatenate(decays, axis=0)
    lane_head = _lane_head(256)
    avg = jnp.where(_head_block_mask(), 1.0 / HD, 0.0).astype(BF16)
    gn = gn_ref[...]
    dn = (((1,), (1,)), ((), ()))
    chunk_rows = [slice(c * CH, (c + 1) * CH) for c in range(nck)]
    cross, scores = [], []
    for c, rows in enumerate(chunk_rows):
        qb16 = q_ref[rows, :]
        q = qb16.astype(F32)
        sf_c, sb_c = states(c)
        cross.append(jnp.dot((q * wqf).astype(BF16), sf_c, preferred_element_type=F32)
                     + jnp.dot((q * wqb).astype(BF16), sb_c, preferred_element_type=F32))
        qs = jnp.concatenate([jnp.where(lane_head == h, qb16, jnp.zeros_like(qb16))
                              for h in range(RET_HEADS)], axis=0)
        scores.append(lax.dot_general(qs, k_ref[rows, :], dn, preferred_element_type=F32))
    outs = []
    for c, rows in enumerate(chunk_rows):
        p = (scores[c] * decay).astype(BF16)
        of = jnp.dot(p, v_ref[rows, :], preferred_element_type=F32)
        o = cross[c]
        for h in range(RET_HEADS):
            o = o + jnp.where(lane_head == h, of[h * CH:(h + 1) * CH], 0.0)
        outs.append(o)
    o = jnp.concatenate(outs, axis=0)
    mu = _split_dot(o, avg)
    cen = o - mu
    var = _split_dot(cen * cen, avg)
    on = cen * lax.rsqrt(var + EPS) * gn
    g = g_ref[...].astype(F32)
    o_ref[...] = (g * jax.nn.sigmoid(g) * on).astype(BF16)


def _ret_out_ctx(dec, gn, rq, rk, rv, rg, sfc, sbc, layer):
    rmap = lambda b: (CTX_ROWBLK + b, 0)
    smap = lambda b: (b, 0, 0, 0)
    return pl.pallas_call(
        functools.partial(_ret_out_body, NCH_CTX),
        grid=(B,),
        in_specs=[
            pl.BlockSpec((None, 2, 256), lambda b: (layer, 0, 0)),
            pl.BlockSpec((None, 1, 256), lambda b: (layer, 0, 0)),
            pl.BlockSpec((CTX, 256), rmap),
            pl.BlockSpec((CTX, 256), rmap),
            pl.BlockSpec((CTX, 256), rmap),
            pl.BlockSpec((CTX, 256), rmap),
            pl.BlockSpec((1, NCH_CTX, 256, 256), smap),
            pl.BlockSpec((1, NCH_CTX, 256, 256), smap),
        ],
        out_specs=pl.BlockSpec((CTX, 256), lambda b: (b, 0)),
        out_shape=jax.ShapeDtypeStruct((R_CTX, 256), BF16),
        compiler_params=_params(("arbitrary",)),
        name="ret_out_ctx",
    )(dec, gn, rq, rk, rv, rg, sfc, sbc)


def _tile_head(x_all, hk):
    own = jnp.where(_lane_head(KV_W) == hk, x_all, jnp.zeros_like(x_all))
    both = own + pltpu.roll(own.astype(F32), HD, axis=1).astype(BF16)
    return jnp.concatenate([both, both], axis=1)


def _gqa_groups(groups):
    lane_head = _lane_head(256)
    dn = (((1,), (1,)), ((), ()))
    chains = []
    for gi, (q_grp, k_parts, v_parts, sinks, mask_fn) in enumerate(groups):
        for h0 in range(0, ATT_GROUP, ATT_STACK):
            chains.append((gi, tuple(range(h0, h0 + ATT_STACK)), q_grp, k_parts, v_parts, sinks, mask_fn))
    pieces = [[] for _ in groups]

    def run_wave(wave):
        scores, maxima = [], []
        for _, heads, q_grp, k_parts, _, _, mask_fn in wave:
            qs = jnp.concatenate([jnp.where(lane_head == g, q_grp, jnp.zeros_like(q_grp))
                                  for g in heads], axis=0)
            st = mask_fn(jnp.concatenate(
                [lax.dot_general(kp, qs, dn, preferred_element_type=F32) for kp in k_parts], axis=0))
            scores.append(st)
            maxima.append(jnp.max(st, axis=0, keepdims=True))
        probs, sink_terms = [], []
        for (_, heads, q_grp, _, _, sinks, _), st, st_max in zip(wave, scores, maxima):
            m = q_grp.shape[0]
            sink = jnp.concatenate([jnp.full((1, m), sinks[g] * LOG2E, F32) for g in heads], axis=1)
            mx = jnp.maximum(st_max, sink)
            probs.append(jnp.exp2(st - mx).astype(BF16))
            sink_terms.append(jnp.exp2(sink - mx))
        for (gi, _, q_grp, _, v_parts, _, _), p, sink_term in zip(wave, probs, sink_terms):
            m = q_grp.shape[0]
            acc, row = None, 0
            for vp in v_parts:
                n = vp.shape[1]
                t = jnp.dot(vp, p[row:row + n], preferred_element_type=F32)
                acc = t if acc is None else acc + t
                row += n
            ot = acc[0:HD] * (1.0 / (acc[HD:HD + 1] + sink_term))
            stacked = jnp.concatenate([ot[:, i * m:(i + 1) * m] for i in range(ATT_STACK)], axis=0)
            pieces[gi].append(stacked.T)

    per_wave = -(-len(chains) // ATT_WAVES)
    for w0 in range(0, len(chains), per_wave):
        run_wave(chains[w0:w0 + per_wave])
    return [jnp.concatenate(ps, axis=1) for ps in pieces]


def _band_mask_fn(lo_rows, lo_cap, hi_rows, hi_cap, n_keys):
    def mask_fn(st):
        parts = []
        row = 0
        while row < n_keys:
            piece = st[row:row + CH]
            if row == lo_rows:
                piece = jnp.minimum(piece, lo_cap)
            elif row == hi_rows:
                piece = jnp.minimum(piece, hi_cap)
            parts.append(piece)
            row += CH
        return jnp.concatenate(parts, axis=0)
    return mask_fn


def _with_ones_rows(v_t):
    return jnp.concatenate([v_t, jnp.ones((ONES_ROWS, v_t.shape[1]), BF16)], axis=0)


def _attn_lat_body(layer, sink_ref, q_ref, kp_ref, kc_ref, kn_ref, vp_ref, vc_ref, vn_ref, kx_ref, vx_ref, o_ref):
    step = pl.program_id(1)
    k_loc = jnp.concatenate([kp_ref[...], kc_ref[...], kn_ref[...]], axis=0)
    v_loc = jnp.concatenate([vp_ref[...], vc_ref[...], vn_ref[...]], axis=0)
    v_loc_t = v_loc.astype(F32).T.astype(BF16)
    v_ctx_t = vx_ref[...].astype(F32).T.astype(BF16)
    key = lax.broadcasted_iota(jnp.int32, (CH, ATT_STACK * CH), 0)
    qry = lax.broadcasted_iota(jnp.int32, (CH, ATT_STACK * CH), 1) & (CH - 1)
    below = key >= qry
    above = key <= qry
    cap = lambda ok: jnp.where(ok, jnp.inf, NEG_INF)
    lo_cap, lo_cap_first = cap(below), cap(below & (step > 0))
    hi_cap, hi_cap_last = cap(above), cap(above & (step < NCH_LAT // ATT_QB - 1))
    n_keys = 3 * CH + CTX
    masks = []
    for i in range(ATT_QB):
        masks.append(_band_mask_fn(0, lo_cap_first if i == 0 else lo_cap,
                                   2 * CH, hi_cap_last if i == ATT_QB - 1 else hi_cap, n_keys))
    groups = []
    for hk in range(ATT_KV):
        sinks = [sink_ref[layer, hk * ATT_GROUP + g] for g in range(ATT_GROUP)]
        kt_loc = _tile_head(k_loc, hk)
        kt_ctx = _tile_head(kx_ref[...], hk)
        vt_loc = _with_ones_rows(v_loc_t[hk * HD:(hk + 1) * HD])
        vt_ctx = _with_ones_rows(v_ctx_t[hk * HD:(hk + 1) * HD])
        cols = slice(hk * 256, (hk + 1) * 256)
        for i in range(ATT_QB):
            win = slice(i * CH, (i + 3) * CH)
            groups.append((q_ref[i * CH:(i + 1) * CH, cols], [jnp.concatenate([kt_loc[win], kt_ctx], axis=0)],
                           [jnp.concatenate([vt_loc[:, win], vt_ctx], axis=1)], sinks, masks[i]))
    outs = _gqa_groups(groups)
    for hk in range(ATT_KV):
        cols = slice(hk * 256, (hk + 1) * 256)
        for i in range(ATT_QB):
            o_ref[i * CH:(i + 1) * CH, cols] = outs[hk * ATT_QB + i].astype(BF16)


def _attn_lat(sink, aq, ak, av, layer):
    nq = NCH_LAT // ATT_QB
    qmap = lambda b, n: (b * nq + n, 0)
    pmap = lambda b, n: (b * NCH_LAT + jnp.maximum(ATT_QB * n - 1, 0), 0)
    nmap = lambda b, n: (b * NCH_LAT + jnp.minimum(ATT_QB * (n + 1), NCH_LAT - 1), 0)
    xmap = lambda b, n: (CTX_ROWBLK + b, 0)
    kv1 = lambda m: pl.BlockSpec((CH, KV_W), m)
    kv2 = pl.BlockSpec((ATT_QB * CH, KV_W), qmap)
    ctx_spec = pl.BlockSpec((CTX, KV_W), xmap)
    return pl.pallas_call(
        functools.partial(_attn_lat_body, layer),
        grid=(B, nq),
        in_specs=[
            pl.BlockSpec(memory_space=pltpu.SMEM),
            pl.BlockSpec((ATT_QB * CH, ATT_HEADS * HD), qmap),
            kv1(pmap), kv2, kv1(nmap),
            kv1(pmap), kv2, kv1(nmap),
            ctx_spec, ctx_spec,
        ],
        out_specs=pl.BlockSpec((ATT_QB * CH, ATT_HEADS * HD), qmap),
        out_shape=jax.ShapeDtypeStruct((R_LAT, ATT_HEADS * HD), BF16),
        compiler_params=_params(("arbitrary", "arbitrary")),
        name="swa_attn",
    )(sink, aq, ak, ak, ak, av, av, av, ak, av)


def _attn_ctx_body(layer, sink_ref, q_ref, kx_ref, vx_ref, o_ref):
    groups = []
    v_all_t = vx_ref[...].astype(F32).T.astype(BF16)
    for hk in range(ATT_KV):
        sinks = [sink_ref[layer, hk * ATT_GROUP + g] for g in range(ATT_GROUP)]
        groups.append((q_ref[:, hk * 256:(hk + 1) * 256], [_tile_head(kx_ref[...], hk)],
                       [_with_ones_rows(v_all_t[hk * HD:(hk + 1) * HD])], sinks, lambda s: s))
    outs = _gqa_groups(groups)
    for hk in range(ATT_KV):
        o_ref[:, hk * 256:(hk + 1) * 256] = outs[hk].astype(BF16)


def _attn_ctx(sink, aq, ak, av, layer):
    xmap = lambda b: (CTX_ROWBLK + b, 0)
    return pl.pallas_call(
        functools.partial(_attn_ctx_body, layer),
        grid=(B,),
        in_specs=[
            pl.BlockSpec(memory_space=pltpu.SMEM),
            pl.BlockSpec((CTX, ATT_HEADS * HD), xmap),
            pl.BlockSpec((CTX, KV_W), xmap),
            pl.BlockSpec((CTX, KV_W), xmap),
        ],
        out_specs=pl.BlockSpec((CTX, ATT_HEADS * HD), lambda b: (b, 0)),
        out_shape=jax.ShapeDtypeStruct((R_CTX, ATT_HEADS * HD), BF16),
        compiler_params=_params(("arbitrary",)),
        name="ctx_attn",
    )(sink, aq, ak, av)


def _mix_ffn_body(has_ctx, final, n_cast, *refs):
    refs = list(refs)
    h_ref = refs.pop(0)
    mixers = []
    for _ in range(3):
        if has_ctx:
            lat_ref, ctx_ref = refs.pop(0), refs.pop(0)
            mixers.append(_pick(lat_ref, ctx_ref))
        else:
            mixers.append(refs.pop(0)[...])
    m_ref, nw_ref, wo_ref, w_in_ref, w_out_ref = refs[:5]
    refs = refs[5:]
    if final:
        fw_ref = refs.pop(0)
    cast_src, refs = refs[:n_cast], refs[n_cast:]
    o_ref = refs.pop(0)
    _cast_blocks(cast_src, refs)
    yf, yr, ya = mixers
    m = m_ref[...]
    mixed = jnp.concatenate([yf.astype(BF16), yr, ya], axis=1)
    y = jnp.dot(mixed, wo_ref[...], preferred_element_type=F32)
    h = h_ref[...] + _mod_slice(m, 5) * y
    h = _ffn_half_step(h, m, nw_ref[...], 6, w_in_ref, w_out_ref)
    if final:
        ms = jnp.mean(h * h, axis=-1, keepdims=True)
        h = h * lax.rsqrt(ms + EPS) * fw_ref[...]
    o_ref[...] = h


def _mix_ffn(h, mixers, mod_all, norm_w, weights, cast_sources, final_w, layer):
    has_ctx = mixers[0][1] is not None
    final = final_w is not None
    nt = NT_ALL if has_ctx else NT_LAT
    cast_in, cast_out, cast_shapes = _cast_specs(cast_sources, nt)
    in_specs = [_row_spec(D)]
    args = [h]
    for (lat, ctx), width in zip(mixers, (256, 256, 512)):
        if has_ctx:
            in_specs += _lat_or_ctx_specs(width)
            args += [lat, ctx]
        else:
            in_specs += [_row_spec(width)]
            args += [lat]
    in_specs += [_mod_spec(layer), _norm_spec(layer, 2),
                 _const_spec((D, D)), _const_spec((D, 2 * D_FF)), _const_spec((D_FF, D))]
    args += [mod_all, norm_w, *weights]
    if final:
        in_specs += [_const_spec((1, D))]
        args += [final_w]
    in_specs += cast_in
    args += [s[0] for s in cast_sources]
    return pl.pallas_call(
        functools.partial(_mix_ffn_body, has_ctx, final, len(cast_sources)),
        grid=(nt,),
        in_specs=in_specs,
        out_specs=[_row_spec(D)] + cast_out,
        out_shape=[jax.ShapeDtypeStruct((nt * TM, D), F32)] + cast_shapes,
        compiler_params=_params(("arbitrary",)),
        name="mix_ffn_final" if final else "mix_ffn",
    )(*args)


def kernel(x, c, ctx, c_ctx, norm_w, w_ada, b_ada, ffn_w_in, ffn_w_out, w_in, w_o, ret_decay, ret_gn_w,
           attn_sink, final_norm_w):
    assert x.shape == (B, S, D) and ctx.shape == (B, CTX, D)
    cos_t, sin_t = _rope_tables()
    chan, w1, twr, twi, w2, wc = _fourier_tables()

    def ffn_sources(layer, j):
        return [(ffn_w_in, (layer, j), D, 2 * D_FF), (ffn_w_out, (layer, j), D_FF, D)]

    cond_t = jnp.zeros((D, 8), F32).at[:, 0:B].set(c.T).at[:, B].set(c_ctx)
    mod_all, *pre_w = _ada(cond_t, w_ada, b_ada, ffn_sources(0, 0) + [(w_in, (0,), D, D_IN)])
    norm4 = norm_w.reshape(DEPTH, 3, 1, D)
    gn3 = ret_gn_w.reshape(DEPTH, 1, 256)
    dec = jnp.repeat(ret_decay.astype(F32), HD, axis=2)
    sink = attn_sink.astype(F32)

    h = None
    out = None
    for layer in range(DEPTH):
        last = layer == DEPTH - 1
        rows_in = (x.reshape(R_LAT, D), ctx.reshape(R_CTX, D)) if layer == 0 else (h,)
        mix_sources = [(w_o, (layer,), D, D)] + ffn_sources(layer, 1)
        (h, gr, gi, rq, rk, rv, rg, aq, ak, av, *mix_w) = _ffn_inproj(
            rows_in, mod_all, norm4, pre_w, mix_sources, cos_t, sin_t, chan, layer)

        yf = _fft(w1, w2, twr, twi, gr, gi)
        yr, sfc, sbc = _ret_lat(dec, gn3, rq, rk, rv, rg, layer)
        ya = _attn_lat(sink, aq, ak, av, layer)
        if last:
            yfc = yrc = yac = None
        else:
            yfc = _fft_ctx(wc, gr, gi)
            yrc = _ret_out_ctx(dec, gn3, rq, rk, rv, rg, sfc, sbc, layer)
            yac = _attn_ctx(sink, aq, ak, av, layer)

        next_sources = [] if last else ffn_sources(layer + 1, 0) + [(w_in, (layer + 1,), D, D_IN)]
        res, *pre_w = _mix_ffn(h, [(yf, yfc), (yr, yrc), (ya, yac)], mod_all, norm4, mix_w, next_sources,
                               final_norm_w.reshape(1, D) if last else None, layer)
        if last:
            out = res
        else:
            h = res
    return out.reshape(B, S, D)
```
